```python
import math
import jax, jax.numpy as jnp
from jax import lax
import numpy as np

D_MODEL = 1024
BATCH = 2
SEQ = 16384
DEPTH = 1

PLE_DIM = 256
MOBA_HEADS = 8
MOBA_HEAD_DIM = D_MODEL // 16
MOBA_WIDTH = MOBA_HEADS * MOBA_HEAD_DIM
MOBA_BLOCK = 256
MOBA_TOPK = 3
MOBA_Q_CHUNK = 64
RET_HEADS = 4
RET_HEAD_DIM = D_MODEL // 8
RET_WIDTH = RET_HEADS * RET_HEAD_DIM
RET_CHUNK = 256
MIX_WIDTH = MOBA_WIDTH + RET_WIDTH
IN_PROJ_WIDTH = 3 * MOBA_WIDTH + 4 * RET_WIDTH
SEQ_PAD_MULTIPLE = 256
D_FF = 2816
CONV_WIDTH = 3
RMS_EPS = 1e-6
GN_EPS = 1e-5

kernel_name = "hymba_moba_retnet_convffn_ple"


def rmsnorm(x, g):
    xf = x.astype(jnp.float32)
    ms = jnp.mean(xf * xf, axis=-1, keepdims=True)
    return (xf * lax.rsqrt(ms + RMS_EPS)).astype(x.dtype) * g


def alibi_slopes(n_heads):
    return jnp.exp2(-8.0 * jnp.arange(1, n_heads + 1, dtype=jnp.float32) / n_heads)


def to_heads(t, n_heads, head_dim):
    b, s, _ = t.shape
    return t.reshape(b, s, n_heads, head_dim).transpose(0, 2, 1, 3)


def from_heads(t):
    b, h, s, d = t.shape
    return t.transpose(0, 2, 1, 3).reshape(b, s, h * d)


def moba_attention(q, k, v):
    B, H, S, Dh = q.shape
    nb = S // MOBA_BLOCK
    k_eff = min(MOBA_TOPK, nb)
    n_sel = k_eff * MOBA_BLOCK
    scale = Dh ** -0.5
    slopes = alibi_slopes(H)[:, None, None]
    kb = k.reshape(B, H, nb, MOBA_BLOCK, Dh)
    vb = v.reshape(B, H, nb, MOBA_BLOCK, Dh)
    kmean = jnp.mean(kb.astype(jnp.float32), axis=3).astype(k.dtype)
    kb_flat = kb.reshape(B * H * nb, MOBA_BLOCK, Dh)
    vb_flat = vb.reshape(B * H * nb, MOBA_BLOCK, Dh)
    head_base = (jnp.arange(B * H, dtype=jnp.int32) * nb).reshape(B, H, 1, 1)
    chunks_per_block = MOBA_BLOCK // MOBA_Q_CHUNK
    blk_pos = jnp.arange(MOBA_BLOCK, dtype=jnp.int32)
    blk_ids = jnp.arange(nb, dtype=jnp.int32)

    def one_chunk(c):
        start = c * MOBA_Q_CHUNK
        qc = lax.dynamic_slice_in_dim(q, start, MOBA_Q_CHUNK, axis=2)
        own = c // chunks_per_block
        q_pos = start + jnp.arange(MOBA_Q_CHUNK, dtype=jnp.int32)
        gate = jnp.einsum('bhqd,bhnd->bhqn', qc, kmean).astype(jnp.float32)
        gate = jnp.where(blk_ids < own, gate, -jnp.inf)
        gate_val, sel = lax.top_k(gate, k_eff)
        sel_valid = jnp.isfinite(gate_val)
        flat = head_base + sel
        k_sel = kb_flat[flat].reshape(B, H, MOBA_Q_CHUNK, n_sel, Dh)
        v_sel = vb_flat[flat].reshape(B, H, MOBA_Q_CHUNK, n_sel, Dh)
        s_sel = jnp.einsum('bhqd,bhqkd->bhqk', qc, k_sel).astype(jnp.float32) * scale
        key_pos_sel = (sel[..., None] * MOBA_BLOCK + blk_pos).reshape(B, H, MOBA_Q_CHUNK, n_sel)
        dist_sel = (q_pos[:, None] - key_pos_sel).astype(jnp.float32)
        s_sel = jnp.where(jnp.repeat(sel_valid, MOBA_BLOCK, axis=-1),
                          s_sel - slopes * dist_sel, -jnp.inf)
        k_own = lax.dynamic_index_in_dim(kb, own, axis=2, keepdims=False)
        v_own = lax.dynamic_index_in_dim(vb, own, axis=2, keepdims=False)
        s_own = jnp.einsum('bhqd,bhkd->bhqk', qc, k_own).astype(jnp.float32) * scale
        dist_own = q_pos[:, None] - (own * MOBA_BLOCK + blk_pos)[None, :]
        s_own = jnp.where(dist_own >= 0, s_own - slopes * dist_own.astype(jnp.float32), -jnp.inf)
        probs = jax.nn.softmax(jnp.concatenate([s_sel, s_own], axis=-1), axis=-1).astype(v.dtype)
        out = (jnp.einsum('bhqk,bhqkd->bhqd', probs[..., :n_sel], v_sel)
               + jnp.einsum('bhqk,bhkd->bhqd', probs[..., n_sel:], v_own))
        return out

    outs = lax.map(one_chunk, jnp.arange(S // MOBA_Q_CHUNK, dtype=jnp.int32))
    return jnp.moveaxis(outs, 0, 2).reshape(B, H, S, Dh)


def retention_chunkwise(q, k, v):
    B, H, S, Dh = q.shape
    C = RET_CHUNK
    nc = S // C
    gamma = 1.0 - jnp.exp2(-5.0 - jnp.arange(H, dtype=jnp.float32))
    log_g = jnp.log(gamma)
    idx = jnp.arange(C, dtype=jnp.float32)
    diff = idx[:, None] - idx[None, :]
    decay_intra = jnp.where(diff >= 0, jnp.exp(log_g[:, None, None] * jnp.maximum(diff, 0.0)), 0.0)
    q_decay = jnp.exp(log_g[:, None] * (idx + 1.0))[..., None]
    k_decay = jnp.exp(log_g[:, None] * (C - 1.0 - idx))[..., None]
    chunk_decay = jnp.exp(log_g * C)[:, None, None]
    k = k * (Dh ** -0.5)

    def chunks(t):
        return jnp.moveaxis(t.reshape(B, H, nc, C, Dh), 2, 0)

    def step(state, inp):
        qc, kc, vc = inp
        s = jnp.einsum('bhqd,bhkd->bhqk', qc, kc) * decay_intra
        inner = jnp.einsum('bhqk,bhkd->bhqd', s, vc)
        cross = jnp.einsum('bhqd,bhde->bhqe', qc * q_decay, state)
        new_state = chunk_decay * state + jnp.einsum('bhkd,bhke->bhde', kc * k_decay, vc)
        return new_state, inner + cross

    state0 = jnp.zeros((B, H, Dh, Dh), jnp.float32)
    _, outs = lax.scan(step, state0, (chunks(q), chunks(k), chunks(v)))
    return jnp.moveaxis(outs, 0, 2).reshape(B, H, S, Dh)


def head_groupnorm(o, g):
    mu = jnp.mean(o, axis=-1, keepdims=True)
    var = jnp.mean(jnp.square(o - mu), axis=-1, keepdims=True)
    return from_heads((o - mu) * lax.rsqrt(var + GN_EPS)) * g


def causal_dwconv(u, w, b):
    S = u.shape[1]
    up = jnp.pad(u, ((0, 0), (CONV_WIDTH - 1, 0), (0, 0)))
    out = b
    for j in range(CONV_WIDTH):
        out = out + w[j] * up[:, j:j + S]
    return out


def setup_inputs(seed: int = 0) -> dict:
    key = jax.random.key(seed)
    ks = jax.random.split(key, 17)
    f32 = jnp.float32

    def nrm(k, shape, scale):
        return jax.random.normal(k, shape, f32) * scale

    def gain(k, shape):
        return 1.0 + 0.02 * jax.random.normal(k, shape, f32)

    return {
        "x": nrm(ks[0], (BATCH, SEQ, D_MODEL), 1.0),
        "p": nrm(ks[1], (DEPTH, BATCH, SEQ, PLE_DIM), 1.0),
        "attn_norm": gain(ks[2], (DEPTH, D_MODEL)),
        "w_in": nrm(ks[3], (DEPTH, D_MODEL, IN_PROJ_WIDTH), D_MODEL ** -0.5),
        "ret_norm": gain(ks[4], (DEPTH, RET_WIDTH)),
        "w_out": nrm(ks[5], (DEPTH, MIX_WIDTH, D_MODEL), MIX_WIDTH ** -0.5),
        "ffn_norm": gain(ks[6], (DEPTH, D_MODEL)),
        "w_up": nrm(ks[7], (DEPTH, D_MODEL, D_FF), D_MODEL ** -0.5),
        "w_gate": nrm(ks[8], (DEPTH, D_MODEL, D_FF), D_MODEL ** -0.5),
        "conv_w": nrm(ks[9], (DEPTH, CONV_WIDTH, D_FF), CONV_WIDTH ** -0.5),
        "conv_b": nrm(ks[10], (DEPTH, D_FF), 0.01),
        "w_down": nrm(ks[11], (DEPTH, D_FF, D_MODEL), D_FF ** -0.5),
        "ple_norm": gain(ks[12], (DEPTH, D_MODEL)),
        "w_ple": nrm(ks[13], (DEPTH, PLE_DIM, D_MODEL), PLE_DIM ** -0.5),
        "w_ple_gate": nrm(ks[14], (DEPTH, D_MODEL, D_MODEL), D_MODEL ** -0.5),
        "final_norm": gain(ks[15], (D_MODEL,)),
    }


def reference(x, p, attn_norm, w_in, ret_norm, w_out, ffn_norm, w_up, w_gate, conv_w,
              conv_b, w_down, ple_norm, w_ple, w_ple_gate, final_norm):
    B, S, _ = x.shape
    S_pad = -(-S // SEQ_PAD_MULTIPLE) * SEQ_PAD_MULTIPLE
    split_points = [MOBA_WIDTH, 2 * MOBA_WIDTH, 3 * MOBA_WIDTH,
                    3 * MOBA_WIDTH + RET_WIDTH, 3 * MOBA_WIDTH + 2 * RET_WIDTH,
                    3 * MOBA_WIDTH + 3 * RET_WIDTH]
    for i in range(DEPTH):
        h = rmsnorm(x, attn_norm[i])
        proj = h @ w_in[i]
        proj = jnp.pad(proj, ((0, 0), (0, S_pad - S), (0, 0)))
        mq, mk, mv, rq, rk, rv, rg = jnp.split(proj, split_points, axis=-1)
        moba_out = from_heads(moba_attention(to_heads(mq, MOBA_HEADS, MOBA_HEAD_DIM),
                                             to_heads(mk, MOBA_HEADS, MOBA_HEAD_DIM),
                                             to_heads(mv, MOBA_HEADS, MOBA_HEAD_DIM)))
        ret = retention_chunkwise(to_heads(rq, RET_HEADS, RET_HEAD_DIM),
                                  to_heads(rk, RET_HEADS, RET_HEAD_DIM),
                                  to_heads(rv, RET_HEADS, RET_HEAD_DIM))
        ret_out = (head_groupnorm(ret, ret_norm[i]) * jax.nn.silu(rg)).astype(x.dtype)
        mix = jnp.concatenate([moba_out, ret_out], axis=-1)[:, :S]
        x = x + mix @ w_out[i]
        h = rmsnorm(x, ffn_norm[i])
        u = causal_dwconv(h @ w_up[i], conv_w[i], conv_b[i])
        x = x + (jax.nn.gelu(u) * (h @ w_gate[i])) @ w_down[i]
        g = jax.nn.sigmoid(rmsnorm(x, ple_norm[i]) @ w_ple_gate[i])
        x = x + (p[i] @ w_ple[i]) * g
    return rmsnorm(x, final_norm)
```

```python
import functools
import math

import jax
import jax.numpy as jnp
from jax import lax
from jax.experimental import pallas as pl
from jax.experimental.pallas import tpu as pltpu

D_MODEL = 1024
PLE_DIM = 256
MOBA_HEADS = 8
MOBA_HEAD_DIM = 64
MOBA_WIDTH = MOBA_HEADS * MOBA_HEAD_DIM
MOBA_BLOCK = 256
MOBA_TOPK = 3
RET_HEADS = 4
RET_HEAD_DIM = 128
RET_WIDTH = RET_HEADS * RET_HEAD_DIM
RET_CHUNK = 256
D_FF = 2816
CONV_WIDTH = 3
RMS_EPS = 1e-6
GN_EPS = 1e-5

BLK = 256
LANES = 128
FF_CHUNK = 256
N_FF_CHUNKS = D_FF // FF_CHUNK
ROW_TILE = 512
VMEM_LIMIT = 56 * 1024 * 1024
NEG = -1e30

_NT = (((1,), (1,)), ((), ()))


def _const_spec(shape):
    nd = len(shape)
    return pl.BlockSpec(shape, lambda *_: (0,) * nd, pipeline_mode=pl.Buffered(1))


def _rms(x, g):
    ms = jnp.mean(x * x, axis=-1, keepdims=True)
    return x * lax.rsqrt(ms + RMS_EPS) * g


def _in_proj_kernel(x_ref, g_ref, wqT_ref, wk_ref, wvT_ref, wrq_ref, wrkT_ref, wrv_ref, wrg_ref,
                    qT_ref, k_ref, vT_ref, kmean_ref, rq_ref, rkT_ref, rv_ref, rg_ref):
    h = _rms(x_ref[...], g_ref[...]).astype(jnp.bfloat16)
    nblk = ROW_TILE // BLK

    def nt(w_ref):
        return lax.dot_general(w_ref[...], h, _NT, preferred_element_type=jnp.float32)

    def nn(w_ref):
        return jnp.dot(h, w_ref[...], preferred_element_type=jnp.float32)

    qT = (nt(wqT_ref) * (MOBA_HEAD_DIM ** -0.5)).astype(jnp.bfloat16)
    vT = nt(wvT_ref).astype(jnp.bfloat16)
    rkT = (nt(wrkT_ref) * (RET_HEAD_DIM ** -0.5)).astype(jnp.bfloat16)
    for b in range(nblk):
        sl = slice(b * BLK, (b + 1) * BLK)
        qT_ref[b] = qT[:, sl]
        vT_ref[b] = vT[:, sl]
        rkT_ref[b] = rkT[:, sl]
    k = nn(wk_ref)
    k_ref[...] = k.astype(jnp.bfloat16)
    for b in range(nblk):
        kmean_ref[b] = jnp.mean(k[b * BLK:(b + 1) * BLK], axis=0, keepdims=True)
    rq_ref[...] = nn(wrq_ref).astype(jnp.bfloat16)
    rv_ref[...] = nn(wrv_ref).astype(jnp.bfloat16)
    rg_ref[...] = nn(wrg_ref).astype(jnp.bfloat16)


def _in_proj(x, g, wqT, wk, wvT, wrq, wrkT, wrv, wrg):
    B, S, D = x.shape
    nb = S // BLK
    nblk = ROW_TILE // BLK
    grid = (B, S // ROW_TILE)
    bf = jnp.bfloat16
    nat = lambda w: pl.BlockSpec((None, ROW_TILE, w), lambda b, t: (b, t, 0))
    tr = lambda w: pl.BlockSpec((None, nblk, w, BLK), lambda b, t: (b, t, 0, 0))
    out_shape = (
        jax.ShapeDtypeStruct((B, nb, MOBA_WIDTH, BLK), bf),
        jax.ShapeDtypeStruct((B, S, MOBA_WIDTH), bf),
        jax.ShapeDtypeStruct((B, nb, MOBA_WIDTH, BLK), bf),
        jax.ShapeDtypeStruct((B, nb, 1, MOBA_WIDTH), jnp.float32),
        jax.ShapeDtypeStruct((B, S, RET_WIDTH), bf),
        jax.ShapeDtypeStruct((B, nb, RET_WIDTH, BLK), bf),
        jax.ShapeDtypeStruct((B, S, RET_WIDTH), bf),
        jax.ShapeDtypeStruct((B, S, RET_WIDTH), bf),
    )
    out_specs = (
        tr(MOBA_WIDTH), nat(MOBA_WIDTH), tr(MOBA_WIDTH),
        pl.BlockSpec((None, nblk, 1, MOBA_WIDTH), lambda b, t: (b, t, 0, 0)),
        nat(RET_WIDTH), tr(RET_WIDTH), nat(RET_WIDTH), nat(RET_WIDTH),
    )
    in_specs = [
        pl.BlockSpec((None, ROW_TILE, D), lambda b, t: (b, t, 0)),
        _const_spec(g.shape),
        _const_spec(wqT.shape), _const_spec(wk.shape), _const_spec(wvT.shape),
        _const_spec(wrq.shape), _const_spec(wrkT.shape), _const_spec(wrv.shape),
        _const_spec(wrg.shape),
    ]
    return pl.pallas_call(
        _in_proj_kernel, grid=grid, in_specs=in_specs, out_specs=out_specs, out_shape=out_shape,
        compiler_params=pltpu.CompilerParams(
            dimension_semantics=("parallel", "parallel"), vmem_limit_bytes=VMEM_LIMIT),
        name="in_proj",
    )(x, g, wqT, wk, wvT, wrq, wrkT, wrv, wrg)


def _moba_kernel(slopes_ref, qT_ref, k_ref, vT_ref, kmean_ref, o_ref, bias_ref, oT_ref):
    hp = pl.program_id(1)
    i = pl.program_id(2)
    nb = k_ref.shape[0]
    qT = qT_ref[...]
    row128 = lax.broadcasted_iota(jnp.int32, (LANES, BLK), 0)
    blk_id = lax.broadcasted_iota(jnp.int32, (nb, BLK), 0)
    kpos = lax.broadcasted_iota(jnp.int32, (BLK, BLK), 0)
    qpos = lax.broadcasted_iota(jnp.int32, (BLK, BLK), 1)
    rel = (kpos - qpos).astype(jnp.float32)
    km = kmean_ref[...]
    km_hi = km.astype(jnp.bfloat16)
    km_lo = (km - km_hi.astype(jnp.float32)).astype(jnp.bfloat16)

    for a in range(2):
        slope = slopes_ref[hp * 2 + a]
        head_rows = (row128 >= a * MOBA_HEAD_DIM) & (row128 < (a + 1) * MOBA_HEAD_DIM)
        qTh = jnp.where(head_rows, qT, jnp.zeros_like(qT))

        gate = (jnp.dot(km_hi, qTh, preferred_element_type=jnp.float32)
                + jnp.dot(km_lo, qTh, preferred_element_type=jnp.float32))
        gate = jnp.where(blk_id < i, gate, -jnp.inf)
        sel = jnp.zeros((nb, BLK), jnp.bool_)
        for _ in range(MOBA_TOPK):
            top = jnp.max(gate, axis=0, keepdims=True)
            is_top = (gate == top) & (top > -jnp.inf)
            first = jnp.min(jnp.where(is_top, blk_id, nb), axis=0, keepdims=True)
            pick = blk_id == first
            sel = sel | pick
            gate = jnp.where(pick, -jnp.inf, gate)
        blk_dist = ((i - blk_id) * MOBA_BLOCK).astype(jnp.float32)
        bias_ref[...] = jnp.where(sel, -slope * blk_dist, NEG)

        alibi = slope * rel
        vrows = pl.ds(a * MOBA_HEAD_DIM, MOBA_HEAD_DIM)

        sT = jnp.dot(k_ref[i], qTh, preferred_element_type=jnp.float32)
        sT = jnp.where(kpos <= qpos, sT + alibi, NEG)
        m0 = jnp.max(sT, axis=0, keepdims=True)
        p = jnp.exp(sT - m0)
        l0 = jnp.sum(p, axis=0, keepdims=True)
        acc0 = jnp.dot(vT_ref[i, vrows, :], p.astype(jnp.bfloat16),
                       preferred_element_type=jnp.float32)

        def body(j, carry):
            m, l, acc = carry
            s = jnp.dot(k_ref[j], qTh, preferred_element_type=jnp.float32)
            s = s + alibi + bias_ref[pl.ds(j, 1), :]
            m_new = jnp.maximum(m, jnp.max(s, axis=0, keepdims=True))
            alpha = jnp.exp(m - m_new)
            pj = jnp.exp(s - m_new)
            l_new = alpha * l + jnp.sum(pj, axis=0, keepdims=True)
            acc_new = alpha * acc + jnp.dot(vT_ref[j, vrows, :], pj.astype(jnp.bfloat16),
                                            preferred_element_type=jnp.float32)
            return m_new, l_new, acc_new

        _, l, acc = lax.fori_loop(0, i, body, (m0, l0, acc0))
        oT_ref[vrows, :] = acc / l

    o_ref[...] = oT_ref[...].T.astype(o_ref.dtype)


def _moba(slopes, qT, k, vT, kmean):
    B, nb, W, _ = qT.shape
    S = nb * BLK
    k4 = k.reshape(B, nb, BLK, W)
    kmean3 = kmean.reshape(B, nb, W)
    grid = (B, W // LANES, nb)
    grid_spec = pltpu.PrefetchScalarGridSpec(
        num_scalar_prefetch=1,
        grid=grid,
        in_specs=[
            pl.BlockSpec((None, None, LANES, BLK), lambda b, h, i, s: (b, i, h, 0)),
            pl.BlockSpec((None, nb, BLK, LANES), lambda b, h, i, s: (b, 0, 0, h)),
            pl.BlockSpec((None, nb, LANES, BLK), lambda b, h, i, s: (b, 0, h, 0)),
            pl.BlockSpec((None, nb, LANES), lambda b, h, i, s: (b, 0, h)),
        ],
        out_specs=pl.BlockSpec((None, BLK, LANES), lambda b, h, i, s: (b, i, h)),
        scratch_shapes=[
            pltpu.VMEM((nb, BLK), jnp.float32),
            pltpu.VMEM((LANES, BLK), jnp.float32),
        ],
    )
    return pl.pallas_call(
        _moba_kernel, grid_spec=grid_spec,
        out_shape=jax.ShapeDtypeStruct((B, S, W), jnp.bfloat16),
        compiler_params=pltpu.CompilerParams(
            dimension_semantics=("parallel", "parallel", "arbitrary"),
            vmem_limit_bytes=VMEM_LIMIT),
        name="moba",
    )(slopes, qT, k4, vT, kmean3)


def _ret_kernel(cd_ref, q_ref, kT_ref, v_ref, g_ref, dintra_ref, qdec_ref, kdec_ref, gn_ref,
                o_ref, state_ref):
    h = pl.program_id(1)
    c = pl.program_id(2)

    @pl.when(c == 0)
    def _():
        state_ref[...] = jnp.zeros_like(state_ref)

    q = q_ref[...]
    kT = kT_ref[...]
    v = v_ref[...]
    state = state_ref[...]
    s = jnp.dot(q, kT, preferred_element_type=jnp.float32) * dintra_ref[...]
    inner = jnp.dot(s.astype(jnp.bfloat16), v, preferred_element_type=jnp.float32)
    qd = (q.astype(jnp.float32) * qdec_ref[...]).astype(jnp.bfloat16)
    cross = jnp.dot(qd, state.astype(jnp.bfloat16), preferred_element_type=jnp.float32)
    kd = (kT.astype(jnp.float32) * kdec_ref[...]).astype(jnp.bfloat16)
    state_ref[...] = cd_ref[h] * state + jnp.dot(kd, v, preferred_element_type=jnp.float32)
    o = inner + cross
    mu = jnp.mean(o, axis=-1, keepdims=True)
    d = o - mu
    var = jnp.mean(d * d, axis=-1, keepdims=True)
    gate = g_ref[...].astype(jnp.float32)
    y = d * lax.rsqrt(var + GN_EPS) * gn_ref[...] * (gate * jax.nn.sigmoid(gate))
    o_ref[...] = y.astype(o_ref.dtype)


def _retention(rq, rkT, rv, rg, ret_norm):
    B, S, W = rq.shape
    nc = S // BLK
    H = RET_HEADS
    C = RET_CHUNK
    f32 = jnp.float32
    gamma = 1.0 - jnp.exp2(-5.0 - jnp.arange(H, dtype=f32))
    log_g = jnp.log(gamma)
    idx = jnp.arange(C, dtype=f32)
    diff = idx[:, None] - idx[None, :]
    dintra = jnp.where(diff >= 0, jnp.exp(log_g[:, None, None] * jnp.maximum(diff, 0.0)), 0.0)
    qdec = jnp.broadcast_to(jnp.exp(log_g[:, None] * (idx + 1.0))[..., None], (H, C, LANES))
    kdec = jnp.exp(log_g[:, None] * (C - 1.0 - idx))[:, None, :]
    cdec = jnp.exp(log_g * C)
    gn = ret_norm.reshape(1, W).astype(f32)

    nat = pl.BlockSpec((None, C, LANES), lambda b, h, c, s: (b, c, h))
    grid_spec = pltpu.PrefetchScalarGridSpec(
        num_scalar_prefetch=1,
        grid=(B, H, nc),
        in_specs=[
            nat,
            pl.BlockSpec((None, None, LANES, C), lambda b, h, c, s: (b, c, h, 0)),
            nat, nat,
            pl.BlockSpec((None, C, C), lambda b, h, c, s: (h, 0, 0)),
            pl.BlockSpec((None, C, LANES), lambda b, h, c, s: (h, 0, 0)),
            pl.BlockSpec((None, 1, C), lambda b, h, c, s: (h, 0, 0)),
            pl.BlockSpec((1, LANES), lambda b, h, c, s: (0, h)),
        ],
        out_specs=nat,
        scratch_shapes=[pltpu.VMEM((RET_HEAD_DIM, RET_HEAD_DIM), f32)],
    )
    return pl.pallas_call(
        _ret_kernel, grid_spec=grid_spec,
        out_shape=jax.ShapeDtypeStruct((B, S, W), jnp.bfloat16),
        compiler_params=pltpu.CompilerParams(
            dimension_semantics=("parallel", "parallel", "arbitrary"),
            vmem_limit_bytes=VMEM_LIMIT),
        name="retention",
    )(cdec, rq, rkT, rv, rg, dintra, qdec, kdec, gn)


def _gelu_tanh(u):
    c = math.sqrt(2.0 / math.pi)
    return 0.5 * u * (1.0 + jnp.tanh(c * (u + 0.044715 * (u * u * u))))


def _tail_kernel(x_ref, moba_ref, ret_ref, p_ref, wo_m_ref, wo_r_ref, ffn_g_ref, wup_ref, wgate_ref,
                 convw_ref, convb_ref, wdown_ref, ple_g_ref, wple_ref, wpg_ref, fin_g_ref,
                 o_ref, upbuf_ref, carry_ref, *, tiles_per_seq):
    t = pl.program_id(0)
    R = ROW_TILE
    HALO = 8

    @pl.when(t % tiles_per_seq == 0)
    def _():
        carry_ref[...] = jnp.zeros_like(carry_ref)

    x1 = (x_ref[...]
          + jnp.dot(moba_ref[...], wo_m_ref[...], preferred_element_type=jnp.float32)
          + jnp.dot(ret_ref[...], wo_r_ref[...], preferred_element_type=jnp.float32))
    h = _rms(x1, ffn_g_ref[...]).astype(jnp.bfloat16)

    acc = jnp.zeros((R, D_MODEL), jnp.float32)
    for c in range(N_FF_CHUNKS):
        up = jnp.dot(h, wup_ref[c], preferred_element_type=jnp.float32)
        gt = jnp.dot(h, wgate_ref[c], preferred_element_type=jnp.float32)
        upbuf_ref[0:HALO, :] = carry_ref[c]
        upbuf_ref[HALO:HALO + R, :] = up
        carry_ref[c] = up[R - HALO:R, :]
        w = convw_ref[c]
        u = (convb_ref[c]
             + w[0:1, :] * upbuf_ref[pl.ds(HALO - 2, R), :]
             + w[1:2, :] * upbuf_ref[pl.ds(HALO - 1, R), :]
             + w[2:3, :] * up)
        y = (_gelu_tanh(u) * gt).astype(jnp.bfloat16)
        acc = acc + jnp.dot(y, wdown_ref[c], preferred_element_type=jnp.float32)
    x2 = x1 + acc

    hn = _rms(x2, ple_g_ref[...]).astype(jnp.bfloat16)
    g = jax.nn.sigmoid(jnp.dot(hn, wpg_ref[...], preferred_element_type=jnp.float32))
    pe = jnp.dot(p_ref[...].astype(jnp.bfloat16), wple_ref[...], preferred_element_type=jnp.float32)
    x3 = x2 + pe * g
    o_ref[...] = _rms(x3, fin_g_ref[...])


def _tail(x2d, moba2d, ret2d, p2d, wo_m, wo_r, ffn_g, wup, wgate, convw, convb, wdown, ple_g, wple,
          wpg, fin_g, seq_len):
    T, D = x2d.shape
    R = ROW_TILE
    row = lambda w: pl.BlockSpec((R, w), lambda t: (t, 0))
    consts = (wo_m, wo_r, ffn_g, wup, wgate, convw, convb, wdown, ple_g, wple, wpg, fin_g)
    in_specs = [row(D), row(MOBA_WIDTH), row(RET_WIDTH), row(PLE_DIM)] + [
        _const_spec(a.shape) for a in consts]
    return pl.pallas_call(
        functools.partial(_tail_kernel, tiles_per_seq=seq_len // R),
        grid=(T // R,), in_specs=in_specs, out_specs=row(D),
        out_shape=jax.ShapeDtypeStruct((T, D), jnp.float32),
        scratch_shapes=[
            pltpu.VMEM((R + 8, FF_CHUNK), jnp.float32),
            pltpu.VMEM((N_FF_CHUNKS, 8, FF_CHUNK), jnp.float32),
        ],
        compiler_params=pltpu.CompilerParams(
            dimension_semantics=("arbitrary",), vmem_limit_bytes=VMEM_LIMIT),
        name="tail",
    )(x2d, moba2d, ret2d, p2d, *consts)


def kernel(x, p, attn_norm, w_in, ret_norm, w_out, ffn_norm, w_up, w_gate, conv_w, conv_b, w_down,
           ple_norm, w_ple, w_ple_gate, final_norm):
    B, S, D = x.shape
    assert D == D_MODEL and S % ROW_TILE == 0 and w_in.shape[0] == 1
    bf = jnp.bfloat16
    MW, RW = MOBA_WIDTH, RET_WIDTH
    w = w_in[0].astype(bf)
    wqT = w[:, 0:MW].T
    wk = w[:, MW:2 * MW]
    wvT = w[:, 2 * MW:3 * MW].T
    o = 3 * MW
    wrq = w[:, o:o + RW]
    wrkT = w[:, o + RW:o + 2 * RW].T
    wrv = w[:, o + 2 * RW:o + 3 * RW]
    wrg = w[:, o + 3 * RW:o + 4 * RW]
    row = lambda a: a.reshape(1, -1).astype(jnp.float32)

    qT, k, vT, kmean, rq, rkT, rv, rg = _in_proj(x, row(attn_norm[0]), wqT, wk, wvT, wrq, wrkT, wrv, wrg)

    slopes = jnp.exp2(-8.0 * jnp.arange(1, MOBA_HEADS + 1, dtype=jnp.float32) / MOBA_HEADS)
    moba_out = _moba(slopes, qT, k, vT, kmean)
    ret_out = _retention(rq, rkT, rv, rg, ret_norm[0])

    wo = w_out[0].astype(bf)
    chunked = lambda a: a.reshape(D, N_FF_CHUNKS, FF_CHUNK).transpose(1, 0, 2)
    out = _tail(
        x.reshape(B * S, D), moba_out.reshape(B * S, MW), ret_out.reshape(B * S, RW),
        p[0].reshape(B * S, PLE_DIM),
        wo[:MW], wo[MW:], row(ffn_norm[0]),
        chunked(w_up[0].astype(bf)), chunked(w_gate[0].astype(bf)),
        conv_w[0].reshape(CONV_WIDTH, N_FF_CHUNKS, FF_CHUNK).transpose(1, 0, 2).astype(jnp.float32),
        conv_b[0].reshape(N_FF_CHUNKS, 1, FF_CHUNK).astype(jnp.float32),
        w_down[0].astype(bf).reshape(N_FF_CHUNKS, FF_CHUNK, D),
        row(ple_norm[0]), w_ple[0].astype(bf), w_ple_gate[0].astype(bf), row(final_norm),
        seq_len=S)
    return out.reshape(B, S, D)
```

```python
import functools
import math

import jax
import jax.numpy as jnp
from jax import lax
from jax.experimental import pallas as pl
from jax.experimental.pallas import tpu as pltpu

D_MODEL = 1024
PLE_DIM = 256
MOBA_HEADS = 8
MOBA_HEAD_DIM = 64
MOBA_WIDTH = MOBA_HEADS * MOBA_HEAD_DIM
MOBA_BLOCK = 256
MOBA_TOPK = 3
RET_HEADS = 4
RET_HEAD_DIM = 128
RET_WIDTH = RET_HEADS * RET_HEAD_DIM
RET_CHUNK = 256
D_FF = 2816
CONV_WIDTH = 3
RMS_EPS = 1e-6
GN_EPS = 1e-5

BLK = 256
LANES = 128
FF_CHUNK = 256
N_FF_CHUNKS = D_FF // FF_CHUNK
ROW_TILE = 512
VMEM_LIMIT = 56 * 1024 * 1024
NEG = -1e30
LOG2E = math.log2(math.e)

QK_PAD = LANES
ALIBI_ROW = MOBA_HEAD_DIM
V_PAD = 80
ONES_ROW = MOBA_HEAD_DIM

_NT = (((1,), (1,)), ((), ()))


def _const_spec(shape):
    nd = len(shape)
    return pl.BlockSpec(shape, lambda *_: (0,) * nd, pipeline_mode=pl.Buffered(1))


def _rms(x, g):
    ms = jnp.mean(x * x, axis=-1, keepdims=True)
    return x * lax.rsqrt(ms + RMS_EPS) * g


def _in_proj_kernel(x_ref, g_ref, wqT_ref, qone_ref, wk_ref, kpos_ref, wvT_ref, vone_ref,
                    wrq_ref, wrkT_ref, wrv_ref, wrg_ref,
                    qT_ref, k_ref, vT_ref, kmean_ref, rq_ref, rkT_ref, rv_ref, rg_ref):
    h = _rms(x_ref[...], g_ref[...]).astype(jnp.bfloat16)
    nblk = ROW_TILE // BLK

    def nt(w_ref):
        return lax.dot_general(w_ref[...], h, _NT, preferred_element_type=jnp.float32)

    def nn(w_ref):
        return jnp.dot(h, w_ref[...], preferred_element_type=jnp.float32)

    qT = (nt(wqT_ref) * (MOBA_HEAD_DIM ** -0.5 * LOG2E) + qone_ref[...]).astype(jnp.bfloat16)
    vT = (nt(wvT_ref) + vone_ref[...]).astype(jnp.bfloat16)
    rkT = (nt(wrkT_ref) * (RET_HEAD_DIM ** -0.5)).astype(jnp.bfloat16)
    for b in range(nblk):
        sl = slice(b * BLK, (b + 1) * BLK)
        qT_ref[b] = qT[:, sl]
        vT_ref[b] = vT[:, sl]
        rkT_ref[b] = rkT[:, sl]
    k = nn(wk_ref)
    for b in range(nblk):
        kb = k[b * BLK:(b + 1) * BLK]
        kmean_ref[b] = jnp.mean(kb, axis=0, keepdims=True)
        k_ref[b * BLK:(b + 1) * BLK, :] = (kb + kpos_ref[...]).astype(jnp.bfloat16)
    rq_ref[...] = nn(wrq_ref).astype(jnp.bfloat16)
    rv_ref[...] = nn(wrv_ref).astype(jnp.bfloat16)
    rg_ref[...] = nn(wrg_ref).astype(jnp.bfloat16)


def _in_proj(x, g, wqT, qone, wk, kpos, wvT, vone, wrq, wrkT, wrv, wrg):
    B, S, D = x.shape
    nb = S // BLK
    nblk = ROW_TILE // BLK
    grid = (B, S // ROW_TILE)
    bf = jnp.bfloat16
    QW, VW = MOBA_HEADS * QK_PAD, MOBA_HEADS * V_PAD
    nat = lambda w: pl.BlockSpec((None, ROW_TILE, w), lambda b, t: (b, t, 0))
    tr = lambda w: pl.BlockSpec((None, nblk, w, BLK), lambda b, t: (b, t, 0, 0))
    out_shape = (
        jax.ShapeDtypeStruct((B, nb, QW, BLK), bf),
        jax.ShapeDtypeStruct((B, S, QW), bf),
        jax.ShapeDtypeStruct((B, nb, VW, BLK), bf),
        jax.ShapeDtypeStruct((B, nb, 1, QW), jnp.float32),
        jax.ShapeDtypeStruct((B, S, RET_WIDTH), bf),
        jax.ShapeDtypeStruct((B, nb, RET_WIDTH, BLK), bf),
        jax.ShapeDtypeStruct((B, S, RET_WIDTH), bf),
        jax.ShapeDtypeStruct((B, S, RET_WIDTH), bf),
    )
    out_specs = (
        tr(QW), nat(QW), tr(VW),
        pl.BlockSpec((None, nblk, 1, QW), lambda b, t: (b, t, 0, 0)),
        nat(RET_WIDTH), tr(RET_WIDTH), nat(RET_WIDTH), nat(RET_WIDTH),
    )
    consts = (g, wqT, qone, wk, kpos, wvT, vone, wrq, wrkT, wrv, wrg)
    in_specs = [pl.BlockSpec((None, ROW_TILE, D), lambda b, t: (b, t, 0))] + [
        _const_spec(a.shape) for a in consts]
    return pl.pallas_call(
        _in_proj_kernel, grid=grid, in_specs=in_specs, out_specs=out_specs, out_shape=out_shape,
        compiler_params=pltpu.CompilerParams(
            dimension_semantics=("parallel", "parallel"), vmem_limit_bytes=VMEM_LIMIT),
        name="in_proj",
    )(x, *consts)


def _moba_kernel(slopes_ref, qT_ref, k_ref, vT_ref, kmean_ref, o_ref,
                 bias_ref, s_ref, p_ref, oT_ref):
    hp = pl.program_id(1)
    i = pl.program_id(2)
    nb = k_ref.shape[0]
    own_row, neg_row = nb + 1, nb
    blk_id = lax.broadcasted_iota(jnp.int32, (nb, BLK), 0)
    kpos = lax.broadcasted_iota(jnp.int32, (BLK, BLK), 0)
    qpos = lax.broadcasted_iota(jnp.int32, (BLK, BLK), 1)
    qcol = lax.broadcasted_iota(jnp.int32, (1, BLK), 1).astype(jnp.float32)
    last_past = jnp.maximum(i - 1, 0)

    def q_of(a):
        return qT_ref[a * QK_PAD:(a + 1) * QK_PAD, :]

    def k_of(a, j):
        return k_ref[j, :, a * QK_PAD:(a + 1) * QK_PAD]

    def v_of(a, j):
        return vT_ref[j, a * V_PAD:(a + 1) * V_PAD, :]

    for a in range(2):
        slope = slopes_ref[hp * 2 + a]
        qTh = q_of(a)
        km = kmean_ref[:, a * QK_PAD:(a + 1) * QK_PAD]
        km_hi = km.astype(jnp.bfloat16)
        km_lo = (km - km_hi.astype(jnp.float32)).astype(jnp.bfloat16)
        gate = (jnp.dot(km_hi, qTh, preferred_element_type=jnp.float32)
                + jnp.dot(km_lo, qTh, preferred_element_type=jnp.float32))
        gate = jnp.where(blk_id < i, gate, -jnp.inf)
        sel = jnp.zeros((nb, BLK), jnp.bool_)
        for _ in range(MOBA_TOPK):
            top = jnp.max(gate, axis=0, keepdims=True)
            is_top = (gate == top) & (top > -jnp.inf)
            first = jnp.min(jnp.where(is_top, blk_id, nb), axis=0, keepdims=True)
            pick = blk_id == first
            sel = sel | pick
            gate = jnp.where(pick, -jnp.inf, gate)
        qterm = -slope * qcol
        blk_dist = ((i - blk_id) * MOBA_BLOCK).astype(jnp.float32)
        bias_ref[a, 0:nb, :] = jnp.where(sel, -slope * blk_dist, NEG) + qterm
        bias_ref[a, neg_row:neg_row + 1, :] = jnp.full((1, BLK), NEG, jnp.float32)
        bias_ref[a, own_row:own_row + 1, :] = qterm
        s0 = jnp.dot(k_of(a, i), qTh, preferred_element_type=jnp.float32)
        s_ref[a, 0] = jnp.where(kpos <= qpos, s0, NEG)
        p_ref[a, 1] = jnp.zeros((BLK, BLK), jnp.bfloat16)

    def step(t, slot, carry):
        out = []
        for a in range(2):
            m, alpha_prev, acc = carry[a]
            jv = jnp.where(t == 1, i, jnp.clip(t - 2, 0, last_past))
            acc = alpha_prev * acc + jnp.dot(v_of(a, jv), p_ref[a, 1 - slot],
                                             preferred_element_type=jnp.float32)
            jk = jnp.minimum(t, last_past)
            s_ref[a, 1 - slot] = jnp.dot(k_of(a, jk), q_of(a), preferred_element_type=jnp.float32)
            r = jnp.where(t == 0, own_row, jnp.where(t <= i, t - 1, neg_row))
            c = bias_ref[a, pl.ds(r, 1), :]
            s = s_ref[a, slot]
            m_new = jnp.maximum(m, jnp.max(s, axis=0, keepdims=True) + c)
            alpha = jnp.exp2(m - m_new)
            p_ref[a, slot] = jnp.exp2(s - (m_new - c)).astype(jnp.bfloat16)
            out.append((m_new, alpha, acc))
        return tuple(out)

    def body(u, carry):
        carry = step(2 * u, 0, carry)
        return step(2 * u + 1, 1, carry)

    init = tuple((jnp.full((1, BLK), -3e38, jnp.float32), jnp.ones((1, BLK), jnp.float32),
                  jnp.zeros((V_PAD, BLK), jnp.float32)) for _ in range(2))
    n_iter = (i + 2) // 2
    carry = lax.fori_loop(0, n_iter, body, init)
    t_last = 2 * n_iter - 1
    for a in range(2):
        _, alpha_prev, acc = carry[a]
        jv = jnp.clip(t_last - 1, 0, last_past)
        acc = alpha_prev * acc + jnp.dot(v_of(a, jv), p_ref[a, 1], preferred_element_type=jnp.float32)
        oT_ref[a * MOBA_HEAD_DIM:(a + 1) * MOBA_HEAD_DIM, :] = (
            acc[0:MOBA_HEAD_DIM] / acc[ONES_ROW:ONES_ROW + 1])
    o_ref[...] = oT_ref[...].T.astype(o_ref.dtype)


def _moba(slopes_l2, qT, k, vT, kmean):
    B, nb, QW, _ = qT.shape
    S = nb * BLK
    k4 = k.reshape(B, nb, BLK, QW)
    kmean3 = kmean.reshape(B, nb, QW)
    n_pairs = MOBA_HEADS // 2
    grid = (B, n_pairs, nb)
    grid_spec = pltpu.PrefetchScalarGridSpec(
        num_scalar_prefetch=1,
        grid=grid,
        in_specs=[
            pl.BlockSpec((None, None, 2 * QK_PAD, BLK), lambda b, h, i, s: (b, i, h, 0)),
            pl.BlockSpec((None, nb, BLK, 2 * QK_PAD), lambda b, h, i, s: (b, 0, 0, h)),
            pl.BlockSpec((None, nb, 2 * V_PAD, BLK), lambda b, h, i, s: (b, 0, h, 0)),
            pl.BlockSpec((None, nb, 2 * QK_PAD), lambda b, h, i, s: (b, 0, h)),
        ],
        out_specs=pl.BlockSpec((None, BLK, 2 * MOBA_HEAD_DIM), lambda b, h, i, s: (b, i, h)),
        scratch_shapes=[
            pltpu.VMEM((2, nb + 8, BLK), jnp.float32),
            pltpu.VMEM((2, 2, BLK, BLK), jnp.float32),
            pltpu.VMEM((2, 2, BLK, BLK), jnp.bfloat16),
            pltpu.VMEM((2 * MOBA_HEAD_DIM, BLK), jnp.float32),
        ],
    )
    return pl.pallas_call(
        _moba_kernel, grid_spec=grid_spec,
        out_shape=jax.ShapeDtypeStruct((B, S, MOBA_WIDTH), jnp.bfloat16),
        compiler_params=pltpu.CompilerParams(
            dimension_semantics=("parallel", "parallel", "arbitrary"),
            vmem_limit_bytes=VMEM_LIMIT),
        name="moba",
    )(slopes_l2, qT, k4, vT, kmean3)


def _ret_kernel(cd_ref, q_ref, kT_ref, v_ref, g_ref, dintra_ref, qdec_ref, kdec_ref, gn_ref,
                o_ref, state_ref):
    h = pl.program_id(1)
    c = pl.program_id(2)

    @pl.when(c == 0)
    def _():
        state_ref[...] = jnp.zeros_like(state_ref)

    q = q_ref[...]
    kT = kT_ref[...]
    v = v_ref[...]
    state = state_ref[...]
    s = jnp.dot(q, kT, preferred_element_type=jnp.float32) * dintra_ref[...]
    inner = jnp.dot(s.astype(jnp.bfloat16), v, preferred_element_type=jnp.float32)
    qd = (q.astype(jnp.float32) * qdec_ref[...]).astype(jnp.bfloat16)
    cross = jnp.dot(qd, state.astype(jnp.bfloat16), preferred_element_type=jnp.float32)
    kd = (kT.astype(jnp.float32) * kdec_ref[...]).astype(jnp.bfloat16)
    state_ref[...] = cd_ref[h] * state + jnp.dot(kd, v, preferred_element_type=jnp.float32)
    o = inner + cross
    mu = jnp.mean(o, axis=-1, keepdims=True)
    d = o - mu
    var = jnp.mean(d * d, axis=-1, keepdims=True)
    gate = g_ref[...].astype(jnp.float32)
    y = d * lax.rsqrt(var + GN_EPS) * gn_ref[...] * (gate * jax.nn.sigmoid(gate))
    o_ref[...] = y.astype(o_ref.dtype)


def _retention(rq, rkT, rv, rg, ret_norm):
    B, S, W = rq.shape
    nc = S // BLK
    H = RET_HEADS
    C = RET_CHUNK
    f32 = jnp.float32
    gamma = 1.0 - jnp.exp2(-5.0 - jnp.arange(H, dtype=f32))
    log_g = jnp.log(gamma)
    idx = jnp.arange(C, dtype=f32)
    diff = idx[:, None] - idx[None, :]
    dintra = jnp.where(diff >= 0, jnp.exp(log_g[:, None, None] * jnp.maximum(diff, 0.0)), 0.0)
    qdec = jnp.broadcast_to(jnp.exp(log_g[:, None] * (idx + 1.0))[..., None], (H, C, LANES))
    kdec = jnp.exp(log_g[:, None] * (C - 1.0 - idx))[:, None, :]
    cdec = jnp.exp(log_g * C)
    gn = ret_norm.reshape(1, W).astype(f32)

    nat = pl.BlockSpec((None, C, LANES), lambda b, h, c, s: (b, c, h))
    grid_spec = pltpu.PrefetchScalarGridSpec(
        num_scalar_prefetch=1,
        grid=(B, H, nc),
        in_specs=[
            nat,
            pl.BlockSpec((None, None, LANES, C), lambda b, h, c, s: (b, c, h, 0)),
            nat, nat,
            pl.BlockSpec((None, C, C), lambda b, h, c, s: (h, 0, 0)),
            pl.BlockSpec((None, C, LANES), lambda b, h, c, s: (h, 0, 0)),
            pl.BlockSpec((None, 1, C), lambda b, h, c, s: (h, 0, 0)),
            pl.BlockSpec((1, LANES), lambda b, h, c, s: (0, h)),
        ],
        out_specs=nat,
        scratch_shapes=[pltpu.VMEM((RET_HEAD_DIM, RET_HEAD_DIM), f32)],
    )
    return pl.pallas_call(
        _ret_kernel, grid_spec=grid_spec,
        out_shape=jax.ShapeDtypeStruct((B, S, W), jnp.bfloat16),
        compiler_params=pltpu.CompilerParams(
            dimension_semantics=("parallel", "parallel", "arbitrary"),
            vmem_limit_bytes=VMEM_LIMIT),
        name="retention",
    )(cdec, rq, rkT, rv, rg, dintra, qdec, kdec, gn)


def _gelu_tanh(u):
    c = math.sqrt(2.0 / math.pi)
    return 0.5 * u * (1.0 + jnp.tanh(c * (u + 0.044715 * (u * u * u))))


def _tail_kernel(x_ref, moba_ref, ret_ref, p_ref, wo_m_ref, wo_r_ref, ffn_g_ref, wup_ref, wgate_ref,
                 convw_ref, convb_ref, wdown_ref, ple_g_ref, wple_ref, wpg_ref, fin_g_ref,
                 o_ref, upbuf_ref, carry_ref, *, tiles_per_seq):
    t = pl.program_id(0)
    R = ROW_TILE
    HALO = 8

    @pl.when(t % tiles_per_seq == 0)
    def _():
        carry_ref[...] = jnp.zeros_like(carry_ref)

    x1 = (x_ref[...]
          + jnp.dot(moba_ref[...], wo_m_ref[...], preferred_element_type=jnp.float32)
          + jnp.dot(ret_ref[...], wo_r_ref[...], preferred_element_type=jnp.float32))
    h = _rms(x1, ffn_g_ref[...]).astype(jnp.bfloat16)

    acc = jnp.zeros((R, D_MODEL), jnp.float32)
    for c in range(N_FF_CHUNKS):
        up = jnp.dot(h, wup_ref[c], preferred_element_type=jnp.float32)
        gt = jnp.dot(h, wgate_ref[c], preferred_element_type=jnp.float32)
        upbuf_ref[0:HALO, :] = carry_ref[c]
        upbuf_ref[HALO:HALO + R, :] = up
        carry_ref[c] = up[R - HALO:R, :]
        w = convw_ref[c]
        u = (convb_ref[c]
             + w[0:1, :] * upbuf_ref[pl.ds(HALO - 2, R), :]
             + w[1:2, :] * upbuf_ref[pl.ds(HALO - 1, R), :]
             + w[2:3, :] * up)
        y = (_gelu_tanh(u) * gt).astype(jnp.bfloat16)
        acc = acc + jnp.dot(y, wdown_ref[c], preferred_element_type=jnp.float32)
    x2 = x1 + acc

    hn = _rms(x2, ple_g_ref[...]).astype(jnp.bfloat16)
    g = jax.nn.sigmoid(jnp.dot(hn, wpg_ref[...], preferred_element_type=jnp.float32))
    pe = jnp.dot(p_ref[...].astype(jnp.bfloat16), wple_ref[...], preferred_element_type=jnp.float32)
    x3 = x2 + pe * g
    o_ref[...] = _rms(x3, fin_g_ref[...])


def _tail(x2d, moba2d, ret2d, p2d, wo_m, wo_r, ffn_g, wup, wgate, convw, convb, wdown, ple_g, wple,
          wpg, fin_g, seq_len):
    T, D = x2d.shape
    R = ROW_TILE
    row = lambda w: pl.BlockSpec((R, w), lambda t: (t, 0))
    consts = (wo_m, wo_r, ffn_g, wup, wgate, convw, convb, wdown, ple_g, wple, wpg, fin_g)
    in_specs = [row(D), row(MOBA_WIDTH), row(RET_WIDTH), row(PLE_DIM)] + [
        _const_spec(a.shape) for a in consts]
    return pl.pallas_call(
        functools.partial(_tail_kernel, tiles_per_seq=seq_len // R),
        grid=(T // R,), in_specs=in_specs, out_specs=row(D),
        out_shape=jax.ShapeDtypeStruct((T, D), jnp.float32),
        scratch_shapes=[
            pltpu.VMEM((R + 8, FF_CHUNK), jnp.float32),
            pltpu.VMEM((N_FF_CHUNKS, 8, FF_CHUNK), jnp.float32),
        ],
        compiler_params=pltpu.CompilerParams(
            dimension_semantics=("arbitrary",), vmem_limit_bytes=VMEM_LIMIT),
        name="tail",
    )(x2d, moba2d, ret2d, p2d, *consts)


def _pad_heads(w, pad):
    d = w.shape[0]
    w3 = w.reshape(d, MOBA_HEADS, MOBA_HEAD_DIM)
    return jnp.pad(w3, ((0, 0), (0, 0), (0, pad - MOBA_HEAD_DIM))).reshape(d, MOBA_HEADS * pad)


def kernel(x, p, attn_norm, w_in, ret_norm, w_out, ffn_norm, w_up, w_gate, conv_w, conv_b, w_down,
           ple_norm, w_ple, w_ple_gate, final_norm):
    B, S, D = x.shape
    assert D == D_MODEL and S % ROW_TILE == 0 and w_in.shape[0] == 1
    bf, f32 = jnp.bfloat16, jnp.float32
    MW, RW = MOBA_WIDTH, RET_WIDTH
    w = w_in[0].astype(bf)
    wqT = _pad_heads(w[:, 0:MW], QK_PAD).T
    wk = _pad_heads(w[:, MW:2 * MW], QK_PAD)
    wvT = _pad_heads(w[:, 2 * MW:3 * MW], V_PAD).T
    o = 3 * MW
    wrq = w[:, o:o + RW]
    wrkT = w[:, o + RW:o + 2 * RW].T
    wrv = w[:, o + 2 * RW:o + 3 * RW]
    wrg = w[:, o + 3 * RW:o + 4 * RW]
    row = lambda a: a.reshape(1, -1).astype(f32)

    slopes_l2 = jnp.exp2(-8.0 * jnp.arange(1, MOBA_HEADS + 1, dtype=f32) / MOBA_HEADS) * LOG2E
    in_head = jnp.arange(QK_PAD)
    qone = jnp.tile(((in_head == ALIBI_ROW) | (in_head == ALIBI_ROW + 1)).astype(f32), MOBA_HEADS)[:, None]
    vone = jnp.tile((jnp.arange(V_PAD) == ONES_ROW).astype(f32), MOBA_HEADS)[:, None]
    kterm = slopes_l2[None, :] * jnp.arange(BLK, dtype=f32)[:, None]
    kterm_hi = lax.bitcast_convert_type(
        lax.bitcast_convert_type(kterm, jnp.uint32) & jnp.uint32(0xFFFF0000), f32)
    kterm_lo = kterm - kterm_hi
    kpos = jnp.zeros((BLK, MOBA_HEADS, QK_PAD), f32)
    kpos = kpos.at[:, :, ALIBI_ROW].set(kterm_hi).at[:, :, ALIBI_ROW + 1].set(kterm_lo)
    kpos = kpos.reshape(BLK, MOBA_HEADS * QK_PAD)

    qT, k, vT, kmean, rq, rkT, rv, rg = _in_proj(
        x, row(attn_norm[0]), wqT, qone, wk, kpos, wvT, vone, wrq, wrkT, wrv, wrg)

    moba_out = _moba(slopes_l2, qT, k, vT, kmean)
    ret_out = _retention(rq, rkT, rv, rg, ret_norm[0])

    wo = w_out[0].astype(bf)
    chunked = lambda a: a.reshape(D, N_FF_CHUNKS, FF_CHUNK).transpose(1, 0, 2)
    out = _tail(
        x.reshape(B * S, D), moba_out.reshape(B * S, MW), ret_out.reshape(B * S, RW),
        p[0].reshape(B * S, PLE_DIM),
        wo[:MW], wo[MW:], row(ffn_norm[0]),
        chunked(w_up[0].astype(bf)), chunked(w_gate[0].astype(bf)),
        conv_w[0].reshape(CONV_WIDTH, N_FF_CHUNKS, FF_CHUNK).transpose(1, 0, 2).astype(f32),
        conv_b[0].reshape(N_FF_CHUNKS, 1, FF_CHUNK).astype(f32),
        w_down[0].astype(bf).reshape(N_FF_CHUNKS, FF_CHUNK, D),
        row(ple_norm[0]), w_ple[0].astype(bf), w_ple_gate[0].astype(bf), row(final_norm),
        seq_len=S)
    return out.reshape(B, S, D)
```

```python
import functools
import math

import jax
import jax.numpy as jnp
from jax import lax
from jax.experimental import pallas as pl
from jax.experimental.pallas import tpu as pltpu

D_MODEL = 1024
PLE_DIM = 256
MOBA_HEADS = 8
MOBA_HEAD_DIM = 64
MOBA_WIDTH = MOBA_HEADS * MOBA_HEAD_DIM
MOBA_BLOCK = 256
MOBA_TOPK = 3
RET_HEADS = 4
RET_HEAD_DIM = 128
RET_WIDTH = RET_HEADS * RET_HEAD_DIM
RET_CHUNK = 256
D_FF = 2816
CONV_WIDTH = 3
RMS_EPS = 1e-6
GN_EPS = 1e-5

BLK = 256
LANES = 128
FF_CHUNK = 256
N_FF_CHUNKS = D_FF // FF_CHUNK
ROW_TILE = 512
VMEM_LIMIT = 56 * 1024 * 1024
NEG = -1e30
LOG2E = math.log2(math.e)

QK_PAD = LANES
ALIBI_ROW = MOBA_HEAD_DIM
V_PAD = 80
ONES_ROW = MOBA_HEAD_DIM

_NT = (((1,), (1,)), ((), ()))


def _const_spec(shape):
    nd = len(shape)
    return pl.BlockSpec(shape, lambda *_: (0,) * nd, pipeline_mode=pl.Buffered(1))


def _rms(x, g):
    ms = jnp.mean(x * x, axis=-1, keepdims=True)
    return x * lax.rsqrt(ms + RMS_EPS) * g


def _in_proj_kernel(x_ref, g_ref, wqT_ref, qone_ref, wk_ref, kpos_ref, wvT_ref, vone_ref,
                    wrq_ref, wrkT_ref, wrv_ref, wrg_ref,
                    qT_ref, k_ref, vT_ref, kmean_ref, rq_ref, rkT_ref, rv_ref, rg_ref):
    h = _rms(x_ref[...], g_ref[...]).astype(jnp.bfloat16)
    nblk = ROW_TILE // BLK

    def nt(w_ref):
        return lax.dot_general(w_ref[...], h, _NT, preferred_element_type=jnp.float32)

    def nn(w_ref):
        return jnp.dot(h, w_ref[...], preferred_element_type=jnp.float32)

    qT = (nt(wqT_ref) * (MOBA_HEAD_DIM ** -0.5 * LOG2E) + qone_ref[...]).astype(jnp.bfloat16)
    vT = (nt(wvT_ref) + vone_ref[...]).astype(jnp.bfloat16)
    rkT = (nt(wrkT_ref) * (RET_HEAD_DIM ** -0.5)).astype(jnp.bfloat16)
    for b in range(nblk):
        sl = slice(b * BLK, (b + 1) * BLK)
        qT_ref[b] = qT[:, sl]
        vT_ref[b] = vT[:, sl]
        rkT_ref[b] = rkT[:, sl]
    k = nn(wk_ref)
    for b in range(nblk):
        kb = k[b * BLK:(b + 1) * BLK]
        kmean_ref[b] = jnp.mean(kb, axis=0, keepdims=True)
        k_ref[b * BLK:(b + 1) * BLK, :] = (kb + kpos_ref[...]).astype(jnp.bfloat16)
    rq_ref[...] = nn(wrq_ref).astype(jnp.bfloat16)
    rv_ref[...] = nn(wrv_ref).astype(jnp.bfloat16)
    rg_ref[...] = nn(wrg_ref).astype(jnp.bfloat16)


def _in_proj(x, g, wqT, qone, wk, kpos, wvT, vone, wrq, wrkT, wrv, wrg):
    B, S, D = x.shape
    nb = S // BLK
    nblk = ROW_TILE // BLK
    grid = (B, S // ROW_TILE)
    bf = jnp.bfloat16
    QW, VW = MOBA_HEADS * QK_PAD, MOBA_HEADS * V_PAD
    nat = lambda w: pl.BlockSpec((None, ROW_TILE, w), lambda b, t: (b, t, 0))
    tr = lambda w: pl.BlockSpec((None, nblk, w, BLK), lambda b, t: (b, t, 0, 0))
    out_shape = (
        jax.ShapeDtypeStruct((B, nb, QW, BLK), bf),
        jax.ShapeDtypeStruct((B, S, QW), bf),
        jax.ShapeDtypeStruct((B, nb, VW, BLK), bf),
        jax.ShapeDtypeStruct((B, nb, 1, QW), jnp.float32),
        jax.ShapeDtypeStruct((B, S, RET_WIDTH), bf),
        jax.ShapeDtypeStruct((B, nb, RET_WIDTH, BLK), bf),
        jax.ShapeDtypeStruct((B, S, RET_WIDTH), bf),
        jax.ShapeDtypeStruct((B, S, RET_WIDTH), bf),
    )
    out_specs = (
        tr(QW), nat(QW), tr(VW),
        pl.BlockSpec((None, nblk, 1, QW), lambda b, t: (b, t, 0, 0)),
        nat(RET_WIDTH), tr(RET_WIDTH), nat(RET_WIDTH), nat(RET_WIDTH),
    )
    consts = (g, wqT, qone, wk, kpos, wvT, vone, wrq, wrkT, wrv, wrg)
    in_specs = [pl.BlockSpec((None, ROW_TILE, D), lambda b, t: (b, t, 0))] + [
        _const_spec(a.shape) for a in consts]
    return pl.pallas_call(
        _in_proj_kernel, grid=grid, in_specs=in_specs, out_specs=out_specs, out_shape=out_shape,
        compiler_params=pltpu.CompilerParams(
            dimension_semantics=("parallel", "parallel"), vmem_limit_bytes=VMEM_LIMIT),
        name="in_proj",
    )(x, *consts)


def _moba_kernel(slopes_ref, qT_ref, k_ref, vT_ref, kmean_ref, o_ref,
                 bias_ref, s_ref, p_ref, oT_ref):
    hp = pl.program_id(1)
    i = pl.program_id(2)
    nb = k_ref.shape[0]
    own_row, neg_row = nb + 1, nb
    blk_id = lax.broadcasted_iota(jnp.int32, (nb, BLK), 0)
    kpos = lax.broadcasted_iota(jnp.int32, (BLK, BLK), 0)
    qpos = lax.broadcasted_iota(jnp.int32, (BLK, BLK), 1)
    qcol = lax.broadcasted_iota(jnp.int32, (1, BLK), 1).astype(jnp.float32)
    last_past = jnp.maximum(i - 1, 0)

    def q_of(a):
        return qT_ref[a * QK_PAD:(a + 1) * QK_PAD, :]

    def k_of(a, j):
        return k_ref[j, :, a * QK_PAD:(a + 1) * QK_PAD]

    def v_of(a, j):
        return vT_ref[j, a * V_PAD:(a + 1) * V_PAD, :]

    for a in range(2):
        slope = slopes_ref[hp * 2 + a]
        qTh = q_of(a)
        km = kmean_ref[:, a * QK_PAD:(a + 1) * QK_PAD]
        km_hi = km.astype(jnp.bfloat16)
        km_lo = (km - km_hi.astype(jnp.float32)).astype(jnp.bfloat16)
        gate = (jnp.dot(km_hi, qTh, preferred_element_type=jnp.float32)
                + jnp.dot(km_lo, qTh, preferred_element_type=jnp.float32))
        gate = jnp.where(blk_id < i, gate, -jnp.inf)
        sel = jnp.zeros((nb, BLK), jnp.bool_)
        for _ in range(MOBA_TOPK):
            top = jnp.max(gate, axis=0, keepdims=True)
            is_top = (gate == top) & (top > -jnp.inf)
            first = jnp.min(jnp.where(is_top, blk_id, nb), axis=0, keepdims=True)
            pick = blk_id == first
            sel = sel | pick
            gate = jnp.where(pick, -jnp.inf, gate)
        qterm = -slope * qcol
        blk_dist = ((i - blk_id) * MOBA_BLOCK).astype(jnp.float32)
        bias_ref[a, 0:nb, :] = jnp.where(sel, -slope * blk_dist, NEG) + qterm
        bias_ref[a, neg_row:neg_row + 1, :] = jnp.full((1, BLK), NEG, jnp.float32)
        bias_ref[a, own_row:own_row + 1, :] = qterm
        s0 = jnp.dot(k_of(a, i), qTh, preferred_element_type=jnp.float32)
        s_ref[a, 0, 0] = jnp.where(kpos <= qpos, s0, NEG)
        s_ref[a, 0, 1] = jnp.dot(k_of(a, 0), qTh, preferred_element_type=jnp.float32)
        p_ref[a, 1] = jnp.zeros((2 * BLK, BLK), jnp.bfloat16)

    def tile_block(t):
        return jnp.where(t == 0, i, jnp.clip(t - 1, 0, last_past))

    def tile_row(t):
        return jnp.where(t == 0, own_row, jnp.where(t <= i, t - 1, neg_row))

    def pv_pair(a, t0, par):
        return (jnp.dot(v_of(a, tile_block(t0)), p_ref[a, par, 0:BLK, :],
                        preferred_element_type=jnp.float32)
                + jnp.dot(v_of(a, tile_block(t0 + 1)), p_ref[a, par, BLK:2 * BLK, :],
                          preferred_element_type=jnp.float32))

    def stage(u, par, carry):
        out = []
        for a in range(2):
            m, alpha_prev, acc = carry[a]
            acc = alpha_prev * acc + pv_pair(a, 2 * u - 2, 1 - par)
            c0 = bias_ref[a, pl.ds(tile_row(2 * u), 1), :]
            c1 = bias_ref[a, pl.ds(tile_row(2 * u + 1), 1), :]
            sa = s_ref[a, par, 0]
            sb = s_ref[a, par, 1]
            m_new = jnp.maximum(m, jnp.maximum(jnp.max(sa, axis=0, keepdims=True) + c0,
                                               jnp.max(sb, axis=0, keepdims=True) + c1))
            alpha = jnp.exp2(m - m_new)
            p_ref[a, par, 0:BLK, :] = jnp.exp2(sa - (m_new - c0)).astype(jnp.bfloat16)
            p_ref[a, par, BLK:2 * BLK, :] = jnp.exp2(sb - (m_new - c1)).astype(jnp.bfloat16)
            for w in range(2):
                s_ref[a, 1 - par, w] = jnp.dot(k_of(a, tile_block(2 * u + 2 + w)), q_of(a),
                                               preferred_element_type=jnp.float32)
            out.append((m_new, alpha, acc))
        return tuple(out)

    def body(w, carry):
        return stage(2 * w + 1, 1, stage(2 * w, 0, carry))

    init = tuple((jnp.full((1, BLK), -3e38, jnp.float32), jnp.ones((1, BLK), jnp.float32),
                  jnp.zeros((V_PAD, BLK), jnp.float32)) for _ in range(2))
    n_iter = (i + 4) // 4
    carry = lax.fori_loop(0, n_iter, body, init)
    for a in range(2):
        _, alpha_prev, acc = carry[a]
        acc = alpha_prev * acc + pv_pair(a, 4 * n_iter - 2, 1)
        oT_ref[a * MOBA_HEAD_DIM:(a + 1) * MOBA_HEAD_DIM, :] = (
            acc[0:MOBA_HEAD_DIM] / acc[ONES_ROW:ONES_ROW + 1])
    o_ref[...] = oT_ref[...].T.astype(o_ref.dtype)


def _moba(slopes_l2, qT, k, vT, kmean):
    B, nb, QW, _ = qT.shape
    S = nb * BLK
    k4 = k.reshape(B, nb, BLK, QW)
    kmean3 = kmean.reshape(B, nb, QW)
    n_pairs = MOBA_HEADS // 2
    grid = (B, n_pairs, nb)
    grid_spec = pltpu.PrefetchScalarGridSpec(
        num_scalar_prefetch=1,
        grid=grid,
        in_specs=[
            pl.BlockSpec((None, None, 2 * QK_PAD, BLK), lambda b, h, i, s: (b, i, h, 0)),
            pl.BlockSpec((None, nb, BLK, 2 * QK_PAD), lambda b, h, i, s: (b, 0, 0, h)),
            pl.BlockSpec((None, nb, 2 * V_PAD, BLK), lambda b, h, i, s: (b, 0, h, 0)),
            pl.BlockSpec((None, nb, 2 * QK_PAD), lambda b, h, i, s: (b, 0, h)),
        ],
        out_specs=pl.BlockSpec((None, BLK, 2 * MOBA_HEAD_DIM), lambda b, h, i, s: (b, i, h)),
        scratch_shapes=[
            pltpu.VMEM((2, nb + 8, BLK), jnp.float32),
            pltpu.VMEM((2, 2, 2, BLK, BLK), jnp.float32),
            pltpu.VMEM((2, 2, 2 * BLK, BLK), jnp.bfloat16),
            pltpu.VMEM((2 * MOBA_HEAD_DIM, BLK), jnp.float32),
        ],
    )
    return pl.pallas_call(
        _moba_kernel, grid_spec=grid_spec,
        out_shape=jax.ShapeDtypeStruct((B, S, MOBA_WIDTH), jnp.bfloat16),
        compiler_params=pltpu.CompilerParams(
            dimension_semantics=("parallel", "parallel", "arbitrary"),
            vmem_limit_bytes=VMEM_LIMIT),
        name="moba",
    )(slopes_l2, qT, k4, vT, kmean3)


def _ret_kernel(cd_ref, q_ref, kT_ref, v_ref, g_ref, dintra_ref, qdec_ref, kdec_ref, gn_ref,
                o_ref, state_ref):
    c = pl.program_id(1)

    @pl.when(c == 0)
    def _():
        state_ref[...] = jnp.zeros_like(state_ref)

    for h in range(RET_HEADS):
        cols = slice(h * RET_HEAD_DIM, (h + 1) * RET_HEAD_DIM)
        q = q_ref[:, cols]
        kT = kT_ref[cols, :]
        v = v_ref[:, cols]
        state = state_ref[h]
        s = jnp.dot(q, kT, preferred_element_type=jnp.float32) * dintra_ref[h]
        inner = jnp.dot(s.astype(jnp.bfloat16), v, preferred_element_type=jnp.float32)
        qd = (q.astype(jnp.float32) * qdec_ref[h]).astype(jnp.bfloat16)
        cross = jnp.dot(qd, state.astype(jnp.bfloat16), preferred_element_type=jnp.float32)
        kd = (kT.astype(jnp.float32) * kdec_ref[h]).astype(jnp.bfloat16)
        state_ref[h] = cd_ref[h] * state + jnp.dot(kd, v, preferred_element_type=jnp.float32)
        o = inner + cross
        mu = jnp.mean(o, axis=-1, keepdims=True)
        d = o - mu
        var = jnp.mean(d * d, axis=-1, keepdims=True)
        gate = g_ref[:, cols].astype(jnp.float32)
        y = d * lax.rsqrt(var + GN_EPS) * gn_ref[:, cols] * (gate * jax.nn.sigmoid(gate))
        o_ref[:, cols] = y.astype(o_ref.dtype)


def _retention(rq, rkT, rv, rg, ret_norm):
    B, S, W = rq.shape
    nc = S // BLK
    H = RET_HEADS
    C = RET_CHUNK
    f32 = jnp.float32
    gamma = 1.0 - jnp.exp2(-5.0 - jnp.arange(H, dtype=f32))
    log_g = jnp.log(gamma)
    idx = jnp.arange(C, dtype=f32)
    diff = idx[:, None] - idx[None, :]
    dintra = jnp.where(diff >= 0, jnp.exp(log_g[:, None, None] * jnp.maximum(diff, 0.0)), 0.0)
    qdec = jnp.broadcast_to(jnp.exp(log_g[:, None] * (idx + 1.0))[..., None], (H, C, LANES))
    kdec = jnp.exp(log_g[:, None] * (C - 1.0 - idx))[:, None, :]
    cdec = jnp.exp(log_g * C)
    gn = ret_norm.reshape(1, W).astype(f32)

    nat = pl.BlockSpec((None, C, W), lambda b, c, s: (b, c, 0))
    const = lambda a: pl.BlockSpec(a.shape, lambda b, c, s: (0,) * a.ndim,
                                   pipeline_mode=pl.Buffered(1))
    grid_spec = pltpu.PrefetchScalarGridSpec(
        num_scalar_prefetch=1,
        grid=(B, nc),
        in_specs=[
            nat,
            pl.BlockSpec((None, None, W, C), lambda b, c, s: (b, c, 0, 0)),
            nat, nat,
            const(dintra), const(qdec), const(kdec), const(gn),
        ],
        out_specs=nat,
        scratch_shapes=[pltpu.VMEM((H, RET_HEAD_DIM, RET_HEAD_DIM), f32)],
    )
    return pl.pallas_call(
        _ret_kernel, grid_spec=grid_spec,
        out_shape=jax.ShapeDtypeStruct((B, S, W), jnp.bfloat16),
        compiler_params=pltpu.CompilerParams(
            dimension_semantics=("parallel", "arbitrary"),
            vmem_limit_bytes=VMEM_LIMIT),
        name="retention",
    )(cdec, rq, rkT, rv, rg, dintra, qdec, kdec, gn)


def _gelu_tanh(u):
    c = math.sqrt(2.0 / math.pi)
    return 0.5 * u * (1.0 + jnp.tanh(c * (u + 0.044715 * (u * u * u))))


def _tail_kernel(x_ref, moba_ref, ret_ref, p_ref, wo_m_ref, wo_r_ref, ffn_g_ref, wup_ref, wgate_ref,
                 convw_ref, convb_ref, wdown_ref, ple_g_ref, wple_ref, wpg_ref, fin_g_ref,
                 o_ref, upbuf_ref, carry_ref, *, tiles_per_seq):
    t = pl.program_id(0)
    R = ROW_TILE
    HALO = 8

    @pl.when(t % tiles_per_seq == 0)
    def _():
        carry_ref[...] = jnp.zeros_like(carry_ref)

    x1 = (x_ref[...]
          + jnp.dot(moba_ref[...], wo_m_ref[...], preferred_element_type=jnp.float32)
          + jnp.dot(ret_ref[...], wo_r_ref[...], preferred_element_type=jnp.float32))
    h = _rms(x1, ffn_g_ref[...]).astype(jnp.bfloat16)

    acc = jnp.zeros((R, D_MODEL), jnp.float32)
    for c in range(N_FF_CHUNKS):
        up = jnp.dot(h, wup_ref[c], preferred_element_type=jnp.float32)
        gt = jnp.dot(h, wgate_ref[c], preferred_element_type=jnp.float32)
        upbuf_ref[0:HALO, :] = carry_ref[c]
        upbuf_ref[HALO:HALO + R, :] = up
        carry_ref[c] = up[R - HALO:R, :]
        w = convw_ref[c]
        u = (convb_ref[c]
             + w[0:1, :] * upbuf_ref[pl.ds(HALO - 2, R), :]
             + w[1:2, :] * upbuf_ref[pl.ds(HALO - 1, R), :]
             + w[2:3, :] * up)
        y = (_gelu_tanh(u) * gt).astype(jnp.bfloat16)
        acc = acc + jnp.dot(y, wdown_ref[c], preferred_element_type=jnp.float32)
    x2 = x1 + acc

    hn = _rms(x2, ple_g_ref[...]).astype(jnp.bfloat16)
    g = jax.nn.sigmoid(jnp.dot(hn, wpg_ref[...], preferred_element_type=jnp.float32))
    pe = jnp.dot(p_ref[...].astype(jnp.bfloat16), wple_ref[...], preferred_element_type=jnp.float32)
    x3 = x2 + pe * g
    o_ref[...] = _rms(x3, fin_g_ref[...])


def _tail(x2d, moba2d, ret2d, p2d, wo_m, wo_r, ffn_g, wup, wgate, convw, convb, wdown, ple_g, wple,
          wpg, fin_g, seq_len):
    T, D = x2d.shape
    R = ROW_TILE
    row = lambda w: pl.BlockSpec((R, w), lambda t: (t, 0))
    consts = (wo_m, wo_r, ffn_g, wup, wgate, convw, convb, wdown, ple_g, wple, wpg, fin_g)
    in_specs = [row(D), row(MOBA_WIDTH), row(RET_WIDTH), row(PLE_DIM)] + [
        _const_spec(a.shape) for a in consts]
    return pl.pallas_call(
        functools.partial(_tail_kernel, tiles_per_seq=seq_len // R),
        grid=(T // R,), in_specs=in_specs, out_specs=row(D),
        out_shape=jax.ShapeDtypeStruct((T, D), jnp.float32),
        scratch_shapes=[
            pltpu.VMEM((R + 8, FF_CHUNK), jnp.float32),
            pltpu.VMEM((N_FF_CHUNKS, 8, FF_CHUNK), jnp.float32),
        ],
        compiler_params=pltpu.CompilerParams(
            dimension_semantics=("arbitrary",), vmem_limit_bytes=VMEM_LIMIT),
        name="tail",
    )(x2d, moba2d, ret2d, p2d, *consts)


def _pad_heads(w, pad):
    d = w.shape[0]
    w3 = w.reshape(d, MOBA_HEADS, MOBA_HEAD_DIM)
    return jnp.pad(w3, ((0, 0), (0, 0), (0, pad - MOBA_HEAD_DIM))).reshape(d, MOBA_HEADS * pad)


def kernel(x, p, attn_norm, w_in, ret_norm, w_out, ffn_norm, w_up, w_gate, conv_w, conv_b, w_down,
           ple_norm, w_ple, w_ple_gate, final_norm):
    B, S, D = x.shape
    assert D == D_MODEL and S % ROW_TILE == 0 and w_in.shape[0] == 1
    bf, f32 = jnp.bfloat16, jnp.float32
    MW, RW = MOBA_WIDTH, RET_WIDTH
    w = w_in[0].astype(bf)
    wqT = _pad_heads(w[:, 0:MW], QK_PAD).T
    wk = _pad_heads(w[:, MW:2 * MW], QK_PAD)
    wvT = _pad_heads(w[:, 2 * MW:3 * MW], V_PAD).T
    o = 3 * MW
    wrq = w[:, o:o + RW]
    wrkT = w[:, o + RW:o + 2 * RW].T
    wrv = w[:, o + 2 * RW:o + 3 * RW]
    wrg = w[:, o + 3 * RW:o + 4 * RW]
    row = lambda a: a.reshape(1, -1).astype(f32)

    slopes_l2 = jnp.exp2(-8.0 * jnp.arange(1, MOBA_HEADS + 1, dtype=f32) / MOBA_HEADS) * LOG2E
    in_head = jnp.arange(QK_PAD)
    qone = jnp.tile(((in_head == ALIBI_ROW) | (in_head == ALIBI_ROW + 1)).astype(f32), MOBA_HEADS)[:, None]
    vone = jnp.tile((jnp.arange(V_PAD) == ONES_ROW).astype(f32), MOBA_HEADS)[:, None]
    kterm = slopes_l2[None, :] * jnp.arange(BLK, dtype=f32)[:, None]
    kterm_hi = lax.bitcast_convert_type(
        lax.bitcast_convert_type(kterm, jnp.uint32) & jnp.uint32(0xFFFF0000), f32)
    kterm_lo = kterm - kterm_hi
    kpos = jnp.zeros((BLK, MOBA_HEADS, QK_PAD), f32)
    kpos = kpos.at[:, :, ALIBI_ROW].set(kterm_hi).at[:, :, ALIBI_ROW + 1].set(kterm_lo)
    kpos = kpos.reshape(BLK, MOBA_HEADS * QK_PAD)

    qT, k, vT, kmean, rq, rkT, rv, rg = _in_proj(
        x, row(attn_norm[0]), wqT, qone, wk, kpos, wvT, vone, wrq, wrkT, wrv, wrg)

    moba_out = _moba(slopes_l2, qT, k, vT, kmean)
    ret_out = _retention(rq, rkT, rv, rg, ret_norm[0])

    wo = w_out[0].astype(bf)
    chunked = lambda a: a.reshape(D, N_FF_CHUNKS, FF_CHUNK).transpose(1, 0, 2)
    out = _tail(
        x.reshape(B * S, D), moba_out.reshape(B * S, MW), ret_out.reshape(B * S, RW),
        p[0].reshape(B * S, PLE_DIM),
        wo[:MW], wo[MW:], row(ffn_norm[0]),
        chunked(w_up[0].astype(bf)), chunked(w_gate[0].astype(bf)),
        conv_w[0].reshape(CONV_WIDTH, N_FF_CHUNKS, FF_CHUNK).transpose(1, 0, 2).astype(f32),
        conv_b[0].reshape(N_FF_CHUNKS, 1, FF_CHUNK).astype(f32),
        w_down[0].astype(bf).reshape(N_FF_CHUNKS, FF_CHUNK, D),
        row(ple_norm[0]), w_ple[0].astype(bf), w_ple_gate[0].astype(bf), row(final_norm),
        seq_len=S)
    return out.reshape(B, S, D)
```

```python
import functools
import math

import jax
import jax.numpy as jnp
from jax import lax
from jax.experimental import pallas as pl
from jax.experimental.pallas import tpu as pltpu

D_MODEL = 1024
PLE_DIM = 256
MOBA_HEADS = 8
MOBA_HEAD_DIM = 64
MOBA_WIDTH = MOBA_HEADS * MOBA_HEAD_DIM
MOBA_BLOCK = 256
MOBA_TOPK = 3
RET_HEADS = 4
RET_HEAD_DIM = 128
RET_WIDTH = RET_HEADS * RET_HEAD_DIM
RET_CHUNK = 256
D_FF = 2816
CONV_WIDTH = 3
RMS_EPS = 1e-6
GN_EPS = 1e-5

BLK = 256
LANES = 128
FF_CHUNK = 256
N_FF_CHUNKS = D_FF // FF_CHUNK
ROW_TILE = 512
VMEM_LIMIT = 56 * 1024 * 1024
NEG = -1e30
LOG2E = math.log2(math.e)

QK_PAD = LANES
ALIBI_ROW = MOBA_HEAD_DIM
V_PAD = 80
ONES_ROW = MOBA_HEAD_DIM
GROUP = 4

_NT = (((1,), (1,)), ((), ()))


def _const_spec(shape):
    nd = len(shape)
    return pl.BlockSpec(shape, lambda *_: (0,) * nd, pipeline_mode=pl.Buffered(1))


def _rms(x, g):
    ms = jnp.mean(x * x, axis=-1, keepdims=True)
    return x * lax.rsqrt(ms + RMS_EPS) * g


def _in_proj_kernel(x_ref, g_ref, wqT_ref, qone_ref, wk_ref, kpos_ref, wvT_ref, vone_ref,
                    wrq_ref, wrkT_ref, wrv_ref, wrg_ref,
                    qT_ref, k_ref, vT_ref, kmean_ref, rq_ref, rkT_ref, rv_ref, rg_ref):
    h = _rms(x_ref[...], g_ref[...]).astype(jnp.bfloat16)
    nblk = ROW_TILE // BLK

    def nt(w_ref):
        return lax.dot_general(w_ref[...], h, _NT, preferred_element_type=jnp.float32)

    def nn(w_ref):
        return jnp.dot(h, w_ref[...], preferred_element_type=jnp.float32)

    qT = (nt(wqT_ref) * (MOBA_HEAD_DIM ** -0.5 * LOG2E) + qone_ref[...]).astype(jnp.bfloat16)
    vT = (nt(wvT_ref) + vone_ref[...]).astype(jnp.bfloat16)
    rkT = (nt(wrkT_ref) * (RET_HEAD_DIM ** -0.5)).astype(jnp.bfloat16)
    for b in range(nblk):
        sl = slice(b * BLK, (b + 1) * BLK)
        qT_ref[b] = qT[:, sl]
        vT_ref[b] = vT[:, sl]
        rkT_ref[b] = rkT[:, sl]
    k = nn(wk_ref)
    for b in range(nblk):
        kb = k[b * BLK:(b + 1) * BLK]
        kmean_ref[b] = jnp.mean(kb, axis=0, keepdims=True)
        k_ref[b * BLK:(b + 1) * BLK, :] = (kb + kpos_ref[...]).astype(jnp.bfloat16)
    rq_ref[...] = nn(wrq_ref).astype(jnp.bfloat16)
    rv_ref[...] = nn(wrv_ref).astype(jnp.bfloat16)
    rg_ref[...] = nn(wrg_ref).astype(jnp.bfloat16)


def _in_proj(x, g, wqT, qone, wk, kpos, wvT, vone, wrq, wrkT, wrv, wrg):
    B, S, D = x.shape
    nb = S // BLK
    nblk = ROW_TILE // BLK
    grid = (B, S // ROW_TILE)
    bf = jnp.bfloat16
    QW, VW = MOBA_HEADS * QK_PAD, MOBA_HEADS * V_PAD
    nat = lambda w: pl.BlockSpec((None, ROW_TILE, w), lambda b, t: (b, t, 0))
    tr = lambda w: pl.BlockSpec((None, nblk, w, BLK), lambda b, t: (b, t, 0, 0))
    out_shape = (
        jax.ShapeDtypeStruct((B, nb, QW, BLK), bf),
        jax.ShapeDtypeStruct((B, S, QW), bf),
        jax.ShapeDtypeStruct((B, nb, VW, BLK), bf),
        jax.ShapeDtypeStruct((B, nb, 1, QW), jnp.float32),
        jax.ShapeDtypeStruct((B, S, RET_WIDTH), bf),
        jax.ShapeDtypeStruct((B, nb, RET_WIDTH, BLK), bf),
        jax.ShapeDtypeStruct((B, S, RET_WIDTH), bf),
        jax.ShapeDtypeStruct((B, S, RET_WIDTH), bf),
    )
    out_specs = (
        tr(QW), nat(QW), tr(VW),
        pl.BlockSpec((None, nblk, 1, QW), lambda b, t: (b, t, 0, 0)),
        nat(RET_WIDTH), tr(RET_WIDTH), nat(RET_WIDTH), nat(RET_WIDTH),
    )
    consts = (g, wqT, qone, wk, kpos, wvT, vone, wrq, wrkT, wrv, wrg)
    in_specs = [pl.BlockSpec((None, ROW_TILE, D), lambda b, t: (b, t, 0))] + [
        _const_spec(a.shape) for a in consts]
    return pl.pallas_call(
        _in_proj_kernel, grid=grid, in_specs=in_specs, out_specs=out_specs, out_shape=out_shape,
        compiler_params=pltpu.CompilerParams(
            dimension_semantics=("parallel", "parallel"), vmem_limit_bytes=VMEM_LIMIT),
        name="in_proj",
    )(x, *consts)


def _moba_kernel(slopes_ref, qT_ref, k_ref, vT_ref, kmean_ref, o_ref,
                 bias_ref, s_ref, smax_ref, p_ref, acc_ref, oT_ref):
    hp = pl.program_id(1)
    i = pl.program_id(2)
    nb = k_ref.shape[0]
    own_row, neg_row = nb + 1, nb
    blk_id = lax.broadcasted_iota(jnp.int32, (nb, BLK), 0)
    kpos = lax.broadcasted_iota(jnp.int32, (BLK, BLK), 0)
    qpos = lax.broadcasted_iota(jnp.int32, (BLK, BLK), 1)
    qcol = lax.broadcasted_iota(jnp.int32, (1, BLK), 1).astype(jnp.float32)
    last_past = jnp.maximum(i - 1, 0)

    def q_of(a):
        return qT_ref[a * QK_PAD:(a + 1) * QK_PAD, :]

    def k_of(a, j):
        return k_ref[j, :, a * QK_PAD:(a + 1) * QK_PAD]

    def v_of(a, j):
        return vT_ref[j, a * V_PAD:(a + 1) * V_PAD, :]

    def tile_block(t):
        return jnp.where(t == 0, i, jnp.clip(t - 1, 0, last_past))

    def tile_row(t):
        return jnp.where(t == 0, own_row, jnp.where(t <= i, t - 1, neg_row))

    def put_scores(a, par, w, s):
        s_ref[a, par, w] = s
        smax_ref[a, par, w:w + 1, :] = jnp.max(s, axis=0, keepdims=True)

    for a in range(2):
        slope = slopes_ref[hp * 2 + a]
        qTh = q_of(a)
        km = kmean_ref[:, a * QK_PAD:(a + 1) * QK_PAD]
        km_hi = km.astype(jnp.bfloat16)
        km_lo = (km - km_hi.astype(jnp.float32)).astype(jnp.bfloat16)
        gate = (jnp.dot(km_hi, qTh, preferred_element_type=jnp.float32)
                + jnp.dot(km_lo, qTh, preferred_element_type=jnp.float32))
        gate = jnp.where(blk_id < i, gate, -jnp.inf)
        sel = jnp.zeros((nb, BLK), jnp.bool_)
        for _ in range(MOBA_TOPK):
            top = jnp.max(gate, axis=0, keepdims=True)
            is_top = (gate == top) & (top > -jnp.inf)
            first = jnp.min(jnp.where(is_top, blk_id, nb), axis=0, keepdims=True)
            pick = blk_id == first
            sel = sel | pick
            gate = jnp.where(pick, -jnp.inf, gate)
        qterm = -slope * qcol
        blk_dist = ((i - blk_id) * MOBA_BLOCK).astype(jnp.float32)
        bias_ref[a, 0:nb, :] = jnp.where(sel, -slope * blk_dist, NEG) + qterm
        bias_ref[a, neg_row:neg_row + 1, :] = jnp.full((1, BLK), NEG, jnp.float32)
        bias_ref[a, own_row:own_row + 1, :] = qterm
        for e in range(GROUP):
            s0 = jnp.dot(k_of(a, tile_block(e)), qTh, preferred_element_type=jnp.float32)
            put_scores(a, 0, e, jnp.where(kpos <= qpos, s0, NEG) if e == 0 else s0)
        p_ref[a, 1] = jnp.zeros((GROUP * BLK, BLK), jnp.bfloat16)
        acc_ref[a] = jnp.zeros((V_PAD, BLK), jnp.float32)

    def pv_group(a, g, par):
        vT = jnp.concatenate([v_of(a, tile_block(GROUP * g + e)) for e in range(GROUP)], axis=1)
        return jnp.dot(vT, p_ref[a, par], preferred_element_type=jnp.float32)

    def trip(g, par, carry):
        out = []
        for a in range(2):
            m, alpha_prev = carry[a]
            acc_ref[a] = alpha_prev * acc_ref[a] + pv_group(a, g - 1, 1 - par)
            cs = [bias_ref[a, pl.ds(tile_row(GROUP * g + e), 1), :] for e in range(GROUP)]
            m_new = m
            for e in range(GROUP):
                m_new = jnp.maximum(m_new, smax_ref[a, par, e:e + 1, :] + cs[e])
            alpha = jnp.exp2(m - m_new)
            for e in range(GROUP):
                p_ref[a, par, e * BLK:(e + 1) * BLK, :] = (
                    jnp.exp2(s_ref[a, par, e] - (m_new - cs[e])).astype(jnp.bfloat16))
            for e in range(GROUP):
                put_scores(a, 1 - par, e,
                           jnp.dot(k_of(a, tile_block(GROUP * (g + 1) + e)), q_of(a),
                                   preferred_element_type=jnp.float32))
            out.append((m_new, alpha))
        return tuple(out)

    def body(g, carry):
        return lax.cond(g % 2 == 0, lambda c: trip(g, 0, c), lambda c: trip(g, 1, c), carry)

    init = tuple((jnp.full((1, BLK), -3e38, jnp.float32), jnp.ones((1, BLK), jnp.float32))
                 for _ in range(2))
    n_groups = (i + GROUP) // GROUP
    carry = lax.fori_loop(0, n_groups, body, init)
    last = n_groups - 1

    def finish(par):
        for a in range(2):
            acc = carry[a][1] * acc_ref[a] + pv_group(a, last, par)
            oT_ref[a * MOBA_HEAD_DIM:(a + 1) * MOBA_HEAD_DIM, :] = (
                acc[0:MOBA_HEAD_DIM] / acc[ONES_ROW:ONES_ROW + 1])

    lax.cond(last % 2 == 0, lambda: finish(0), lambda: finish(1))
    o_ref[...] = oT_ref[...].T.astype(o_ref.dtype)


def _moba(slopes_l2, qT, k, vT, kmean):
    B, nb, QW, _ = qT.shape
    S = nb * BLK
    k4 = k.reshape(B, nb, BLK, QW)
    kmean3 = kmean.reshape(B, nb, QW)
    n_pairs = MOBA_HEADS // 2
    grid = (B, n_pairs, nb)
    grid_spec = pltpu.PrefetchScalarGridSpec(
        num_scalar_prefetch=1,
        grid=grid,
        in_specs=[
            pl.BlockSpec((None, None, 2 * QK_PAD, BLK), lambda b, h, i, s: (b, i, h, 0)),
            pl.BlockSpec((None, nb, BLK, 2 * QK_PAD), lambda b, h, i, s: (b, 0, 0, h)),
            pl.BlockSpec((None, nb, 2 * V_PAD, BLK), lambda b, h, i, s: (b, 0, h, 0)),
            pl.BlockSpec((None, nb, 2 * QK_PAD), lambda b, h, i, s: (b, 0, h)),
        ],
        out_specs=pl.BlockSpec((None, BLK, 2 * MOBA_HEAD_DIM), lambda b, h, i, s: (b, i, h)),
        scratch_shapes=[
            pltpu.VMEM((2, nb + 8, BLK), jnp.float32),
            pltpu.VMEM((2, 2, GROUP, BLK, BLK), jnp.float32),
            pltpu.VMEM((2, 2, 8, BLK), jnp.float32),
            pltpu.VMEM((2, 2, GROUP * BLK, BLK), jnp.bfloat16),
            pltpu.VMEM((2, V_PAD, BLK), jnp.float32),
            pltpu.VMEM((2 * MOBA_HEAD_DIM, BLK), jnp.float32),
        ],
    )
    return pl.pallas_call(
        _moba_kernel, grid_spec=grid_spec,
        out_shape=jax.ShapeDtypeStruct((B, S, MOBA_WIDTH), jnp.bfloat16),
        compiler_params=pltpu.CompilerParams(
            dimension_semantics=("parallel", "parallel", "arbitrary"),
            vmem_limit_bytes=VMEM_LIMIT),
        name="moba",
    )(slopes_l2, qT, k4, vT, kmean3)


def _ret_kernel(cd_ref, q_ref, kT_ref, v_ref, g_ref, dintra_ref, qdec_ref, kdec_ref, gn_ref,
                o_ref, state_ref):
    c = pl.program_id(1)

    @pl.when(c == 0)
    def _():
        state_ref[...] = jnp.zeros_like(state_ref)

    for h in range(RET_HEADS):
        cols = slice(h * RET_HEAD_DIM, (h + 1) * RET_HEAD_DIM)
        q = q_ref[:, cols]
        kT = kT_ref[cols, :]
        v = v_ref[:, cols]
        state = state_ref[h]
        s = jnp.dot(q, kT, preferred_element_type=jnp.float32) * dintra_ref[h]
        inner = jnp.dot(s.astype(jnp.bfloat16), v, preferred_element_type=jnp.float32)
        qd = (q.astype(jnp.float32) * qdec_ref[h]).astype(jnp.bfloat16)
        cross = jnp.dot(qd, state.astype(jnp.bfloat16), preferred_element_type=jnp.float32)
        kd = (kT.astype(jnp.float32) * kdec_ref[h]).astype(jnp.bfloat16)
        state_ref[h] = cd_ref[h] * state + jnp.dot(kd, v, preferred_element_type=jnp.float32)
        o = inner + cross
        mu = jnp.mean(o, axis=-1, keepdims=True)
        d = o - mu
        var = jnp.mean(d * d, axis=-1, keepdims=True)
        gate = g_ref[:, cols].astype(jnp.float32)
        y = d * lax.rsqrt(var + GN_EPS) * gn_ref[:, cols] * (gate * jax.nn.sigmoid(gate))
        o_ref[:, cols] = y.astype(o_ref.dtype)


def _retention(rq, rkT, rv, rg, ret_norm):
    B, S, W = rq.shape
    nc = S // BLK
    H = RET_HEADS
    C = RET_CHUNK
    f32 = jnp.float32
    gamma = 1.0 - jnp.exp2(-5.0 - jnp.arange(H, dtype=f32))
    log_g = jnp.log(gamma)
    idx = jnp.arange(C, dtype=f32)
    diff = idx[:, None] - idx[None, :]
    dintra = jnp.where(diff >= 0, jnp.exp(log_g[:, None, None] * jnp.maximum(diff, 0.0)), 0.0)
    qdec = jnp.broadcast_to(jnp.exp(log_g[:, None] * (idx + 1.0))[..., None], (H, C, LANES))
    kdec = jnp.exp(log_g[:, None] * (C - 1.0 - idx))[:, None, :]
    cdec = jnp.exp(log_g * C)
    gn = ret_norm.reshape(1, W).astype(f32)

    nat = pl.BlockSpec((None, C, W), lambda b, c, s: (b, c, 0))
    const = lambda a: pl.BlockSpec(a.shape, lambda b, c, s: (0,) * a.ndim,
                                   pipeline_mode=pl.Buffered(1))
    grid_spec = pltpu.PrefetchScalarGridSpec(
        num_scalar_prefetch=1,
        grid=(B, nc),
        in_specs=[
            nat,
            pl.BlockSpec((None, None, W, C), lambda b, c, s: (b, c, 0, 0)),
            nat, nat,
            const(dintra), const(qdec), const(kdec), const(gn),
        ],
        out_specs=nat,
        scratch_shapes=[pltpu.VMEM((H, RET_HEAD_DIM, RET_HEAD_DIM), f32)],
    )
    return pl.pallas_call(
        _ret_kernel, grid_spec=grid_spec,
        out_shape=jax.ShapeDtypeStruct((B, S, W), jnp.bfloat16),
        compiler_params=pltpu.CompilerParams(
            dimension_semantics=("parallel", "arbitrary"),
            vmem_limit_bytes=VMEM_LIMIT),
        name="retention",
    )(cdec, rq, rkT, rv, rg, dintra, qdec, kdec, gn)


def _gelu_tanh(u):
    k = -2.0 * math.sqrt(2.0 / math.pi) * LOG2E
    return u / (1.0 + jnp.exp2(u * (k + (k * 0.044715) * (u * u))))


def _shift_rows(cur, prev_tail, n):
    rolled = pltpu.roll(cur, n, axis=0)
    prev_rolled = pltpu.roll(prev_tail, n, axis=0)
    row = lax.broadcasted_iota(jnp.int32, prev_tail.shape, 0)
    head = jnp.where(row < n, prev_rolled, rolled[0:8])
    return jnp.concatenate([head, rolled[8:]], axis=0)


def _tail_kernel(x_ref, moba_ref, ret_ref, p_ref, wo_ref, ffn_g_ref, wup_ref, wgate_ref,
                 convw_ref, convb_ref, wdown_ref, ple_g_ref, wple_ref, wpg_ref, fin_g_ref,
                 o_ref, y_ref, carry_ref, *, tiles_per_seq):
    t = pl.program_id(0)
    R = ROW_TILE

    @pl.when(t % tiles_per_seq == 0)
    def _():
        carry_ref[...] = jnp.zeros_like(carry_ref)

    mix = jnp.concatenate([moba_ref[...], ret_ref[...]], axis=1)
    x1 = x_ref[...] + jnp.dot(mix, wo_ref[...], preferred_element_type=jnp.float32)
    h = _rms(x1, ffn_g_ref[...]).astype(jnp.bfloat16)

    for c in range(N_FF_CHUNKS):
        up = jnp.dot(h, wup_ref[c], preferred_element_type=jnp.float32)
        gt = jnp.dot(h, wgate_ref[c], preferred_element_type=jnp.float32)
        prev_tail = carry_ref[c]
        carry_ref[c] = up[R - 8:R, :]
        w = convw_ref[c]
        u = (convb_ref[c]
             + w[0:1, :] * _shift_rows(up, prev_tail, 2)
             + w[1:2, :] * _shift_rows(up, prev_tail, 1)
             + w[2:3, :] * up)
        y_ref[:, c * FF_CHUNK:(c + 1) * FF_CHUNK] = (_gelu_tanh(u) * gt).astype(jnp.bfloat16)
    x2 = x1 + jnp.dot(y_ref[...], wdown_ref[...], preferred_element_type=jnp.float32)

    hn = _rms(x2, ple_g_ref[...]).astype(jnp.bfloat16)
    g = jax.nn.sigmoid(jnp.dot(hn, wpg_ref[...], preferred_element_type=jnp.float32))
    pe = jnp.dot(p_ref[...].astype(jnp.bfloat16), wple_ref[...], preferred_element_type=jnp.float32)
    x3 = x2 + pe * g
    o_ref[...] = _rms(x3, fin_g_ref[...])


def _tail(x2d, moba2d, ret2d, p2d, wo, ffn_g, wup, wgate, convw, convb, wdown, ple_g, wple,
          wpg, fin_g, seq_len):
    T, D = x2d.shape
    R = ROW_TILE
    row = lambda w: pl.BlockSpec((R, w), lambda t: (t, 0))
    consts = (wo, ffn_g, wup, wgate, convw, convb, wdown, ple_g, wple, wpg, fin_g)
    in_specs = [row(D), row(MOBA_WIDTH), row(RET_WIDTH), row(PLE_DIM)] + [
        _const_spec(a.shape) for a in consts]
    return pl.pallas_call(
        functools.partial(_tail_kernel, tiles_per_seq=seq_len // R),
        grid=(T // R,), in_specs=in_specs, out_specs=row(D),
        out_shape=jax.ShapeDtypeStruct((T, D), jnp.float32),
        scratch_shapes=[
            pltpu.VMEM((R, D_FF), jnp.bfloat16),
            pltpu.VMEM((N_FF_CHUNKS, 8, FF_CHUNK), jnp.float32),
        ],
        compiler_params=pltpu.CompilerParams(
            dimension_semantics=("arbitrary",), vmem_limit_bytes=VMEM_LIMIT),
        name="tail",
    )(x2d, moba2d, ret2d, p2d, *consts)


def _pad_heads(w, pad):
    d = w.shape[0]
    w3 = w.reshape(d, MOBA_HEADS, MOBA_HEAD_DIM)
    return jnp.pad(w3, ((0, 0), (0, 0), (0, pad - MOBA_HEAD_DIM))).reshape(d, MOBA_HEADS * pad)


def kernel(x, p, attn_norm, w_in, ret_norm, w_out, ffn_norm, w_up, w_gate, conv_w, conv_b, w_down,
           ple_norm, w_ple, w_ple_gate, final_norm):
    B, S, D = x.shape
    assert D == D_MODEL and S % ROW_TILE == 0 and w_in.shape[0] == 1
    bf, f32 = jnp.bfloat16, jnp.float32
    MW, RW = MOBA_WIDTH, RET_WIDTH
    w = w_in[0].astype(bf)
    wqT = _pad_heads(w[:, 0:MW], QK_PAD).T
    wk = _pad_heads(w[:, MW:2 * MW], QK_PAD)
    wvT = _pad_heads(w[:, 2 * MW:3 * MW], V_PAD).T
    o = 3 * MW
    wrq = w[:, o:o + RW]
    wrkT = w[:, o + RW:o + 2 * RW].T
    wrv = w[:, o + 2 * RW:o + 3 * RW]
    wrg = w[:, o + 3 * RW:o + 4 * RW]
    row = lambda a: a.reshape(1, -1).astype(f32)

    slopes_l2 = jnp.exp2(-8.0 * jnp.arange(1, MOBA_HEADS + 1, dtype=f32) / MOBA_HEADS) * LOG2E
    in_head = jnp.arange(QK_PAD)
    qone = jnp.tile(((in_head == ALIBI_ROW) | (in_head == ALIBI_ROW + 1)).astype(f32), MOBA_HEADS)[:, None]
    vone = jnp.tile((jnp.arange(V_PAD) == ONES_ROW).astype(f32), MOBA_HEADS)[:, None]
    kterm = slopes_l2[None, :] * jnp.arange(BLK, dtype=f32)[:, None]
    kterm_hi = lax.bitcast_convert_type(
        lax.bitcast_convert_type(kterm, jnp.uint32) & jnp.uint32(0xFFFF0000), f32)
    kterm_lo = kterm - kterm_hi
    kpos = jnp.zeros((BLK, MOBA_HEADS, QK_PAD), f32)
    kpos = kpos.at[:, :, ALIBI_ROW].set(kterm_hi).at[:, :, ALIBI_ROW + 1].set(kterm_lo)
    kpos = kpos.reshape(BLK, MOBA_HEADS * QK_PAD)

    qT, k, vT, kmean, rq, rkT, rv, rg = _in_proj(
        x, row(attn_norm[0]), wqT, qone, wk, kpos, wvT, vone, wrq, wrkT, wrv, wrg)

    moba_out = _moba(slopes_l2, qT, k, vT, kmean)
    ret_out = _retention(rq, rkT, rv, rg, ret_norm[0])

    wo = w_out[0].astype(bf)
    chunked = lambda a: a.reshape(D, N_FF_CHUNKS, FF_CHUNK).transpose(1, 0, 2)
    out = _tail(
        x.reshape(B * S, D), moba_out.reshape(B * S, MW), ret_out.reshape(B * S, RW),
        p[0].reshape(B * S, PLE_DIM),
        wo, row(ffn_norm[0]),
        chunked(w_up[0].astype(bf)), chunked(w_gate[0].astype(bf)),
        conv_w[0].reshape(CONV_WIDTH, N_FF_CHUNKS, FF_CHUNK).transpose(1, 0, 2).astype(f32),
        conv_b[0].reshape(N_FF_CHUNKS, 1, FF_CHUNK).astype(f32),
        w_down[0].astype(bf),
        row(ple_norm[0]), w_ple[0].astype(bf), w_ple_gate[0].astype(bf), row(final_norm),
        seq_len=S)
    return out.reshape(B, S, D)
```

```python
import functools
import math

import jax
import jax.numpy as jnp
from jax import lax
from jax.experimental import pallas as pl
from jax.experimental.pallas import tpu as pltpu

D_MODEL = 1024
PLE_DIM = 256
MOBA_HEADS = 8
MOBA_HEAD_DIM = 64
MOBA_WIDTH = MOBA_HEADS * MOBA_HEAD_DIM
MOBA_BLOCK = 256
MOBA_TOPK = 3
RET_HEADS = 4
RET_HEAD_DIM = 128
RET_WIDTH = RET_HEADS * RET_HEAD_DIM
RET_CHUNK = 256
D_FF = 2816
CONV_WIDTH = 3
RMS_EPS = 1e-6
GN_EPS = 1e-5

BLK = 256
LANES = 128
FF_CHUNK = 256
N_FF_CHUNKS = D_FF // FF_CHUNK
ROW_TILE = 512
VMEM_LIMIT = 56 * 1024 * 1024
NEG = -1e30
LOG2E = math.log2(math.e)

QK_PAD = LANES
ALIBI_ROW = MOBA_HEAD_DIM
V_PAD = 80
ONES_ROW = MOBA_HEAD_DIM
GROUP = 16
GROUP_OWN = 4

_NT = (((1,), (1,)), ((), ()))


def _const_spec(shape):
    nd = len(shape)
    return pl.BlockSpec(shape, lambda *_: (0,) * nd, pipeline_mode=pl.Buffered(1))


def _rms(x, g):
    ms = jnp.mean(x * x, axis=-1, keepdims=True)
    return x * lax.rsqrt(ms + RMS_EPS) * g


def _in_proj_kernel(x_ref, g_ref, wqT_ref, qone_ref, wk_ref, kpos_ref, wvT_ref, vone_ref,
                    wrq_ref, wrkT_ref, wrv_ref, wrg_ref,
                    qT_ref, k_ref, vT_ref, kmean_ref, rq_ref, rkT_ref, rv_ref, rg_ref):
    h = _rms(x_ref[...], g_ref[...]).astype(jnp.bfloat16)
    nblk = ROW_TILE // BLK

    def nt(w_ref):
        return lax.dot_general(w_ref[...], h, _NT, preferred_element_type=jnp.float32)

    def nn(w_ref):
        return jnp.dot(h, w_ref[...], preferred_element_type=jnp.float32)

    qT = (nt(wqT_ref) * (MOBA_HEAD_DIM ** -0.5 * LOG2E) + qone_ref[...]).astype(jnp.bfloat16)
    vT = (nt(wvT_ref) + vone_ref[...]).astype(jnp.bfloat16)
    rkT = (nt(wrkT_ref) * (RET_HEAD_DIM ** -0.5)).astype(jnp.bfloat16)
    for b in range(nblk):
        sl = slice(b * BLK, (b + 1) * BLK)
        qT_ref[b] = qT[:, sl]
        vT_ref[b] = vT[:, sl]
        rkT_ref[b] = rkT[:, sl]
    k = nn(wk_ref)
    for b in range(nblk):
        kb = k[b * BLK:(b + 1) * BLK]
        kmean_ref[b] = jnp.mean(kb, axis=0, keepdims=True)
        k_ref[b * BLK:(b + 1) * BLK, :] = (kb + kpos_ref[...]).astype(jnp.bfloat16)
    rq_ref[...] = nn(wrq_ref).astype(jnp.bfloat16)
    rv_ref[...] = nn(wrv_ref).astype(jnp.bfloat16)
    rg_ref[...] = nn(wrg_ref).astype(jnp.bfloat16)


def _in_proj(x, g, wqT, qone, wk, kpos, wvT, vone, wrq, wrkT, wrv, wrg):
    B, S, D = x.shape
    nb = S // BLK
    nblk = ROW_TILE // BLK
    grid = (B, S // ROW_TILE)
    bf = jnp.bfloat16
    QW, VW = MOBA_HEADS * QK_PAD, MOBA_HEADS * V_PAD
    nat = lambda w: pl.BlockSpec((None, ROW_TILE, w), lambda b, t: (b, t, 0))
    tr = lambda w: pl.BlockSpec((None, nblk, w, BLK), lambda b, t: (b, t, 0, 0))
    out_shape = (
        jax.ShapeDtypeStruct((B, nb, QW, BLK), bf),
        jax.ShapeDtypeStruct((B, S, QW), bf),
        jax.ShapeDtypeStruct((B, nb, VW, BLK), bf),
        jax.ShapeDtypeStruct((B, nb, 1, QW), jnp.float32),
        jax.ShapeDtypeStruct((B, S, RET_WIDTH), bf),
        jax.ShapeDtypeStruct((B, nb, RET_WIDTH, BLK), bf),
        jax.ShapeDtypeStruct((B, S, RET_WIDTH), bf),
        jax.ShapeDtypeStruct((B, S, RET_WIDTH), bf),
    )
    out_specs = (
        tr(QW), nat(QW), tr(VW),
        pl.BlockSpec((None, nblk, 1, QW), lambda b, t: (b, t, 0, 0)),
        nat(RET_WIDTH), tr(RET_WIDTH), nat(RET_WIDTH), nat(RET_WIDTH),
    )
    consts = (g, wqT, qone, wk, kpos, wvT, vone, wrq, wrkT, wrv, wrg)
    in_specs = [pl.BlockSpec((None, ROW_TILE, D), lambda b, t: (b, t, 0))] + [
        _const_spec(a.shape) for a in consts]
    return pl.pallas_call(
        _in_proj_kernel, grid=grid, in_specs=in_specs, out_specs=out_specs, out_shape=out_shape,
        compiler_params=pltpu.CompilerParams(
            dimension_semantics=("parallel", "parallel"), vmem_limit_bytes=VMEM_LIMIT),
        name="in_proj",
    )(x, *consts)


def _route_kernel(qT_ref, kmean_ref, sel_ref):
    i = pl.program_id(2)
    nb = kmean_ref.shape[0]
    blk_id = lax.broadcasted_iota(jnp.int32, (nb, BLK), 0)
    for a in range(2):
        qTh = qT_ref[a * QK_PAD:(a + 1) * QK_PAD, :]
        km = kmean_ref[:, a * QK_PAD:(a + 1) * QK_PAD]
        km_hi = km.astype(jnp.bfloat16)
        km_lo = (km - km_hi.astype(jnp.float32)).astype(jnp.bfloat16)
        gate = (jnp.dot(km_hi, qTh, preferred_element_type=jnp.float32)
                + jnp.dot(km_lo, qTh, preferred_element_type=jnp.float32))
        gate = jnp.where(blk_id < i, gate, -jnp.inf)
        rows = []
        for _ in range(MOBA_TOPK):
            top = jnp.max(gate, axis=0, keepdims=True)
            is_top = (gate == top) & (top > -jnp.inf)
            first = jnp.min(jnp.where(is_top, blk_id, nb), axis=0, keepdims=True)
            rows.append(first)
            gate = jnp.where(blk_id == first, -jnp.inf, gate)
        rows.append(jnp.full((8 - MOBA_TOPK, BLK), nb, jnp.int32))
        sel_ref[a] = jnp.concatenate(rows, axis=0)


def _route(qT, kmean):
    B, nb, QW, _ = qT.shape
    kmean3 = kmean.reshape(B, nb, QW)
    return pl.pallas_call(
        _route_kernel,
        grid=(B, MOBA_HEADS // 2, nb),
        in_specs=[
            pl.BlockSpec((None, None, 2 * QK_PAD, BLK), lambda b, h, i: (b, i, h, 0)),
            pl.BlockSpec((None, nb, 2 * QK_PAD), lambda b, h, i: (b, 0, h)),
        ],
        out_specs=pl.BlockSpec((None, 2, None, 8, BLK), lambda b, h, i: (b, h, i, 0, 0)),
        out_shape=jax.ShapeDtypeStruct((B, MOBA_HEADS, nb, 8, BLK), jnp.int32),
        compiler_params=pltpu.CompilerParams(
            dimension_semantics=("parallel", "parallel", "arbitrary"), vmem_limit_bytes=VMEM_LIMIT),
        name="moba_route",
    )(qT, kmean3)


def _moba_kernel(slopes_ref, ti_ref, tj_ref, qT_ref, k_ref, vT_ref, sel_ref, o_ref,
                 m_ref, acc_ref, s_ref, smax_ref, alpha_ref, p_ref,
                 s_own_ref, smax_own_ref, alpha_own_ref, p_own_ref, *, n_past_groups):
    h = pl.program_id(1)
    nb = k_ref.shape[0]
    slope = slopes_ref[h]
    kpos = lax.broadcasted_iota(jnp.int32, (BLK, BLK), 0)
    qpos = lax.broadcasted_iota(jnp.int32, (BLK, BLK), 1)
    qterm = -slope * lax.broadcasted_iota(jnp.int32, (1, BLK), 1).astype(jnp.float32)

    m_ref[...] = jnp.full(m_ref.shape, -3e38, jnp.float32)
    acc_ref[...] = jnp.zeros(acc_ref.shape, jnp.float32)

    def own_tile(g, e):
        t = g * GROUP_OWN + e
        return t, t

    def past_tile(g, e):
        t = g * GROUP + e
        return ti_ref[t], tj_ref[t]

    def run_stream(n_groups, tile_of, own, bufs):
        s_ref, smax_ref, alpha_ref, p_ref = bufs
        GROUP = s_ref.shape[1]
        def scores(par, e, i_t, j_t):
            s = jnp.dot(k_ref[j_t], qT_ref[i_t], preferred_element_type=jnp.float32)
            if own:
                s = jnp.where(kpos <= qpos, s, NEG)
            s_ref[par, e] = s
            smax_ref[par, e:e + 1, :] = jnp.max(s, axis=0, keepdims=True)

        def values(par, e, i_t, j_t):
            pv = jnp.dot(vT_ref[j_t], p_ref[par, e * BLK:(e + 1) * BLK, :],
                         preferred_element_type=jnp.float32)
            acc_ref[i_t] = alpha_ref[par, e:e + 1, :] * acc_ref[i_t] + pv

        def softmax(par, e, i_t, j_t):
            if own:
                c = qterm
            else:
                sel = sel_ref[i_t]
                hit = (sel[0:1] == j_t) | (sel[1:2] == j_t) | (sel[2:3] == j_t)
                blk_dist = ((i_t - j_t) * MOBA_BLOCK).astype(jnp.float32)
                c = jnp.where(hit, -slope * blk_dist, NEG) + qterm
            m_old = m_ref[pl.ds(i_t, 1), :]
            m_new = jnp.maximum(m_old, smax_ref[par, e:e + 1, :] + c)
            alpha_ref[par, e:e + 1, :] = jnp.exp2(m_old - m_new)
            m_ref[pl.ds(i_t, 1), :] = m_new
            p_ref[par, e * BLK:(e + 1) * BLK, :] = (
                jnp.exp2(s_ref[par, e] - (m_new - c)).astype(jnp.bfloat16))

        def trip(g, par):
            g_next = jnp.minimum(g + 1, n_groups - 1)
            g_prev = jnp.maximum(g - 1, 0)
            for e in range(GROUP):
                scores(1 - par, e, *tile_of(g_next, e))
                softmax(par, e, *tile_of(g, e))
                values(1 - par, e, *tile_of(g_prev, e))

        alpha_ref[1] = jnp.ones(alpha_ref.shape[1:], jnp.float32)
        p_ref[1] = jnp.zeros(p_ref.shape[1:], jnp.bfloat16)
        for e in range(GROUP):
            scores(0, e, *tile_of(0, e))

        def body(g, carry):
            lax.cond(g % 2 == 0, lambda: trip(g, 0), lambda: trip(g, 1))
            return carry

        lax.fori_loop(0, n_groups, body, 0)
        for e in range(GROUP):
            values((n_groups - 1) % 2, e, *tile_of(n_groups - 1, e))

    run_stream(nb // GROUP_OWN, own_tile, True, (s_own_ref, smax_own_ref, alpha_own_ref, p_own_ref))
    run_stream(n_past_groups, past_tile, False, (s_ref, smax_ref, alpha_ref, p_ref))

    def write_out(half):
        def one(i, carry):
            acc = acc_ref[i]
            oT = acc[0:MOBA_HEAD_DIM] / acc[ONES_ROW:ONES_ROW + 1]
            rows = pl.ds(pl.multiple_of(i * BLK, BLK), BLK)
            o_ref[rows, half * MOBA_HEAD_DIM:(half + 1) * MOBA_HEAD_DIM] = oT.T.astype(o_ref.dtype)
            return carry
        lax.fori_loop(0, nb, one, 0)

    lax.cond(h % 2 == 0, lambda: write_out(0), lambda: write_out(1))


def _moba(slopes_l2, qT, k, vT, sel):
    B, nb, QW, _ = qT.shape
    S = nb * BLK
    assert nb % GROUP == 0
    k4 = k.reshape(B, nb, BLK, QW)
    pairs = [(i, j) for j in range(nb - 1) for i in range(j + 1, nb)]
    pairs += [(0, 0)] * (-len(pairs) % GROUP)
    ti = jnp.asarray([p[0] for p in pairs], jnp.int32)
    tj = jnp.asarray([p[1] for p in pairs], jnp.int32)
    grid_spec = pltpu.PrefetchScalarGridSpec(
        num_scalar_prefetch=3,
        grid=(B, MOBA_HEADS),
        in_specs=[
            pl.BlockSpec((None, nb, QK_PAD, BLK), lambda b, h, *_: (b, 0, h, 0)),
            pl.BlockSpec((None, nb, BLK, QK_PAD), lambda b, h, *_: (b, 0, 0, h)),
            pl.BlockSpec((None, nb, V_PAD, BLK), lambda b, h, *_: (b, 0, h, 0)),
            pl.BlockSpec((None, None, nb, 8, BLK), lambda b, h, *_: (b, h, 0, 0, 0)),
        ],
        out_specs=pl.BlockSpec((None, S, 2 * MOBA_HEAD_DIM), lambda b, h, *_: (b, 0, h // 2)),
        scratch_shapes=[
            pltpu.VMEM((nb, BLK), jnp.float32),
            pltpu.VMEM((nb, V_PAD, BLK), jnp.float32),
            pltpu.VMEM((2, GROUP, BLK, BLK), jnp.float32),
            pltpu.VMEM((2, GROUP, BLK), jnp.float32),
            pltpu.VMEM((2, GROUP, BLK), jnp.float32),
            pltpu.VMEM((2, GROUP * BLK, BLK), jnp.bfloat16),
            pltpu.VMEM((2, GROUP_OWN, BLK, BLK), jnp.float32),
            pltpu.VMEM((2, 8, BLK), jnp.float32),
            pltpu.VMEM((2, 8, BLK), jnp.float32),
            pltpu.VMEM((2, GROUP_OWN * BLK, BLK), jnp.bfloat16),
        ],
    )
    return pl.pallas_call(
        functools.partial(_moba_kernel, n_past_groups=len(pairs) // GROUP),
        grid_spec=grid_spec,
        out_shape=jax.ShapeDtypeStruct((B, S, MOBA_WIDTH), jnp.bfloat16),
        compiler_params=pltpu.CompilerParams(
            dimension_semantics=("arbitrary", "arbitrary"), vmem_limit_bytes=VMEM_LIMIT),
        name="moba",
    )(slopes_l2, ti, tj, qT, k4, vT, sel)


def _ret_kernel(cd_ref, q_ref, kT_ref, v_ref, g_ref, dintra_ref, qdec_ref, kdec_ref, gn_ref,
                o_ref, state_ref):
    c = pl.program_id(1)

    @pl.when(c == 0)
    def _():
        state_ref[...] = jnp.zeros_like(state_ref)

    for h in range(RET_HEADS):
        cols = slice(h * RET_HEAD_DIM, (h + 1) * RET_HEAD_DIM)
        q = q_ref[:, cols]
        kT = kT_ref[cols, :]
        v = v_ref[:, cols]
        state = state_ref[h]
        s = jnp.dot(q, kT, preferred_element_type=jnp.float32) * dintra_ref[h]
        inner = jnp.dot(s.astype(jnp.bfloat16), v, preferred_element_type=jnp.float32)
        qd = (q.astype(jnp.float32) * qdec_ref[h]).astype(jnp.bfloat16)
        cross = jnp.dot(qd, state.astype(jnp.bfloat16), preferred_element_type=jnp.float32)
        kd = (kT.astype(jnp.float32) * kdec_ref[h]).astype(jnp.bfloat16)
        state_ref[h] = cd_ref[h] * state + jnp.dot(kd, v, preferred_element_type=jnp.float32)
        o = inner + cross
        mu = jnp.mean(o, axis=-1, keepdims=True)
        d = o - mu
        var = jnp.mean(d * d, axis=-1, keepdims=True)
        gate = g_ref[:, cols].astype(jnp.float32)
        y = d * lax.rsqrt(var + GN_EPS) * gn_ref[:, cols] * (gate * jax.nn.sigmoid(gate))
        o_ref[:, cols] = y.astype(o_ref.dtype)


def _retention(rq, rkT, rv, rg, ret_norm):
    B, S, W = rq.shape
    nc = S // BLK
    H = RET_HEADS
    C = RET_CHUNK
    f32 = jnp.float32
    gamma = 1.0 - jnp.exp2(-5.0 - jnp.arange(H, dtype=f32))
    log_g = jnp.log(gamma)
    idx = jnp.arange(C, dtype=f32)
    diff = idx[:, None] - idx[None, :]
    dintra = jnp.where(diff >= 0, jnp.exp(log_g[:, None, None] * jnp.maximum(diff, 0.0)), 0.0)
    qdec = jnp.broadcast_to(jnp.exp(log_g[:, None] * (idx + 1.0))[..., None], (H, C, LANES))
    kdec = jnp.exp(log_g[:, None] * (C - 1.0 - idx))[:, None, :]
    cdec = jnp.exp(log_g * C)
    gn = ret_norm.reshape(1, W).astype(f32)

    nat = pl.BlockSpec((None, C, W), lambda b, c, s: (b, c, 0))
    const = lambda a: pl.BlockSpec(a.shape, lambda b, c, s: (0,) * a.ndim,
                                   pipeline_mode=pl.Buffered(1))
    grid_spec = pltpu.PrefetchScalarGridSpec(
        num_scalar_prefetch=1,
        grid=(B, nc),
        in_specs=[
            nat,
            pl.BlockSpec((None, None, W, C), lambda b, c, s: (b, c, 0, 0)),
            nat, nat,
            const(dintra), const(qdec), const(kdec), const(gn),
        ],
        out_specs=nat,
        scratch_shapes=[pltpu.VMEM((H, RET_HEAD_DIM, RET_HEAD_DIM), f32)],
    )
    return pl.pallas_call(
        _ret_kernel, grid_spec=grid_spec,
        out_shape=jax.ShapeDtypeStruct((B, S, W), jnp.bfloat16),
        compiler_params=pltpu.CompilerParams(
            dimension_semantics=("parallel", "arbitrary"),
            vmem_limit_bytes=VMEM_LIMIT),
        name="retention",
    )(cdec, rq, rkT, rv, rg, dintra, qdec, kdec, gn)


def _gelu_tanh(u):
    k = -2.0 * math.sqrt(2.0 / math.pi) * LOG2E
    return u / (1.0 + jnp.exp2(u * (k + (k * 0.044715) * (u * u))))


def _shift_rows(cur, prev_tail, n):
    rolled = pltpu.roll(cur, n, axis=0)
    prev_rolled = pltpu.roll(prev_tail, n, axis=0)
    row = lax.broadcasted_iota(jnp.int32, prev_tail.shape, 0)
    head = jnp.where(row < n, prev_rolled, rolled[0:8])
    return jnp.concatenate([head, rolled[8:]], axis=0)


def _tail_kernel(x_ref, moba_ref, ret_ref, p_ref, wo_ref, ffn_g_ref, wup_ref, wgate_ref,
                 convw_ref, convb_ref, wdown_ref, ple_g_ref, wple_ref, wpg_ref, fin_g_ref,
                 o_ref, y_ref, carry_ref, *, tiles_per_seq):
    t = pl.program_id(0)
    R = ROW_TILE

    @pl.when(t % tiles_per_seq == 0)
    def _():
        carry_ref[...] = jnp.zeros_like(carry_ref)

    mix = jnp.concatenate([moba_ref[...], ret_ref[...]], axis=1)
    x1 = x_ref[...] + jnp.dot(mix, wo_ref[...], preferred_element_type=jnp.float32)
    h = _rms(x1, ffn_g_ref[...]).astype(jnp.bfloat16)

    for c in range(N_FF_CHUNKS):
        up = jnp.dot(h, wup_ref[c], preferred_element_type=jnp.float32)
        gt = jnp.dot(h, wgate_ref[c], preferred_element_type=jnp.float32)
        prev_tail = carry_ref[c]
        carry_ref[c] = up[R - 8:R, :]
        w = convw_ref[c]
        u = (convb_ref[c]
             + w[0:1, :] * _shift_rows(up, prev_tail, 2)
             + w[1:2, :] * _shift_rows(up, prev_tail, 1)
             + w[2:3, :] * up)
        y_ref[:, c * FF_CHUNK:(c + 1) * FF_CHUNK] = (_gelu_tanh(u) * gt).astype(jnp.bfloat16)
    x2 = x1 + jnp.dot(y_ref[...], wdown_ref[...], preferred_element_type=jnp.float32)

    hn = _rms(x2, ple_g_ref[...]).astype(jnp.bfloat16)
    g = jax.nn.sigmoid(jnp.dot(hn, wpg_ref[...], preferred_element_type=jnp.float32))
    pe = jnp.dot(p_ref[...].astype(jnp.bfloat16), wple_ref[...], preferred_element_type=jnp.float32)
    x3 = x2 + pe * g
    o_ref[...] = _rms(x3, fin_g_ref[...])


def _tail(x2d, moba2d, ret2d, p2d, wo, ffn_g, wup, wgate, convw, convb, wdown, ple_g, wple,
          wpg, fin_g, seq_len):
    T, D = x2d.shape
    R = ROW_TILE
    row = lambda w: pl.BlockSpec((R, w), lambda t: (t, 0))
    consts = (wo, ffn_g, wup, wgate, convw, convb, wdown, ple_g, wple, wpg, fin_g)
    in_specs = [row(D), row(MOBA_WIDTH), row(RET_WIDTH), row(PLE_DIM)] + [
        _const_spec(a.shape) for a in consts]
    return pl.pallas_call(
        functools.partial(_tail_kernel, tiles_per_seq=seq_len // R),
        grid=(T // R,), in_specs=in_specs, out_specs=row(D),
        out_shape=jax.ShapeDtypeStruct((T, D), jnp.float32),
        scratch_shapes=[
            pltpu.VMEM((R, D_FF), jnp.bfloat16),
            pltpu.VMEM((N_FF_CHUNKS, 8, FF_CHUNK), jnp.float32),
        ],
        compiler_params=pltpu.CompilerParams(
            dimension_semantics=("arbitrary",), vmem_limit_bytes=VMEM_LIMIT),
        name="tail",
    )(x2d, moba2d, ret2d, p2d, *consts)


def _pad_heads(w, pad):
    d = w.shape[0]
    w3 = w.reshape(d, MOBA_HEADS, MOBA_HEAD_DIM)
    return jnp.pad(w3, ((0, 0), (0, 0), (0, pad - MOBA_HEAD_DIM))).reshape(d, MOBA_HEADS * pad)


def kernel(x, p, attn_norm, w_in, ret_norm, w_out, ffn_norm, w_up, w_gate, conv_w, conv_b, w_down,
           ple_norm, w_ple, w_ple_gate, final_norm):
    B, S, D = x.shape
    assert D == D_MODEL and S % ROW_TILE == 0 and w_in.shape[0] == 1
    bf, f32 = jnp.bfloat16, jnp.float32
    MW, RW = MOBA_WIDTH, RET_WIDTH
    w = w_in[0].astype(bf)
    wqT = _pad_heads(w[:, 0:MW], QK_PAD).T
    wk = _pad_heads(w[:, MW:2 * MW], QK_PAD)
    wvT = _pad_heads(w[:, 2 * MW:3 * MW], V_PAD).T
    o = 3 * MW
    wrq = w[:, o:o + RW]
    wrkT = w[:, o + RW:o + 2 * RW].T
    wrv = w[:, o + 2 * RW:o + 3 * RW]
    wrg = w[:, o + 3 * RW:o + 4 * RW]
    row = lambda a: a.reshape(1, -1).astype(f32)

    slopes_l2 = jnp.exp2(-8.0 * jnp.arange(1, MOBA_HEADS + 1, dtype=f32) / MOBA_HEADS) * LOG2E
    in_head = jnp.arange(QK_PAD)
    qone = jnp.tile(((in_head == ALIBI_ROW) | (in_head == ALIBI_ROW + 1)).astype(f32), MOBA_HEADS)[:, None]
    vone = jnp.tile((jnp.arange(V_PAD) == ONES_ROW).astype(f32), MOBA_HEADS)[:, None]
    kterm = slopes_l2[None, :] * jnp.arange(BLK, dtype=f32)[:, None]
    kterm_hi = lax.bitcast_convert_type(
        lax.bitcast_convert_type(kterm, jnp.uint32) & jnp.uint32(0xFFFF0000), f32)
    kterm_lo = kterm - kterm_hi
    kpos = jnp.zeros((BLK, MOBA_HEADS, QK_PAD), f32)
    kpos = kpos.at[:, :, ALIBI_ROW].set(kterm_hi).at[:, :, ALIBI_ROW + 1].set(kterm_lo)
    kpos = kpos.reshape(BLK, MOBA_HEADS * QK_PAD)

    qT, k, vT, kmean, rq, rkT, rv, rg = _in_proj(
        x, row(attn_norm[0]), wqT, qone, wk, kpos, wvT, vone, wrq, wrkT, wrv, wrg)

    sel = _route(qT, kmean)
    moba_out = _moba(slopes_l2, qT, k, vT, sel)
    ret_out = _retention(rq, rkT, rv, rg, ret_norm[0])

    wo = w_out[0].astype(bf)
    chunked = lambda a: a.reshape(D, N_FF_CHUNKS, FF_CHUNK).transpose(1, 0, 2)
    out = _tail(
        x.reshape(B * S, D), moba_out.reshape(B * S, MW), ret_out.reshape(B * S, RW),
        p[0].reshape(B * S, PLE_DIM),
        wo, row(ffn_norm[0]),
        chunked(w_up[0].astype(bf)), chunked(w_gate[0].astype(bf)),
        conv_w[0].reshape(CONV_WIDTH, N_FF_CHUNKS, FF_CHUNK).transpose(1, 0, 2).astype(f32),
        conv_b[0].reshape(N_FF_CHUNKS, 1, FF_CHUNK).astype(f32),
        w_down[0].astype(bf),
        row(ple_norm[0]), w_ple[0].astype(bf), w_ple_gate[0].astype(bf), row(final_norm),
        seq_len=S)
    return out.reshape(B, S, D)
```

```python
import functools
import math

import jax
import jax.numpy as jnp
from jax import lax
from jax.experimental import pallas as pl
from jax.experimental.pallas import tpu as pltpu

D_MODEL = 1024
PLE_DIM = 256
MOBA_HEADS = 8
MOBA_HEAD_DIM = 64
MOBA_WIDTH = MOBA_HEADS * MOBA_HEAD_DIM
MOBA_BLOCK = 256
MOBA_TOPK = 3
RET_HEADS = 4
RET_HEAD_DIM = 128
RET_WIDTH = RET_HEADS * RET_HEAD_DIM
RET_CHUNK = 256
D_FF = 2816
CONV_WIDTH = 3
RMS_EPS = 1e-6
GN_EPS = 1e-5

BLK = 256
LANES = 128
FF_CHUNK = 256
N_FF_CHUNKS = D_FF // FF_CHUNK
ROW_TILE = 512
VMEM_LIMIT = 56 * 1024 * 1024
NEG = -1e30
LOG2E = math.log2(math.e)

QK_PAD = LANES
ALIBI_ROW = MOBA_HEAD_DIM
V_PAD = 80
ONES_ROW = MOBA_HEAD_DIM
GROUP = 16
GROUP_OWN = 4

_NT = (((1,), (1,)), ((), ()))


def _const_spec(shape):
    nd = len(shape)
    return pl.BlockSpec(shape, lambda *_: (0,) * nd, pipeline_mode=pl.Buffered(1))


def _rms(x, g):
    ms = jnp.mean(x * x, axis=-1, keepdims=True)
    return x * lax.rsqrt(ms + RMS_EPS) * g


def _in_proj_kernel(x_ref, g_ref, wqT_ref, qone_ref, wk_ref, kpos_ref, wvT_ref, vone_ref,
                    wrq_ref, wrkT_ref, wrv_ref, wrg_ref,
                    qT_ref, k_ref, vT_ref, sel_ref, rq_ref, rkT_ref, rv_ref, rg_ref, kmean_ref):
    t = pl.program_id(1)
    h = _rms(x_ref[...], g_ref[...]).astype(jnp.bfloat16)
    nblk = ROW_TILE // BLK
    nb = kmean_ref.shape[0]

    @pl.when(t == 0)
    def _():
        kmean_ref[...] = jnp.zeros_like(kmean_ref)

    def nt(w_ref):
        return lax.dot_general(w_ref[...], h, _NT, preferred_element_type=jnp.float32)

    def nn(w_ref):
        return jnp.dot(h, w_ref[...], preferred_element_type=jnp.float32)

    qT = (nt(wqT_ref) * (MOBA_HEAD_DIM ** -0.5 * LOG2E) + qone_ref[...]).astype(jnp.bfloat16)
    vT = (nt(wvT_ref) + vone_ref[...]).astype(jnp.bfloat16)
    rkT = (nt(wrkT_ref) * (RET_HEAD_DIM ** -0.5)).astype(jnp.bfloat16)
    for b in range(nblk):
        sl = slice(b * BLK, (b + 1) * BLK)
        qT_ref[b] = qT[:, sl]
        vT_ref[b] = vT[:, sl]
        rkT_ref[b] = rkT[:, sl]
    k = nn(wk_ref)
    for b in range(nblk):
        kb = k[b * BLK:(b + 1) * BLK]
        kmean_ref[pl.ds(t * nblk + b, 1), :] = jnp.mean(kb, axis=0, keepdims=True)
        k_ref[b * BLK:(b + 1) * BLK, :] = (kb + kpos_ref[...]).astype(jnp.bfloat16)
    rq_ref[...] = nn(wrq_ref).astype(jnp.bfloat16)
    rv_ref[...] = nn(wrv_ref).astype(jnp.bfloat16)
    rg_ref[...] = nn(wrg_ref).astype(jnp.bfloat16)

    km = kmean_ref[...]
    km_hi = km.astype(jnp.bfloat16)
    km_lo = (km - km_hi.astype(jnp.float32)).astype(jnp.bfloat16)
    blk_id = lax.broadcasted_iota(jnp.int32, (nb, BLK), 0)
    none = jnp.full((8 - MOBA_TOPK, BLK), nb, jnp.int32)
    for b in range(nblk):
        i = t * nblk + b
        for a in range(MOBA_HEADS):
            rows = slice(a * QK_PAD, (a + 1) * QK_PAD)
            qTh = qT[rows, b * BLK:(b + 1) * BLK]
            gate = (jnp.dot(km_hi[:, rows], qTh, preferred_element_type=jnp.float32)
                    + jnp.dot(km_lo[:, rows], qTh, preferred_element_type=jnp.float32))
            gate = jnp.where(blk_id < i, gate, -jnp.inf)
            picks = []
            for _ in range(MOBA_TOPK):
                top = jnp.max(gate, axis=0, keepdims=True)
                is_top = (gate == top) & (top > -jnp.inf)
                first = jnp.min(jnp.where(is_top, blk_id, nb), axis=0, keepdims=True)
                picks.append(first)
                gate = jnp.where(blk_id == first, -jnp.inf, gate)
            sel_ref[a, b] = jnp.concatenate(picks + [none], axis=0)


def _in_proj(x, g, wqT, qone, wk, kpos, wvT, vone, wrq, wrkT, wrv, wrg):
    B, S, D = x.shape
    nb = S // BLK
    nblk = ROW_TILE // BLK
    grid = (B, S // ROW_TILE)
    bf = jnp.bfloat16
    QW, VW = MOBA_HEADS * QK_PAD, MOBA_HEADS * V_PAD
    nat = lambda w: pl.BlockSpec((None, ROW_TILE, w), lambda b, t: (b, t, 0))
    tr = lambda w: pl.BlockSpec((None, nblk, w, BLK), lambda b, t: (b, t, 0, 0))
    out_shape = (
        jax.ShapeDtypeStruct((B, nb, QW, BLK), bf),
        jax.ShapeDtypeStruct((B, S, QW), bf),
        jax.ShapeDtypeStruct((B, nb, VW, BLK), bf),
        jax.ShapeDtypeStruct((B, MOBA_HEADS, nb, 8, BLK), jnp.int32),
        jax.ShapeDtypeStruct((B, S, RET_WIDTH), bf),
        jax.ShapeDtypeStruct((B, nb, RET_WIDTH, BLK), bf),
        jax.ShapeDtypeStruct((B, S, RET_WIDTH), bf),
        jax.ShapeDtypeStruct((B, S, RET_WIDTH), bf),
    )
    out_specs = (
        tr(QW), nat(QW), tr(VW),
        pl.BlockSpec((None, MOBA_HEADS, nblk, 8, BLK), lambda b, t: (b, 0, t, 0, 0)),
        nat(RET_WIDTH), tr(RET_WIDTH), nat(RET_WIDTH), nat(RET_WIDTH),
    )
    consts = (g, wqT, qone, wk, kpos, wvT, vone, wrq, wrkT, wrv, wrg)
    in_specs = [pl.BlockSpec((None, ROW_TILE, D), lambda b, t: (b, t, 0))] + [
        _const_spec(a.shape) for a in consts]
    return pl.pallas_call(
        _in_proj_kernel, grid=grid, in_specs=in_specs, out_specs=out_specs, out_shape=out_shape,
        scratch_shapes=[pltpu.VMEM((nb, QW), jnp.float32)],
        compiler_params=pltpu.CompilerParams(
            dimension_semantics=("arbitrary", "arbitrary"), vmem_limit_bytes=VMEM_LIMIT),
        name="in_proj",
    )(x, *consts)


def _moba_kernel(slopes_ref, ti_ref, tj_ref, qT_ref, k_ref, vT_ref, sel_ref, o_ref,
                 m_ref, acc_ref, s_ref, smax_ref, alpha_ref, p_ref,
                 s_own_ref, smax_own_ref, alpha_own_ref, p_own_ref, *, n_past_groups):
    h = pl.program_id(1)
    nb = k_ref.shape[0]
    slope = slopes_ref[h]
    kpos = lax.broadcasted_iota(jnp.int32, (BLK, BLK), 0)
    qpos = lax.broadcasted_iota(jnp.int32, (BLK, BLK), 1)
    qterm = -slope * lax.broadcasted_iota(jnp.int32, (1, BLK), 1).astype(jnp.float32)

    m_ref[...] = jnp.full(m_ref.shape, -3e38, jnp.float32)
    acc_ref[...] = jnp.zeros(acc_ref.shape, jnp.float32)

    def own_tile(g, e):
        t = g * GROUP_OWN + e
        return t, t

    def past_tile(g, e):
        t = g * GROUP + e
        return ti_ref[t], tj_ref[t]

    def run_stream(n_groups, tile_of, own, bufs):
        s_ref, smax_ref, alpha_ref, p_ref = bufs
        GROUP = s_ref.shape[1]
        def scores(par, e, i_t, j_t):
            s = jnp.dot(k_ref[j_t], qT_ref[i_t], preferred_element_type=jnp.float32)
            if own:
                s = jnp.where(kpos <= qpos, s, NEG)
            s_ref[par, e] = s
            smax_ref[par, e:e + 1, :] = jnp.max(s, axis=0, keepdims=True)

        def values(par, e, i_t, j_t):
            pv = jnp.dot(vT_ref[j_t], p_ref[par, e * BLK:(e + 1) * BLK, :],
                         preferred_element_type=jnp.float32)
            acc_ref[i_t] = alpha_ref[par, e:e + 1, :] * acc_ref[i_t] + pv

        def softmax(par, e, i_t, j_t):
            if own:
                c = qterm
            else:
                sel = sel_ref[i_t]
                hit = (sel[0:1] == j_t) | (sel[1:2] == j_t) | (sel[2:3] == j_t)
                blk_dist = ((i_t - j_t) * MOBA_BLOCK).astype(jnp.float32)
                c = jnp.where(hit, -slope * blk_dist, NEG) + qterm
            m_old = m_ref[pl.ds(i_t, 1), :]
            m_new = jnp.maximum(m_old, smax_ref[par, e:e + 1, :] + c)
            alpha_ref[par, e:e + 1, :] = jnp.exp2(m_old - m_new)
            m_ref[pl.ds(i_t, 1), :] = m_new
            p_ref[par, e * BLK:(e + 1) * BLK, :] = (
                jnp.exp2(s_ref[par, e] - (m_new - c)).astype(jnp.bfloat16))

        def trip(g, par):
            g_next = jnp.minimum(g + 1, n_groups - 1)
            g_prev = jnp.maximum(g - 1, 0)
            for e in range(GROUP):
                scores(1 - par, e, *tile_of(g_next, e))
                softmax(par, e, *tile_of(g, e))
                values(1 - par, e, *tile_of(g_prev, e))

        alpha_ref[1] = jnp.ones(alpha_ref.shape[1:], jnp.float32)
        p_ref[1] = jnp.zeros(p_ref.shape[1:], jnp.bfloat16)
        for e in range(GROUP):
            scores(0, e, *tile_of(0, e))

        def body(g, carry):
            lax.cond(g % 2 == 0, lambda: trip(g, 0), lambda: trip(g, 1))
            return carry

        lax.fori_loop(0, n_groups, body, 0)
        for e in range(GROUP):
            values((n_groups - 1) % 2, e, *tile_of(n_groups - 1, e))

    run_stream(nb // GROUP_OWN, own_tile, True, (s_own_ref, smax_own_ref, alpha_own_ref, p_own_ref))
    run_stream(n_past_groups, past_tile, False, (s_ref, smax_ref, alpha_ref, p_ref))

    def write_out(half):
        def one(i, carry):
            acc = acc_ref[i]
            oT = acc[0:MOBA_HEAD_DIM] / acc[ONES_ROW:ONES_ROW + 1]
            rows = pl.ds(pl.multiple_of(i * BLK, BLK), BLK)
            o_ref[rows, half * MOBA_HEAD_DIM:(half + 1) * MOBA_HEAD_DIM] = oT.T.astype(o_ref.dtype)
            return carry
        lax.fori_loop(0, nb, one, 0)

    lax.cond(h % 2 == 0, lambda: write_out(0), lambda: write_out(1))


def _moba(slopes_l2, qT, k, vT, sel):
    B, nb, QW, _ = qT.shape
    S = nb * BLK
    assert nb % GROUP == 0
    k4 = k.reshape(B, nb, BLK, QW)
    pairs = [(i, j) for j in range(nb - 1) for i in range(j + 1, nb)]
    pairs += [(0, 0)] * (-len(pairs) % GROUP)
    ti = jnp.asarray([p[0] for p in pairs], jnp.int32)
    tj = jnp.asarray([p[1] for p in pairs], jnp.int32)
    grid_spec = pltpu.PrefetchScalarGridSpec(
        num_scalar_prefetch=3,
        grid=(B, MOBA_HEADS),
        in_specs=[
            pl.BlockSpec((None, nb, QK_PAD, BLK), lambda b, h, *_: (b, 0, h, 0)),
            pl.BlockSpec((None, nb, BLK, QK_PAD), lambda b, h, *_: (b, 0, 0, h)),
            pl.BlockSpec((None, nb, V_PAD, BLK), lambda b, h, *_: (b, 0, h, 0)),
            pl.BlockSpec((None, None, nb, 8, BLK), lambda b, h, *_: (b, h, 0, 0, 0)),
        ],
        out_specs=pl.BlockSpec((None, S, 2 * MOBA_HEAD_DIM), lambda b, h, *_: (b, 0, h // 2)),
        scratch_shapes=[
            pltpu.VMEM((nb, BLK), jnp.float32),
            pltpu.VMEM((nb, V_PAD, BLK), jnp.float32),
            pltpu.VMEM((2, GROUP, BLK, BLK), jnp.float32),
            pltpu.VMEM((2, GROUP, BLK), jnp.float32),
            pltpu.VMEM((2, GROUP, BLK), jnp.float32),
            pltpu.VMEM((2, GROUP * BLK, BLK), jnp.bfloat16),
            pltpu.VMEM((2, GROUP_OWN, BLK, BLK), jnp.float32),
            pltpu.VMEM((2, 8, BLK), jnp.float32),
            pltpu.VMEM((2, 8, BLK), jnp.float32),
            pltpu.VMEM((2, GROUP_OWN * BLK, BLK), jnp.bfloat16),
        ],
    )
    return pl.pallas_call(
        functools.partial(_moba_kernel, n_past_groups=len(pairs) // GROUP),
        grid_spec=grid_spec,
        out_shape=jax.ShapeDtypeStruct((B, S, MOBA_WIDTH), jnp.bfloat16),
        compiler_params=pltpu.CompilerParams(
            dimension_semantics=("arbitrary", "arbitrary"), vmem_limit_bytes=VMEM_LIMIT),
        name="moba",
    )(slopes_l2, ti, tj, qT, k4, vT, sel)


def _ret_kernel(cd_ref, q_ref, kT_ref, v_ref, g_ref, dintra_ref, qdec_ref, kdec_ref, gn_ref,
                o_ref, state_ref):
    c = pl.program_id(1)

    @pl.when(c == 0)
    def _():
        state_ref[...] = jnp.zeros_like(state_ref)

    for h in range(RET_HEADS):
        cols = slice(h * RET_HEAD_DIM, (h + 1) * RET_HEAD_DIM)
        q = q_ref[:, cols]
        kT = kT_ref[cols, :]
        v = v_ref[:, cols]
        state = state_ref[h]
        s = jnp.dot(q, kT, preferred_element_type=jnp.float32) * dintra_ref[h]
        inner = jnp.dot(s.astype(jnp.bfloat16), v, preferred_element_type=jnp.float32)
        qd = (q.astype(jnp.float32) * qdec_ref[h]).astype(jnp.bfloat16)
        cross = jnp.dot(qd, state.astype(jnp.bfloat16), preferred_element_type=jnp.float32)
        kd = (kT.astype(jnp.float32) * kdec_ref[h]).astype(jnp.bfloat16)
        state_ref[h] = cd_ref[h] * state + jnp.dot(kd, v, preferred_element_type=jnp.float32)
        o = inner + cross
        mu = jnp.mean(o, axis=-1, keepdims=True)
        d = o - mu
        var = jnp.mean(d * d, axis=-1, keepdims=True)
        gate = g_ref[:, cols].astype(jnp.float32)
        y = d * lax.rsqrt(var + GN_EPS) * gn_ref[:, cols] * (gate * jax.nn.sigmoid(gate))
        o_ref[:, cols] = y.astype(o_ref.dtype)


def _retention(rq, rkT, rv, rg, ret_norm):
    B, S, W = rq.shape
    nc = S // BLK
    H = RET_HEADS
    C = RET_CHUNK
    f32 = jnp.float32
    gamma = 1.0 - jnp.exp2(-5.0 - jnp.arange(H, dtype=f32))
    log_g = jnp.log(gamma)
    idx = jnp.arange(C, dtype=f32)
    diff = idx[:, None] - idx[None, :]
    dintra = jnp.where(diff >= 0, jnp.exp(log_g[:, None, None] * jnp.maximum(diff, 0.0)), 0.0)
    qdec = jnp.broadcast_to(jnp.exp(log_g[:, None] * (idx + 1.0))[..., None], (H, C, LANES))
    kdec = jnp.exp(log_g[:, None] * (C - 1.0 - idx))[:, None, :]
    cdec = jnp.exp(log_g * C)
    gn = ret_norm.reshape(1, W).astype(f32)

    nat = pl.BlockSpec((None, C, W), lambda b, c, s: (b, c, 0))
    const = lambda a: pl.BlockSpec(a.shape, lambda b, c, s: (0,) * a.ndim,
                                   pipeline_mode=pl.Buffered(1))
    grid_spec = pltpu.PrefetchScalarGridSpec(
        num_scalar_prefetch=1,
        grid=(B, nc),
        in_specs=[
            nat,
            pl.BlockSpec((None, None, W, C), lambda b, c, s: (b, c, 0, 0)),
            nat, nat,
            const(dintra), const(qdec), const(kdec), const(gn),
        ],
        out_specs=nat,
        scratch_shapes=[pltpu.VMEM((H, RET_HEAD_DIM, RET_HEAD_DIM), f32)],
    )
    return pl.pallas_call(
        _ret_kernel, grid_spec=grid_spec,
        out_shape=jax.ShapeDtypeStruct((B, S, W), jnp.bfloat16),
        compiler_params=pltpu.CompilerParams(
            dimension_semantics=("parallel", "arbitrary"),
            vmem_limit_bytes=VMEM_LIMIT),
        name="retention",
    )(cdec, rq, rkT, rv, rg, dintra, qdec, kdec, gn)


def _gelu_tanh(u):
    k = -2.0 * math.sqrt(2.0 / math.pi) * LOG2E
    return u / (1.0 + jnp.exp2(u * (k + (k * 0.044715) * (u * u))))


def _shift_rows(cur, prev_tail, n):
    rolled = pltpu.roll(cur, n, axis=0)
    prev_rolled = pltpu.roll(prev_tail, n, axis=0)
    row = lax.broadcasted_iota(jnp.int32, prev_tail.shape, 0)
    head = jnp.where(row < n, prev_rolled, rolled[0:8])
    return jnp.concatenate([head, rolled[8:]], axis=0)


def _tail_kernel(x_ref, moba_ref, ret_ref, p_ref, wo_ref, ffn_g_ref, wup_ref, wgate_ref,
                 convw_ref, convb_ref, wdown_ref, ple_g_ref, wple_ref, wpg_ref, fin_g_ref,
                 o_ref, y_ref, carry_ref, *, tiles_per_seq):
    t = pl.program_id(0)
    R = ROW_TILE

    @pl.when(t % tiles_per_seq == 0)
    def _():
        carry_ref[...] = jnp.zeros_like(carry_ref)

    mix = jnp.concatenate([moba_ref[...], ret_ref[...]], axis=1)
    x1 = x_ref[...] + jnp.dot(mix, wo_ref[...], preferred_element_type=jnp.float32)
    h = _rms(x1, ffn_g_ref[...]).astype(jnp.bfloat16)

    for c in range(N_FF_CHUNKS):
        up = jnp.dot(h, wup_ref[c], preferred_element_type=jnp.float32)
        gt = jnp.dot(h, wgate_ref[c], preferred_element_type=jnp.float32)
        prev_tail = carry_ref[c]
        carry_ref[c] = up[R - 8:R, :]
        w = convw_ref[c]
        u = (convb_ref[c]
             + w[0:1, :] * _shift_rows(up, prev_tail, 2)
             + w[1:2, :] * _shift_rows(up, prev_tail, 1)
             + w[2:3, :] * up)
        y_ref[:, c * FF_CHUNK:(c + 1) * FF_CHUNK] = (_gelu_tanh(u) * gt).astype(jnp.bfloat16)
    x2 = x1 + jnp.dot(y_ref[...], wdown_ref[...], preferred_element_type=jnp.float32)

    hn = _rms(x2, ple_g_ref[...]).astype(jnp.bfloat16)
    g = jax.nn.sigmoid(jnp.dot(hn, wpg_ref[...], preferred_element_type=jnp.float32))
    pe = jnp.dot(p_ref[...].astype(jnp.bfloat16), wple_ref[...], preferred_element_type=jnp.float32)
    x3 = x2 + pe * g
    o_ref[...] = _rms(x3, fin_g_ref[...])


def _tail(x2d, moba2d, ret2d, p2d, wo, ffn_g, wup, wgate, convw, convb, wdown, ple_g, wple,
          wpg, fin_g, seq_len):
    T, D = x2d.shape
    R = ROW_TILE
    row = lambda w: pl.BlockSpec((R, w), lambda t: (t, 0))
    consts = (wo, ffn_g, wup, wgate, convw, convb, wdown, ple_g, wple, wpg, fin_g)
    in_specs = [row(D), row(MOBA_WIDTH), row(RET_WIDTH), row(PLE_DIM)] + [
        _const_spec(a.shape) for a in consts]
    return pl.pallas_call(
        functools.partial(_tail_kernel, tiles_per_seq=seq_len // R),
        grid=(T // R,), in_specs=in_specs, out_specs=row(D),
        out_shape=jax.ShapeDtypeStruct((T, D), jnp.float32),
        scratch_shapes=[
            pltpu.VMEM((R, D_FF), jnp.bfloat16),
            pltpu.VMEM((N_FF_CHUNKS, 8, FF_CHUNK), jnp.float32),
        ],
        compiler_params=pltpu.CompilerParams(
            dimension_semantics=("arbitrary",), vmem_limit_bytes=VMEM_LIMIT),
        name="tail",
    )(x2d, moba2d, ret2d, p2d, *consts)


def _pad_heads(w, pad):
    d = w.shape[0]
    w3 = w.reshape(d, MOBA_HEADS, MOBA_HEAD_DIM)
    return jnp.pad(w3, ((0, 0), (0, 0), (0, pad - MOBA_HEAD_DIM))).reshape(d, MOBA_HEADS * pad)


def kernel(x, p, attn_norm, w_in, ret_norm, w_out, ffn_norm, w_up, w_gate, conv_w, conv_b, w_down,
           ple_norm, w_ple, w_ple_gate, final_norm):
    B, S, D = x.shape
    assert D == D_MODEL and S % ROW_TILE == 0 and w_in.shape[0] == 1
    bf, f32 = jnp.bfloat16, jnp.float32
    MW, RW = MOBA_WIDTH, RET_WIDTH
    w = w_in[0].astype(bf)
    wqT = _pad_heads(w[:, 0:MW], QK_PAD).T
    wk = _pad_heads(w[:, MW:2 * MW], QK_PAD)
    wvT = _pad_heads(w[:, 2 * MW:3 * MW], V_PAD).T
    o = 3 * MW
    wrq = w[:, o:o + RW]
    wrkT = w[:, o + RW:o + 2 * RW].T
    wrv = w[:, o + 2 * RW:o + 3 * RW]
    wrg = w[:, o + 3 * RW:o + 4 * RW]
    row = lambda a: a.reshape(1, -1).astype(f32)

    slopes_l2 = jnp.exp2(-8.0 * jnp.arange(1, MOBA_HEADS + 1, dtype=f32) / MOBA_HEADS) * LOG2E
    in_head = jnp.arange(QK_PAD)
    qone = jnp.tile(((in_head == ALIBI_ROW) | (in_head == ALIBI_ROW + 1)).astype(f32), MOBA_HEADS)[:, None]
    vone = jnp.tile((jnp.arange(V_PAD) == ONES_ROW).astype(f32), MOBA_HEADS)[:, None]
    kterm = slopes_l2[None, :] * jnp.arange(BLK, dtype=f32)[:, None]
    kterm_hi = lax.bitcast_convert_type(
        lax.bitcast_convert_type(kterm, jnp.uint32) & jnp.uint32(0xFFFF0000), f32)
    kterm_lo = kterm - kterm_hi
    kpos = jnp.zeros((BLK, MOBA_HEADS, QK_PAD), f32)
    kpos = kpos.at[:, :, ALIBI_ROW].set(kterm_hi).at[:, :, ALIBI_ROW + 1].set(kterm_lo)
    kpos = kpos.reshape(BLK, MOBA_HEADS * QK_PAD)

    qT, k, vT, sel, rq, rkT, rv, rg = _in_proj(
        x, row(attn_norm[0]), wqT, qone, wk, kpos, wvT, vone, wrq, wrkT, wrv, wrg)

    moba_out = _moba(slopes_l2, qT, k, vT, sel)
    ret_out = _retention(rq, rkT, rv, rg, ret_norm[0])

    wo = w_out[0].astype(bf)
    chunked = lambda a: a.reshape(D, N_FF_CHUNKS, FF_CHUNK).transpose(1, 0, 2)
    out = _tail(
        x.reshape(B * S, D), moba_out.reshape(B * S, MW), ret_out.reshape(B * S, RW),
        p[0].reshape(B * S, PLE_DIM),
        wo, row(ffn_norm[0]),
        chunked(w_up[0].astype(bf)), chunked(w_gate[0].astype(bf)),
        conv_w[0].reshape(CONV_WIDTH, N_FF_CHUNKS, FF_CHUNK).transpose(1, 0, 2).astype(f32),
        conv_b[0].reshape(N_FF_CHUNKS, 1, FF_CHUNK).astype(f32),
        w_down[0].astype(bf),
        row(ple_norm[0]), w_ple[0].astype(bf), w_ple_gate[0].astype(bf), row(final_norm),
        seq_len=S)
    return out.reshape(B, S, D)
```

```python
import functools
import math

import jax
import jax.numpy as jnp
from jax import lax
from jax.experimental import pallas as pl
from jax.experimental.pallas import tpu as pltpu

D_MODEL = 1024
PLE_DIM = 256
MOBA_HEADS = 8
MOBA_HEAD_DIM = 64
MOBA_WIDTH = MOBA_HEADS * MOBA_HEAD_DIM
MOBA_BLOCK = 256
MOBA_TOPK = 3
RET_HEADS = 4
RET_HEAD_DIM = 128
RET_WIDTH = RET_HEADS * RET_HEAD_DIM
RET_CHUNK = 256
D_FF = 2816
CONV_WIDTH = 3
RMS_EPS = 1e-6
GN_EPS = 1e-5

BLK = 256
LANES = 128
FF_CHUNK = 256
N_FF_CHUNKS = D_FF // FF_CHUNK
ROW_TILE = 512
VMEM_LIMIT = 56 * 1024 * 1024
NEG = -1e30
LOG2E = math.log2(math.e)

QK_PAD = LANES
ALIBI_ROW = MOBA_HEAD_DIM
V_PAD = 80
ONES_ROW = MOBA_HEAD_DIM
GROUP = 16
GROUP_OWN = 8

_NT = (((1,), (1,)), ((), ()))


def _const_spec(shape):
    nd = len(shape)
    return pl.BlockSpec(shape, lambda *_: (0,) * nd, pipeline_mode=pl.Buffered(1))


def _rms(x, g):
    ms = jnp.mean(x * x, axis=-1, keepdims=True)
    return x * lax.rsqrt(ms + RMS_EPS) * g


def _in_proj_kernel(x_ref, g_ref, wqT_ref, wk_ref, kpos_ref, wvT_ref,
                    wrq_ref, wrkT_ref, wrv_ref, wrg_ref,
                    qT_ref, k_ref, vT_ref, sel_ref, rq_ref, rkT_ref, rv_ref, rg_ref, kmean_ref):
    t = pl.program_id(1)
    h = _rms(x_ref[...], g_ref[...]).astype(jnp.bfloat16)
    nblk = ROW_TILE // BLK
    nb = kmean_ref.shape[0]

    @pl.when(t == 0)
    def _():
        kmean_ref[...] = jnp.zeros_like(kmean_ref)

    def nt(w_ref):
        return lax.dot_general(w_ref[...], h, _NT, preferred_element_type=jnp.float32)

    def nn(w_ref):
        return jnp.dot(h, w_ref[...], preferred_element_type=jnp.float32)

    Dh = MOBA_HEAD_DIM
    qT = (nt(wqT_ref) * (Dh ** -0.5 * LOG2E)).astype(jnp.bfloat16)
    vT = nt(wvT_ref).astype(jnp.bfloat16)
    rkT = (nt(wrkT_ref) * (RET_HEAD_DIM ** -0.5)).astype(jnp.bfloat16)
    q_pad = (lax.broadcasted_iota(jnp.int32, (QK_PAD - Dh, BLK), 0) < 2).astype(jnp.bfloat16)
    v_pad = (lax.broadcasted_iota(jnp.int32, (V_PAD - Dh, BLK), 0) < 1).astype(jnp.bfloat16)
    for b in range(nblk):
        sl = slice(b * BLK, (b + 1) * BLK)
        for a in range(MOBA_HEADS):
            qT_ref[b, a * QK_PAD:a * QK_PAD + Dh, :] = qT[a * Dh:(a + 1) * Dh, sl]
            qT_ref[b, a * QK_PAD + Dh:(a + 1) * QK_PAD, :] = q_pad
            vT_ref[b, a * V_PAD:a * V_PAD + Dh, :] = vT[a * Dh:(a + 1) * Dh, sl]
            vT_ref[b, a * V_PAD + Dh:(a + 1) * V_PAD, :] = v_pad
        rkT_ref[b] = rkT[:, sl]
    k = nn(wk_ref)
    for b in range(nblk):
        kb = k[b * BLK:(b + 1) * BLK]
        kmean_ref[pl.ds(t * nblk + b, 1), :] = jnp.mean(kb, axis=0, keepdims=True)
        k_ref[b * BLK:(b + 1) * BLK, :] = (kb + kpos_ref[...]).astype(jnp.bfloat16)
    rq_ref[...] = nn(wrq_ref).astype(jnp.bfloat16)
    rv_ref[...] = nn(wrv_ref).astype(jnp.bfloat16)
    rg_ref[...] = nn(wrg_ref).astype(jnp.bfloat16)

    km = kmean_ref[...]
    km_hi = km.astype(jnp.bfloat16)
    km_lo = (km - km_hi.astype(jnp.float32)).astype(jnp.bfloat16)
    blk_id = lax.broadcasted_iota(jnp.int32, (nb, BLK), 0)
    none = jnp.full((8 - MOBA_TOPK, BLK), nb, jnp.int32)
    for b in range(nblk):
        i = t * nblk + b
        for a in range(MOBA_HEADS):
            dims = slice(a * QK_PAD, a * QK_PAD + Dh)
            qTh = qT[a * Dh:(a + 1) * Dh, b * BLK:(b + 1) * BLK]
            gate = (jnp.dot(km_hi[:, dims], qTh, preferred_element_type=jnp.float32)
                    + jnp.dot(km_lo[:, dims], qTh, preferred_element_type=jnp.float32))
            gate = jnp.where(blk_id < i, gate, -jnp.inf)
            picks = []
            for _ in range(MOBA_TOPK):
                top = jnp.max(gate, axis=0, keepdims=True)
                is_top = (gate == top) & (top > -jnp.inf)
                first = jnp.min(jnp.where(is_top, blk_id, nb), axis=0, keepdims=True)
                picks.append(first)
                gate = jnp.where(blk_id == first, -jnp.inf, gate)
            sel_ref[a, b] = jnp.concatenate(picks + [none], axis=0)


def _in_proj(x, g, wqT, wk, kpos, wvT, wrq, wrkT, wrv, wrg):
    B, S, D = x.shape
    nb = S // BLK
    nblk = ROW_TILE // BLK
    grid = (B, S // ROW_TILE)
    bf = jnp.bfloat16
    QW, VW = MOBA_HEADS * QK_PAD, MOBA_HEADS * V_PAD
    nat = lambda w: pl.BlockSpec((None, ROW_TILE, w), lambda b, t: (b, t, 0))
    tr = lambda w: pl.BlockSpec((None, nblk, w, BLK), lambda b, t: (b, t, 0, 0))
    out_shape = (
        jax.ShapeDtypeStruct((B, nb, QW, BLK), bf),
        jax.ShapeDtypeStruct((B, S, QW), bf),
        jax.ShapeDtypeStruct((B, nb, VW, BLK), bf),
        jax.ShapeDtypeStruct((B, MOBA_HEADS, nb, 8, BLK), jnp.int32),
        jax.ShapeDtypeStruct((B, S, RET_WIDTH), bf),
        jax.ShapeDtypeStruct((B, nb, RET_WIDTH, BLK), bf),
        jax.ShapeDtypeStruct((B, S, RET_WIDTH), bf),
        jax.ShapeDtypeStruct((B, S, RET_WIDTH), bf),
    )
    out_specs = (
        tr(QW), nat(QW), tr(VW),
        pl.BlockSpec((None, MOBA_HEADS, nblk, 8, BLK), lambda b, t: (b, 0, t, 0, 0)),
        nat(RET_WIDTH), tr(RET_WIDTH), nat(RET_WIDTH), nat(RET_WIDTH),
    )
    consts = (g, wqT, wk, kpos, wvT, wrq, wrkT, wrv, wrg)
    in_specs = [pl.BlockSpec((None, ROW_TILE, D), lambda b, t: (b, t, 0))] + [
        _const_spec(a.shape) for a in consts]
    return pl.pallas_call(
        _in_proj_kernel, grid=grid, in_specs=in_specs, out_specs=out_specs, out_shape=out_shape,
        scratch_shapes=[pltpu.VMEM((nb, QW), jnp.float32)],
        compiler_params=pltpu.CompilerParams(
            dimension_semantics=("arbitrary", "arbitrary"), vmem_limit_bytes=VMEM_LIMIT),
        name="in_proj",
    )(x, *consts)


def _moba_kernel(slopes_ref, ti_ref, tj_ref, qT_ref, k_ref, vT_ref, sel_ref, o_ref,
                 m_ref, acc_ref, s_ref, smax_ref, alpha_ref, p_ref,
                 s_own_ref, smax_own_ref, alpha_own_ref, p_own_ref, *, n_past_groups):
    h = pl.program_id(1)
    nb = k_ref.shape[0]
    slope = slopes_ref[h]
    kpos = lax.broadcasted_iota(jnp.int32, (BLK, BLK), 0)
    qpos = lax.broadcasted_iota(jnp.int32, (BLK, BLK), 1)
    qterm = -slope * lax.broadcasted_iota(jnp.int32, (1, BLK), 1).astype(jnp.float32)

    m_ref[...] = jnp.full(m_ref.shape, -3e38, jnp.float32)
    acc_ref[...] = jnp.zeros(acc_ref.shape, jnp.float32)

    def own_tile(g, e):
        t = g * GROUP_OWN + e
        return t, t

    def past_tile(g, e):
        t = g * GROUP + e
        return ti_ref[t], tj_ref[t]

    def run_stream(n_groups, tile_of, own, bufs):
        s_ref, smax_ref, alpha_ref, p_ref = bufs
        GROUP = s_ref.shape[1]
        def scores(par, e, i_t, j_t):
            s = jnp.dot(k_ref[j_t], qT_ref[i_t], preferred_element_type=jnp.float32)
            if own:
                s = jnp.where(kpos <= qpos, s, NEG)
            s_ref[par, e] = s
            smax_ref[par, e:e + 1, :] = jnp.max(s, axis=0, keepdims=True)

        def values(par, e, i_t, j_t):
            pv = jnp.dot(vT_ref[j_t], p_ref[par, e * BLK:(e + 1) * BLK, :],
                         preferred_element_type=jnp.float32)
            acc_ref[i_t] = alpha_ref[par, e:e + 1, :] * acc_ref[i_t] + pv

        def softmax(par, e, i_t, j_t):
            if own:
                c = qterm
            else:
                sel = sel_ref[i_t]
                hit = (sel[0:1] == j_t) | (sel[1:2] == j_t) | (sel[2:3] == j_t)
                blk_dist = ((i_t - j_t) * MOBA_BLOCK).astype(jnp.float32)
                c = jnp.where(hit, -slope * blk_dist, NEG) + qterm
            m_old = m_ref[pl.ds(i_t, 1), :]
            m_new = jnp.maximum(m_old, smax_ref[par, e:e + 1, :] + c)
            alpha_ref[par, e:e + 1, :] = jnp.exp2(m_old - m_new)
            m_ref[pl.ds(i_t, 1), :] = m_new
            p_ref[par, e * BLK:(e + 1) * BLK, :] = (
                jnp.exp2(s_ref[par, e] - (m_new - c)).astype(jnp.bfloat16))

        def trip(g, par):
            g_next = jnp.minimum(g + 1, n_groups - 1)
            g_prev = jnp.maximum(g - 1, 0)
            for e in range(GROUP):
                scores(1 - par, e, *tile_of(g_next, e))
                softmax(par, e, *tile_of(g, e))
                values(1 - par, e, *tile_of(g_prev, e))

        alpha_ref[1] = jnp.ones(alpha_ref.shape[1:], jnp.float32)
        p_ref[1] = jnp.zeros(p_ref.shape[1:], jnp.bfloat16)
        for e in range(GROUP):
            scores(0, e, *tile_of(0, e))

        def body(g, carry):
            lax.cond(g % 2 == 0, lambda: trip(g, 0), lambda: trip(g, 1))
            return carry

        lax.fori_loop(0, n_groups, body, 0)
        for e in range(GROUP):
            values((n_groups - 1) % 2, e, *tile_of(n_groups - 1, e))

    run_stream(nb // GROUP_OWN, own_tile, True, (s_own_ref, smax_own_ref, alpha_own_ref, p_own_ref))
    run_stream(n_past_groups, past_tile, False, (s_ref, smax_ref, alpha_ref, p_ref))

    def write_out(half):
        unroll = 4
        def some(u, carry):
            for d in range(unroll):
                i = u * unroll + d
                acc = acc_ref[i]
                oT = acc[0:MOBA_HEAD_DIM] / acc[ONES_ROW:ONES_ROW + 1]
                rows = pl.ds(pl.multiple_of(i * BLK, BLK), BLK)
                o_ref[rows, half * MOBA_HEAD_DIM:(half + 1) * MOBA_HEAD_DIM] = oT.T.astype(o_ref.dtype)
            return carry
        lax.fori_loop(0, nb // unroll, some, 0)

    lax.cond(h % 2 == 0, lambda: write_out(0), lambda: write_out(1))


def _moba(slopes_l2, qT, k, vT, sel):
    B, nb, QW, _ = qT.shape
    S = nb * BLK
    assert nb % GROUP == 0
    k4 = k.reshape(B, nb, BLK, QW)
    pairs = [(i, j) for j in range(nb - 1) for i in range(j + 1, nb)]
    pairs += [(0, 0)] * (-len(pairs) % GROUP)
    ti = jnp.asarray([p[0] for p in pairs], jnp.int32)
    tj = jnp.asarray([p[1] for p in pairs], jnp.int32)
    grid_spec = pltpu.PrefetchScalarGridSpec(
        num_scalar_prefetch=3,
        grid=(B, MOBA_HEADS),
        in_specs=[
            pl.BlockSpec((None, nb, QK_PAD, BLK), lambda b, h, *_: (b, 0, h, 0)),
            pl.BlockSpec((None, nb, BLK, QK_PAD), lambda b, h, *_: (b, 0, 0, h)),
            pl.BlockSpec((None, nb, V_PAD, BLK), lambda b, h, *_: (b, 0, h, 0)),
            pl.BlockSpec((None, None, nb, 8, BLK), lambda b, h, *_: (b, h, 0, 0, 0)),
        ],
        out_specs=pl.BlockSpec((None, S, 2 * MOBA_HEAD_DIM), lambda b, h, *_: (b, 0, h // 2)),
        scratch_shapes=[
            pltpu.VMEM((nb, BLK), jnp.float32),
            pltpu.VMEM((nb, V_PAD, BLK), jnp.float32),
            pltpu.VMEM((2, GROUP, BLK, BLK), jnp.float32),
            pltpu.VMEM((2, GROUP, BLK), jnp.float32),
            pltpu.VMEM((2, GROUP, BLK), jnp.float32),
            pltpu.VMEM((2, GROUP * BLK, BLK), jnp.bfloat16),
            pltpu.VMEM((2, GROUP_OWN, BLK, BLK), jnp.float32),
            pltpu.VMEM((2, 8, BLK), jnp.float32),
            pltpu.VMEM((2, 8, BLK), jnp.float32),
            pltpu.VMEM((2, GROUP_OWN * BLK, BLK), jnp.bfloat16),
        ],
    )
    return pl.pallas_call(
        functools.partial(_moba_kernel, n_past_groups=len(pairs) // GROUP),
        grid_spec=grid_spec,
        out_shape=jax.ShapeDtypeStruct((B, S, MOBA_WIDTH), jnp.bfloat16),
        compiler_params=pltpu.CompilerParams(
            dimension_semantics=("arbitrary", "arbitrary"), vmem_limit_bytes=VMEM_LIMIT),
        name="moba",
    )(slopes_l2, ti, tj, qT, k4, vT, sel)


def _ret_kernel(cd_ref, q_ref, kT_ref, v_ref, g_ref, dintra_ref, qdec_ref, kdec_ref, gn_ref,
                o_ref, state_ref):
    c = pl.program_id(1)

    @pl.when(c == 0)
    def _():
        state_ref[...] = jnp.zeros_like(state_ref)

    for h in range(RET_HEADS):
        cols = slice(h * RET_HEAD_DIM, (h + 1) * RET_HEAD_DIM)
        q = q_ref[:, cols]
        kT = kT_ref[cols, :]
        v = v_ref[:, cols]
        state = state_ref[h]
        s = jnp.dot(q, kT, preferred_element_type=jnp.float32) * dintra_ref[h]
        inner = jnp.dot(s.astype(jnp.bfloat16), v, preferred_element_type=jnp.float32)
        qd = (q.astype(jnp.float32) * qdec_ref[h]).astype(jnp.bfloat16)
        cross = jnp.dot(qd, state.astype(jnp.bfloat16), preferred_element_type=jnp.float32)
        kd = (kT.astype(jnp.float32) * kdec_ref[h]).astype(jnp.bfloat16)
        state_ref[h] = cd_ref[h] * state + jnp.dot(kd, v, preferred_element_type=jnp.float32)
        o = inner + cross
        mu = jnp.mean(o, axis=-1, keepdims=True)
        d = o - mu
        var = jnp.mean(d * d, axis=-1, keepdims=True)
        gate = g_ref[:, cols].astype(jnp.float32)
        y = d * lax.rsqrt(var + GN_EPS) * gn_ref[:, cols] * (gate * jax.nn.sigmoid(gate))
        o_ref[:, cols] = y.astype(o_ref.dtype)


def _retention(rq, rkT, rv, rg, ret_norm):
    B, S, W = rq.shape
    nc = S // BLK
    H = RET_HEADS
    C = RET_CHUNK
    f32 = jnp.float32
    gamma = 1.0 - jnp.exp2(-5.0 - jnp.arange(H, dtype=f32))
    log_g = jnp.log(gamma)
    idx = jnp.arange(C, dtype=f32)
    diff = idx[:, None] - idx[None, :]
    dintra = jnp.where(diff >= 0, jnp.exp(log_g[:, None, None] * jnp.maximum(diff, 0.0)), 0.0)
    qdec = jnp.broadcast_to(jnp.exp(log_g[:, None] * (idx + 1.0))[..., None], (H, C, LANES))
    kdec = jnp.exp(log_g[:, None] * (C - 1.0 - idx))[:, None, :]
    cdec = jnp.exp(log_g * C)
    gn = ret_norm.reshape(1, W).astype(f32)

    nat = pl.BlockSpec((None, C, W), lambda b, c, s: (b, c, 0))
    const = lambda a: pl.BlockSpec(a.shape, lambda b, c, s: (0,) * a.ndim,
                                   pipeline_mode=pl.Buffered(1))
    grid_spec = pltpu.PrefetchScalarGridSpec(
        num_scalar_prefetch=1,
        grid=(B, nc),
        in_specs=[
            nat,
            pl.BlockSpec((None, None, W, C), lambda b, c, s: (b, c, 0, 0)),
            nat, nat,
            const(dintra), const(qdec), const(kdec), const(gn),
        ],
        out_specs=nat,
        scratch_shapes=[pltpu.VMEM((H, RET_HEAD_DIM, RET_HEAD_DIM), f32)],
    )
    return pl.pallas_call(
        _ret_kernel, grid_spec=grid_spec,
        out_shape=jax.ShapeDtypeStruct((B, S, W), jnp.bfloat16),
        compiler_params=pltpu.CompilerParams(
            dimension_semantics=("parallel", "arbitrary"),
            vmem_limit_bytes=VMEM_LIMIT),
        name="retention",
    )(cdec, rq, rkT, rv, rg, dintra, qdec, kdec, gn)


def _gelu_tanh(u):
    k = -2.0 * math.sqrt(2.0 / math.pi) * LOG2E
    return u / (1.0 + jnp.exp2(u * (k + (k * 0.044715) * (u * u))))


def _shift_rows(cur, prev_tail, n):
    rolled = pltpu.roll(cur, n, axis=0)
    prev_rolled = pltpu.roll(prev_tail, n, axis=0)
    row = lax.broadcasted_iota(jnp.int32, prev_tail.shape, 0)
    head = jnp.where(row < n, prev_rolled, rolled[0:8])
    return jnp.concatenate([head, rolled[8:]], axis=0)


def _tail_kernel(x_ref, moba_ref, ret_ref, p_ref, wo_ref, ffn_g_ref, wup_ref, wgate_ref,
                 convw_ref, convb_ref, wdown_ref, ple_g_ref, wple_ref, wpg_ref, fin_g_ref,
                 o_ref, y_ref, carry_ref, *, tiles_per_seq):
    t = pl.program_id(0)
    R = ROW_TILE

    @pl.when(t % tiles_per_seq == 0)
    def _():
        carry_ref[...] = jnp.zeros_like(carry_ref)

    mix = jnp.concatenate([moba_ref[...], ret_ref[...]], axis=1)
    x1 = x_ref[...] + jnp.dot(mix, wo_ref[...], preferred_element_type=jnp.float32)
    h = _rms(x1, ffn_g_ref[...]).astype(jnp.bfloat16)

    for c in range(N_FF_CHUNKS):
        up = jnp.dot(h, wup_ref[c], preferred_element_type=jnp.float32)
        gt = jnp.dot(h, wgate_ref[c], preferred_element_type=jnp.float32)
        prev_tail = carry_ref[c]
        carry_ref[c] = up[R - 8:R, :]
        w = convw_ref[c]
        u = (convb_ref[c]
             + w[0:1, :] * _shift_rows(up, prev_tail, 2)
             + w[1:2, :] * _shift_rows(up, prev_tail, 1)
             + w[2:3, :] * up)
        y_ref[:, c * FF_CHUNK:(c + 1) * FF_CHUNK] = (_gelu_tanh(u) * gt).astype(jnp.bfloat16)
    x2 = x1 + jnp.dot(y_ref[...], wdown_ref[...], preferred_element_type=jnp.float32)

    hn = _rms(x2, ple_g_ref[...]).astype(jnp.bfloat16)
    g = jax.nn.sigmoid(jnp.dot(hn, wpg_ref[...], preferred_element_type=jnp.float32))
    pe = jnp.dot(p_ref[...].astype(jnp.bfloat16), wple_ref[...], preferred_element_type=jnp.float32)
    x3 = x2 + pe * g
    o_ref[...] = _rms(x3, fin_g_ref[...])


def _tail(x2d, moba2d, ret2d, p2d, wo, ffn_g, wup, wgate, convw, convb, wdown, ple_g, wple,
          wpg, fin_g, seq_len):
    T, D = x2d.shape
    R = ROW_TILE
    row = lambda w: pl.BlockSpec((R, w), lambda t: (t, 0))
    consts = (wo, ffn_g, wup, wgate, convw, convb, wdown, ple_g, wple, wpg, fin_g)
    in_specs = [row(D), row(MOBA_WIDTH), row(RET_WIDTH), row(PLE_DIM)] + [
        _const_spec(a.shape) for a in consts]
    return pl.pallas_call(
        functools.partial(_tail_kernel, tiles_per_seq=seq_len // R),
        grid=(T // R,), in_specs=in_specs, out_specs=row(D),
        out_shape=jax.ShapeDtypeStruct((T, D), jnp.float32),
        scratch_shapes=[
            pltpu.VMEM((R, D_FF), jnp.bfloat16),
            pltpu.VMEM((N_FF_CHUNKS, 8, FF_CHUNK), jnp.float32),
        ],
        compiler_params=pltpu.CompilerParams(
            dimension_semantics=("arbitrary",), vmem_limit_bytes=VMEM_LIMIT),
        name="tail",
    )(x2d, moba2d, ret2d, p2d, *consts)


def _pad_heads(w, pad):
    d = w.shape[0]
    w3 = w.reshape(d, MOBA_HEADS, MOBA_HEAD_DIM)
    return jnp.pad(w3, ((0, 0), (0, 0), (0, pad - MOBA_HEAD_DIM))).reshape(d, MOBA_HEADS * pad)


def kernel(x, p, attn_norm, w_in, ret_norm, w_out, ffn_norm, w_up, w_gate, conv_w, conv_b, w_down,
           ple_norm, w_ple, w_ple_gate, final_norm):
    B, S, D = x.shape
    assert D == D_MODEL and S % ROW_TILE == 0 and w_in.shape[0] == 1
    bf, f32 = jnp.bfloat16, jnp.float32
    MW, RW = MOBA_WIDTH, RET_WIDTH
    w = w_in[0].astype(bf)
    wqT = w[:, 0:MW].T
    wk = _pad_heads(w[:, MW:2 * MW], QK_PAD)
    wvT = w[:, 2 * MW:3 * MW].T
    o = 3 * MW
    wrq = w[:, o:o + RW]
    wrkT = w[:, o + RW:o + 2 * RW].T
    wrv = w[:, o + 2 * RW:o + 3 * RW]
    wrg = w[:, o + 3 * RW:o + 4 * RW]
    row = lambda a: a.reshape(1, -1).astype(f32)

    slopes_l2 = jnp.exp2(-8.0 * jnp.arange(1, MOBA_HEADS + 1, dtype=f32) / MOBA_HEADS) * LOG2E
    kterm = slopes_l2[None, :] * jnp.arange(BLK, dtype=f32)[:, None]
    kterm_hi = lax.bitcast_convert_type(
        lax.bitcast_convert_type(kterm, jnp.uint32) & jnp.uint32(0xFFFF0000), f32)
    kterm_lo = kterm - kterm_hi
    kpos = jnp.zeros((BLK, MOBA_HEADS, QK_PAD), f32)
    kpos = kpos.at[:, :, ALIBI_ROW].set(kterm_hi).at[:, :, ALIBI_ROW + 1].set(kterm_lo)
    kpos = kpos.reshape(BLK, MOBA_HEADS * QK_PAD)

    qT, k, vT, sel, rq, rkT, rv, rg = _in_proj(
        x, row(attn_norm[0]), wqT, wk, kpos, wvT, wrq, wrkT, wrv, wrg)

    moba_out = _moba(slopes_l2, qT, k, vT, sel)
    ret_out = _retention(rq, rkT, rv, rg, ret_norm[0])

    wo = w_out[0].astype(bf)
    chunked = lambda a: a.reshape(D, N_FF_CHUNKS, FF_CHUNK).transpose(1, 0, 2)
    out = _tail(
        x.reshape(B * S, D), moba_out.reshape(B * S, MW), ret_out.reshape(B * S, RW),
        p[0].reshape(B * S, PLE_DIM),
        wo, row(ffn_norm[0]),
        chunked(w_up[0].astype(bf)), chunked(w_gate[0].astype(bf)),
        conv_w[0].reshape(CONV_WIDTH, N_FF_CHUNKS, FF_CHUNK).transpose(1, 0, 2).astype(f32),
        conv_b[0].reshape(N_FF_CHUNKS, 1, FF_CHUNK).astype(f32),
        w_down[0].astype(bf),
        row(ple_norm[0]), w_ple[0].astype(bf), w_ple_gate[0].astype(bf), row(final_norm),
        seq_len=S)
    return out.reshape(B, S, D)
```

```python
import functools
import math

import jax
import jax.numpy as jnp
from jax import lax
from jax.experimental import pallas as pl
from jax.experimental.pallas import tpu as pltpu

D_MODEL = 1024
PLE_DIM = 256
MOBA_HEADS = 8
MOBA_HEAD_DIM = 64
MOBA_WIDTH = MOBA_HEADS * MOBA_HEAD_DIM
MOBA_BLOCK = 256
MOBA_TOPK = 3
RET_HEADS = 4
RET_HEAD_DIM = 128
RET_WIDTH = RET_HEADS * RET_HEAD_DIM
RET_CHUNK = 256
D_FF = 2816
CONV_WIDTH = 3
RMS_EPS = 1e-6
GN_EPS = 1e-5

BLK = 256
LANES = 128
FF_CHUNK = 256
N_FF_CHUNKS = D_FF // FF_CHUNK
ROW_TILE = 512
VMEM_LIMIT = 56 * 1024 * 1024
NEG = -1e30
LOG2E = math.log2(math.e)

QK_PAD = LANES
ALIBI_ROW = MOBA_HEAD_DIM
V_PAD = 80
ONES_ROW = MOBA_HEAD_DIM
GROUP = 16
GROUP_OWN = 8

_NT = (((1,), (1,)), ((), ()))


def _const_spec(shape):
    nd = len(shape)
    return pl.BlockSpec(shape, lambda *_: (0,) * nd, pipeline_mode=pl.Buffered(1))


def _rms(x, g):
    ms = jnp.mean(x * x, axis=-1, keepdims=True)
    return x * lax.rsqrt(ms + RMS_EPS) * g


def _in_proj_kernel(x_ref, g_ref, wqT_ref, wk_ref, kpos_ref, wvT_ref,
                    wrq_ref, wrkT_ref, wrv_ref, wrg_ref,
                    qT_ref, k_ref, vT_ref, sel_ref, rq_ref, rkT_ref, rv_ref, rg_ref, kmean_ref):
    t = pl.program_id(1)
    h = _rms(x_ref[...], g_ref[...]).astype(jnp.bfloat16)
    nblk = ROW_TILE // BLK
    nb = kmean_ref.shape[0]

    @pl.when(t == 0)
    def _():
        kmean_ref[...] = jnp.zeros_like(kmean_ref)

    def nt(w_ref):
        return lax.dot_general(w_ref[...], h, _NT, preferred_element_type=jnp.float32)

    def nn(w_ref):
        return jnp.dot(h, w_ref[...], preferred_element_type=jnp.float32)

    Dh = MOBA_HEAD_DIM
    qT = (nt(wqT_ref) * (Dh ** -0.5 * LOG2E)).astype(jnp.bfloat16)
    vT = nt(wvT_ref).astype(jnp.bfloat16)
    rkT = (nt(wrkT_ref) * (RET_HEAD_DIM ** -0.5)).astype(jnp.bfloat16)
    q_pad = (lax.broadcasted_iota(jnp.int32, (QK_PAD - Dh, BLK), 0) < 2).astype(jnp.bfloat16)
    v_pad = (lax.broadcasted_iota(jnp.int32, (V_PAD - Dh, BLK), 0) < 1).astype(jnp.bfloat16)
    for b in range(nblk):
        sl = slice(b * BLK, (b + 1) * BLK)
        for a in range(MOBA_HEADS):
            qT_ref[b, a * QK_PAD:a * QK_PAD + Dh, :] = qT[a * Dh:(a + 1) * Dh, sl]
            qT_ref[b, a * QK_PAD + Dh:(a + 1) * QK_PAD, :] = q_pad
            vT_ref[b, a * V_PAD:a * V_PAD + Dh, :] = vT[a * Dh:(a + 1) * Dh, sl]
            vT_ref[b, a * V_PAD + Dh:(a + 1) * V_PAD, :] = v_pad
        rkT_ref[b] = rkT[:, sl]
    k = nn(wk_ref)
    for b in range(nblk):
        kb = k[b * BLK:(b + 1) * BLK]
        kmean_ref[pl.ds(t * nblk + b, 1), :] = jnp.mean(kb, axis=0, keepdims=True)
        k_ref[b * BLK:(b + 1) * BLK, :] = (kb + kpos_ref[...]).astype(jnp.bfloat16)
    rq_ref[...] = nn(wrq_ref).astype(jnp.bfloat16)
    rv_ref[...] = nn(wrv_ref).astype(jnp.bfloat16)
    rg_ref[...] = nn(wrg_ref).astype(jnp.bfloat16)

    km = kmean_ref[...]
    km_hi = km.astype(jnp.bfloat16)
    km_lo = (km - km_hi.astype(jnp.float32)).astype(jnp.bfloat16)
    blk_id = lax.broadcasted_iota(jnp.int32, (nb, BLK), 0)
    none = jnp.full((8 - MOBA_TOPK, BLK), nb, jnp.int32)
    for b in range(nblk):
        i = t * nblk + b
        for a in range(MOBA_HEADS):
            dims = slice(a * QK_PAD, a * QK_PAD + Dh)
            qTh = qT[a * Dh:(a + 1) * Dh, b * BLK:(b + 1) * BLK]
            gate = (jnp.dot(km_hi[:, dims], qTh, preferred_element_type=jnp.float32)
                    + jnp.dot(km_lo[:, dims], qTh, preferred_element_type=jnp.float32))
            gate = jnp.where(blk_id < i, gate, -jnp.inf)
            picks = []
            for _ in range(MOBA_TOPK):
                top = jnp.max(gate, axis=0, keepdims=True)
                is_top = (gate == top) & (top > -jnp.inf)
                first = jnp.min(jnp.where(is_top, blk_id, nb), axis=0, keepdims=True)
                picks.append(first)
                gate = jnp.where(blk_id == first, -jnp.inf, gate)
            sel_ref[a, b] = jnp.concatenate(picks + [none], axis=0)


def _in_proj(x, g, wqT, wk, kpos, wvT, wrq, wrkT, wrv, wrg):
    B, S, D = x.shape
    nb = S // BLK
    nblk = ROW_TILE // BLK
    grid = (B, S // ROW_TILE)
    bf = jnp.bfloat16
    QW, VW = MOBA_HEADS * QK_PAD, MOBA_HEADS * V_PAD
    nat = lambda w: pl.BlockSpec((None, ROW_TILE, w), lambda b, t: (b, t, 0))
    tr = lambda w: pl.BlockSpec((None, nblk, w, BLK), lambda b, t: (b, t, 0, 0))
    out_shape = (
        jax.ShapeDtypeStruct((B, nb, QW, BLK), bf),
        jax.ShapeDtypeStruct((B, S, QW), bf),
        jax.ShapeDtypeStruct((B, nb, VW, BLK), bf),
        jax.ShapeDtypeStruct((B, MOBA_HEADS, nb, 8, BLK), jnp.int32),
        jax.ShapeDtypeStruct((B, S, RET_WIDTH), bf),
        jax.ShapeDtypeStruct((B, nb, RET_WIDTH, BLK), bf),
        jax.ShapeDtypeStruct((B, S, RET_WIDTH), bf),
        jax.ShapeDtypeStruct((B, S, RET_WIDTH), bf),
    )
    out_specs = (
        tr(QW), nat(QW), tr(VW),
        pl.BlockSpec((None, MOBA_HEADS, nblk, 8, BLK), lambda b, t: (b, 0, t, 0, 0)),
        nat(RET_WIDTH), tr(RET_WIDTH), nat(RET_WIDTH), nat(RET_WIDTH),
    )
    consts = (g, wqT, wk, kpos, wvT, wrq, wrkT, wrv, wrg)
    in_specs = [pl.BlockSpec((None, ROW_TILE, D), lambda b, t: (b, t, 0))] + [
        _const_spec(a.shape) for a in consts]
    return pl.pallas_call(
        _in_proj_kernel, grid=grid, in_specs=in_specs, out_specs=out_specs, out_shape=out_shape,
        scratch_shapes=[pltpu.VMEM((nb, QW), jnp.float32)],
        compiler_params=pltpu.CompilerParams(
            dimension_semantics=("arbitrary", "arbitrary"), vmem_limit_bytes=VMEM_LIMIT),
        name="in_proj",
    )(x, *consts)


def _moba_kernel(slopes_ref, ti_ref, tj_ref, qT_ref, k_ref, vT_ref, sel_ref, o_ref,
                 m_ref, acc_ref, s_ref, smax_ref, alpha_ref, p_ref,
                 s_own_ref, smax_own_ref, alpha_own_ref, p_own_ref, *, n_past_groups):
    h = pl.program_id(1)
    nb = k_ref.shape[0]
    slope = slopes_ref[h]
    kpos = lax.broadcasted_iota(jnp.int32, (BLK, BLK), 0)
    qpos = lax.broadcasted_iota(jnp.int32, (BLK, BLK), 1)
    qterm = -slope * lax.broadcasted_iota(jnp.int32, (1, BLK), 1).astype(jnp.float32)

    m_ref[...] = jnp.full(m_ref.shape, -3e38, jnp.float32)
    acc_ref[...] = jnp.zeros(acc_ref.shape, jnp.float32)

    def own_tile(g, e):
        t = g * GROUP_OWN + e
        return t, t

    def past_tile(g, e):
        t = g * GROUP + e
        return ti_ref[t], tj_ref[t]

    def run_stream(n_groups, tile_of, own, bufs):
        s_ref, smax_ref, alpha_ref, p_ref = bufs
        GROUP = s_ref.shape[1]
        def scores(par, e, i_t, j_t):
            s = jnp.dot(k_ref[j_t], qT_ref[i_t], preferred_element_type=jnp.float32)
            if own:
                s = jnp.where(kpos <= qpos, s, NEG)
            s_ref[par, e] = s
            smax_ref[par, e:e + 1, :] = jnp.max(s, axis=0, keepdims=True)

        def values(par, e, i_t, j_t):
            pv = jnp.dot(vT_ref[j_t], p_ref[par, e * BLK:(e + 1) * BLK, :],
                         preferred_element_type=jnp.float32)
            acc_ref[i_t] = alpha_ref[par, e:e + 1, :] * acc_ref[i_t] + pv

        def softmax(par, e, i_t, j_t):
            if own:
                c = qterm
            else:
                sel = sel_ref[i_t]
                hit = (sel[0:1] == j_t) | (sel[1:2] == j_t) | (sel[2:3] == j_t)
                blk_dist = ((i_t - j_t) * MOBA_BLOCK).astype(jnp.float32)
                c = jnp.where(hit, -slope * blk_dist, NEG) + qterm
            m_old = m_ref[pl.ds(i_t, 1), :]
            m_new = jnp.maximum(m_old, smax_ref[par, e:e + 1, :] + c)
            alpha_ref[par, e:e + 1, :] = jnp.exp2(m_old - m_new)
            m_ref[pl.ds(i_t, 1), :] = m_new
            p_ref[par, e * BLK:(e + 1) * BLK, :] = (
                jnp.exp2(s_ref[par, e] - (m_new - c)).astype(jnp.bfloat16))

        def trip(g, par):
            g_next = jnp.minimum(g + 1, n_groups - 1)
            g_prev = jnp.maximum(g - 1, 0)
            for e in range(-2, GROUP):
                if e + 2 < GROUP:
                    scores(1 - par, e + 2, *tile_of(g_next, e + 2))
                if e >= 0:
                    values(1 - par, e, *tile_of(g_prev, e))
                    softmax(par, e, *tile_of(g, e))

        alpha_ref[1] = jnp.ones(alpha_ref.shape[1:], jnp.float32)
        p_ref[1] = jnp.zeros(p_ref.shape[1:], jnp.bfloat16)
        for e in range(GROUP):
            scores(0, e, *tile_of(0, e))

        def body(w, carry):
            trip(2 * w, 0)
            trip(2 * w + 1, 1)
            return carry

        assert n_groups % 2 == 0
        lax.fori_loop(0, n_groups // 2, body, 0)
        for e in range(GROUP):
            values((n_groups - 1) % 2, e, *tile_of(n_groups - 1, e))

    run_stream(nb // GROUP_OWN, own_tile, True, (s_own_ref, smax_own_ref, alpha_own_ref, p_own_ref))
    run_stream(n_past_groups, past_tile, False, (s_ref, smax_ref, alpha_ref, p_ref))

    def write_out(half):
        unroll = 4
        def some(u, carry):
            for d in range(unroll):
                i = u * unroll + d
                acc = acc_ref[i]
                oT = acc[0:MOBA_HEAD_DIM] / acc[ONES_ROW:ONES_ROW + 1]
                rows = pl.ds(pl.multiple_of(i * BLK, BLK), BLK)
                o_ref[rows, half * MOBA_HEAD_DIM:(half + 1) * MOBA_HEAD_DIM] = oT.T.astype(o_ref.dtype)
            return carry
        lax.fori_loop(0, nb // unroll, some, 0)

    lax.cond(h % 2 == 0, lambda: write_out(0), lambda: write_out(1))


def _moba(slopes_l2, qT, k, vT, sel):
    B, nb, QW, _ = qT.shape
    S = nb * BLK
    assert nb % GROUP == 0
    k4 = k.reshape(B, nb, BLK, QW)
    pairs = [(i, j) for j in range(nb - 1) for i in range(j + 1, nb)]
    pairs += [(0, 0)] * (-len(pairs) % GROUP)
    ti = jnp.asarray([p[0] for p in pairs], jnp.int32)
    tj = jnp.asarray([p[1] for p in pairs], jnp.int32)
    grid_spec = pltpu.PrefetchScalarGridSpec(
        num_scalar_prefetch=3,
        grid=(B, MOBA_HEADS),
        in_specs=[
            pl.BlockSpec((None, nb, QK_PAD, BLK), lambda b, h, *_: (b, 0, h, 0)),
            pl.BlockSpec((None, nb, BLK, QK_PAD), lambda b, h, *_: (b, 0, 0, h)),
            pl.BlockSpec((None, nb, V_PAD, BLK), lambda b, h, *_: (b, 0, h, 0)),
            pl.BlockSpec((None, None, nb, 8, BLK), lambda b, h, *_: (b, h, 0, 0, 0)),
        ],
        out_specs=pl.BlockSpec((None, S, 2 * MOBA_HEAD_DIM), lambda b, h, *_: (b, 0, h // 2)),
        scratch_shapes=[
            pltpu.VMEM((nb, BLK), jnp.float32),
            pltpu.VMEM((nb, V_PAD, BLK), jnp.float32),
            pltpu.VMEM((2, GROUP, BLK, BLK), jnp.float32),
            pltpu.VMEM((2, GROUP, BLK), jnp.float32),
            pltpu.VMEM((2, GROUP, BLK), jnp.float32),
            pltpu.VMEM((2, GROUP * BLK, BLK), jnp.bfloat16),
            pltpu.VMEM((2, GROUP_OWN, BLK, BLK), jnp.float32),
            pltpu.VMEM((2, 8, BLK), jnp.float32),
            pltpu.VMEM((2, 8, BLK), jnp.float32),
            pltpu.VMEM((2, GROUP_OWN * BLK, BLK), jnp.bfloat16),
        ],
    )
    return pl.pallas_call(
        functools.partial(_moba_kernel, n_past_groups=len(pairs) // GROUP),
        grid_spec=grid_spec,
        out_shape=jax.ShapeDtypeStruct((B, S, MOBA_WIDTH), jnp.bfloat16),
        compiler_params=pltpu.CompilerParams(
            dimension_semantics=("arbitrary", "arbitrary"), vmem_limit_bytes=VMEM_LIMIT),
        name="moba",
    )(slopes_l2, ti, tj, qT, k4, vT, sel)


def _ret_kernel(cd_ref, q_ref, kT_ref, v_ref, g_ref, dintra_ref, qdec_ref, kdec_ref, gn_ref,
                o_ref, state_ref):
    c = pl.program_id(1)

    @pl.when(c == 0)
    def _():
        state_ref[...] = jnp.zeros_like(state_ref)

    for h in range(RET_HEADS):
        cols = slice(h * RET_HEAD_DIM, (h + 1) * RET_HEAD_DIM)
        q = q_ref[:, cols]
        kT = kT_ref[cols, :]
        v = v_ref[:, cols]
        state = state_ref[h]
        s = jnp.dot(q, kT, preferred_element_type=jnp.float32) * dintra_ref[h]
        inner = jnp.dot(s.astype(jnp.bfloat16), v, preferred_element_type=jnp.float32)
        qd = (q.astype(jnp.float32) * qdec_ref[h]).astype(jnp.bfloat16)
        cross = jnp.dot(qd, state.astype(jnp.bfloat16), preferred_element_type=jnp.float32)
        kd = (kT.astype(jnp.float32) * kdec_ref[h]).astype(jnp.bfloat16)
        state_ref[h] = cd_ref[h] * state + jnp.dot(kd, v, preferred_element_type=jnp.float32)
        o = inner + cross
        mu = jnp.mean(o, axis=-1, keepdims=True)
        d = o - mu
        var = jnp.mean(d * d, axis=-1, keepdims=True)
        gate = g_ref[:, cols].astype(jnp.float32)
        y = d * lax.rsqrt(var + GN_EPS) * gn_ref[:, cols] * (gate * jax.nn.sigmoid(gate))
        o_ref[:, cols] = y.astype(o_ref.dtype)


def _retention(rq, rkT, rv, rg, ret_norm):
    B, S, W = rq.shape
    nc = S // BLK
    H = RET_HEADS
    C = RET_CHUNK
    f32 = jnp.float32
    gamma = 1.0 - jnp.exp2(-5.0 - jnp.arange(H, dtype=f32))
    log_g = jnp.log(gamma)
    idx = jnp.arange(C, dtype=f32)
    diff = idx[:, None] - idx[None, :]
    dintra = jnp.where(diff >= 0, jnp.exp(log_g[:, None, None] * jnp.maximum(diff, 0.0)), 0.0)
    qdec = jnp.broadcast_to(jnp.exp(log_g[:, None] * (idx + 1.0))[..., None], (H, C, LANES))
    kdec = jnp.exp(log_g[:, None] * (C - 1.0 - idx))[:, None, :]
    cdec = jnp.exp(log_g * C)
    gn = ret_norm.reshape(1, W).astype(f32)

    nat = pl.BlockSpec((None, C, W), lambda b, c, s: (b, c, 0))
    const = lambda a: pl.BlockSpec(a.shape, lambda b, c, s: (0,) * a.ndim,
                                   pipeline_mode=pl.Buffered(1))
    grid_spec = pltpu.PrefetchScalarGridSpec(
        num_scalar_prefetch=1,
        grid=(B, nc),
        in_specs=[
            nat,
            pl.BlockSpec((None, None, W, C), lambda b, c, s: (b, c, 0, 0)),
            nat, nat,
            const(dintra), const(qdec), const(kdec), const(gn),
        ],
        out_specs=nat,
        scratch_shapes=[pltpu.VMEM((H, RET_HEAD_DIM, RET_HEAD_DIM), f32)],
    )
    return pl.pallas_call(
        _ret_kernel, grid_spec=grid_spec,
        out_shape=jax.ShapeDtypeStruct((B, S, W), jnp.bfloat16),
        compiler_params=pltpu.CompilerParams(
            dimension_semantics=("parallel", "arbitrary"),
            vmem_limit_bytes=VMEM_LIMIT),
        name="retention",
    )(cdec, rq, rkT, rv, rg, dintra, qdec, kdec, gn)


def _gelu_tanh(u):
    k = -2.0 * math.sqrt(2.0 / math.pi) * LOG2E
    return u / (1.0 + jnp.exp2(u * (k + (k * 0.044715) * (u * u))))


def _shift_rows(cur, prev_tail, n):
    rolled = pltpu.roll(cur, n, axis=0)
    prev_rolled = pltpu.roll(prev_tail, n, axis=0)
    row = lax.broadcasted_iota(jnp.int32, prev_tail.shape, 0)
    head = jnp.where(row < n, prev_rolled, rolled[0:8])
    return jnp.concatenate([head, rolled[8:]], axis=0)


def _tail_kernel(x_ref, moba_ref, ret_ref, p_ref, wo_ref, ffn_g_ref, wup_ref, wgate_ref,
                 convw_ref, convb_ref, wdown_ref, ple_g_ref, wple_ref, wpg_ref, fin_g_ref,
                 o_ref, y_ref, carry_ref, *, tiles_per_seq):
    t = pl.program_id(0)
    R = ROW_TILE

    @pl.when(t % tiles_per_seq == 0)
    def _():
        carry_ref[...] = jnp.zeros_like(carry_ref)

    mix = jnp.concatenate([moba_ref[...], ret_ref[...]], axis=1)
    x1 = x_ref[...] + jnp.dot(mix, wo_ref[...], preferred_element_type=jnp.float32)
    h = _rms(x1, ffn_g_ref[...]).astype(jnp.bfloat16)

    for c in range(N_FF_CHUNKS):
        up = jnp.dot(h, wup_ref[c], preferred_element_type=jnp.float32)
        gt = jnp.dot(h, wgate_ref[c], preferred_element_type=jnp.float32)
        prev_tail = carry_ref[c]
        carry_ref[c] = up[R - 8:R, :]
        w = convw_ref[c]
        u = (convb_ref[c]
             + w[0:1, :] * _shift_rows(up, prev_tail, 2)
             + w[1:2, :] * _shift_rows(up, prev_tail, 1)
             + w[2:3, :] * up)
        y_ref[:, c * FF_CHUNK:(c + 1) * FF_CHUNK] = (_gelu_tanh(u) * gt).astype(jnp.bfloat16)
    x2 = x1 + jnp.dot(y_ref[...], wdown_ref[...], preferred_element_type=jnp.float32)

    hn = _rms(x2, ple_g_ref[...]).astype(jnp.bfloat16)
    g = jax.nn.sigmoid(jnp.dot(hn, wpg_ref[...], preferred_element_type=jnp.float32))
    pe = jnp.dot(p_ref[...].astype(jnp.bfloat16), wple_ref[...], preferred_element_type=jnp.float32)
    x3 = x2 + pe * g
    o_ref[...] = _rms(x3, fin_g_ref[...])


def _tail(x2d, moba2d, ret2d, p2d, wo, ffn_g, wup, wgate, convw, convb, wdown, ple_g, wple,
          wpg, fin_g, seq_len):
    T, D = x2d.shape
    R = ROW_TILE
    row = lambda w: pl.BlockSpec((R, w), lambda t: (t, 0))
    consts = (wo, ffn_g, wup, wgate, convw, convb, wdown, ple_g, wple, wpg, fin_g)
    in_specs = [row(D), row(MOBA_WIDTH), row(RET_WIDTH), row(PLE_DIM)] + [
        _const_spec(a.shape) for a in consts]
    return pl.pallas_call(
        functools.partial(_tail_kernel, tiles_per_seq=seq_len // R),
        grid=(T // R,), in_specs=in_specs, out_specs=row(D),
        out_shape=jax.ShapeDtypeStruct((T, D), jnp.float32),
        scratch_shapes=[
            pltpu.VMEM((R, D_FF), jnp.bfloat16),
            pltpu.VMEM((N_FF_CHUNKS, 8, FF_CHUNK), jnp.float32),
        ],
        compiler_params=pltpu.CompilerParams(
            dimension_semantics=("arbitrary",), vmem_limit_bytes=VMEM_LIMIT),
        name="tail",
    )(x2d, moba2d, ret2d, p2d, *consts)


def _pad_heads(w, pad):
    d = w.shape[0]
    w3 = w.reshape(d, MOBA_HEADS, MOBA_HEAD_DIM)
    return jnp.pad(w3, ((0, 0), (0, 0), (0, pad - MOBA_HEAD_DIM))).reshape(d, MOBA_HEADS * pad)


def kernel(x, p, attn_norm, w_in, ret_norm, w_out, ffn_norm, w_up, w_gate, conv_w, conv_b, w_down,
           ple_norm, w_ple, w_ple_gate, final_norm):
    B, S, D = x.shape
    assert D == D_MODEL and S % ROW_TILE == 0 and w_in.shape[0] == 1
    bf, f32 = jnp.bfloat16, jnp.float32
    MW, RW = MOBA_WIDTH, RET_WIDTH
    w = w_in[0].astype(bf)
    wqT = w[:, 0:MW].T
    wk = _pad_heads(w[:, MW:2 * MW], QK_PAD)
    wvT = w[:, 2 * MW:3 * MW].T
    o = 3 * MW
    wrq = w[:, o:o + RW]
    wrkT = w[:, o + RW:o + 2 * RW].T
    wrv = w[:, o + 2 * RW:o + 3 * RW]
    wrg = w[:, o + 3 * RW:o + 4 * RW]
    row = lambda a: a.reshape(1, -1).astype(f32)

    slopes_l2 = jnp.exp2(-8.0 * jnp.arange(1, MOBA_HEADS + 1, dtype=f32) / MOBA_HEADS) * LOG2E
    kterm = slopes_l2[None, :] * jnp.arange(BLK, dtype=f32)[:, None]
    kterm_hi = lax.bitcast_convert_type(
        lax.bitcast_convert_type(kterm, jnp.uint32) & jnp.uint32(0xFFFF0000), f32)
    kterm_lo = kterm - kterm_hi
    kpos = jnp.zeros((BLK, MOBA_HEADS, QK_PAD), f32)
    kpos = kpos.at[:, :, ALIBI_ROW].set(kterm_hi).at[:, :, ALIBI_ROW + 1].set(kterm_lo)
    kpos = kpos.reshape(BLK, MOBA_HEADS * QK_PAD)

    qT, k, vT, sel, rq, rkT, rv, rg = _in_proj(
        x, row(attn_norm[0]), wqT, wk, kpos, wvT, wrq, wrkT, wrv, wrg)

    moba_out = _moba(slopes_l2, qT, k, vT, sel)
    ret_out = _retention(rq, rkT, rv, rg, ret_norm[0])

    wo = w_out[0].astype(bf)
    chunked = lambda a: a.reshape(D, N_FF_CHUNKS, FF_CHUNK).transpose(1, 0, 2)
    out = _tail(
        x.reshape(B * S, D), moba_out.reshape(B * S, MW), ret_out.reshape(B * S, RW),
        p[0].reshape(B * S, PLE_DIM),
        wo, row(ffn_norm[0]),
        chunked(w_up[0].astype(bf)), chunked(w_gate[0].astype(bf)),
        conv_w[0].reshape(CONV_WIDTH, N_FF_CHUNKS, FF_CHUNK).transpose(1, 0, 2).astype(f32),
        conv_b[0].reshape(N_FF_CHUNKS, 1, FF_CHUNK).astype(f32),
        w_down[0].astype(bf),
        row(ple_norm[0]), w_ple[0].astype(bf), w_ple_gate[0].astype(bf), row(final_norm),
        seq_len=S)
    return out.reshape(B, S, D)
```

```python
import functools
import math

import jax
import jax.numpy as jnp
from jax import lax
from jax.experimental import pallas as pl
from jax.experimental.pallas import tpu as pltpu

D_MODEL = 1024
PLE_DIM = 256
MOBA_HEADS = 8
MOBA_HEAD_DIM = 64
MOBA_WIDTH = MOBA_HEADS * MOBA_HEAD_DIM
MOBA_BLOCK = 256
MOBA_TOPK = 3
RET_HEADS = 4
RET_HEAD_DIM = 128
RET_WIDTH = RET_HEADS * RET_HEAD_DIM
RET_CHUNK = 256
D_FF = 2816
CONV_WIDTH = 3
RMS_EPS = 1e-6
GN_EPS = 1e-5

BLK = 256
LANES = 128
FF_CHUNK = 256
N_FF_CHUNKS = D_FF // FF_CHUNK
ROW_TILE = 512
VMEM_LIMIT = 56 * 1024 * 1024
NEG = -1e30
LOG2E = math.log2(math.e)

QK_PAD = LANES
ALIBI_ROW = MOBA_HEAD_DIM
V_PAD = 80
ONES_ROW = MOBA_HEAD_DIM
GROUP = 16
GROUP_OWN = 8

_NT = (((1,), (1,)), ((), ()))


def _const_spec(shape):
    nd = len(shape)
    return pl.BlockSpec(shape, lambda *_: (0,) * nd, pipeline_mode=pl.Buffered(1))


def _rms(x, g):
    ms = jnp.mean(x * x, axis=-1, keepdims=True)
    return x * lax.rsqrt(ms + RMS_EPS) * g


def _in_proj_kernel(x_ref, g_ref, wqT_ref, wk_ref, kpos_ref, wvT_ref,
                    wrq_ref, wrkT_ref, wrv_ref, wrg_ref,
                    qT_ref, k_ref, vT_ref, sel_ref, rq_ref, rkT_ref, rv_ref, rg_ref, kmean_ref):
    t = pl.program_id(1)
    h = _rms(x_ref[...], g_ref[...]).astype(jnp.bfloat16)
    nblk = ROW_TILE // BLK
    nb = kmean_ref.shape[0]

    @pl.when(t == 0)
    def _():
        kmean_ref[...] = jnp.zeros_like(kmean_ref)

    def nt(w_ref):
        return lax.dot_general(w_ref[...], h, _NT, preferred_element_type=jnp.float32)

    def nn(w_ref):
        return jnp.dot(h, w_ref[...], preferred_element_type=jnp.float32)

    Dh = MOBA_HEAD_DIM
    qT = (nt(wqT_ref) * (Dh ** -0.5 * LOG2E)).astype(jnp.bfloat16)
    vT = nt(wvT_ref).astype(jnp.bfloat16)
    rkT = (nt(wrkT_ref) * (RET_HEAD_DIM ** -0.5)).astype(jnp.bfloat16)
    q_pad = (lax.broadcasted_iota(jnp.int32, (QK_PAD - Dh, BLK), 0) < 2).astype(jnp.bfloat16)
    v_pad = (lax.broadcasted_iota(jnp.int32, (V_PAD - Dh, BLK), 0) < 1).astype(jnp.bfloat16)
    for b in range(nblk):
        sl = slice(b * BLK, (b + 1) * BLK)
        for a in range(MOBA_HEADS):
            qT_ref[b, a * QK_PAD:a * QK_PAD + Dh, :] = qT[a * Dh:(a + 1) * Dh, sl]
            qT_ref[b, a * QK_PAD + Dh:(a + 1) * QK_PAD, :] = q_pad
            vT_ref[b, a * V_PAD:a * V_PAD + Dh, :] = vT[a * Dh:(a + 1) * Dh, sl]
            vT_ref[b, a * V_PAD + Dh:(a + 1) * V_PAD, :] = v_pad
        rkT_ref[b] = rkT[:, sl]
    k = nn(wk_ref)
    for b in range(nblk):
        kb = k[b * BLK:(b + 1) * BLK]
        kmean_ref[pl.ds(t * nblk + b, 1), :] = jnp.mean(kb, axis=0, keepdims=True)
        k_ref[b * BLK:(b + 1) * BLK, :] = (kb + kpos_ref[...]).astype(jnp.bfloat16)
    rq_ref[...] = nn(wrq_ref).astype(jnp.bfloat16)
    rv_ref[...] = nn(wrv_ref).astype(jnp.bfloat16)
    rg_ref[...] = nn(wrg_ref).astype(jnp.bfloat16)

    km = kmean_ref[...]
    km_hi = km.astype(jnp.bfloat16)
    km_lo = (km - km_hi.astype(jnp.float32)).astype(jnp.bfloat16)
    blk_id = lax.broadcasted_iota(jnp.int32, (nb, BLK), 0)
    none = jnp.full((8 - MOBA_TOPK, BLK), nb, jnp.int32)
    for b in range(nblk):
        i = t * nblk + b
        for a in range(MOBA_HEADS):
            dims = slice(a * QK_PAD, a * QK_PAD + Dh)
            qTh = qT[a * Dh:(a + 1) * Dh, b * BLK:(b + 1) * BLK]
            gate = (jnp.dot(km_hi[:, dims], qTh, preferred_element_type=jnp.float32)
                    + jnp.dot(km_lo[:, dims], qTh, preferred_element_type=jnp.float32))
            gate = jnp.where(blk_id < i, gate, -jnp.inf)
            picks = []
            for _ in range(MOBA_TOPK):
                top = jnp.max(gate, axis=0, keepdims=True)
                is_top = (gate == top) & (top > -jnp.inf)
                first = jnp.min(jnp.where(is_top, blk_id, nb), axis=0, keepdims=True)
                picks.append(first)
                gate = jnp.where(blk_id == first, -jnp.inf, gate)
            sel_ref[a, b] = jnp.concatenate(picks + [none], axis=0)


def _in_proj(x, g, wqT, wk, kpos, wvT, wrq, wrkT, wrv, wrg):
    B, S, D = x.shape
    nb = S // BLK
    nblk = ROW_TILE // BLK
    grid = (B, S // ROW_TILE)
    bf = jnp.bfloat16
    QW, VW = MOBA_HEADS * QK_PAD, MOBA_HEADS * V_PAD
    nat = lambda w: pl.BlockSpec((None, ROW_TILE, w), lambda b, t: (b, t, 0))
    tr = lambda w: pl.BlockSpec((None, nblk, w, BLK), lambda b, t: (b, t, 0, 0))
    out_shape = (
        jax.ShapeDtypeStruct((B, nb, QW, BLK), bf),
        jax.ShapeDtypeStruct((B, S, QW), bf),
        jax.ShapeDtypeStruct((B, nb, VW, BLK), bf),
        jax.ShapeDtypeStruct((B, MOBA_HEADS, nb, 8, BLK), jnp.int32),
        jax.ShapeDtypeStruct((B, S, RET_WIDTH), bf),
        jax.ShapeDtypeStruct((B, nb, RET_WIDTH, BLK), bf),
        jax.ShapeDtypeStruct((B, S, RET_WIDTH), bf),
        jax.ShapeDtypeStruct((B, S, RET_WIDTH), bf),
    )
    out_specs = (
        tr(QW), nat(QW), tr(VW),
        pl.BlockSpec((None, MOBA_HEADS, nblk, 8, BLK), lambda b, t: (b, 0, t, 0, 0)),
        nat(RET_WIDTH), tr(RET_WIDTH), nat(RET_WIDTH), nat(RET_WIDTH),
    )
    consts = (g, wqT, wk, kpos, wvT, wrq, wrkT, wrv, wrg)
    in_specs = [pl.BlockSpec((None, ROW_TILE, D), lambda b, t: (b, t, 0))] + [
        _const_spec(a.shape) for a in consts]
    return pl.pallas_call(
        _in_proj_kernel, grid=grid, in_specs=in_specs, out_specs=out_specs, out_shape=out_shape,
        scratch_shapes=[pltpu.VMEM((nb, QW), jnp.float32)],
        compiler_params=pltpu.CompilerParams(
            dimension_semantics=("arbitrary", "arbitrary"), vmem_limit_bytes=VMEM_LIMIT),
        name="in_proj",
    )(x, *consts)


def _moba_kernel(slopes_ref, ti_ref, tj_ref, qT_ref, k_ref, vT_ref, sel_ref, o_ref,
                 m_ref, acc_ref, s_ref, smax_ref, alpha_ref, p_ref,
                 s_own_ref, smax_own_ref, alpha_own_ref, p_own_ref, *, n_past_groups):
    h = pl.program_id(1)
    nb = k_ref.shape[0]
    slope = slopes_ref[h]
    kpos = lax.broadcasted_iota(jnp.int32, (BLK, BLK), 0)
    qpos = lax.broadcasted_iota(jnp.int32, (BLK, BLK), 1)
    qterm = -slope * lax.broadcasted_iota(jnp.int32, (1, BLK), 1).astype(jnp.float32)

    m_ref[...] = jnp.full(m_ref.shape, -3e38, jnp.float32)
    acc_ref[...] = jnp.zeros(acc_ref.shape, jnp.float32)

    def own_tile(g, e):
        t = g * GROUP_OWN + e
        return t, t

    def past_tile(g, e):
        t = g * GROUP + e
        return ti_ref[t], tj_ref[t]

    def run_stream(n_groups, tile_of, own, bufs):
        s_ref, smax_ref, alpha_ref, p_ref = bufs
        GROUP = s_ref.shape[1]
        def scores(par, e, i_t, j_t):
            s = jnp.dot(k_ref[j_t], qT_ref[i_t], preferred_element_type=jnp.float32)
            if own:
                s = jnp.where(kpos <= qpos, s, NEG)
            s_ref[par, e] = s
            smax_ref[par, e:e + 1, :] = jnp.max(s, axis=0, keepdims=True)

        def values(par, e, i_t, j_t):
            pv = jnp.dot(vT_ref[j_t], p_ref[par, e * BLK:(e + 1) * BLK, :],
                         preferred_element_type=jnp.float32)
            acc_ref[i_t] = alpha_ref[par, e:e + 1, :] * acc_ref[i_t] + pv

        def softmax(par, e, i_t, j_t):
            if own:
                c = qterm
            else:
                sel = sel_ref[i_t]
                hit = (sel[0:1] == j_t) | (sel[1:2] == j_t) | (sel[2:3] == j_t)
                blk_dist = ((i_t - j_t) * MOBA_BLOCK).astype(jnp.float32)
                c = jnp.where(hit, -slope * blk_dist, NEG) + qterm
            m_old = m_ref[pl.ds(i_t, 1), :]
            m_new = jnp.maximum(m_old, smax_ref[par, e:e + 1, :] + c)
            alpha_ref[par, e:e + 1, :] = jnp.exp2(m_old - m_new)
            m_ref[pl.ds(i_t, 1), :] = m_new
            p_ref[par, e * BLK:(e + 1) * BLK, :] = (
                jnp.exp2(s_ref[par, e] - (m_new - c)).astype(jnp.bfloat16))

        def trip(g, par):
            g_next = jnp.minimum(g + 1, n_groups - 1)
            g_prev = jnp.maximum(g - 1, 0)
            for e in range(-2, GROUP):
                if e + 2 < GROUP:
                    scores(1 - par, e + 2, *tile_of(g_next, e + 2))
                if e >= 0:
                    values(1 - par, e, *tile_of(g_prev, e))
                    softmax(par, e, *tile_of(g, e))

        alpha_ref[1] = jnp.ones(alpha_ref.shape[1:], jnp.float32)
        p_ref[1] = jnp.zeros(p_ref.shape[1:], jnp.bfloat16)
        for e in range(GROUP):
            scores(0, e, *tile_of(0, e))

        def body(w, carry):
            trip(2 * w, 0)
            trip(2 * w + 1, 1)
            return carry

        assert n_groups % 2 == 0
        lax.fori_loop(0, n_groups // 2, body, 0)
        for e in range(GROUP):
            values((n_groups - 1) % 2, e, *tile_of(n_groups - 1, e))

    run_stream(nb // GROUP_OWN, own_tile, True, (s_own_ref, smax_own_ref, alpha_own_ref, p_own_ref))
    run_stream(n_past_groups, past_tile, False, (s_ref, smax_ref, alpha_ref, p_ref))

    def write_out(half):
        unroll = 4
        def some(u, carry):
            for d in range(unroll):
                i = u * unroll + d
                acc = acc_ref[i]
                oT = acc[0:MOBA_HEAD_DIM] / acc[ONES_ROW:ONES_ROW + 1]
                rows = pl.ds(pl.multiple_of(i * BLK, BLK), BLK)
                o_ref[rows, half * MOBA_HEAD_DIM:(half + 1) * MOBA_HEAD_DIM] = oT.T.astype(o_ref.dtype)
            return carry
        lax.fori_loop(0, nb // unroll, some, 0)

    lax.cond(h % 2 == 0, lambda: write_out(0), lambda: write_out(1))


def _moba(slopes_l2, qT, k, vT, sel):
    B, nb, QW, _ = qT.shape
    S = nb * BLK
    assert nb % GROUP == 0
    k4 = k.reshape(B, nb, BLK, QW)
    pairs = [(i, j) for j in range(nb - 1) for i in range(j + 1, nb)]
    pairs += [(0, 0)] * (-len(pairs) % GROUP)
    ti = jnp.asarray([p[0] for p in pairs], jnp.int32)
    tj = jnp.asarray([p[1] for p in pairs], jnp.int32)
    grid_spec = pltpu.PrefetchScalarGridSpec(
        num_scalar_prefetch=3,
        grid=(B, MOBA_HEADS),
        in_specs=[
            pl.BlockSpec((None, nb, QK_PAD, BLK), lambda b, h, *_: (b, 0, h, 0)),
            pl.BlockSpec((None, nb, BLK, QK_PAD), lambda b, h, *_: (b, 0, 0, h)),
            pl.BlockSpec((None, nb, V_PAD, BLK), lambda b, h, *_: (b, 0, h, 0)),
            pl.BlockSpec((None, None, nb, 8, BLK), lambda b, h, *_: (b, h, 0, 0, 0)),
        ],
        out_specs=pl.BlockSpec((None, S, 2 * MOBA_HEAD_DIM), lambda b, h, *_: (b, 0, h // 2)),
        scratch_shapes=[
            pltpu.VMEM((nb, BLK), jnp.float32),
            pltpu.VMEM((nb, V_PAD, BLK), jnp.float32),
            pltpu.VMEM((2, GROUP, BLK, BLK), jnp.float32),
            pltpu.VMEM((2, GROUP, BLK), jnp.float32),
            pltpu.VMEM((2, GROUP, BLK), jnp.float32),
            pltpu.VMEM((2, GROUP * BLK, BLK), jnp.bfloat16),
            pltpu.VMEM((2, GROUP_OWN, BLK, BLK), jnp.float32),
            pltpu.VMEM((2, 8, BLK), jnp.float32),
            pltpu.VMEM((2, 8, BLK), jnp.float32),
            pltpu.VMEM((2, GROUP_OWN * BLK, BLK), jnp.bfloat16),
        ],
    )
    return pl.pallas_call(
        functools.partial(_moba_kernel, n_past_groups=len(pairs) // GROUP),
        grid_spec=grid_spec,
        out_shape=jax.ShapeDtypeStruct((B, S, MOBA_WIDTH), jnp.bfloat16),
        compiler_params=pltpu.CompilerParams(
            dimension_semantics=("arbitrary", "arbitrary"), vmem_limit_bytes=VMEM_LIMIT),
        name="moba",
    )(slopes_l2, ti, tj, qT, k4, vT, sel)


def _ret_kernel(cd_ref, q_ref, kT_ref, v_ref, g_ref, dintra_ref, qdec_ref, kdec_ref, gn_ref,
                o_ref, state_ref):
    c = pl.program_id(0)

    @pl.when(c == 0)
    def _():
        state_ref[...] = jnp.zeros_like(state_ref)

    for b in range(q_ref.shape[0]):
        for h in range(RET_HEADS):
            cols = slice(h * RET_HEAD_DIM, (h + 1) * RET_HEAD_DIM)
            q = q_ref[b, :, cols]
            kT = kT_ref[b, cols, :]
            v = v_ref[b, :, cols]
            state = state_ref[b, h]
            s = jnp.dot(q, kT, preferred_element_type=jnp.float32) * dintra_ref[h]
            inner = jnp.dot(s.astype(jnp.bfloat16), v, preferred_element_type=jnp.float32)
            qd = (q.astype(jnp.float32) * qdec_ref[h]).astype(jnp.bfloat16)
            cross = jnp.dot(qd, state.astype(jnp.bfloat16), preferred_element_type=jnp.float32)
            kd = (kT.astype(jnp.float32) * kdec_ref[h]).astype(jnp.bfloat16)
            state_ref[b, h] = cd_ref[h] * state + jnp.dot(kd, v, preferred_element_type=jnp.float32)
            o = inner + cross
            mu = jnp.mean(o, axis=-1, keepdims=True)
            d = o - mu
            var = jnp.mean(d * d, axis=-1, keepdims=True)
            gate = g_ref[b, :, cols].astype(jnp.float32)
            y = d * lax.rsqrt(var + GN_EPS) * gn_ref[:, cols] * (gate * jax.nn.sigmoid(gate))
            o_ref[b, :, cols] = y.astype(o_ref.dtype)


def _retention(rq, rkT, rv, rg, ret_norm):
    B, S, W = rq.shape
    nc = S // BLK
    H = RET_HEADS
    C = RET_CHUNK
    f32 = jnp.float32
    gamma = 1.0 - jnp.exp2(-5.0 - jnp.arange(H, dtype=f32))
    log_g = jnp.log(gamma)
    idx = jnp.arange(C, dtype=f32)
    diff = idx[:, None] - idx[None, :]
    dintra = jnp.where(diff >= 0, jnp.exp(log_g[:, None, None] * jnp.maximum(diff, 0.0)), 0.0)
    qdec = jnp.broadcast_to(jnp.exp(log_g[:, None] * (idx + 1.0))[..., None], (H, C, LANES))
    kdec = jnp.exp(log_g[:, None] * (C - 1.0 - idx))[:, None, :]
    cdec = jnp.exp(log_g * C)
    gn = ret_norm.reshape(1, W).astype(f32)

    nat = pl.BlockSpec((B, C, W), lambda c, s: (0, c, 0))
    const = lambda a: pl.BlockSpec(a.shape, lambda c, s: (0,) * a.ndim,
                                   pipeline_mode=pl.Buffered(1))
    grid_spec = pltpu.PrefetchScalarGridSpec(
        num_scalar_prefetch=1,
        grid=(nc,),
        in_specs=[
            nat,
            pl.BlockSpec((B, None, W, C), lambda c, s: (0, c, 0, 0)),
            nat, nat,
            const(dintra), const(qdec), const(kdec), const(gn),
        ],
        out_specs=nat,
        scratch_shapes=[pltpu.VMEM((B, H, RET_HEAD_DIM, RET_HEAD_DIM), f32)],
    )
    return pl.pallas_call(
        _ret_kernel, grid_spec=grid_spec,
        out_shape=jax.ShapeDtypeStruct((B, S, W), jnp.bfloat16),
        compiler_params=pltpu.CompilerParams(
            dimension_semantics=("arbitrary",),
            vmem_limit_bytes=VMEM_LIMIT),
        name="retention",
    )(cdec, rq, rkT, rv, rg, dintra, qdec, kdec, gn)


def _gelu_tanh(u):
    k = -2.0 * math.sqrt(2.0 / math.pi) * LOG2E
    return u / (1.0 + jnp.exp2(u * (k + (k * 0.044715) * (u * u))))


def _shift_rows(cur, prev_tail, n):
    rolled = pltpu.roll(cur, n, axis=0)
    prev_rolled = pltpu.roll(prev_tail, n, axis=0)
    row = lax.broadcasted_iota(jnp.int32, prev_tail.shape, 0)
    head = jnp.where(row < n, prev_rolled, rolled[0:8])
    return jnp.concatenate([head, rolled[8:]], axis=0)


def _tail_kernel(x_ref, moba_ref, ret_ref, p_ref, wo_ref, ffn_g_ref, wup_ref, wgate_ref,
                 convw_ref, convb_ref, wdown_ref, ple_g_ref, wple_ref, wpg_ref, fin_g_ref,
                 o_ref, y_ref, carry_ref, *, tiles_per_seq):
    t = pl.program_id(0)
    R = ROW_TILE

    @pl.when(t % tiles_per_seq == 0)
    def _():
        carry_ref[...] = jnp.zeros_like(carry_ref)

    mix = jnp.concatenate([moba_ref[...], ret_ref[...]], axis=1)
    x1 = x_ref[...] + jnp.dot(mix, wo_ref[...], preferred_element_type=jnp.float32)
    h = _rms(x1, ffn_g_ref[...]).astype(jnp.bfloat16)

    for c in range(N_FF_CHUNKS):
        cols = slice(c * FF_CHUNK, (c + 1) * FF_CHUNK)
        up = jnp.dot(h, wup_ref[:, cols], preferred_element_type=jnp.float32)
        gt = jnp.dot(h, wgate_ref[:, cols], preferred_element_type=jnp.float32)
        prev_tail = carry_ref[c]
        carry_ref[c] = up[R - 8:R, :]
        w = convw_ref[:, cols]
        u = (convb_ref[:, cols]
             + w[0:1, :] * _shift_rows(up, prev_tail, 2)
             + w[1:2, :] * _shift_rows(up, prev_tail, 1)
             + w[2:3, :] * up)
        y_ref[:, cols] = (_gelu_tanh(u) * gt).astype(jnp.bfloat16)
    x2 = x1 + jnp.dot(y_ref[...], wdown_ref[...], preferred_element_type=jnp.float32)

    hn = _rms(x2, ple_g_ref[...]).astype(jnp.bfloat16)
    g = jax.nn.sigmoid(jnp.dot(hn, wpg_ref[...], preferred_element_type=jnp.float32))
    pe = jnp.dot(p_ref[...].astype(jnp.bfloat16), wple_ref[...], preferred_element_type=jnp.float32)
    x3 = x2 + pe * g
    o_ref[...] = _rms(x3, fin_g_ref[...])


def _tail(x2d, moba2d, ret2d, p2d, wo, ffn_g, wup, wgate, convw, convb, wdown, ple_g, wple,
          wpg, fin_g, seq_len):
    T, D = x2d.shape
    R = ROW_TILE
    row = lambda w: pl.BlockSpec((R, w), lambda t: (t, 0))
    consts = (wo, ffn_g, wup, wgate, convw, convb, wdown, ple_g, wple, wpg, fin_g)
    in_specs = [row(D), row(MOBA_WIDTH), row(RET_WIDTH), row(PLE_DIM)] + [
        _const_spec(a.shape) for a in consts]
    return pl.pallas_call(
        functools.partial(_tail_kernel, tiles_per_seq=seq_len // R),
        grid=(T // R,), in_specs=in_specs, out_specs=row(D),
        out_shape=jax.ShapeDtypeStruct((T, D), jnp.float32),
        scratch_shapes=[
            pltpu.VMEM((R, D_FF), jnp.bfloat16),
            pltpu.VMEM((N_FF_CHUNKS, 8, FF_CHUNK), jnp.float32),
        ],
        compiler_params=pltpu.CompilerParams(
            dimension_semantics=("arbitrary",), vmem_limit_bytes=VMEM_LIMIT),
        name="tail",
    )(x2d, moba2d, ret2d, p2d, *consts)


def _pad_heads(w, pad):
    d = w.shape[0]
    w3 = w.reshape(d, MOBA_HEADS, MOBA_HEAD_DIM)
    return jnp.pad(w3, ((0, 0), (0, 0), (0, pad - MOBA_HEAD_DIM))).reshape(d, MOBA_HEADS * pad)


def kernel(x, p, attn_norm, w_in, ret_norm, w_out, ffn_norm, w_up, w_gate, conv_w, conv_b, w_down,
           ple_norm, w_ple, w_ple_gate, final_norm):
    B, S, D = x.shape
    assert D == D_MODEL and S % ROW_TILE == 0 and w_in.shape[0] == 1
    bf, f32 = jnp.bfloat16, jnp.float32
    MW, RW = MOBA_WIDTH, RET_WIDTH
    w = w_in[0].astype(bf)
    wqT = w[:, 0:MW].T
    wk = _pad_heads(w[:, MW:2 * MW], QK_PAD)
    wvT = w[:, 2 * MW:3 * MW].T
    o = 3 * MW
    wrq = w[:, o:o + RW]
    wrkT = w[:, o + RW:o + 2 * RW].T
    wrv = w[:, o + 2 * RW:o + 3 * RW]
    wrg = w[:, o + 3 * RW:o + 4 * RW]
    row = lambda a: a.reshape(1, -1).astype(f32)

    slopes_l2 = jnp.exp2(-8.0 * jnp.arange(1, MOBA_HEADS + 1, dtype=f32) / MOBA_HEADS) * LOG2E
    kterm = slopes_l2[None, :] * jnp.arange(BLK, dtype=f32)[:, None]
    kterm_hi = lax.bitcast_convert_type(
        lax.bitcast_convert_type(kterm, jnp.uint32) & jnp.uint32(0xFFFF0000), f32)
    kterm_lo = kterm - kterm_hi
    kpos = jnp.zeros((BLK, MOBA_HEADS, QK_PAD), f32)
    kpos = kpos.at[:, :, ALIBI_ROW].set(kterm_hi).at[:, :, ALIBI_ROW + 1].set(kterm_lo)
    kpos = kpos.reshape(BLK, MOBA_HEADS * QK_PAD)

    qT, k, vT, sel, rq, rkT, rv, rg = _in_proj(
        x, row(attn_norm[0]), wqT, wk, kpos, wvT, wrq, wrkT, wrv, wrg)

    moba_out = _moba(slopes_l2, qT, k, vT, sel)
    ret_out = _retention(rq, rkT, rv, rg, ret_norm[0])

    wo = w_out[0].astype(bf)
    out = _tail(
        x.reshape(B * S, D), moba_out.reshape(B * S, MW), ret_out.reshape(B * S, RW),
        p[0].reshape(B * S, PLE_DIM),
        wo, row(ffn_norm[0]),
        w_up[0].astype(bf), w_gate[0].astype(bf), conv_w[0].astype(f32), row(conv_b[0]),
        w_down[0].astype(bf),
        row(ple_norm[0]), w_ple[0].astype(bf), w_ple_gate[0].astype(bf), row(final_norm),
        seq_len=S)
    return out.reshape(B, S, D)
```

```python
import functools
import math

import jax
import jax.numpy as jnp
from jax import lax
from jax.experimental import pallas as pl
from jax.experimental.pallas import tpu as pltpu

D_MODEL = 1024
PLE_DIM = 256
MOBA_HEADS = 8
MOBA_HEAD_DIM = 64
MOBA_WIDTH = MOBA_HEADS * MOBA_HEAD_DIM
MOBA_BLOCK = 256
MOBA_TOPK = 3
RET_HEADS = 4
RET_HEAD_DIM = 128
RET_WIDTH = RET_HEADS * RET_HEAD_DIM
RET_CHUNK = 256
D_FF = 2816
CONV_WIDTH = 3
RMS_EPS = 1e-6
GN_EPS = 1e-5

BLK = 256
LANES = 128
FF_CHUNK = 256
N_FF_CHUNKS = D_FF // FF_CHUNK
ROW_TILE = 512
VMEM_LIMIT = 56 * 1024 * 1024
NEG = -1e30
LOG2E = math.log2(math.e)

QK_PAD = LANES
ALIBI_ROW = MOBA_HEAD_DIM
V_PAD = 80
ONES_ROW = MOBA_HEAD_DIM
PAST_WIDTH = 1
GROUP = 16
GROUP_OWN = 8

_NT = (((1,), (1,)), ((), ()))


def _const_spec(shape):
    nd = len(shape)
    return pl.BlockSpec(shape, lambda *_: (0,) * nd, pipeline_mode=pl.Buffered(1))


def _rms(x, g):
    ms = jnp.mean(x * x, axis=-1, keepdims=True)
    return x * lax.rsqrt(ms + RMS_EPS) * g


def _in_proj_kernel(x_ref, g_ref, wqT_ref, wk_ref, kpos_ref, wvT_ref,
                    wrq_ref, wrkT_ref, wrv_ref, wrg_ref,
                    qT_ref, k_ref, vT_ref, sel_ref, rq_ref, rkT_ref, rv_ref, rg_ref, kmean_ref):
    t = pl.program_id(1)
    h = _rms(x_ref[...], g_ref[...]).astype(jnp.bfloat16)
    nblk = ROW_TILE // BLK
    nb = kmean_ref.shape[0]

    @pl.when(t == 0)
    def _():
        kmean_ref[...] = jnp.zeros_like(kmean_ref)

    def nt(w_ref):
        return lax.dot_general(w_ref[...], h, _NT, preferred_element_type=jnp.float32)

    def nn(w_ref):
        return jnp.dot(h, w_ref[...], preferred_element_type=jnp.float32)

    Dh = MOBA_HEAD_DIM
    qT = (nt(wqT_ref) * (Dh ** -0.5 * LOG2E)).astype(jnp.bfloat16)
    q_pad = (lax.broadcasted_iota(jnp.int32, (QK_PAD - Dh, BLK), 0) < 2).astype(jnp.bfloat16)
    v_pad = (lax.broadcasted_iota(jnp.int32, (V_PAD - Dh, BLK), 0) < 1).astype(jnp.bfloat16)
    for b in range(nblk):
        for a in range(MOBA_HEADS):
            qT_ref[b, a * QK_PAD:a * QK_PAD + Dh, :] = qT[a * Dh:(a + 1) * Dh, b * BLK:(b + 1) * BLK]
            qT_ref[b, a * QK_PAD + Dh:(a + 1) * QK_PAD, :] = q_pad
    k = nn(wk_ref)
    for b in range(nblk):
        kb = k[b * BLK:(b + 1) * BLK]
        kmean_ref[pl.ds(t * nblk + b, 1), :] = jnp.mean(kb, axis=0, keepdims=True)
        k_ref[b * BLK:(b + 1) * BLK, :] = (kb + kpos_ref[...]).astype(jnp.bfloat16)

    km = kmean_ref[...]
    km_hi = km.astype(jnp.bfloat16)
    km_lo = (km - km_hi.astype(jnp.float32)).astype(jnp.bfloat16)
    blk_id = lax.broadcasted_iota(jnp.int32, (nb, BLK), 0)
    none = jnp.full((8 - MOBA_TOPK, BLK), nb, jnp.int32)

    def route(b, a):
        i = t * nblk + b
        dims = slice(a * QK_PAD, a * QK_PAD + Dh)
        qTh = qT[a * Dh:(a + 1) * Dh, b * BLK:(b + 1) * BLK]
        gate = (jnp.dot(km_hi[:, dims], qTh, preferred_element_type=jnp.float32)
                + jnp.dot(km_lo[:, dims], qTh, preferred_element_type=jnp.float32))
        gate = jnp.where(blk_id < i, gate, -jnp.inf)
        picks = []
        for _ in range(MOBA_TOPK):
            top = jnp.max(gate, axis=0, keepdims=True)
            is_top = (gate == top) & (top > -jnp.inf)
            first = jnp.min(jnp.where(is_top, blk_id, nb), axis=0, keepdims=True)
            picks.append(first)
            gate = jnp.where(blk_id == first, -jnp.inf, gate)
        sel_ref[a, b] = jnp.concatenate(picks + [none], axis=0)

    def put_vT():
        vT = nt(wvT_ref).astype(jnp.bfloat16)
        for b in range(nblk):
            for a in range(MOBA_HEADS):
                vT_ref[b, a * V_PAD:a * V_PAD + Dh, :] = vT[a * Dh:(a + 1) * Dh, b * BLK:(b + 1) * BLK]
                vT_ref[b, a * V_PAD + Dh:(a + 1) * V_PAD, :] = v_pad

    def put_rkT():
        rkT = (nt(wrkT_ref) * (RET_HEAD_DIM ** -0.5)).astype(jnp.bfloat16)
        for b in range(nblk):
            rkT_ref[b] = rkT[:, b * BLK:(b + 1) * BLK]

    def put_nat(w_ref, o_ref):
        o_ref[...] = nn(w_ref).astype(jnp.bfloat16)

    matmuls = [put_vT, put_rkT, functools.partial(put_nat, wrq_ref, rq_ref),
               functools.partial(put_nat, wrv_ref, rv_ref), functools.partial(put_nat, wrg_ref, rg_ref)]
    units = [(b, a) for b in range(nblk) for a in range(MOBA_HEADS)]
    per = -(-len(units) // len(matmuls))
    for n, mm in enumerate(matmuls):
        for b, a in units[n * per:(n + 1) * per]:
            route(b, a)
        mm()


def _in_proj(x, g, wqT, wk, kpos, wvT, wrq, wrkT, wrv, wrg):
    B, S, D = x.shape
    nb = S // BLK
    nblk = ROW_TILE // BLK
    grid = (B, S // ROW_TILE)
    bf = jnp.bfloat16
    QW, VW = MOBA_HEADS * QK_PAD, MOBA_HEADS * V_PAD
    nat = lambda w: pl.BlockSpec((None, ROW_TILE, w), lambda b, t: (b, t, 0))
    tr = lambda w: pl.BlockSpec((None, nblk, w, BLK), lambda b, t: (b, t, 0, 0))
    out_shape = (
        jax.ShapeDtypeStruct((B, nb, QW, BLK), bf),
        jax.ShapeDtypeStruct((B, S, QW), bf),
        jax.ShapeDtypeStruct((B, nb, VW, BLK), bf),
        jax.ShapeDtypeStruct((B, MOBA_HEADS, nb, 8, BLK), jnp.int32),
        jax.ShapeDtypeStruct((B, S, RET_WIDTH), bf),
        jax.ShapeDtypeStruct((B, nb, RET_WIDTH, BLK), bf),
        jax.ShapeDtypeStruct((B, S, RET_WIDTH), bf),
        jax.ShapeDtypeStruct((B, S, RET_WIDTH), bf),
    )
    out_specs = (
        tr(QW), nat(QW), tr(VW),
        pl.BlockSpec((None, MOBA_HEADS, nblk, 8, BLK), lambda b, t: (b, 0, t, 0, 0)),
        nat(RET_WIDTH), tr(RET_WIDTH), nat(RET_WIDTH), nat(RET_WIDTH),
    )
    consts = (g, wqT, wk, kpos, wvT, wrq, wrkT, wrv, wrg)
    in_specs = [pl.BlockSpec((None, ROW_TILE, D), lambda b, t: (b, t, 0))] + [
        _const_spec(a.shape) for a in consts]
    return pl.pallas_call(
        _in_proj_kernel, grid=grid, in_specs=in_specs, out_specs=out_specs, out_shape=out_shape,
        scratch_shapes=[pltpu.VMEM((nb, QW), jnp.float32)],
        compiler_params=pltpu.CompilerParams(
            dimension_semantics=("arbitrary", "arbitrary"), vmem_limit_bytes=VMEM_LIMIT),
        name="in_proj",
    )(x, *consts)


def _moba_kernel(slopes_ref, ti_ref, tj_ref, qT_ref, k_ref, vT_ref, sel_ref, o_ref,
                 m_ref, acc_ref, s_ref, smax_ref, alpha_ref, p_ref,
                 s_own_ref, smax_own_ref, alpha_own_ref, p_own_ref, *, n_past_groups):
    h = pl.program_id(1)
    nb = k_ref.shape[0]
    slope = slopes_ref[h]
    kpos = lax.broadcasted_iota(jnp.int32, (BLK, BLK), 0)
    qpos = lax.broadcasted_iota(jnp.int32, (BLK, BLK), 1)
    qterm = -slope * lax.broadcasted_iota(jnp.int32, (1, BLK), 1).astype(jnp.float32)

    m_ref[...] = jnp.full(m_ref.shape, -3e38, jnp.float32)
    acc_ref[...] = jnp.zeros(acc_ref.shape, jnp.float32)

    def own_item(g, e):
        t = g * GROUP_OWN + e
        return t, t

    def past_item(g, e):
        t = g * GROUP + e
        return ti_ref[t], tj_ref[t]

    def run_stream(n_groups, item_of, own, bufs):
        s_ref, smax_ref, alpha_ref, p_ref = bufs
        items, width = s_ref.shape[1], s_ref.shape[2] // BLK

        def scores(par, e, i_t, j0):
            kq = k_ref[j0] if width == 1 else k_ref[pl.ds(j0, width)].reshape(width * BLK, QK_PAD)
            s = jnp.dot(kq, qT_ref[i_t], preferred_element_type=jnp.float32)
            if own:
                s = jnp.where(kpos <= qpos, s, NEG)
            s_ref[par, e] = s
            for w in range(width):
                smax_ref[par, e * width + w:e * width + w + 1, :] = (
                    jnp.max(s[w * BLK:(w + 1) * BLK], axis=0, keepdims=True))

        def values(par, e, i_t, j0):
            vq = vT_ref[j0] if width == 1 else jnp.concatenate(
                [vT_ref[j0 + w] for w in range(width)], axis=1)
            pv = jnp.dot(vq, p_ref[par, e * width * BLK:(e + 1) * width * BLK, :],
                         preferred_element_type=jnp.float32)
            acc_ref[i_t] = alpha_ref[par, e:e + 1, :] * acc_ref[i_t] + pv

        def softmax(par, e, i_t, j0):
            if own:
                cs = [qterm]
            else:
                sel = sel_ref[i_t]
                cs = []
                for w in range(width):
                    j_t = j0 + w
                    hit = (sel[0:1] == j_t) | (sel[1:2] == j_t) | (sel[2:3] == j_t)
                    blk_dist = ((i_t - j_t) * MOBA_BLOCK).astype(jnp.float32)
                    cs.append(jnp.where(hit, -slope * blk_dist, NEG) + qterm)
            m_old = m_ref[pl.ds(i_t, 1), :]
            m_new = m_old
            for w in range(width):
                r = e * width + w
                m_new = jnp.maximum(m_new, smax_ref[par, r:r + 1, :] + cs[w])
            alpha_ref[par, e:e + 1, :] = jnp.exp2(m_old - m_new)
            m_ref[pl.ds(i_t, 1), :] = m_new
            for w in range(width):
                r = e * width + w
                p_ref[par, r * BLK:(r + 1) * BLK, :] = (
                    jnp.exp2(s_ref[par, e, w * BLK:(w + 1) * BLK, :] - (m_new - cs[w])).astype(jnp.bfloat16))

        ahead = 2

        def trip(g, par):
            g_next = jnp.minimum(g + 1, n_groups - 1)
            g_prev = jnp.maximum(g - 1, 0)
            for e in range(-ahead, items):
                if e + ahead < items:
                    scores(1 - par, e + ahead, *item_of(g_next, e + ahead))
                if e >= 0:
                    values(1 - par, e, *item_of(g_prev, e))
                    softmax(par, e, *item_of(g, e))

        alpha_ref[1] = jnp.ones(alpha_ref.shape[1:], jnp.float32)
        p_ref[1] = jnp.zeros(p_ref.shape[1:], jnp.bfloat16)
        for e in range(items):
            scores(0, e, *item_of(0, e))

        def body(w, carry):
            trip(2 * w, 0)
            trip(2 * w + 1, 1)
            return carry

        assert n_groups % 2 == 0
        lax.fori_loop(0, n_groups // 2, body, 0)
        for e in range(items):
            values((n_groups - 1) % 2, e, *item_of(n_groups - 1, e))

    run_stream(nb // GROUP_OWN, own_item, True, (s_own_ref, smax_own_ref, alpha_own_ref, p_own_ref))
    run_stream(n_past_groups, past_item, False, (s_ref, smax_ref, alpha_ref, p_ref))

    def write_out(half):
        unroll = 4
        def some(u, carry):
            for d in range(unroll):
                i = u * unroll + d
                acc = acc_ref[i]
                oT = acc[0:MOBA_HEAD_DIM] / acc[ONES_ROW:ONES_ROW + 1]
                rows = pl.ds(pl.multiple_of(i * BLK, BLK), BLK)
                o_ref[rows, half * MOBA_HEAD_DIM:(half + 1) * MOBA_HEAD_DIM] = oT.T.astype(o_ref.dtype)
            return carry
        lax.fori_loop(0, nb // unroll, some, 0)

    lax.cond(h % 2 == 0, lambda: write_out(0), lambda: write_out(1))


def _moba(slopes_l2, qT, k, vT, sel):
    B, nb, QW, _ = qT.shape
    S = nb * BLK
    assert nb % (2 * GROUP_OWN) == 0 and nb % PAST_WIDTH == 0
    k4 = k.reshape(B, nb, BLK, QW)
    pairs = [(i, j0) for j0 in range(0, nb - 1, PAST_WIDTH) for i in range(j0 + 1, nb)]
    pairs += [(0, 0)] * (-len(pairs) % (2 * GROUP))
    ti = jnp.asarray([p[0] for p in pairs], jnp.int32)
    tj = jnp.asarray([p[1] for p in pairs], jnp.int32)
    grid_spec = pltpu.PrefetchScalarGridSpec(
        num_scalar_prefetch=3,
        grid=(B, MOBA_HEADS),
        in_specs=[
            pl.BlockSpec((None, nb, QK_PAD, BLK), lambda b, h, *_: (b, 0, h, 0)),
            pl.BlockSpec((None, nb, BLK, QK_PAD), lambda b, h, *_: (b, 0, 0, h)),
            pl.BlockSpec((None, nb, V_PAD, BLK), lambda b, h, *_: (b, 0, h, 0)),
            pl.BlockSpec((None, None, nb, 8, BLK), lambda b, h, *_: (b, h, 0, 0, 0)),
        ],
        out_specs=pl.BlockSpec((None, S, 2 * MOBA_HEAD_DIM), lambda b, h, *_: (b, 0, h // 2)),
        scratch_shapes=[
            pltpu.VMEM((nb, BLK), jnp.float32),
            pltpu.VMEM((nb, V_PAD, BLK), jnp.float32),
            pltpu.VMEM((2, GROUP, PAST_WIDTH * BLK, BLK), jnp.float32),
            pltpu.VMEM((2, GROUP * PAST_WIDTH, BLK), jnp.float32),
            pltpu.VMEM((2, max(GROUP, 8), BLK), jnp.float32),
            pltpu.VMEM((2, GROUP * PAST_WIDTH * BLK, BLK), jnp.bfloat16),
            pltpu.VMEM((2, GROUP_OWN, BLK, BLK), jnp.float32),
            pltpu.VMEM((2, 8, BLK), jnp.float32),
            pltpu.VMEM((2, 8, BLK), jnp.float32),
            pltpu.VMEM((2, GROUP_OWN * BLK, BLK), jnp.bfloat16),
        ],
    )
    return pl.pallas_call(
        functools.partial(_moba_kernel, n_past_groups=len(pairs) // GROUP),
        grid_spec=grid_spec,
        out_shape=jax.ShapeDtypeStruct((B, S, MOBA_WIDTH), jnp.bfloat16),
        compiler_params=pltpu.CompilerParams(
            dimension_semantics=("arbitrary", "arbitrary"), vmem_limit_bytes=VMEM_LIMIT),
        name="moba",
    )(slopes_l2, ti, tj, qT, k4, vT, sel)


def _ret_kernel(cd_ref, q_ref, kT_ref, v_ref, g_ref, dintra_ref, qdec_ref, kdec_ref, gn_ref,
                o_ref, state_ref):
    c = pl.program_id(0)

    @pl.when(c == 0)
    def _():
        state_ref[...] = jnp.zeros_like(state_ref)

    for b in range(q_ref.shape[0]):
        for h in range(RET_HEADS):
            cols = slice(h * RET_HEAD_DIM, (h + 1) * RET_HEAD_DIM)
            q = q_ref[b, :, cols]
            kT = kT_ref[b, cols, :]
            v = v_ref[b, :, cols]
            state = state_ref[b, h]
            s = jnp.dot(q, kT, preferred_element_type=jnp.float32) * dintra_ref[h]
            inner = jnp.dot(s.astype(jnp.bfloat16), v, preferred_element_type=jnp.float32)
            qd = (q.astype(jnp.float32) * qdec_ref[h]).astype(jnp.bfloat16)
            cross = jnp.dot(qd, state.astype(jnp.bfloat16), preferred_element_type=jnp.float32)
            kd = (kT.astype(jnp.float32) * kdec_ref[h]).astype(jnp.bfloat16)
            state_ref[b, h] = cd_ref[h] * state + jnp.dot(kd, v, preferred_element_type=jnp.float32)
            o = inner + cross
            mu = jnp.mean(o, axis=-1, keepdims=True)
            d = o - mu
            var = jnp.mean(d * d, axis=-1, keepdims=True)
            gate = g_ref[b, :, cols].astype(jnp.float32)
            y = d * lax.rsqrt(var + GN_EPS) * gn_ref[:, cols] * (gate * jax.nn.sigmoid(gate))
            o_ref[b, :, cols] = y.astype(o_ref.dtype)


def _retention(rq, rkT, rv, rg, ret_norm):
    B, S, W = rq.shape
    nc = S // BLK
    H = RET_HEADS
    C = RET_CHUNK
    f32 = jnp.float32
    gamma = 1.0 - jnp.exp2(-5.0 - jnp.arange(H, dtype=f32))
    log_g = jnp.log(gamma)
    idx = jnp.arange(C, dtype=f32)
    diff = idx[:, None] - idx[None, :]
    dintra = jnp.where(diff >= 0, jnp.exp(log_g[:, None, None] * jnp.maximum(diff, 0.0)), 0.0)
    qdec = jnp.broadcast_to(jnp.exp(log_g[:, None] * (idx + 1.0))[..., None], (H, C, LANES))
    kdec = jnp.exp(log_g[:, None] * (C - 1.0 - idx))[:, None, :]
    cdec = jnp.exp(log_g * C)
    gn = ret_norm.reshape(1, W).astype(f32)

    nat = pl.BlockSpec((B, C, W), lambda c, s: (0, c, 0))
    const = lambda a: pl.BlockSpec(a.shape, lambda c, s: (0,) * a.ndim,
                                   pipeline_mode=pl.Buffered(1))
    grid_spec = pltpu.PrefetchScalarGridSpec(
        num_scalar_prefetch=1,
        grid=(nc,),
        in_specs=[
            nat,
            pl.BlockSpec((B, None, W, C), lambda c, s: (0, c, 0, 0)),
            nat, nat,
            const(dintra), const(qdec), const(kdec), const(gn),
        ],
        out_specs=nat,
        scratch_shapes=[pltpu.VMEM((B, H, RET_HEAD_DIM, RET_HEAD_DIM), f32)],
    )
    return pl.pallas_call(
        _ret_kernel, grid_spec=grid_spec,
        out_shape=jax.ShapeDtypeStruct((B, S, W), jnp.bfloat16),
        compiler_params=pltpu.CompilerParams(
            dimension_semantics=("arbitrary",),
            vmem_limit_bytes=VMEM_LIMIT),
        name="retention",
    )(cdec, rq, rkT, rv, rg, dintra, qdec, kdec, gn)


def _gelu_tanh(u):
    k = -2.0 * math.sqrt(2.0 / math.pi) * LOG2E
    return u / (1.0 + jnp.exp2(u * (k + (k * 0.044715) * (u * u))))


def _shift_rows(cur, prev_tail, n):
    rolled = pltpu.roll(cur, n, axis=0)
    prev_rolled = pltpu.roll(prev_tail, n, axis=0)
    row = lax.broadcasted_iota(jnp.int32, prev_tail.shape, 0)
    head = jnp.where(row < n, prev_rolled, rolled[0:8])
    return jnp.concatenate([head, rolled[8:]], axis=0)


def _tail_kernel(x_ref, moba_ref, ret_ref, p_ref, wo_ref, ffn_g_ref, wup_ref, wgate_ref,
                 convw_ref, convb_ref, wdown_ref, ple_g_ref, wple_ref, wpg_ref, fin_g_ref,
                 o_ref, y_ref, carry_ref, *, tiles_per_seq):
    t = pl.program_id(0)
    R = ROW_TILE

    @pl.when(t % tiles_per_seq == 0)
    def _():
        carry_ref[...] = jnp.zeros_like(carry_ref)

    mix = jnp.concatenate([moba_ref[...], ret_ref[...]], axis=1)
    x1 = x_ref[...] + jnp.dot(mix, wo_ref[...], preferred_element_type=jnp.float32)
    h = _rms(x1, ffn_g_ref[...]).astype(jnp.bfloat16)

    for c in range(N_FF_CHUNKS):
        cols = slice(c * FF_CHUNK, (c + 1) * FF_CHUNK)
        up = jnp.dot(h, wup_ref[:, cols], preferred_element_type=jnp.float32)
        gt = jnp.dot(h, wgate_ref[:, cols], preferred_element_type=jnp.float32)
        prev_tail = carry_ref[c]
        carry_ref[c] = up[R - 8:R, :]
        w = convw_ref[:, cols]
        u = (convb_ref[:, cols]
             + w[0:1, :] * _shift_rows(up, prev_tail, 2)
             + w[1:2, :] * _shift_rows(up, prev_tail, 1)
             + w[2:3, :] * up)
        y_ref[:, cols] = (_gelu_tanh(u) * gt).astype(jnp.bfloat16)
    x2 = x1 + jnp.dot(y_ref[...], wdown_ref[...], preferred_element_type=jnp.float32)

    hn = _rms(x2, ple_g_ref[...]).astype(jnp.bfloat16)
    g = jax.nn.sigmoid(jnp.dot(hn, wpg_ref[...], preferred_element_type=jnp.float32))
    pe = jnp.dot(p_ref[...].astype(jnp.bfloat16), wple_ref[...], preferred_element_type=jnp.float32)
    x3 = x2 + pe * g
    o_ref[...] = _rms(x3, fin_g_ref[...])


def _tail(x2d, moba2d, ret2d, p2d, wo, ffn_g, wup, wgate, convw, convb, wdown, ple_g, wple,
          wpg, fin_g, seq_len):
    T, D = x2d.shape
    R = ROW_TILE
    row = lambda w: pl.BlockSpec((R, w), lambda t: (t, 0))
    consts = (wo, ffn_g, wup, wgate, convw, convb, wdown, ple_g, wple, wpg, fin_g)
    in_specs = [row(D), row(MOBA_WIDTH), row(RET_WIDTH), row(PLE_DIM)] + [
        _const_spec(a.shape) for a in consts]
    return pl.pallas_call(
        functools.partial(_tail_kernel, tiles_per_seq=seq_len // R),
        grid=(T // R,), in_specs=in_specs, out_specs=row(D),
        out_shape=jax.ShapeDtypeStruct((T, D), jnp.float32),
        scratch_shapes=[
            pltpu.VMEM((R, D_FF), jnp.bfloat16),
            pltpu.VMEM((N_FF_CHUNKS, 8, FF_CHUNK), jnp.float32),
        ],
        compiler_params=pltpu.CompilerParams(
            dimension_semantics=("arbitrary",), vmem_limit_bytes=VMEM_LIMIT),
        name="tail",
    )(x2d, moba2d, ret2d, p2d, *consts)


def _pad_heads(w, pad):
    d = w.shape[0]
    w3 = w.reshape(d, MOBA_HEADS, MOBA_HEAD_DIM)
    return jnp.pad(w3, ((0, 0), (0, 0), (0, pad - MOBA_HEAD_DIM))).reshape(d, MOBA_HEADS * pad)


def kernel(x, p, attn_norm, w_in, ret_norm, w_out, ffn_norm, w_up, w_gate, conv_w, conv_b, w_down,
           ple_norm, w_ple, w_ple_gate, final_norm):
    B, S, D = x.shape
    assert D == D_MODEL and S % ROW_TILE == 0 and w_in.shape[0] == 1
    bf, f32 = jnp.bfloat16, jnp.float32
    MW, RW = MOBA_WIDTH, RET_WIDTH
    w = w_in[0].astype(bf)
    wqT = w[:, 0:MW].T
    wk = _pad_heads(w[:, MW:2 * MW], QK_PAD)
    wvT = w[:, 2 * MW:3 * MW].T
    o = 3 * MW
    wrq = w[:, o:o + RW]
    wrkT = w[:, o + RW:o + 2 * RW].T
    wrv = w[:, o + 2 * RW:o + 3 * RW]
    wrg = w[:, o + 3 * RW:o + 4 * RW]
    row = lambda a: a.reshape(1, -1).astype(f32)

    slopes_l2 = jnp.exp2(-8.0 * jnp.arange(1, MOBA_HEADS + 1, dtype=f32) / MOBA_HEADS) * LOG2E
    kterm = slopes_l2[None, :] * jnp.arange(BLK, dtype=f32)[:, None]
    kterm_hi = lax.bitcast_convert_type(
        lax.bitcast_convert_type(kterm, jnp.uint32) & jnp.uint32(0xFFFF0000), f32)
    kterm_lo = kterm - kterm_hi
    kpos = jnp.zeros((BLK, MOBA_HEADS, QK_PAD), f32)
    kpos = kpos.at[:, :, ALIBI_ROW].set(kterm_hi).at[:, :, ALIBI_ROW + 1].set(kterm_lo)
    kpos = kpos.reshape(BLK, MOBA_HEADS * QK_PAD)

    qT, k, vT, sel, rq, rkT, rv, rg = _in_proj(
        x, row(attn_norm[0]), wqT, wk, kpos, wvT, wrq, wrkT, wrv, wrg)

    moba_out = _moba(slopes_l2, qT, k, vT, sel)
    ret_out = _retention(rq, rkT, rv, rg, ret_norm[0])

    wo = w_out[0].astype(bf)
    out = _tail(
        x.reshape(B * S, D), moba_out.reshape(B * S, MW), ret_out.reshape(B * S, RW),
        p[0].reshape(B * S, PLE_DIM),
        wo, row(ffn_norm[0]),
        w_up[0].astype(bf), w_gate[0].astype(bf), conv_w[0].astype(f32), row(conv_b[0]),
        w_down[0].astype(bf),
        row(ple_norm[0]), w_ple[0].astype(bf), w_ple_gate[0].astype(bf), row(final_norm),
        seq_len=S)
    return out.reshape(B, S, D)
```

```python
import functools
import math

import jax
import jax.numpy as jnp
from jax import lax
from jax.experimental import pallas as pl
from jax.experimental.pallas import tpu as pltpu

D_MODEL = 1024
PLE_DIM = 256
MOBA_HEADS = 8
MOBA_HEAD_DIM = 64
MOBA_WIDTH = MOBA_HEADS * MOBA_HEAD_DIM
MOBA_BLOCK = 256
MOBA_TOPK = 3
RET_HEADS = 4
RET_HEAD_DIM = 128
RET_WIDTH = RET_HEADS * RET_HEAD_DIM
RET_CHUNK = 256
D_FF = 2816
CONV_WIDTH = 3
RMS_EPS = 1e-6
GN_EPS = 1e-5

BLK = 256
LANES = 128
FF_CHUNK = 256
N_FF_CHUNKS = D_FF // FF_CHUNK
ROW_TILE = 512
VMEM_LIMIT = 56 * 1024 * 1024
NEG = -1e30
LOG2E = math.log2(math.e)

QK_PAD = LANES
ALIBI_ROW = MOBA_HEAD_DIM
V_PAD = 80
ONES_ROW = MOBA_HEAD_DIM
PAST_WIDTH = 1
GROUP = 24
GROUP_OWN = 8

_NT = (((1,), (1,)), ((), ()))


def _const_spec(shape):
    nd = len(shape)
    return pl.BlockSpec(shape, lambda *_: (0,) * nd, pipeline_mode=pl.Buffered(1))


def _rms(x, g):
    ms = jnp.mean(x * x, axis=-1, keepdims=True)
    return x * lax.rsqrt(ms + RMS_EPS) * g


def _in_proj_kernel(x_ref, g_ref, wqT_ref, wk_ref, kpos_ref, wvT_ref,
                    wrq_ref, wrkT_ref, wrv_ref, wrg_ref,
                    qT_ref, k_ref, vT_ref, sel_ref, rq_ref, rkT_ref, rv_ref, rg_ref, kmean_ref):
    t = pl.program_id(1)
    h = _rms(x_ref[...], g_ref[...]).astype(jnp.bfloat16)
    nblk = ROW_TILE // BLK
    nb = kmean_ref.shape[0]

    @pl.when(t == 0)
    def _():
        kmean_ref[...] = jnp.zeros_like(kmean_ref)

    def nt(w_ref):
        return lax.dot_general(w_ref[...], h, _NT, preferred_element_type=jnp.float32)

    def nn(w_ref):
        return jnp.dot(h, w_ref[...], preferred_element_type=jnp.float32)

    Dh = MOBA_HEAD_DIM
    qT = (nt(wqT_ref) * (Dh ** -0.5 * LOG2E)).astype(jnp.bfloat16)
    vT = nt(wvT_ref).astype(jnp.bfloat16)
    rkT = (nt(wrkT_ref) * (RET_HEAD_DIM ** -0.5)).astype(jnp.bfloat16)
    q_pad = (lax.broadcasted_iota(jnp.int32, (QK_PAD - Dh, BLK), 0) < 2).astype(jnp.bfloat16)
    v_pad = (lax.broadcasted_iota(jnp.int32, (V_PAD - Dh, BLK), 0) < 1).astype(jnp.bfloat16)
    for b in range(nblk):
        sl = slice(b * BLK, (b + 1) * BLK)
        for a in range(MOBA_HEADS):
            qT_ref[b, a * QK_PAD:a * QK_PAD + Dh, :] = qT[a * Dh:(a + 1) * Dh, sl]
            qT_ref[b, a * QK_PAD + Dh:(a + 1) * QK_PAD, :] = q_pad
            vT_ref[b, a * V_PAD:a * V_PAD + Dh, :] = vT[a * Dh:(a + 1) * Dh, sl]
            vT_ref[b, a * V_PAD + Dh:(a + 1) * V_PAD, :] = v_pad
        rkT_ref[b] = rkT[:, sl]
    k = nn(wk_ref)
    for b in range(nblk):
        kb = k[b * BLK:(b + 1) * BLK]
        kmean_ref[pl.ds(t * nblk + b, 1), :] = jnp.mean(kb, axis=0, keepdims=True)
        k_ref[b * BLK:(b + 1) * BLK, :] = (kb + kpos_ref[...]).astype(jnp.bfloat16)
    rq_ref[...] = nn(wrq_ref).astype(jnp.bfloat16)
    rv_ref[...] = nn(wrv_ref).astype(jnp.bfloat16)
    rg_ref[...] = nn(wrg_ref).astype(jnp.bfloat16)

    km = kmean_ref[...]
    km_hi = km.astype(jnp.bfloat16)
    km_lo = (km - km_hi.astype(jnp.float32)).astype(jnp.bfloat16)
    blk_id = lax.broadcasted_iota(jnp.int32, (nb, BLK), 0)
    none = jnp.full((8 - MOBA_TOPK, BLK), nb, jnp.int32)

    for b in range(nblk):
        i = t * nblk + b
        for a in range(MOBA_HEADS):
            dims = slice(a * QK_PAD, a * QK_PAD + Dh)
            qTh = qT[a * Dh:(a + 1) * Dh, b * BLK:(b + 1) * BLK]
            gate = (jnp.dot(km_hi[:, dims], qTh, preferred_element_type=jnp.float32)
                    + jnp.dot(km_lo[:, dims], qTh, preferred_element_type=jnp.float32))
            gate = jnp.where(blk_id < i, gate, -jnp.inf)
            picks = []
            for _ in range(MOBA_TOPK):
                top = jnp.max(gate, axis=0, keepdims=True)
                is_top = (gate == top) & (top > -jnp.inf)
                first = jnp.min(jnp.where(is_top, blk_id, nb), axis=0, keepdims=True)
                picks.append(first)
                gate = jnp.where(blk_id == first, -jnp.inf, gate)
            sel_ref[a, b] = jnp.concatenate(picks + [none], axis=0)


def _in_proj(x, g, wqT, wk, kpos, wvT, wrq, wrkT, wrv, wrg):
    B, S, D = x.shape
    nb = S // BLK
    nblk = ROW_TILE // BLK
    grid = (B, S // ROW_TILE)
    bf = jnp.bfloat16
    QW, VW = MOBA_HEADS * QK_PAD, MOBA_HEADS * V_PAD
    nat = lambda w: pl.BlockSpec((None, ROW_TILE, w), lambda b, t: (b, t, 0))
    tr = lambda w: pl.BlockSpec((None, nblk, w, BLK), lambda b, t: (b, t, 0, 0))
    out_shape = (
        jax.ShapeDtypeStruct((B, nb, QW, BLK), bf),
        jax.ShapeDtypeStruct((B, S, QW), bf),
        jax.ShapeDtypeStruct((B, nb, VW, BLK), bf),
        jax.ShapeDtypeStruct((B, MOBA_HEADS, nb, 8, BLK), jnp.int32),
        jax.ShapeDtypeStruct((B, S, RET_WIDTH), bf),
        jax.ShapeDtypeStruct((B, nb, RET_WIDTH, BLK), bf),
        jax.ShapeDtypeStruct((B, S, RET_WIDTH), bf),
        jax.ShapeDtypeStruct((B, S, RET_WIDTH), bf),
    )
    out_specs = (
        tr(QW), nat(QW), tr(VW),
        pl.BlockSpec((None, MOBA_HEADS, nblk, 8, BLK), lambda b, t: (b, 0, t, 0, 0)),
        nat(RET_WIDTH), tr(RET_WIDTH), nat(RET_WIDTH), nat(RET_WIDTH),
    )
    consts = (g, wqT, wk, kpos, wvT, wrq, wrkT, wrv, wrg)
    in_specs = [pl.BlockSpec((None, ROW_TILE, D), lambda b, t: (b, t, 0))] + [
        _const_spec(a.shape) for a in consts]
    return pl.pallas_call(
        _in_proj_kernel, grid=grid, in_specs=in_specs, out_specs=out_specs, out_shape=out_shape,
        scratch_shapes=[pltpu.VMEM((nb, QW), jnp.float32)],
        compiler_params=pltpu.CompilerParams(
            dimension_semantics=("arbitrary", "arbitrary"), vmem_limit_bytes=VMEM_LIMIT),
        name="in_proj",
    )(x, *consts)


def _moba_kernel(slopes_ref, ti_ref, tj_ref, qT_ref, k_ref, vT_ref, sel_ref, o_ref,
                 m_ref, acc_ref, s_ref, smax_ref, alpha_ref, p_ref,
                 s_own_ref, smax_own_ref, alpha_own_ref, p_own_ref, *, n_past_groups):
    h = pl.program_id(1)
    nb = k_ref.shape[0]
    slope = slopes_ref[h]
    kpos = lax.broadcasted_iota(jnp.int32, (BLK, BLK), 0)
    qpos = lax.broadcasted_iota(jnp.int32, (BLK, BLK), 1)
    qterm = -slope * lax.broadcasted_iota(jnp.int32, (1, BLK), 1).astype(jnp.float32)

    m_ref[...] = jnp.full(m_ref.shape, -3e38, jnp.float32)
    acc_ref[...] = jnp.zeros(acc_ref.shape, jnp.float32)

    def own_item(g, e):
        t = g * GROUP_OWN + e
        return t, t

    def past_item(g, e):
        t = g * GROUP + e
        return ti_ref[t], tj_ref[t]

    def run_stream(n_groups, item_of, own, bufs):
        s_ref, smax_ref, alpha_ref, p_ref = bufs
        items, width = s_ref.shape[1], s_ref.shape[2] // BLK

        def scores(par, e, i_t, j0):
            kq = k_ref[j0] if width == 1 else k_ref[pl.ds(j0, width)].reshape(width * BLK, QK_PAD)
            s = jnp.dot(kq, qT_ref[i_t], preferred_element_type=jnp.float32)
            if own:
                s = jnp.where(kpos <= qpos, s, NEG)
            s_ref[par, e] = s
            for w in range(width):
                smax_ref[par, e * width + w:e * width + w + 1, :] = (
                    jnp.max(s[w * BLK:(w + 1) * BLK], axis=0, keepdims=True))

        def values(par, e, i_t, j0):
            vq = vT_ref[j0] if width == 1 else jnp.concatenate(
                [vT_ref[j0 + w] for w in range(width)], axis=1)
            pv = jnp.dot(vq, p_ref[par, e * width * BLK:(e + 1) * width * BLK, :],
                         preferred_element_type=jnp.float32)
            acc_ref[i_t] = alpha_ref[par, e:e + 1, :] * acc_ref[i_t] + pv

        def softmax(par, e, i_t, j0):
            if own:
                cs = [qterm]
            else:
                sel = sel_ref[i_t]
                cs = []
                for w in range(width):
                    j_t = j0 + w
                    hit = (sel[0:1] == j_t) | (sel[1:2] == j_t) | (sel[2:3] == j_t)
                    blk_dist = ((i_t - j_t) * MOBA_BLOCK).astype(jnp.float32)
                    cs.append(jnp.where(hit, -slope * blk_dist, NEG) + qterm)
            m_old = m_ref[pl.ds(i_t, 1), :]
            m_new = m_old
            for w in range(width):
                r = e * width + w
                m_new = jnp.maximum(m_new, smax_ref[par, r:r + 1, :] + cs[w])
            alpha_ref[par, e:e + 1, :] = jnp.exp2(m_old - m_new)
            m_ref[pl.ds(i_t, 1), :] = m_new
            for w in range(width):
                r = e * width + w
                p_ref[par, r * BLK:(r + 1) * BLK, :] = (
                    jnp.exp2(s_ref[par, e, w * BLK:(w + 1) * BLK, :] - (m_new - cs[w])).astype(jnp.bfloat16))

        ahead = 2

        def trip(g, par):
            g_next = jnp.minimum(g + 1, n_groups - 1)
            g_prev = jnp.maximum(g - 1, 0)
            for e in range(-ahead, items):
                if e + ahead < items:
                    scores(1 - par, e + ahead, *item_of(g_next, e + ahead))
                if e >= 0:
                    values(1 - par, e, *item_of(g_prev, e))
                    softmax(par, e, *item_of(g, e))

        alpha_ref[1] = jnp.ones(alpha_ref.shape[1:], jnp.float32)
        p_ref[1] = jnp.zeros(p_ref.shape[1:], jnp.bfloat16)
        for e in range(items):
            scores(0, e, *item_of(0, e))

        def body(w, carry):
            trip(2 * w, 0)
            trip(2 * w + 1, 1)
            return carry

        assert n_groups % 2 == 0
        lax.fori_loop(0, n_groups // 2, body, 0)
        for e in range(items):
            values((n_groups - 1) % 2, e, *item_of(n_groups - 1, e))

    run_stream(nb // GROUP_OWN, own_item, True, (s_own_ref, smax_own_ref, alpha_own_ref, p_own_ref))
    run_stream(n_past_groups, past_item, False, (s_ref, smax_ref, alpha_ref, p_ref))

    def write_out(half):
        unroll = 4
        def some(u, carry):
            for d in range(unroll):
                i = u * unroll + d
                acc = acc_ref[i]
                oT = acc[0:MOBA_HEAD_DIM] / acc[ONES_ROW:ONES_ROW + 1]
                rows = pl.ds(pl.multiple_of(i * BLK, BLK), BLK)
                o_ref[rows, half * MOBA_HEAD_DIM:(half + 1) * MOBA_HEAD_DIM] = oT.T.astype(o_ref.dtype)
            return carry
        lax.fori_loop(0, nb // unroll, some, 0)

    lax.cond(h % 2 == 0, lambda: write_out(0), lambda: write_out(1))


def _moba(slopes_l2, qT, k, vT, sel):
    B, nb, QW, _ = qT.shape
    S = nb * BLK
    assert nb % (2 * GROUP_OWN) == 0 and nb % PAST_WIDTH == 0
    k4 = k.reshape(B, nb, BLK, QW)
    pairs = [(i, j0) for j0 in range(0, nb - 1, PAST_WIDTH) for i in range(j0 + 1, nb)]
    pairs += [(0, 0)] * (-len(pairs) % (2 * GROUP))
    ti = jnp.asarray([p[0] for p in pairs], jnp.int32)
    tj = jnp.asarray([p[1] for p in pairs], jnp.int32)
    once = dict(pipeline_mode=pl.Buffered(1))
    grid_spec = pltpu.PrefetchScalarGridSpec(
        num_scalar_prefetch=3,
        grid=(B, MOBA_HEADS),
        in_specs=[
            pl.BlockSpec((None, nb, QK_PAD, BLK), lambda b, h, *_: (b, 0, h, 0), **once),
            pl.BlockSpec((None, nb, BLK, QK_PAD), lambda b, h, *_: (b, 0, 0, h), **once),
            pl.BlockSpec((None, nb, V_PAD, BLK), lambda b, h, *_: (b, 0, h, 0), **once),
            pl.BlockSpec((None, None, nb, 8, BLK), lambda b, h, *_: (b, h, 0, 0, 0), **once),
        ],
        out_specs=pl.BlockSpec((None, S, 2 * MOBA_HEAD_DIM), lambda b, h, *_: (b, 0, h // 2)),
        scratch_shapes=[
            pltpu.VMEM((nb, BLK), jnp.float32),
            pltpu.VMEM((nb, V_PAD, BLK), jnp.float32),
            pltpu.VMEM((2, GROUP, PAST_WIDTH * BLK, BLK), jnp.float32),
            pltpu.VMEM((2, GROUP * PAST_WIDTH, BLK), jnp.float32),
            pltpu.VMEM((2, max(GROUP, 8), BLK), jnp.float32),
            pltpu.VMEM((2, GROUP * PAST_WIDTH * BLK, BLK), jnp.bfloat16),
            pltpu.VMEM((2, GROUP_OWN, BLK, BLK), jnp.float32),
            pltpu.VMEM((2, 8, BLK), jnp.float32),
            pltpu.VMEM((2, 8, BLK), jnp.float32),
            pltpu.VMEM((2, GROUP_OWN * BLK, BLK), jnp.bfloat16),
        ],
    )
    return pl.pallas_call(
        functools.partial(_moba_kernel, n_past_groups=len(pairs) // GROUP),
        grid_spec=grid_spec,
        out_shape=jax.ShapeDtypeStruct((B, S, MOBA_WIDTH), jnp.bfloat16),
        compiler_params=pltpu.CompilerParams(
            dimension_semantics=("arbitrary", "arbitrary"), vmem_limit_bytes=VMEM_LIMIT),
        name="moba",
    )(slopes_l2, ti, tj, qT, k4, vT, sel)


def _ret_kernel(cd_ref, q_ref, kT_ref, v_ref, g_ref, dintra_ref, qdec_ref, kdec_ref, gn_ref,
                o_ref, state_ref):
    c = pl.program_id(0)

    @pl.when(c == 0)
    def _():
        state_ref[...] = jnp.zeros_like(state_ref)

    for b in range(q_ref.shape[0]):
        for h in range(RET_HEADS):
            cols = slice(h * RET_HEAD_DIM, (h + 1) * RET_HEAD_DIM)
            q = q_ref[b, :, cols]
            kT = kT_ref[b, cols, :]
            v = v_ref[b, :, cols]
            state = state_ref[b, h]
            s = jnp.dot(q, kT, preferred_element_type=jnp.float32) * dintra_ref[h]
            inner = jnp.dot(s.astype(jnp.bfloat16), v, preferred_element_type=jnp.float32)
            qd = (q.astype(jnp.float32) * qdec_ref[h]).astype(jnp.bfloat16)
            cross = jnp.dot(qd, state.astype(jnp.bfloat16), preferred_element_type=jnp.float32)
            kd = (kT.astype(jnp.float32) * kdec_ref[h]).astype(jnp.bfloat16)
            state_ref[b, h] = cd_ref[h] * state + jnp.dot(kd, v, preferred_element_type=jnp.float32)
            o = inner + cross
            mu = jnp.mean(o, axis=-1, keepdims=True)
            d = o - mu
            var = jnp.mean(d * d, axis=-1, keepdims=True)
            gate = g_ref[b, :, cols].astype(jnp.float32)
            y = d * lax.rsqrt(var + GN_EPS) * gn_ref[:, cols] * (gate * jax.nn.sigmoid(gate))
            o_ref[b, :, cols] = y.astype(o_ref.dtype)


def _retention(rq, rkT, rv, rg, ret_norm):
    B, S, W = rq.shape
    nc = S // BLK
    H = RET_HEADS
    C = RET_CHUNK
    f32 = jnp.float32
    gamma = 1.0 - jnp.exp2(-5.0 - jnp.arange(H, dtype=f32))
    log_g = jnp.log(gamma)
    idx = jnp.arange(C, dtype=f32)
    diff = idx[:, None] - idx[None, :]
    dintra = jnp.where(diff >= 0, jnp.exp(log_g[:, None, None] * jnp.maximum(diff, 0.0)), 0.0)
    qdec = jnp.broadcast_to(jnp.exp(log_g[:, None] * (idx + 1.0))[..., None], (H, C, LANES))
    kdec = jnp.exp(log_g[:, None] * (C - 1.0 - idx))[:, None, :]
    cdec = jnp.exp(log_g * C)
    gn = ret_norm.reshape(1, W).astype(f32)

    nat = pl.BlockSpec((B, C, W), lambda c, s: (0, c, 0))
    const = lambda a: pl.BlockSpec(a.shape, lambda c, s: (0,) * a.ndim,
                                   pipeline_mode=pl.Buffered(1))
    grid_spec = pltpu.PrefetchScalarGridSpec(
        num_scalar_prefetch=1,
        grid=(nc,),
        in_specs=[
            nat,
            pl.BlockSpec((B, None, W, C), lambda c, s: (0, c, 0, 0)),
            nat, nat,
            const(dintra), const(qdec), const(kdec), const(gn),
        ],
        out_specs=nat,
        scratch_shapes=[pltpu.VMEM((B, H, RET_HEAD_DIM, RET_HEAD_DIM), f32)],
    )
    return pl.pallas_call(
        _ret_kernel, grid_spec=grid_spec,
        out_shape=jax.ShapeDtypeStruct((B, S, W), jnp.bfloat16),
        compiler_params=pltpu.CompilerParams(
            dimension_semantics=("arbitrary",),
            vmem_limit_bytes=VMEM_LIMIT),
        name="retention",
    )(cdec, rq, rkT, rv, rg, dintra, qdec, kdec, gn)


def _gelu_tanh(u):
    k = -2.0 * math.sqrt(2.0 / math.pi) * LOG2E
    return u / (1.0 + jnp.exp2(u * (k + (k * 0.044715) * (u * u))))


def _shift_rows(cur, prev_tail, n):
    rolled = pltpu.roll(cur, n, axis=0)
    prev_rolled = pltpu.roll(prev_tail, n, axis=0)
    row = lax.broadcasted_iota(jnp.int32, prev_tail.shape, 0)
    head = jnp.where(row < n, prev_rolled, rolled[0:8])
    return jnp.concatenate([head, rolled[8:]], axis=0)


def _tail_kernel(x_ref, moba_ref, ret_ref, p_ref, wo_ref, ffn_g_ref, wup_ref, wgate_ref,
                 convw_ref, convb_ref, wdown_ref, ple_g_ref, wple_ref, wpg_ref, fin_g_ref,
                 o_ref, y_ref, carry_ref, *, tiles_per_seq):
    t = pl.program_id(0)
    R = ROW_TILE

    @pl.when(t % tiles_per_seq == 0)
    def _():
        carry_ref[...] = jnp.zeros_like(carry_ref)

    mix = jnp.concatenate([moba_ref[...], ret_ref[...]], axis=1)
    x1 = x_ref[...] + jnp.dot(mix, wo_ref[...], preferred_element_type=jnp.float32)
    h = _rms(x1, ffn_g_ref[...]).astype(jnp.bfloat16)

    for c in range(N_FF_CHUNKS):
        cols = slice(c * FF_CHUNK, (c + 1) * FF_CHUNK)
        up = jnp.dot(h, wup_ref[:, cols], preferred_element_type=jnp.float32)
        gt = jnp.dot(h, wgate_ref[:, cols], preferred_element_type=jnp.float32)
        prev_tail = carry_ref[c]
        carry_ref[c] = up[R - 8:R, :]
        w = convw_ref[:, cols]
        u = (convb_ref[:, cols]
             + w[0:1, :] * _shift_rows(up, prev_tail, 2)
             + w[1:2, :] * _shift_rows(up, prev_tail, 1)
             + w[2:3, :] * up)
        y_ref[:, cols] = (_gelu_tanh(u) * gt).astype(jnp.bfloat16)
    x2 = x1 + jnp.dot(y_ref[...], wdown_ref[...], preferred_element_type=jnp.float32)

    hn = _rms(x2, ple_g_ref[...]).astype(jnp.bfloat16)
    g = jax.nn.sigmoid(jnp.dot(hn, wpg_ref[...], preferred_element_type=jnp.float32))
    pe = jnp.dot(p_ref[...].astype(jnp.bfloat16), wple_ref[...], preferred_element_type=jnp.float32)
    x3 = x2 + pe * g
    o_ref[...] = _rms(x3, fin_g_ref[...])


def _tail(x2d, moba2d, ret2d, p2d, wo, ffn_g, wup, wgate, convw, convb, wdown, ple_g, wple,
          wpg, fin_g, seq_len):
    T, D = x2d.shape
    R = ROW_TILE
    row = lambda w: pl.BlockSpec((R, w), lambda t: (t, 0))
    consts = (wo, ffn_g, wup, wgate, convw, convb, wdown, ple_g, wple, wpg, fin_g)
    in_specs = [row(D), row(MOBA_WIDTH), row(RET_WIDTH), row(PLE_DIM)] + [
        _const_spec(a.shape) for a in consts]
    return pl.pallas_call(
        functools.partial(_tail_kernel, tiles_per_seq=seq_len // R),
        grid=(T // R,), in_specs=in_specs, out_specs=row(D),
        out_shape=jax.ShapeDtypeStruct((T, D), jnp.float32),
        scratch_shapes=[
            pltpu.VMEM((R, D_FF), jnp.bfloat16),
            pltpu.VMEM((N_FF_CHUNKS, 8, FF_CHUNK), jnp.float32),
        ],
        compiler_params=pltpu.CompilerParams(
            dimension_semantics=("arbitrary",), vmem_limit_bytes=VMEM_LIMIT),
        name="tail",
    )(x2d, moba2d, ret2d, p2d, *consts)


def _pad_heads(w, pad):
    d = w.shape[0]
    w3 = w.reshape(d, MOBA_HEADS, MOBA_HEAD_DIM)
    return jnp.pad(w3, ((0, 0), (0, 0), (0, pad - MOBA_HEAD_DIM))).reshape(d, MOBA_HEADS * pad)


def kernel(x, p, attn_norm, w_in, ret_norm, w_out, ffn_norm, w_up, w_gate, conv_w, conv_b, w_down,
           ple_norm, w_ple, w_ple_gate, final_norm):
    B, S, D = x.shape
    assert D == D_MODEL and S % ROW_TILE == 0 and w_in.shape[0] == 1
    bf, f32 = jnp.bfloat16, jnp.float32
    MW, RW = MOBA_WIDTH, RET_WIDTH
    w = w_in[0].astype(bf)
    wqT = w[:, 0:MW].T
    wk = _pad_heads(w[:, MW:2 * MW], QK_PAD)
    wvT = w[:, 2 * MW:3 * MW].T
    o = 3 * MW
    wrq = w[:, o:o + RW]
    wrkT = w[:, o + RW:o + 2 * RW].T
    wrv = w[:, o + 2 * RW:o + 3 * RW]
    wrg = w[:, o + 3 * RW:o + 4 * RW]
    row = lambda a: a.reshape(1, -1).astype(f32)

    slopes_l2 = jnp.exp2(-8.0 * jnp.arange(1, MOBA_HEADS + 1, dtype=f32) / MOBA_HEADS) * LOG2E
    kterm = slopes_l2[None, :] * jnp.arange(BLK, dtype=f32)[:, None]
    kterm_hi = lax.bitcast_convert_type(
        lax.bitcast_convert_type(kterm, jnp.uint32) & jnp.uint32(0xFFFF0000), f32)
    kterm_lo = kterm - kterm_hi
    kpos = jnp.zeros((BLK, MOBA_HEADS, QK_PAD), f32)
    kpos = kpos.at[:, :, ALIBI_ROW].set(kterm_hi).at[:, :, ALIBI_ROW + 1].set(kterm_lo)
    kpos = kpos.reshape(BLK, MOBA_HEADS * QK_PAD)

    qT, k, vT, sel, rq, rkT, rv, rg = _in_proj(
        x, row(attn_norm[0]), wqT, wk, kpos, wvT, wrq, wrkT, wrv, wrg)

    moba_out = _moba(slopes_l2, qT, k, vT, sel)
    ret_out = _retention(rq, rkT, rv, rg, ret_norm[0])

    wo = w_out[0].astype(bf)
    out = _tail(
        x.reshape(B * S, D), moba_out.reshape(B * S, MW), ret_out.reshape(B * S, RW),
        p[0].reshape(B * S, PLE_DIM),
        wo, row(ffn_norm[0]),
        w_up[0].astype(bf), w_gate[0].astype(bf), conv_w[0].astype(f32), row(conv_b[0]),
        w_down[0].astype(bf),
        row(ple_norm[0]), w_ple[0].astype(bf), w_ple_gate[0].astype(bf), row(final_norm),
        seq_len=S)
    return out.reshape(B, S, D)
```

```python
import functools
import math

import jax
import jax.numpy as jnp
from jax import lax
from jax.experimental import pallas as pl
from jax.experimental.pallas import tpu as pltpu

D_MODEL = 1024
PLE_DIM = 256
MOBA_HEADS = 8
MOBA_HEAD_DIM = 64
MOBA_WIDTH = MOBA_HEADS * MOBA_HEAD_DIM
MOBA_BLOCK = 256
MOBA_TOPK = 3
RET_HEADS = 4
RET_HEAD_DIM = 128
RET_WIDTH = RET_HEADS * RET_HEAD_DIM
RET_CHUNK = 256
D_FF = 2816
CONV_WIDTH = 3
RMS_EPS = 1e-6
GN_EPS = 1e-5

BLK = 256
LANES = 128
FF_CHUNK = 256
N_FF_CHUNKS = D_FF // FF_CHUNK
ROW_TILE = 512
VMEM_LIMIT = 56 * 1024 * 1024
NEG = -1e30
LOG2E = math.log2(math.e)

QK_PAD = LANES
ALIBI_ROW = MOBA_HEAD_DIM
V_PAD = 80
ONES_ROW = MOBA_HEAD_DIM
PAST_WIDTH = 1
GROUP = 8
GROUP_OWN = 8

_NT = (((1,), (1,)), ((), ()))


def _const_spec(shape):
    nd = len(shape)
    return pl.BlockSpec(shape, lambda *_: (0,) * nd, pipeline_mode=pl.Buffered(1))


def _rms(x, g):
    ms = jnp.mean(x * x, axis=-1, keepdims=True)
    return x * lax.rsqrt(ms + RMS_EPS) * g


def _in_proj_kernel(x_ref, g_ref, wqT_ref, wk_ref, kpos_ref, wvT_ref,
                    wrq_ref, wrkT_ref, wrv_ref, wrg_ref,
                    qT_ref, k_ref, vT_ref, sel_ref, rq_ref, rkT_ref, rv_ref, rg_ref, kmean_ref):
    t = pl.program_id(1)
    h = _rms(x_ref[...], g_ref[...]).astype(jnp.bfloat16)
    nblk = ROW_TILE // BLK
    nb = kmean_ref.shape[0]

    @pl.when(t == 0)
    def _():
        kmean_ref[...] = jnp.zeros_like(kmean_ref)

    def nt(w_ref):
        return lax.dot_general(w_ref[...], h, _NT, preferred_element_type=jnp.float32)

    def nn(w_ref):
        return jnp.dot(h, w_ref[...], preferred_element_type=jnp.float32)

    Dh = MOBA_HEAD_DIM
    qT = (nt(wqT_ref) * (Dh ** -0.5 * LOG2E)).astype(jnp.bfloat16)
    vT = nt(wvT_ref).astype(jnp.bfloat16)
    rkT = (nt(wrkT_ref) * (RET_HEAD_DIM ** -0.5)).astype(jnp.bfloat16)
    q_pad = (lax.broadcasted_iota(jnp.int32, (QK_PAD - Dh, BLK), 0) < 2).astype(jnp.bfloat16)
    v_pad = (lax.broadcasted_iota(jnp.int32, (V_PAD - Dh, BLK), 0) < 1).astype(jnp.bfloat16)
    for b in range(nblk):
        sl = slice(b * BLK, (b + 1) * BLK)
        for a in range(MOBA_HEADS):
            qT_ref[b, a * QK_PAD:a * QK_PAD + Dh, :] = qT[a * Dh:(a + 1) * Dh, sl]
            qT_ref[b, a * QK_PAD + Dh:(a + 1) * QK_PAD, :] = q_pad
            vT_ref[b, a * V_PAD:a * V_PAD + Dh, :] = vT[a * Dh:(a + 1) * Dh, sl]
            vT_ref[b, a * V_PAD + Dh:(a + 1) * V_PAD, :] = v_pad
        rkT_ref[b] = rkT[:, sl]
    k = nn(wk_ref)
    for b in range(nblk):
        kb = k[b * BLK:(b + 1) * BLK]
        kmean_ref[pl.ds(t * nblk + b, 1), :] = jnp.mean(kb, axis=0, keepdims=True)
        k_ref[b * BLK:(b + 1) * BLK, :] = (kb + kpos_ref[...]).astype(jnp.bfloat16)
    rq_ref[...] = nn(wrq_ref).astype(jnp.bfloat16)
    rv_ref[...] = nn(wrv_ref).astype(jnp.bfloat16)
    rg_ref[...] = nn(wrg_ref).astype(jnp.bfloat16)

    km = kmean_ref[...]
    km_hi = km.astype(jnp.bfloat16)
    km_lo = (km - km_hi.astype(jnp.float32)).astype(jnp.bfloat16)
    blk_id = lax.broadcasted_iota(jnp.int32, (nb, BLK), 0)
    none = jnp.full((8 - MOBA_TOPK, BLK), nb, jnp.int32)

    for b in range(nblk):
        i = t * nblk + b
        for a in range(MOBA_HEADS):
            dims = slice(a * QK_PAD, a * QK_PAD + Dh)
            qTh = qT[a * Dh:(a + 1) * Dh, b * BLK:(b + 1) * BLK]
            gate = (jnp.dot(km_hi[:, dims], qTh, preferred_element_type=jnp.float32)
                    + jnp.dot(km_lo[:, dims], qTh, preferred_element_type=jnp.float32))
            gate = jnp.where(blk_id < i, gate, -jnp.inf)
            picks = []
            for _ in range(MOBA_TOPK):
                top = jnp.max(gate, axis=0, keepdims=True)
                is_top = (gate == top) & (top > -jnp.inf)
                first = jnp.min(jnp.where(is_top, blk_id, nb), axis=0, keepdims=True)
                picks.append(first)
                gate = jnp.where(blk_id == first, -jnp.inf, gate)
            sel_ref[a, b] = jnp.concatenate(picks + [none], axis=0)


def _in_proj(x, g, wqT, wk, kpos, wvT, wrq, wrkT, wrv, wrg):
    B, S, D = x.shape
    nb = S // BLK
    nblk = ROW_TILE // BLK
    grid = (B, S // ROW_TILE)
    bf = jnp.bfloat16
    QW, VW = MOBA_HEADS * QK_PAD, MOBA_HEADS * V_PAD
    nat = lambda w: pl.BlockSpec((None, ROW_TILE, w), lambda b, t: (b, t, 0))
    tr = lambda w: pl.BlockSpec((None, nblk, w, BLK), lambda b, t: (b, t, 0, 0))
    out_shape = (
        jax.ShapeDtypeStruct((B, nb, QW, BLK), bf),
        jax.ShapeDtypeStruct((B, S, QW), bf),
        jax.ShapeDtypeStruct((B, nb, VW, BLK), bf),
        jax.ShapeDtypeStruct((B, MOBA_HEADS, nb, 8, BLK), jnp.int32),
        jax.ShapeDtypeStruct((B, S, RET_WIDTH), bf),
        jax.ShapeDtypeStruct((B, nb, RET_WIDTH, BLK), bf),
        jax.ShapeDtypeStruct((B, S, RET_WIDTH), bf),
        jax.ShapeDtypeStruct((B, S, RET_WIDTH), bf),
    )
    out_specs = (
        tr(QW), nat(QW), tr(VW),
        pl.BlockSpec((None, MOBA_HEADS, nblk, 8, BLK), lambda b, t: (b, 0, t, 0, 0)),
        nat(RET_WIDTH), tr(RET_WIDTH), nat(RET_WIDTH), nat(RET_WIDTH),
    )
    consts = (g, wqT, wk, kpos, wvT, wrq, wrkT, wrv, wrg)
    in_specs = [pl.BlockSpec((None, ROW_TILE, D), lambda b, t: (b, t, 0))] + [
        _const_spec(a.shape) for a in consts]
    return pl.pallas_call(
        _in_proj_kernel, grid=grid, in_specs=in_specs, out_specs=out_specs, out_shape=out_shape,
        scratch_shapes=[pltpu.VMEM((nb, QW), jnp.float32)],
        compiler_params=pltpu.CompilerParams(
            dimension_semantics=("arbitrary", "arbitrary"), vmem_limit_bytes=VMEM_LIMIT),
        name="in_proj",
    )(x, *consts)


def _moba_kernel(slopes_ref, ti_ref, tj_ref, qT_ref, k_ref, vT_ref, sel_ref, o_ref,
                 m_ref, acc_ref, s_ref, smax_ref, alpha_ref, p_ref,
                 s_own_ref, smax_own_ref, alpha_own_ref, p_own_ref, *, n_past_groups):
    h = pl.program_id(1)
    nb = k_ref.shape[0]
    slope = slopes_ref[h]
    kpos = lax.broadcasted_iota(jnp.int32, (BLK, BLK), 0)
    qpos = lax.broadcasted_iota(jnp.int32, (BLK, BLK), 1)
    qterm = -slope * lax.broadcasted_iota(jnp.int32, (1, BLK), 1).astype(jnp.float32)

    m_ref[...] = jnp.full(m_ref.shape, -3e38, jnp.float32)
    acc_ref[...] = jnp.zeros(acc_ref.shape, jnp.float32)

    def own_item(g, e):
        t = g * GROUP_OWN + e
        return t, t

    def past_item(g, e):
        t = g * GROUP + e
        return ti_ref[t], tj_ref[t]

    def run_stream(n_groups, item_of, own, bufs):
        s_ref, smax_ref, alpha_ref, p_ref = bufs
        items, width = s_ref.shape[1], s_ref.shape[2] // BLK

        def scores(par, e, i_t, j0):
            kq = k_ref[j0] if width == 1 else k_ref[pl.ds(j0, width)].reshape(width * BLK, QK_PAD)
            s = jnp.dot(kq, qT_ref[i_t], preferred_element_type=jnp.float32)
            if own:
                s = jnp.where(kpos <= qpos, s, NEG)
            s_ref[par, e] = s
            for w in range(width):
                smax_ref[par, e * width + w:e * width + w + 1, :] = (
                    jnp.max(s[w * BLK:(w + 1) * BLK], axis=0, keepdims=True))

        def values(par, e, i_t, j0):
            vq = vT_ref[j0] if width == 1 else jnp.concatenate(
                [vT_ref[j0 + w] for w in range(width)], axis=1)
            pv = jnp.dot(vq, p_ref[par, e * width * BLK:(e + 1) * width * BLK, :],
                         preferred_element_type=jnp.float32)
            acc_ref[i_t] = alpha_ref[par, e:e + 1, :] * acc_ref[i_t] + pv

        def softmax(par, e, i_t, j0):
            if own:
                cs = [qterm]
            else:
                sel = sel_ref[i_t]
                cs = []
                for w in range(width):
                    j_t = j0 + w
                    hit = (sel[0:1] == j_t) | (sel[1:2] == j_t) | (sel[2:3] == j_t)
                    blk_dist = ((i_t - j_t) * MOBA_BLOCK).astype(jnp.float32)
                    cs.append(jnp.where(hit, -slope * blk_dist, NEG) + qterm)
            m_old = m_ref[pl.ds(i_t, 1), :]
            m_new = m_old
            for w in range(width):
                r = e * width + w
                m_new = jnp.maximum(m_new, smax_ref[par, r:r + 1, :] + cs[w])
            alpha_ref[par, e:e + 1, :] = jnp.exp2(m_old - m_new)
            m_ref[pl.ds(i_t, 1), :] = m_new
            for w in range(width):
                r = e * width + w
                p_ref[par, r * BLK:(r + 1) * BLK, :] = (
                    jnp.exp2(s_ref[par, e, w * BLK:(w + 1) * BLK, :] - (m_new - cs[w])).astype(jnp.bfloat16))

        ahead = 2

        def trip(g, par):
            g_next = jnp.minimum(g + 1, n_groups - 1)
            g_prev = jnp.maximum(g - 1, 0)
            for e in range(-ahead, items):
                if e + ahead < items:
                    scores(1 - par, e + ahead, *item_of(g_next, e + ahead))
                if e >= 0:
                    values(1 - par, e, *item_of(g_prev, e))
                    softmax(par, e, *item_of(g, e))

        alpha_ref[1] = jnp.ones(alpha_ref.shape[1:], jnp.float32)
        p_ref[1] = jnp.zeros(p_ref.shape[1:], jnp.bfloat16)
        for e in range(items):
            scores(0, e, *item_of(0, e))

        def body(w, carry):
            trip(2 * w, 0)
            trip(2 * w + 1, 1)
            return carry

        assert n_groups % 2 == 0
        lax.fori_loop(0, n_groups // 2, body, 0)
        for e in range(items):
            values((n_groups - 1) % 2, e, *item_of(n_groups - 1, e))

    run_stream(nb // GROUP_OWN, own_item, True, (s_own_ref, smax_own_ref, alpha_own_ref, p_own_ref))
    run_stream(n_past_groups, past_item, False, (s_ref, smax_ref, alpha_ref, p_ref))

    def write_out(half):
        unroll = 4
        def some(u, carry):
            for d in range(unroll):
                i = u * unroll + d
                acc = acc_ref[i]
                oT = acc[0:MOBA_HEAD_DIM] / acc[ONES_ROW:ONES_ROW + 1]
                rows = pl.ds(pl.multiple_of(i * BLK, BLK), BLK)
                o_ref[rows, half * MOBA_HEAD_DIM:(half + 1) * MOBA_HEAD_DIM] = oT.T.astype(o_ref.dtype)
            return carry
        lax.fori_loop(0, nb // unroll, some, 0)

    lax.cond(h % 2 == 0, lambda: write_out(0), lambda: write_out(1))


def _moba(slopes_l2, qT, k, vT, sel):
    B, nb, QW, _ = qT.shape
    S = nb * BLK
    assert nb % (2 * GROUP_OWN) == 0 and nb % PAST_WIDTH == 0
    k4 = k.reshape(B, nb, BLK, QW)
    pairs = [(i, j0) for j0 in range(0, nb - 1, PAST_WIDTH) for i in range(j0 + 1, nb)]
    pairs += [(0, 0)] * (-len(pairs) % (2 * GROUP))
    ti = jnp.asarray([p[0] for p in pairs], jnp.int32)
    tj = jnp.asarray([p[1] for p in pairs], jnp.int32)
    grid_spec = pltpu.PrefetchScalarGridSpec(
        num_scalar_prefetch=3,
        grid=(B, MOBA_HEADS),
        in_specs=[
            pl.BlockSpec((None, nb, QK_PAD, BLK), lambda b, h, *_: (b, 0, h, 0)),
            pl.BlockSpec((None, nb, BLK, QK_PAD), lambda b, h, *_: (b, 0, 0, h)),
            pl.BlockSpec((None, nb, V_PAD, BLK), lambda b, h, *_: (b, 0, h, 0)),
            pl.BlockSpec((None, None, nb, 8, BLK), lambda b, h, *_: (b, h, 0, 0, 0)),
        ],
        out_specs=pl.BlockSpec((None, S, 2 * MOBA_HEAD_DIM), lambda b, h, *_: (b, 0, h // 2)),
        scratch_shapes=[
            pltpu.VMEM((nb, BLK), jnp.float32),
            pltpu.VMEM((nb, V_PAD, BLK), jnp.float32),
            pltpu.VMEM((2, GROUP, PAST_WIDTH * BLK, BLK), jnp.float32),
            pltpu.VMEM((2, GROUP * PAST_WIDTH, BLK), jnp.float32),
            pltpu.VMEM((2, max(GROUP, 8), BLK), jnp.float32),
            pltpu.VMEM((2, GROUP * PAST_WIDTH * BLK, BLK), jnp.bfloat16),
            pltpu.VMEM((2, GROUP_OWN, BLK, BLK), jnp.float32),
            pltpu.VMEM((2, 8, BLK), jnp.float32),
            pltpu.VMEM((2, 8, BLK), jnp.float32),
            pltpu.VMEM((2, GROUP_OWN * BLK, BLK), jnp.bfloat16),
        ],
    )
    return pl.pallas_call(
        functools.partial(_moba_kernel, n_past_groups=len(pairs) // GROUP),
        grid_spec=grid_spec,
        out_shape=jax.ShapeDtypeStruct((B, S, MOBA_WIDTH), jnp.bfloat16),
        compiler_params=pltpu.CompilerParams(
            dimension_semantics=("arbitrary", "arbitrary"), vmem_limit_bytes=VMEM_LIMIT),
        name="moba",
    )(slopes_l2, ti, tj, qT, k4, vT, sel)


def _ret_kernel(cd_ref, q_ref, kT_ref, v_ref, g_ref, dintra_ref, qdec_ref, kdec_ref, gn_ref,
                o_ref, state_ref):
    c = pl.program_id(0)

    @pl.when(c == 0)
    def _():
        state_ref[...] = jnp.zeros_like(state_ref)

    for b in range(q_ref.shape[0]):
        for h in range(RET_HEADS):
            cols = slice(h * RET_HEAD_DIM, (h + 1) * RET_HEAD_DIM)
            q = q_ref[b, :, cols]
            kT = kT_ref[b, cols, :]
            v = v_ref[b, :, cols]
            state = state_ref[b, h]
            s = jnp.dot(q, kT, preferred_element_type=jnp.float32) * dintra_ref[h]
            inner = jnp.dot(s.astype(jnp.bfloat16), v, preferred_element_type=jnp.float32)
            qd = (q.astype(jnp.float32) * qdec_ref[h]).astype(jnp.bfloat16)
            cross = jnp.dot(qd, state.astype(jnp.bfloat16), preferred_element_type=jnp.float32)
            kd = (kT.astype(jnp.float32) * kdec_ref[h]).astype(jnp.bfloat16)
            state_ref[b, h] = cd_ref[h] * state + jnp.dot(kd, v, preferred_element_type=jnp.float32)
            o = inner + cross
            mu = jnp.mean(o, axis=-1, keepdims=True)
            d = o - mu
            var = jnp.mean(d * d, axis=-1, keepdims=True)
            gate = g_ref[b, :, cols].astype(jnp.float32)
            y = d * lax.rsqrt(var + GN_EPS) * gn_ref[:, cols] * (gate * jax.nn.sigmoid(gate))
            o_ref[b, :, cols] = y.astype(o_ref.dtype)


def _retention(rq, rkT, rv, rg, ret_norm):
    B, S, W = rq.shape
    nc = S // BLK
    H = RET_HEADS
    C = RET_CHUNK
    f32 = jnp.float32
    gamma = 1.0 - jnp.exp2(-5.0 - jnp.arange(H, dtype=f32))
    log_g = jnp.log(gamma)
    idx = jnp.arange(C, dtype=f32)
    diff = idx[:, None] - idx[None, :]
    dintra = jnp.where(diff >= 0, jnp.exp(log_g[:, None, None] * jnp.maximum(diff, 0.0)), 0.0)
    qdec = jnp.broadcast_to(jnp.exp(log_g[:, None] * (idx + 1.0))[..., None], (H, C, LANES))
    kdec = jnp.exp(log_g[:, None] * (C - 1.0 - idx))[:, None, :]
    cdec = jnp.exp(log_g * C)
    gn = ret_norm.reshape(1, W).astype(f32)

    nat = pl.BlockSpec((B, C, W), lambda c, s: (0, c, 0))
    const = lambda a: pl.BlockSpec(a.shape, lambda c, s: (0,) * a.ndim,
                                   pipeline_mode=pl.Buffered(1))
    grid_spec = pltpu.PrefetchScalarGridSpec(
        num_scalar_prefetch=1,
        grid=(nc,),
        in_specs=[
            nat,
            pl.BlockSpec((B, None, W, C), lambda c, s: (0, c, 0, 0)),
            nat, nat,
            const(dintra), const(qdec), const(kdec), const(gn),
        ],
        out_specs=nat,
        scratch_shapes=[pltpu.VMEM((B, H, RET_HEAD_DIM, RET_HEAD_DIM), f32)],
    )
    return pl.pallas_call(
        _ret_kernel, grid_spec=grid_spec,
        out_shape=jax.ShapeDtypeStruct((B, S, W), jnp.bfloat16),
        compiler_params=pltpu.CompilerParams(
            dimension_semantics=("arbitrary",),
            vmem_limit_bytes=VMEM_LIMIT),
        name="retention",
    )(cdec, rq, rkT, rv, rg, dintra, qdec, kdec, gn)


def _gelu_tanh(u):
    k = -2.0 * math.sqrt(2.0 / math.pi) * LOG2E
    return u / (1.0 + jnp.exp2(u * (k + (k * 0.044715) * (u * u))))


def _shift_rows(cur, prev_tail, n):
    rolled = pltpu.roll(cur, n, axis=0)
    prev_rolled = pltpu.roll(prev_tail, n, axis=0)
    row = lax.broadcasted_iota(jnp.int32, prev_tail.shape, 0)
    head = jnp.where(row < n, prev_rolled, rolled[0:8])
    return jnp.concatenate([head, rolled[8:]], axis=0)


def _tail_kernel(x_ref, moba_ref, ret_ref, p_ref, wo_ref, ffn_g_ref, wup_ref, wgate_ref,
                 convw_ref, convb_ref, wdown_ref, ple_g_ref, wple_ref, wpg_ref, fin_g_ref,
                 o_ref, y_ref, carry_ref, *, tiles_per_seq):
    t = pl.program_id(0)
    R = ROW_TILE

    @pl.when(t % tiles_per_seq == 0)
    def _():
        carry_ref[...] = jnp.zeros_like(carry_ref)

    mix = jnp.concatenate([moba_ref[...], ret_ref[...]], axis=1)
    x1 = x_ref[...] + jnp.dot(mix, wo_ref[...], preferred_element_type=jnp.float32)
    h = _rms(x1, ffn_g_ref[...]).astype(jnp.bfloat16)

    for c in range(N_FF_CHUNKS):
        cols = slice(c * FF_CHUNK, (c + 1) * FF_CHUNK)
        up = jnp.dot(h, wup_ref[:, cols], preferred_element_type=jnp.float32)
        gt = jnp.dot(h, wgate_ref[:, cols], preferred_element_type=jnp.float32)
        prev_tail = carry_ref[c]
        carry_ref[c] = up[R - 8:R, :]
        w = convw_ref[:, cols]
        u = (convb_ref[:, cols]
             + w[0:1, :] * _shift_rows(up, prev_tail, 2)
             + w[1:2, :] * _shift_rows(up, prev_tail, 1)
             + w[2:3, :] * up)
        y_ref[:, cols] = (_gelu_tanh(u) * gt).astype(jnp.bfloat16)
    x2 = x1 + jnp.dot(y_ref[...], wdown_ref[...], preferred_element_type=jnp.float32)

    hn = _rms(x2, ple_g_ref[...]).astype(jnp.bfloat16)
    g = jax.nn.sigmoid(jnp.dot(hn, wpg_ref[...], preferred_element_type=jnp.float32))
    pe = jnp.dot(p_ref[...].astype(jnp.bfloat16), wple_ref[...], preferred_element_type=jnp.float32)
    x3 = x2 + pe * g
    o_ref[...] = _rms(x3, fin_g_ref[...])


def _tail(x2d, moba2d, ret2d, p2d, wo, ffn_g, wup, wgate, convw, convb, wdown, ple_g, wple,
          wpg, fin_g, seq_len):
    T, D = x2d.shape
    R = ROW_TILE
    row = lambda w: pl.BlockSpec((R, w), lambda t: (t, 0))
    consts = (wo, ffn_g, wup, wgate, convw, convb, wdown, ple_g, wple, wpg, fin_g)
    in_specs = [row(D), row(MOBA_WIDTH), row(RET_WIDTH), row(PLE_DIM)] + [
        _const_spec(a.shape) for a in consts]
    return pl.pallas_call(
        functools.partial(_tail_kernel, tiles_per_seq=seq_len // R),
        grid=(T // R,), in_specs=in_specs, out_specs=row(D),
        out_shape=jax.ShapeDtypeStruct((T, D), jnp.float32),
        scratch_shapes=[
            pltpu.VMEM((R, D_FF), jnp.bfloat16),
            pltpu.VMEM((N_FF_CHUNKS, 8, FF_CHUNK), jnp.float32),
        ],
        compiler_params=pltpu.CompilerParams(
            dimension_semantics=("arbitrary",), vmem_limit_bytes=VMEM_LIMIT),
        name="tail",
    )(x2d, moba2d, ret2d, p2d, *consts)


def _pad_heads(w, pad):
    d = w.shape[0]
    w3 = w.reshape(d, MOBA_HEADS, MOBA_HEAD_DIM)
    return jnp.pad(w3, ((0, 0), (0, 0), (0, pad - MOBA_HEAD_DIM))).reshape(d, MOBA_HEADS * pad)


def kernel(x, p, attn_norm, w_in, ret_norm, w_out, ffn_norm, w_up, w_gate, conv_w, conv_b, w_down,
           ple_norm, w_ple, w_ple_gate, final_norm):
    B, S, D = x.shape
    assert D == D_MODEL and S % ROW_TILE == 0 and w_in.shape[0] == 1
    bf, f32 = jnp.bfloat16, jnp.float32
    MW, RW = MOBA_WIDTH, RET_WIDTH
    w = w_in[0].astype(bf)
    wqT = w[:, 0:MW].T
    wk = _pad_heads(w[:, MW:2 * MW], QK_PAD)
    wvT = w[:, 2 * MW:3 * MW].T
    o = 3 * MW
    wrq = w[:, o:o + RW]
    wrkT = w[:, o + RW:o + 2 * RW].T
    wrv = w[:, o + 2 * RW:o + 3 * RW]
    wrg = w[:, o + 3 * RW:o + 4 * RW]
    row = lambda a: a.reshape(1, -1).astype(f32)

    slopes_l2 = jnp.exp2(-8.0 * jnp.arange(1, MOBA_HEADS + 1, dtype=f32) / MOBA_HEADS) * LOG2E
    kterm = slopes_l2[None, :] * jnp.arange(BLK, dtype=f32)[:, None]
    kterm_hi = lax.bitcast_convert_type(
        lax.bitcast_convert_type(kterm, jnp.uint32) & jnp.uint32(0xFFFF0000), f32)
    kterm_lo = kterm - kterm_hi
    kpos = jnp.zeros((BLK, MOBA_HEADS, QK_PAD), f32)
    kpos = kpos.at[:, :, ALIBI_ROW].set(kterm_hi).at[:, :, ALIBI_ROW + 1].set(kterm_lo)
    kpos = kpos.reshape(BLK, MOBA_HEADS * QK_PAD)

    qT, k, vT, sel, rq, rkT, rv, rg = _in_proj(
        x, row(attn_norm[0]), wqT, wk, kpos, wvT, wrq, wrkT, wrv, wrg)

    moba_out = _moba(slopes_l2, qT, k, vT, sel)
    ret_out = _retention(rq, rkT, rv, rg, ret_norm[0])

    wo = w_out[0].astype(bf)
    out = _tail(
        x.reshape(B * S, D), moba_out.reshape(B * S, MW), ret_out.reshape(B * S, RW),
        p[0].reshape(B * S, PLE_DIM),
        wo, row(ffn_norm[0]),
        w_up[0].astype(bf), w_gate[0].astype(bf), conv_w[0].astype(f32), row(conv_b[0]),
        w_down[0].astype(bf),
        row(ple_norm[0]), w_ple[0].astype(bf), w_ple_gate[0].astype(bf), row(final_norm),
        seq_len=S)
    return out.reshape(B, S, D)
```

```python
import functools
import math

import jax
import jax.numpy as jnp
from jax import lax
from jax.experimental import pallas as pl
from jax.experimental.pallas import tpu as pltpu

D_MODEL = 1024
PLE_DIM = 256
MOBA_HEADS = 8
MOBA_HEAD_DIM = 64
MOBA_WIDTH = MOBA_HEADS * MOBA_HEAD_DIM
MOBA_BLOCK = 256
MOBA_TOPK = 3
RET_HEADS = 4
RET_HEAD_DIM = 128
RET_WIDTH = RET_HEADS * RET_HEAD_DIM
RET_CHUNK = 256
D_FF = 2816
CONV_WIDTH = 3
RMS_EPS = 1e-6
GN_EPS = 1e-5

BLK = 256
LANES = 128
FF_CHUNK = 256
N_FF_CHUNKS = D_FF // FF_CHUNK
ROW_TILE = 512
VMEM_LIMIT = 56 * 1024 * 1024
NEG = -1e30
LOG2E = math.log2(math.e)

QK_PAD = LANES
ALIBI_ROW = MOBA_HEAD_DIM
V_PAD = 80
ONES_ROW = MOBA_HEAD_DIM
PAST_WIDTH = 1
GROUP = 16
GROUP_OWN = 8

_NT = (((1,), (1,)), ((), ()))


def _const_spec(shape):
    nd = len(shape)
    return pl.BlockSpec(shape, lambda *_: (0,) * nd, pipeline_mode=pl.Buffered(1))


def _rms(x, g):
    ms = jnp.mean(x * x, axis=-1, keepdims=True)
    return x * lax.rsqrt(ms + RMS_EPS) * g


def _in_proj_kernel(x_ref, g_ref, wqT_ref, wk_ref, kpos_ref, wvT_ref,
                    wrq_ref, wrkT_ref, wrv_ref, wrg_ref,
                    qT_ref, k_ref, vT_ref, sel_ref, rq_ref, rkT_ref, rv_ref, rg_ref, kmean_ref):
    t = pl.program_id(1)
    h = _rms(x_ref[...], g_ref[...]).astype(jnp.bfloat16)
    nblk = ROW_TILE // BLK
    nb = kmean_ref.shape[0]

    @pl.when(t == 0)
    def _():
        kmean_ref[...] = jnp.zeros_like(kmean_ref)

    def nt(w_ref):
        return lax.dot_general(w_ref[...], h, _NT, preferred_element_type=jnp.float32)

    def nn(w_ref):
        return jnp.dot(h, w_ref[...], preferred_element_type=jnp.float32)

    Dh = MOBA_HEAD_DIM
    qT = (nt(wqT_ref) * (Dh ** -0.5 * LOG2E)).astype(jnp.bfloat16)
    vT = nt(wvT_ref).astype(jnp.bfloat16)
    rkT = (nt(wrkT_ref) * (RET_HEAD_DIM ** -0.5)).astype(jnp.bfloat16)
    q_pad = (lax.broadcasted_iota(jnp.int32, (QK_PAD - Dh, BLK), 0) < 2).astype(jnp.bfloat16)
    v_pad = (lax.broadcasted_iota(jnp.int32, (V_PAD - Dh, BLK), 0) < 1).astype(jnp.bfloat16)
    for b in range(nblk):
        sl = slice(b * BLK, (b + 1) * BLK)
        for a in range(MOBA_HEADS):
            qT_ref[b, a * QK_PAD:a * QK_PAD + Dh, :] = qT[a * Dh:(a + 1) * Dh, sl]
            qT_ref[b, a * QK_PAD + Dh:(a + 1) * QK_PAD, :] = q_pad
            vT_ref[b, a * V_PAD:a * V_PAD + Dh, :] = vT[a * Dh:(a + 1) * Dh, sl]
            vT_ref[b, a * V_PAD + Dh:(a + 1) * V_PAD, :] = v_pad
        rkT_ref[b] = rkT[:, sl]
    k = nn(wk_ref)
    for b in range(nblk):
        kb = k[b * BLK:(b + 1) * BLK]
        kmean_ref[pl.ds(t * nblk + b, 1), :] = jnp.mean(kb, axis=0, keepdims=True)
        k_ref[b * BLK:(b + 1) * BLK, :] = (kb + kpos_ref[...]).astype(jnp.bfloat16)
    rq_ref[...] = nn(wrq_ref).astype(jnp.bfloat16)
    rv_ref[...] = nn(wrv_ref).astype(jnp.bfloat16)
    rg_ref[...] = nn(wrg_ref).astype(jnp.bfloat16)

    km = kmean_ref[...]
    km_hi = km.astype(jnp.bfloat16)
    km_lo = (km - km_hi.astype(jnp.float32)).astype(jnp.bfloat16)
    blk_id = lax.broadcasted_iota(jnp.int32, (nb, BLK), 0)
    none = jnp.full((8 - MOBA_TOPK, BLK), nb, jnp.int32)

    for b in range(nblk):
        i = t * nblk + b
        for a in range(MOBA_HEADS):
            dims = slice(a * QK_PAD, a * QK_PAD + Dh)
            qTh = qT[a * Dh:(a + 1) * Dh, b * BLK:(b + 1) * BLK]
            gate = (jnp.dot(km_hi[:, dims], qTh, preferred_element_type=jnp.float32)
                    + jnp.dot(km_lo[:, dims], qTh, preferred_element_type=jnp.float32))
            gate = jnp.where(blk_id < i, gate, -jnp.inf)
            picks = []
            for _ in range(MOBA_TOPK):
                top = jnp.max(gate, axis=0, keepdims=True)
                is_top = (gate == top) & (top > -jnp.inf)
                first = jnp.min(jnp.where(is_top, blk_id, nb), axis=0, keepdims=True)
                picks.append(first)
                gate = jnp.where(blk_id == first, -jnp.inf, gate)
            sel_ref[a, b] = jnp.concatenate(picks + [none], axis=0)


def _in_proj(x, g, wqT, wk, kpos, wvT, wrq, wrkT, wrv, wrg):
    B, S, D = x.shape
    nb = S // BLK
    nblk = ROW_TILE // BLK
    grid = (B, S // ROW_TILE)
    bf = jnp.bfloat16
    QW, VW = MOBA_HEADS * QK_PAD, MOBA_HEADS * V_PAD
    nat = lambda w: pl.BlockSpec((None, ROW_TILE, w), lambda b, t: (b, t, 0))
    tr = lambda w: pl.BlockSpec((None, nblk, w, BLK), lambda b, t: (b, t, 0, 0))
    out_shape = (
        jax.ShapeDtypeStruct((B, nb, QW, BLK), bf),
        jax.ShapeDtypeStruct((B, S, QW), bf),
        jax.ShapeDtypeStruct((B, nb, VW, BLK), bf),
        jax.ShapeDtypeStruct((B, MOBA_HEADS, nb, 8, BLK), jnp.int32),
        jax.ShapeDtypeStruct((B, S, RET_WIDTH), bf),
        jax.ShapeDtypeStruct((B, nb, RET_WIDTH, BLK), bf),
        jax.ShapeDtypeStruct((B, S, RET_WIDTH), bf),
        jax.ShapeDtypeStruct((B, S, RET_WIDTH), bf),
    )
    out_specs = (
        tr(QW), nat(QW), tr(VW),
        pl.BlockSpec((None, MOBA_HEADS, nblk, 8, BLK), lambda b, t: (b, 0, t, 0, 0)),
        nat(RET_WIDTH), tr(RET_WIDTH), nat(RET_WIDTH), nat(RET_WIDTH),
    )
    consts = (g, wqT, wk, kpos, wvT, wrq, wrkT, wrv, wrg)
    in_specs = [pl.BlockSpec((None, ROW_TILE, D), lambda b, t: (b, t, 0))] + [
        _const_spec(a.shape) for a in consts]
    return pl.pallas_call(
        _in_proj_kernel, grid=grid, in_specs=in_specs, out_specs=out_specs, out_shape=out_shape,
        scratch_shapes=[pltpu.VMEM((nb, QW), jnp.float32)],
        compiler_params=pltpu.CompilerParams(
            dimension_semantics=("arbitrary", "arbitrary"), vmem_limit_bytes=VMEM_LIMIT),
        name="in_proj",
    )(x, *consts)


def _moba_kernel(slopes_ref, ti_ref, tj_ref, qT_ref, k_ref, vT_ref, sel_ref, o_ref,
                 m_ref, acc_ref, s_ref, smax_ref, alpha_ref, p_ref,
                 s_own_ref, smax_own_ref, alpha_own_ref, p_own_ref, *, n_past_groups):
    h = pl.program_id(1)
    nb = k_ref.shape[0]
    slope = slopes_ref[h]
    kpos = lax.broadcasted_iota(jnp.int32, (BLK, BLK), 0)
    qpos = lax.broadcasted_iota(jnp.int32, (BLK, BLK), 1)
    qterm = -slope * lax.broadcasted_iota(jnp.int32, (1, BLK), 1).astype(jnp.float32)

    m_ref[...] = jnp.full(m_ref.shape, -3e38, jnp.float32)
    acc_ref[...] = jnp.zeros(acc_ref.shape, jnp.float32)

    def own_item(g, e):
        t = g * GROUP_OWN + e
        return t, t

    def past_item(g, e):
        t = g * GROUP + e
        return ti_ref[t], tj_ref[t]

    def run_stream(n_groups, item_of, own, bufs):
        s_ref, smax_ref, alpha_ref, p_ref = bufs
        items, width = s_ref.shape[1], s_ref.shape[2] // BLK

        def scores(par, e, i_t, j0):
            kq = k_ref[j0] if width == 1 else k_ref[pl.ds(j0, width)].reshape(width * BLK, QK_PAD)
            s = jnp.dot(kq, qT_ref[i_t], preferred_element_type=jnp.float32)
            if own:
                s = jnp.where(kpos <= qpos, s, NEG)
            s_ref[par, e] = s
            for w in range(width):
                smax_ref[par, e * width + w:e * width + w + 1, :] = (
                    jnp.max(s[w * BLK:(w + 1) * BLK], axis=0, keepdims=True))

        def values(par, e, i_t, j0):
            vq = vT_ref[j0] if width == 1 else jnp.concatenate(
                [vT_ref[j0 + w] for w in range(width)], axis=1)
            pv = jnp.dot(vq, p_ref[par, e * width * BLK:(e + 1) * width * BLK, :],
                         preferred_element_type=jnp.float32)
            acc_ref[i_t] = alpha_ref[par, e:e + 1, :] * acc_ref[i_t] + pv

        def softmax(par, e, i_t, j0):
            if own:
                cs = [qterm]
            else:
                sel = sel_ref[i_t]
                cs = []
                for w in range(width):
                    j_t = j0 + w
                    hit = (sel[0:1] == j_t) | (sel[1:2] == j_t) | (sel[2:3] == j_t)
                    blk_dist = ((i_t - j_t) * MOBA_BLOCK).astype(jnp.float32)
                    cs.append(jnp.where(hit, -slope * blk_dist, NEG) + qterm)
            m_old = m_ref[pl.ds(i_t, 1), :]
            m_new = m_old
            for w in range(width):
                r = e * width + w
                m_new = jnp.maximum(m_new, smax_ref[par, r:r + 1, :] + cs[w])
            alpha_ref[par, e:e + 1, :] = jnp.exp2(m_old - m_new)
            m_ref[pl.ds(i_t, 1), :] = m_new
            for w in range(width):
                r = e * width + w
                p_ref[par, r * BLK:(r + 1) * BLK, :] = (
                    jnp.exp2(s_ref[par, e, w * BLK:(w + 1) * BLK, :] - (m_new - cs[w])).astype(jnp.bfloat16))

        ahead = 2

        def trip(g, par):
            g_next = jnp.minimum(g + 1, n_groups - 1)
            g_prev = jnp.maximum(g - 1, 0)
            for e in range(-ahead, items):
                if e + ahead < items:
                    scores(1 - par, e + ahead, *item_of(g_next, e + ahead))
                if e >= 0:
                    values(1 - par, e, *item_of(g_prev, e))
                    softmax(par, e, *item_of(g, e))

        alpha_ref[1] = jnp.ones(alpha_ref.shape[1:], jnp.float32)
        p_ref[1] = jnp.zeros(p_ref.shape[1:], jnp.bfloat16)
        for e in range(items):
            scores(0, e, *item_of(0, e))

        unroll = 2

        def body(w, carry):
            for d in range(unroll):
                trip(unroll * w + d, d % 2)
            return carry

        assert n_groups % 2 == 0
        lax.fori_loop(0, n_groups // unroll, body, 0)
        for g in range(n_groups - n_groups % unroll, n_groups):
            trip(g, g % 2)
        for e in range(items):
            values((n_groups - 1) % 2, e, *item_of(n_groups - 1, e))

    run_stream(nb // GROUP_OWN, own_item, True, (s_own_ref, smax_own_ref, alpha_own_ref, p_own_ref))
    run_stream(n_past_groups, past_item, False, (s_ref, smax_ref, alpha_ref, p_ref))

    def write_out(half):
        unroll = 4
        def some(u, carry):
            for d in range(unroll):
                i = u * unroll + d
                acc = acc_ref[i]
                oT = acc[0:MOBA_HEAD_DIM] / acc[ONES_ROW:ONES_ROW + 1]
                rows = pl.ds(pl.multiple_of(i * BLK, BLK), BLK)
                o_ref[rows, half * MOBA_HEAD_DIM:(half + 1) * MOBA_HEAD_DIM] = oT.T.astype(o_ref.dtype)
            return carry
        lax.fori_loop(0, nb // unroll, some, 0)

    lax.cond(h % 2 == 0, lambda: write_out(0), lambda: write_out(1))


def _moba(slopes_l2, qT, k, vT, sel):
    B, nb, QW, _ = qT.shape
    S = nb * BLK
    assert nb % (2 * GROUP_OWN) == 0 and nb % PAST_WIDTH == 0
    k4 = k.reshape(B, nb, BLK, QW)
    pairs = [(i, j0) for j0 in range(0, nb - 1, PAST_WIDTH) for i in range(j0 + 1, nb)]
    pairs += [(0, 0)] * (-len(pairs) % (2 * GROUP))
    ti = jnp.asarray([p[0] for p in pairs], jnp.int32)
    tj = jnp.asarray([p[1] for p in pairs], jnp.int32)
    grid_spec = pltpu.PrefetchScalarGridSpec(
        num_scalar_prefetch=3,
        grid=(B, MOBA_HEADS),
        in_specs=[
            pl.BlockSpec((None, nb, QK_PAD, BLK), lambda b, h, *_: (b, 0, h, 0)),
            pl.BlockSpec((None, nb, BLK, QK_PAD), lambda b, h, *_: (b, 0, 0, h)),
            pl.BlockSpec((None, nb, V_PAD, BLK), lambda b, h, *_: (b, 0, h, 0)),
            pl.BlockSpec((None, None, nb, 8, BLK), lambda b, h, *_: (b, h, 0, 0, 0)),
        ],
        out_specs=pl.BlockSpec((None, S, 2 * MOBA_HEAD_DIM), lambda b, h, *_: (b, 0, h // 2)),
        scratch_shapes=[
            pltpu.VMEM((nb, BLK), jnp.float32),
            pltpu.VMEM((nb, V_PAD, BLK), jnp.float32),
            pltpu.VMEM((2, GROUP, PAST_WIDTH * BLK, BLK), jnp.float32),
            pltpu.VMEM((2, GROUP * PAST_WIDTH, BLK), jnp.float32),
            pltpu.VMEM((2, max(GROUP, 8), BLK), jnp.float32),
            pltpu.VMEM((2, GROUP * PAST_WIDTH * BLK, BLK), jnp.bfloat16),
            pltpu.VMEM((2, GROUP_OWN, BLK, BLK), jnp.float32),
            pltpu.VMEM((2, 8, BLK), jnp.float32),
            pltpu.VMEM((2, 8, BLK), jnp.float32),
            pltpu.VMEM((2, GROUP_OWN * BLK, BLK), jnp.bfloat16),
        ],
    )
    return pl.pallas_call(
        functools.partial(_moba_kernel, n_past_groups=len(pairs) // GROUP),
        grid_spec=grid_spec,
        out_shape=jax.ShapeDtypeStruct((B, S, MOBA_WIDTH), jnp.bfloat16),
        compiler_params=pltpu.CompilerParams(
            dimension_semantics=("arbitrary", "arbitrary"), vmem_limit_bytes=VMEM_LIMIT),
        name="moba",
    )(slopes_l2, ti, tj, qT, k4, vT, sel)


def _ret_kernel(cd_ref, q_ref, kT_ref, v_ref, g_ref, dintra_ref, qdec_ref, kdec_ref, gn_ref,
                o_ref, state_ref):
    c = pl.program_id(0)

    @pl.when(c == 0)
    def _():
        state_ref[...] = jnp.zeros_like(state_ref)

    for b in range(q_ref.shape[0]):
        for h in range(RET_HEADS):
            cols = slice(h * RET_HEAD_DIM, (h + 1) * RET_HEAD_DIM)
            q = q_ref[b, :, cols]
            kT = kT_ref[b, cols, :]
            v = v_ref[b, :, cols]
            state = state_ref[b, h]
            s = jnp.dot(q, kT, preferred_element_type=jnp.float32) * dintra_ref[h]
            inner = jnp.dot(s.astype(jnp.bfloat16), v, preferred_element_type=jnp.float32)
            qd = (q.astype(jnp.float32) * qdec_ref[h]).astype(jnp.bfloat16)
            cross = jnp.dot(qd, state.astype(jnp.bfloat16), preferred_element_type=jnp.float32)
            kd = (kT.astype(jnp.float32) * kdec_ref[h]).astype(jnp.bfloat16)
            state_ref[b, h] = cd_ref[h] * state + jnp.dot(kd, v, preferred_element_type=jnp.float32)
            o = inner + cross
            mu = jnp.mean(o, axis=-1, keepdims=True)
            d = o - mu
            var = jnp.mean(d * d, axis=-1, keepdims=True)
            gate = g_ref[b, :, cols].astype(jnp.float32)
            y = d * lax.rsqrt(var + GN_EPS) * gn_ref[:, cols] * (gate * jax.nn.sigmoid(gate))
            o_ref[b, :, cols] = y.astype(o_ref.dtype)


def _retention(rq, rkT, rv, rg, ret_norm):
    B, S, W = rq.shape
    nc = S // BLK
    H = RET_HEADS
    C = RET_CHUNK
    f32 = jnp.float32
    gamma = 1.0 - jnp.exp2(-5.0 - jnp.arange(H, dtype=f32))
    log_g = jnp.log(gamma)
    idx = jnp.arange(C, dtype=f32)
    diff = idx[:, None] - idx[None, :]
    dintra = jnp.where(diff >= 0, jnp.exp(log_g[:, None, None] * jnp.maximum(diff, 0.0)), 0.0)
    qdec = jnp.broadcast_to(jnp.exp(log_g[:, None] * (idx + 1.0))[..., None], (H, C, LANES))
    kdec = jnp.exp(log_g[:, None] * (C - 1.0 - idx))[:, None, :]
    cdec = jnp.exp(log_g * C)
    gn = ret_norm.reshape(1, W).astype(f32)

    nat = pl.BlockSpec((B, C, W), lambda c, s: (0, c, 0))
    const = lambda a: pl.BlockSpec(a.shape, lambda c, s: (0,) * a.ndim,
                                   pipeline_mode=pl.Buffered(1))
    grid_spec = pltpu.PrefetchScalarGridSpec(
        num_scalar_prefetch=1,
        grid=(nc,),
        in_specs=[
            nat,
            pl.BlockSpec((B, None, W, C), lambda c, s: (0, c, 0, 0)),
            nat, nat,
            const(dintra), const(qdec), const(kdec), const(gn),
        ],
        out_specs=nat,
        scratch_shapes=[pltpu.VMEM((B, H, RET_HEAD_DIM, RET_HEAD_DIM), f32)],
    )
    return pl.pallas_call(
        _ret_kernel, grid_spec=grid_spec,
        out_shape=jax.ShapeDtypeStruct((B, S, W), jnp.bfloat16),
        compiler_params=pltpu.CompilerParams(
            dimension_semantics=("arbitrary",),
            vmem_limit_bytes=VMEM_LIMIT),
        name="retention",
    )(cdec, rq, rkT, rv, rg, dintra, qdec, kdec, gn)


def _gelu_tanh(u):
    k = -2.0 * math.sqrt(2.0 / math.pi) * LOG2E
    return u / (1.0 + jnp.exp2(u * (k + (k * 0.044715) * (u * u))))


def _shift_rows(cur, prev_tail, n):
    rolled = pltpu.roll(cur, n, axis=0)
    prev_rolled = pltpu.roll(prev_tail, n, axis=0)
    row = lax.broadcasted_iota(jnp.int32, prev_tail.shape, 0)
    head = jnp.where(row < n, prev_rolled, rolled[0:8])
    return jnp.concatenate([head, rolled[8:]], axis=0)


def _tail_kernel(x_ref, moba_ref, ret_ref, p_ref, wo_ref, ffn_g_ref, wup_ref, wgate_ref,
                 convw_ref, convb_ref, wdown_ref, ple_g_ref, wple_ref, wpg_ref, fin_g_ref,
                 o_ref, y_ref, carry_ref, *, tiles_per_seq):
    t = pl.program_id(0)
    R = ROW_TILE

    @pl.when(t % tiles_per_seq == 0)
    def _():
        carry_ref[...] = jnp.zeros_like(carry_ref)

    mix = jnp.concatenate([moba_ref[...], ret_ref[...]], axis=1)
    x1 = x_ref[...] + jnp.dot(mix, wo_ref[...], preferred_element_type=jnp.float32)
    h = _rms(x1, ffn_g_ref[...]).astype(jnp.bfloat16)

    for c in range(N_FF_CHUNKS):
        cols = slice(c * FF_CHUNK, (c + 1) * FF_CHUNK)
        up = jnp.dot(h, wup_ref[:, cols], preferred_element_type=jnp.float32)
        gt = jnp.dot(h, wgate_ref[:, cols], preferred_element_type=jnp.float32)
        prev_tail = carry_ref[c]
        carry_ref[c] = up[R - 8:R, :]
        w = convw_ref[:, cols]
        u = (convb_ref[:, cols]
             + w[0:1, :] * _shift_rows(up, prev_tail, 2)
             + w[1:2, :] * _shift_rows(up, prev_tail, 1)
             + w[2:3, :] * up)
        y_ref[:, cols] = (_gelu_tanh(u) * gt).astype(jnp.bfloat16)
    x2 = x1 + jnp.dot(y_ref[...], wdown_ref[...], preferred_element_type=jnp.float32)

    hn = _rms(x2, ple_g_ref[...]).astype(jnp.bfloat16)
    g = jax.nn.sigmoid(jnp.dot(hn, wpg_ref[...], preferred_element_type=jnp.float32))
    pe = jnp.dot(p_ref[...].astype(jnp.bfloat16), wple_ref[...], preferred_element_type=jnp.float32)
    x3 = x2 + pe * g
    o_ref[...] = _rms(x3, fin_g_ref[...])


def _tail(x2d, moba2d, ret2d, p2d, wo, ffn_g, wup, wgate, convw, convb, wdown, ple_g, wple,
          wpg, fin_g, seq_len):
    T, D = x2d.shape
    R = ROW_TILE
    row = lambda w: pl.BlockSpec((R, w), lambda t: (t, 0))
    consts = (wo, ffn_g, wup, wgate, convw, convb, wdown, ple_g, wple, wpg, fin_g)
    in_specs = [row(D), row(MOBA_WIDTH), row(RET_WIDTH), row(PLE_DIM)] + [
        _const_spec(a.shape) for a in consts]
    return pl.pallas_call(
        functools.partial(_tail_kernel, tiles_per_seq=seq_len // R),
        grid=(T // R,), in_specs=in_specs, out_specs=row(D),
        out_shape=jax.ShapeDtypeStruct((T, D), jnp.float32),
        scratch_shapes=[
            pltpu.VMEM((R, D_FF), jnp.bfloat16),
            pltpu.VMEM((N_FF_CHUNKS, 8, FF_CHUNK), jnp.float32),
        ],
        compiler_params=pltpu.CompilerParams(
            dimension_semantics=("arbitrary",), vmem_limit_bytes=VMEM_LIMIT),
        name="tail",
    )(x2d, moba2d, ret2d, p2d, *consts)


def _pad_heads(w, pad):
    d = w.shape[0]
    w3 = w.reshape(d, MOBA_HEADS, MOBA_HEAD_DIM)
    return jnp.pad(w3, ((0, 0), (0, 0), (0, pad - MOBA_HEAD_DIM))).reshape(d, MOBA_HEADS * pad)


def kernel(x, p, attn_norm, w_in, ret_norm, w_out, ffn_norm, w_up, w_gate, conv_w, conv_b, w_down,
           ple_norm, w_ple, w_ple_gate, final_norm):
    B, S, D = x.shape
    assert D == D_MODEL and S % ROW_TILE == 0 and w_in.shape[0] == 1
    bf, f32 = jnp.bfloat16, jnp.float32
    MW, RW = MOBA_WIDTH, RET_WIDTH
    w = w_in[0].astype(bf)
    wqT = w[:, 0:MW].T
    wk = _pad_heads(w[:, MW:2 * MW], QK_PAD)
    wvT = w[:, 2 * MW:3 * MW].T
    o = 3 * MW
    wrq = w[:, o:o + RW]
    wrkT = w[:, o + RW:o + 2 * RW].T
    wrv = w[:, o + 2 * RW:o + 3 * RW]
    wrg = w[:, o + 3 * RW:o + 4 * RW]
    row = lambda a: a.reshape(1, -1).astype(f32)

    slopes_l2 = jnp.exp2(-8.0 * jnp.arange(1, MOBA_HEADS + 1, dtype=f32) / MOBA_HEADS) * LOG2E
    kterm = slopes_l2[None, :] * jnp.arange(BLK, dtype=f32)[:, None]
    kterm_hi = lax.bitcast_convert_type(
        lax.bitcast_convert_type(kterm, jnp.uint32) & jnp.uint32(0xFFFF0000), f32)
    kterm_lo = kterm - kterm_hi
    kpos = jnp.zeros((BLK, MOBA_HEADS, QK_PAD), f32)
    kpos = kpos.at[:, :, ALIBI_ROW].set(kterm_hi).at[:, :, ALIBI_ROW + 1].set(kterm_lo)
    kpos = kpos.reshape(BLK, MOBA_HEADS * QK_PAD)

    qT, k, vT, sel, rq, rkT, rv, rg = _in_proj(
        x, row(attn_norm[0]), wqT, wk, kpos, wvT, wrq, wrkT, wrv, wrg)

    moba_out = _moba(slopes_l2, qT, k, vT, sel)
    ret_out = _retention(rq, rkT, rv, rg, ret_norm[0])

    wo = w_out[0].astype(bf)
    out = _tail(
        x.reshape(B * S, D), moba_out.reshape(B * S, MW), ret_out.reshape(B * S, RW),
        p[0].reshape(B * S, PLE_DIM),
        wo, row(ffn_norm[0]),
        w_up[0].astype(bf), w_gate[0].astype(bf), conv_w[0].astype(f32), row(conv_b[0]),
        w_down[0].astype(bf),
        row(ple_norm[0]), w_ple[0].astype(bf), w_ple_gate[0].astype(bf), row(final_norm),
        seq_len=S)
    return out.reshape(B, S, D)
```

```python
import functools
import math

import jax
import jax.numpy as jnp
from jax import lax
from jax.experimental import pallas as pl
from jax.experimental.pallas import tpu as pltpu

D_MODEL = 1024
PLE_DIM = 256
MOBA_HEADS = 8
MOBA_HEAD_DIM = 64
MOBA_WIDTH = MOBA_HEADS * MOBA_HEAD_DIM
MOBA_BLOCK = 256
MOBA_TOPK = 3
RET_HEADS = 4
RET_HEAD_DIM = 128
RET_WIDTH = RET_HEADS * RET_HEAD_DIM
RET_CHUNK = 256
D_FF = 2816
CONV_WIDTH = 3
RMS_EPS = 1e-6
GN_EPS = 1e-5

BLK = 256
LANES = 128
SUBLANES = 8
FF_CHUNK = 256
N_FF_CHUNKS = D_FF // FF_CHUNK
ROW_TILE = 512
VMEM_LIMIT = 56 * 1024 * 1024
NEG = -1e30
LOG2E = math.log2(math.e)

QK_PAD = LANES
ALIBI_ROW = MOBA_HEAD_DIM
V_PAD = 80
ONES_ROW = MOBA_HEAD_DIM
GROUP = 16
GROUP_OWN = 8

_NT = (((1,), (1,)), ((), ()))


def _const_spec(shape):
    nd = len(shape)
    return pl.BlockSpec(shape, lambda *_: (0,) * nd, pipeline_mode=pl.Buffered(1))


def _rms(x, g):
    ms = jnp.mean(x * x, axis=-1, keepdims=True)
    return x * lax.rsqrt(ms + RMS_EPS) * g


def _in_proj_kernel(x_ref, g_ref, wqT_ref, wk_ref, kpos_ref, wvT_ref,
                    wrq_ref, wrkT_ref, wrv_ref, wrg_ref,
                    qT_ref, k_ref, vT_ref, sel_ref, rq_ref, rkT_ref, rv_ref, rg_ref, kmean_ref):
    t = pl.program_id(1)
    h = _rms(x_ref[...], g_ref[...]).astype(jnp.bfloat16)
    nblk = ROW_TILE // BLK
    nb = kmean_ref.shape[0]

    @pl.when(t == 0)
    def _():
        kmean_ref[...] = jnp.zeros_like(kmean_ref)

    def nt(w_ref):
        return lax.dot_general(w_ref[...], h, _NT, preferred_element_type=jnp.float32)

    def nn(w_ref):
        return jnp.dot(h, w_ref[...], preferred_element_type=jnp.float32)

    Dh = MOBA_HEAD_DIM
    qT = (nt(wqT_ref) * (Dh ** -0.5 * LOG2E)).astype(jnp.bfloat16)
    vT = nt(wvT_ref).astype(jnp.bfloat16)
    rkT = (nt(wrkT_ref) * (RET_HEAD_DIM ** -0.5)).astype(jnp.bfloat16)
    q_pad = (lax.broadcasted_iota(jnp.int32, (QK_PAD - Dh, BLK), 0) < 2).astype(jnp.bfloat16)
    v_pad = (lax.broadcasted_iota(jnp.int32, (V_PAD - Dh, BLK), 0) < 1).astype(jnp.bfloat16)
    for b in range(nblk):
        sl = slice(b * BLK, (b + 1) * BLK)
        for a in range(MOBA_HEADS):
            qT_ref[b, a * QK_PAD:a * QK_PAD + Dh, :] = qT[a * Dh:(a + 1) * Dh, sl]
            qT_ref[b, a * QK_PAD + Dh:(a + 1) * QK_PAD, :] = q_pad
            vT_ref[b, a * V_PAD:a * V_PAD + Dh, :] = vT[a * Dh:(a + 1) * Dh, sl]
            vT_ref[b, a * V_PAD + Dh:(a + 1) * V_PAD, :] = v_pad
        rkT_ref[b] = rkT[:, sl]
    k = nn(wk_ref)
    for b in range(nblk):
        kb = k[b * BLK:(b + 1) * BLK]
        kmean_ref[pl.ds(t * nblk + b, 1), :] = jnp.mean(kb, axis=0, keepdims=True)
        k_ref[b * BLK:(b + 1) * BLK, :] = (kb + kpos_ref[...]).astype(jnp.bfloat16)
    rq_ref[...] = nn(wrq_ref).astype(jnp.bfloat16)
    rv_ref[...] = nn(wrv_ref).astype(jnp.bfloat16)
    rg_ref[...] = nn(wrg_ref).astype(jnp.bfloat16)

    km = kmean_ref[...]
    km_hi = km.astype(jnp.bfloat16)
    km_lo = (km - km_hi.astype(jnp.float32)).astype(jnp.bfloat16)
    blk_id = lax.broadcasted_iota(jnp.int32, (nb, BLK), 0)
    none = jnp.full((SUBLANES - MOBA_TOPK, BLK), nb, jnp.int32)
    for b in range(nblk):
        i = t * nblk + b
        for a in range(MOBA_HEADS):
            dims = slice(a * QK_PAD, a * QK_PAD + Dh)
            qTh = qT[a * Dh:(a + 1) * Dh, b * BLK:(b + 1) * BLK]
            gate = (jnp.dot(km_hi[:, dims], qTh, preferred_element_type=jnp.float32)
                    + jnp.dot(km_lo[:, dims], qTh, preferred_element_type=jnp.float32))
            gate = jnp.where(blk_id < i, gate, -jnp.inf)
            picks = []
            for _ in range(MOBA_TOPK):
                top = jnp.max(gate, axis=0, keepdims=True)
                is_top = (gate == top) & (top > -jnp.inf)
                first = jnp.min(jnp.where(is_top, blk_id, nb), axis=0, keepdims=True)
                picks.append(first)
                gate = jnp.where(blk_id == first, -jnp.inf, gate)
            sel_ref[a, b] = jnp.concatenate(picks + [none], axis=0)


def _in_proj(x, g, wqT, wk, kpos, wvT, wrq, wrkT, wrv, wrg):
    B, S, D = x.shape
    nb = S // BLK
    nblk = ROW_TILE // BLK
    grid = (B, S // ROW_TILE)
    bf = jnp.bfloat16
    QW, VW = MOBA_HEADS * QK_PAD, MOBA_HEADS * V_PAD
    nat = lambda w: pl.BlockSpec((None, ROW_TILE, w), lambda b, t: (b, t, 0))
    tr = lambda w: pl.BlockSpec((None, nblk, w, BLK), lambda b, t: (b, t, 0, 0))
    out_shape = (
        jax.ShapeDtypeStruct((B, nb, QW, BLK), bf),
        jax.ShapeDtypeStruct((B, S, QW), bf),
        jax.ShapeDtypeStruct((B, nb, VW, BLK), bf),
        jax.ShapeDtypeStruct((B, MOBA_HEADS, nb, SUBLANES, BLK), jnp.int32),
        jax.ShapeDtypeStruct((B, S, RET_WIDTH), bf),
        jax.ShapeDtypeStruct((B, nb, RET_WIDTH, BLK), bf),
        jax.ShapeDtypeStruct((B, S, RET_WIDTH), bf),
        jax.ShapeDtypeStruct((B, S, RET_WIDTH), bf),
    )
    out_specs = (
        tr(QW), nat(QW), tr(VW),
        pl.BlockSpec((None, MOBA_HEADS, nblk, SUBLANES, BLK), lambda b, t: (b, 0, t, 0, 0)),
        nat(RET_WIDTH), tr(RET_WIDTH), nat(RET_WIDTH), nat(RET_WIDTH),
    )
    consts = (g, wqT, wk, kpos, wvT, wrq, wrkT, wrv, wrg)
    in_specs = [pl.BlockSpec((None, ROW_TILE, D), lambda b, t: (b, t, 0))] + [
        _const_spec(a.shape) for a in consts]
    return pl.pallas_call(
        _in_proj_kernel, grid=grid, in_specs=in_specs, out_specs=out_specs, out_shape=out_shape,
        scratch_shapes=[pltpu.VMEM((nb, QW), jnp.float32)],
        compiler_params=pltpu.CompilerParams(
            dimension_semantics=("arbitrary", "arbitrary"), vmem_limit_bytes=VMEM_LIMIT),
        name="in_proj",
    )(x, *consts)


def _moba_kernel(slopes_ref, ti_ref, tj_ref, qT_ref, k_ref, vT_ref, sel_ref, o_ref,
                 m_ref, acc_ref, s_ref, smax_ref, alpha_ref, p_ref,
                 s_own_ref, smax_own_ref, alpha_own_ref, p_own_ref, *, n_past_groups):
    h = pl.program_id(1)
    nb = k_ref.shape[0]
    slope = slopes_ref[h]
    kpos = lax.broadcasted_iota(jnp.int32, (BLK, BLK), 0)
    qpos = lax.broadcasted_iota(jnp.int32, (BLK, BLK), 1)
    qterm = -slope * lax.broadcasted_iota(jnp.int32, (1, BLK), 1).astype(jnp.float32)

    m_ref[...] = jnp.full(m_ref.shape, -3e38, jnp.float32)
    acc_ref[...] = jnp.zeros(acc_ref.shape, jnp.float32)

    def own_tile(g, e):
        t = g * GROUP_OWN + e
        return t, t

    def past_tile(g, e):
        t = g * GROUP + e
        return ti_ref[t], tj_ref[t]

    def run_stream(n_groups, tile_of, own, bufs):
        s_ref, smax_ref, alpha_ref, p_ref = bufs
        GROUP = s_ref.shape[1]
        def scores(par, e, i_t, j_t):
            s = jnp.dot(k_ref[j_t], qT_ref[i_t], preferred_element_type=jnp.float32)
            if own:
                s = jnp.where(kpos <= qpos, s, NEG)
            s_ref[par, e] = s
            smax_ref[par, e:e + 1, :] = jnp.max(s, axis=0, keepdims=True)

        def values(par, e, i_t, j_t):
            pv = jnp.dot(vT_ref[j_t], p_ref[par, e * BLK:(e + 1) * BLK, :],
                         preferred_element_type=jnp.float32)
            acc_ref[i_t] = alpha_ref[par, e:e + 1, :] * acc_ref[i_t] + pv

        def softmax(par, e, i_t, j_t):
            if own:
                c = qterm
            else:
                sel = sel_ref[i_t]
                hit = (sel[0:1] == j_t) | (sel[1:2] == j_t) | (sel[2:3] == j_t)
                blk_dist = ((i_t - j_t) * MOBA_BLOCK).astype(jnp.float32)
                c = jnp.where(hit, -slope * blk_dist, NEG) + qterm
            m_old = m_ref[pl.ds(i_t, 1), :]
            m_new = jnp.maximum(m_old, smax_ref[par, e:e + 1, :] + c)
            alpha_ref[par, e:e + 1, :] = jnp.exp2(m_old - m_new)
            m_ref[pl.ds(i_t, 1), :] = m_new
            p_ref[par, e * BLK:(e + 1) * BLK, :] = (
                jnp.exp2(s_ref[par, e] - (m_new - c)).astype(jnp.bfloat16))

        def trip(g, par):
            g_next = jnp.minimum(g + 1, n_groups - 1)
            g_prev = jnp.maximum(g - 1, 0)
            for e in range(-2, GROUP):
                if e + 2 < GROUP:
                    scores(1 - par, e + 2, *tile_of(g_next, e + 2))
                if e >= 0:
                    values(1 - par, e, *tile_of(g_prev, e))
                    softmax(par, e, *tile_of(g, e))

        alpha_ref[1] = jnp.ones(alpha_ref.shape[1:], jnp.float32)
        p_ref[1] = jnp.zeros(p_ref.shape[1:], jnp.bfloat16)
        for e in range(GROUP):
            scores(0, e, *tile_of(0, e))

        def body(w, carry):
            trip(2 * w, 0)
            trip(2 * w + 1, 1)
            return carry

        assert n_groups % 2 == 0
        lax.fori_loop(0, n_groups // 2, body, 0)
        for e in range(GROUP):
            values((n_groups - 1) % 2, e, *tile_of(n_groups - 1, e))

    run_stream(nb // GROUP_OWN, own_tile, True, (s_own_ref, smax_own_ref, alpha_own_ref, p_own_ref))
    run_stream(n_past_groups, past_tile, False, (s_ref, smax_ref, alpha_ref, p_ref))

    def write_out(half):
        unroll = 4
        def some(u, carry):
            for d in range(unroll):
                i = u * unroll + d
                acc = acc_ref[i]
                oT = acc[0:MOBA_HEAD_DIM] / acc[ONES_ROW:ONES_ROW + 1]
                pad = jnp.zeros_like(oT)
                o2 = jnp.concatenate([pad, oT] if half else [oT, pad], axis=0).T
                rows = pl.ds(pl.multiple_of(i * BLK, BLK), BLK)
                lanes = slice(half * MOBA_HEAD_DIM, (half + 1) * MOBA_HEAD_DIM)
                o_ref[rows, lanes] = o2[:, lanes].astype(o_ref.dtype)
            return carry
        lax.fori_loop(0, nb // unroll, some, 0)

    lax.cond(h % 2 == 0, lambda: write_out(0), lambda: write_out(1))


def _moba(slopes_l2, qT, k, vT, sel):
    B, nb, QW, _ = qT.shape
    S = nb * BLK
    assert nb % GROUP == 0
    k4 = k.reshape(B, nb, BLK, QW)
    pairs = [(i, j) for j in range(nb - 1) for i in range(j + 1, nb)]
    pairs += [(0, 0)] * (-len(pairs) % GROUP)
    ti = jnp.asarray([p[0] for p in pairs], jnp.int32)
    tj = jnp.asarray([p[1] for p in pairs], jnp.int32)
    grid_spec = pltpu.PrefetchScalarGridSpec(
        num_scalar_prefetch=3,
        grid=(B, MOBA_HEADS),
        in_specs=[
            pl.BlockSpec((None, nb, QK_PAD, BLK), lambda b, h, *_: (b, 0, h, 0)),
            pl.BlockSpec((None, nb, BLK, QK_PAD), lambda b, h, *_: (b, 0, 0, h)),
            pl.BlockSpec((None, nb, V_PAD, BLK), lambda b, h, *_: (b, 0, h, 0)),
            pl.BlockSpec((None, None, nb, SUBLANES, BLK), lambda b, h, *_: (b, h, 0, 0, 0)),
        ],
        out_specs=pl.BlockSpec((None, S, 2 * MOBA_HEAD_DIM), lambda b, h, *_: (b, 0, h // 2)),
        scratch_shapes=[
            pltpu.VMEM((nb, BLK), jnp.float32),
            pltpu.VMEM((nb, V_PAD, BLK), jnp.float32),
            pltpu.VMEM((2, GROUP, BLK, BLK), jnp.float32),
            pltpu.VMEM((2, GROUP, BLK), jnp.float32),
            pltpu.VMEM((2, GROUP, BLK), jnp.float32),
            pltpu.VMEM((2, GROUP * BLK, BLK), jnp.bfloat16),
            pltpu.VMEM((2, GROUP_OWN, BLK, BLK), jnp.float32),
            pltpu.VMEM((2, GROUP_OWN, BLK), jnp.float32),
            pltpu.VMEM((2, GROUP_OWN, BLK), jnp.float32),
            pltpu.VMEM((2, GROUP_OWN * BLK, BLK), jnp.bfloat16),
        ],
    )
    return pl.pallas_call(
        functools.partial(_moba_kernel, n_past_groups=len(pairs) // GROUP),
        grid_spec=grid_spec,
        out_shape=jax.ShapeDtypeStruct((B, S, MOBA_WIDTH), jnp.bfloat16),
        compiler_params=pltpu.CompilerParams(
            dimension_semantics=("arbitrary", "arbitrary"), vmem_limit_bytes=VMEM_LIMIT),
        name="moba",
    )(slopes_l2, ti, tj, qT, k4, vT, sel)


def _ret_kernel(cd_ref, q_ref, kT_ref, v_ref, g_ref, dintra_ref, qdec_ref, kdec_ref, gn_ref,
                o_ref, state_ref):
    c = pl.program_id(0)

    @pl.when(c == 0)
    def _():
        state_ref[...] = jnp.zeros_like(state_ref)

    for b in range(q_ref.shape[0]):
        for h in range(RET_HEADS):
            cols = slice(h * RET_HEAD_DIM, (h + 1) * RET_HEAD_DIM)
            q = q_ref[b, :, cols]
            kT = kT_ref[b, cols, :]
            v = v_ref[b, :, cols]
            state = state_ref[b, h]
            s = jnp.dot(q, kT, preferred_element_type=jnp.float32) * dintra_ref[h]
            inner = jnp.dot(s.astype(jnp.bfloat16), v, preferred_element_type=jnp.float32)
            qd = (q.astype(jnp.float32) * qdec_ref[h]).astype(jnp.bfloat16)
            cross = jnp.dot(qd, state.astype(jnp.bfloat16), preferred_element_type=jnp.float32)
            kd = (kT.astype(jnp.float32) * kdec_ref[h]).astype(jnp.bfloat16)
            state_ref[b, h] = cd_ref[h] * state + jnp.dot(kd, v, preferred_element_type=jnp.float32)
            o = inner + cross
            mu = jnp.mean(o, axis=-1, keepdims=True)
            d = o - mu
            var = jnp.mean(d * d, axis=-1, keepdims=True)
            gate = g_ref[b, :, cols].astype(jnp.float32)
            y = d * lax.rsqrt(var + GN_EPS) * gn_ref[:, cols] * (gate * jax.nn.sigmoid(gate))
            o_ref[b, :, cols] = y.astype(o_ref.dtype)


def _retention(rq, rkT, rv, rg, ret_norm):
    B, S, W = rq.shape
    nc = S // BLK
    H = RET_HEADS
    C = RET_CHUNK
    f32 = jnp.float32
    gamma = 1.0 - jnp.exp2(-5.0 - jnp.arange(H, dtype=f32))
    log_g = jnp.log(gamma)
    idx = jnp.arange(C, dtype=f32)
    diff = idx[:, None] - idx[None, :]
    dintra = jnp.where(diff >= 0, jnp.exp(log_g[:, None, None] * jnp.maximum(diff, 0.0)), 0.0)
    qdec = jnp.broadcast_to(jnp.exp(log_g[:, None] * (idx + 1.0))[..., None], (H, C, LANES))
    kdec = jnp.exp(log_g[:, None] * (C - 1.0 - idx))[:, None, :]
    cdec = jnp.exp(log_g * C)
    gn = ret_norm.reshape(1, W).astype(f32)

    nat = pl.BlockSpec((B, C, W), lambda c, s: (0, c, 0))
    const = lambda a: pl.BlockSpec(a.shape, lambda c, s: (0,) * a.ndim,
                                   pipeline_mode=pl.Buffered(1))
    grid_spec = pltpu.PrefetchScalarGridSpec(
        num_scalar_prefetch=1,
        grid=(nc,),
        in_specs=[
            nat,
            pl.BlockSpec((B, None, W, C), lambda c, s: (0, c, 0, 0)),
            nat, nat,
            const(dintra), const(qdec), const(kdec), const(gn),
        ],
        out_specs=nat,
        scratch_shapes=[pltpu.VMEM((B, H, RET_HEAD_DIM, RET_HEAD_DIM), f32)],
    )
    return pl.pallas_call(
        _ret_kernel, grid_spec=grid_spec,
        out_shape=jax.ShapeDtypeStruct((B, S, W), jnp.bfloat16),
        compiler_params=pltpu.CompilerParams(
            dimension_semantics=("arbitrary",),
            vmem_limit_bytes=VMEM_LIMIT),
        name="retention",
    )(cdec, rq, rkT, rv, rg, dintra, qdec, kdec, gn)


def _gelu_tanh(u):
    k = -2.0 * math.sqrt(2.0 / math.pi) * LOG2E
    return u / (1.0 + jnp.exp2(u * (k + (k * 0.044715) * (u * u))))


def _shift_rows(cur, prev_tail, n):
    rolled = pltpu.roll(cur, n, axis=0)
    prev_rolled = pltpu.roll(prev_tail, n, axis=0)
    row = lax.broadcasted_iota(jnp.int32, prev_tail.shape, 0)
    head = jnp.where(row < n, prev_rolled, rolled[0:SUBLANES])
    return jnp.concatenate([head, rolled[SUBLANES:]], axis=0)


def _tail_kernel(x_ref, moba_ref, ret_ref, p_ref, wo_ref, ffn_g_ref, wup_ref, wgate_ref,
                 convw_ref, convb_ref, wdown_ref, ple_g_ref, wple_ref, wpg_ref, fin_g_ref,
                 o_ref, y_ref, carry_ref, *, tiles_per_seq):
    t = pl.program_id(0)
    R = ROW_TILE

    @pl.when(t % tiles_per_seq == 0)
    def _():
        carry_ref[...] = jnp.zeros_like(carry_ref)

    mix = jnp.concatenate([moba_ref[...], ret_ref[...]], axis=1)
    x1 = x_ref[...] + jnp.dot(mix, wo_ref[...], preferred_element_type=jnp.float32)
    h = _rms(x1, ffn_g_ref[...]).astype(jnp.bfloat16)

    for c in range(N_FF_CHUNKS):
        cols = slice(c * FF_CHUNK, (c + 1) * FF_CHUNK)
        up = jnp.dot(h, wup_ref[:, cols], preferred_element_type=jnp.float32)
        gt = jnp.dot(h, wgate_ref[:, cols], preferred_element_type=jnp.float32)
        prev_tail = carry_ref[c]
        carry_ref[c] = up[R - SUBLANES:R, :]
        w = convw_ref[:, cols]
        u = (convb_ref[:, cols]
             + w[0:1, :] * _shift_rows(up, prev_tail, 2)
             + w[1:2, :] * _shift_rows(up, prev_tail, 1)
             + w[2:3, :] * up)
        y_ref[:, cols] = (_gelu_tanh(u) * gt).astype(jnp.bfloat16)
    x2 = x1 + jnp.dot(y_ref[...], wdown_ref[...], preferred_element_type=jnp.float32)

    hn = _rms(x2, ple_g_ref[...]).astype(jnp.bfloat16)
    g = jax.nn.sigmoid(jnp.dot(hn, wpg_ref[...], preferred_element_type=jnp.float32))
    pe = jnp.dot(p_ref[...].astype(jnp.bfloat16), wple_ref[...], preferred_element_type=jnp.float32)
    x3 = x2 + pe * g
    o_ref[...] = _rms(x3, fin_g_ref[...])


def _tail(x2d, moba2d, ret2d, p2d, wo, ffn_g, wup, wgate, convw, convb, wdown, ple_g, wple,
          wpg, fin_g, seq_len):
    T, D = x2d.shape
    R = ROW_TILE
    row = lambda w: pl.BlockSpec((R, w), lambda t: (t, 0))
    consts = (wo, ffn_g, wup, wgate, convw, convb, wdown, ple_g, wple, wpg, fin_g)
    in_specs = [row(D), row(MOBA_WIDTH), row(RET_WIDTH), row(PLE_DIM)] + [
        _const_spec(a.shape) for a in consts]
    return pl.pallas_call(
        functools.partial(_tail_kernel, tiles_per_seq=seq_len // R),
        grid=(T // R,), in_specs=in_specs, out_specs=row(D),
        out_shape=jax.ShapeDtypeStruct((T, D), jnp.float32),
        scratch_shapes=[
            pltpu.VMEM((R, D_FF), jnp.bfloat16),
            pltpu.VMEM((N_FF_CHUNKS, SUBLANES, FF_CHUNK), jnp.float32),
        ],
        compiler_params=pltpu.CompilerParams(
            dimension_semantics=("arbitrary",), vmem_limit_bytes=VMEM_LIMIT),
        name="tail",
    )(x2d, moba2d, ret2d, p2d, *consts)


def _pad_heads(w, pad):
    d = w.shape[0]
    w3 = w.reshape(d, MOBA_HEADS, MOBA_HEAD_DIM)
    return jnp.pad(w3, ((0, 0), (0, 0), (0, pad - MOBA_HEAD_DIM))).reshape(d, MOBA_HEADS * pad)


def kernel(x, p, attn_norm, w_in, ret_norm, w_out, ffn_norm, w_up, w_gate, conv_w, conv_b, w_down,
           ple_norm, w_ple, w_ple_gate, final_norm):
    B, S, D = x.shape
    assert D == D_MODEL and S % ROW_TILE == 0 and w_in.shape[0] == 1
    bf, f32 = jnp.bfloat16, jnp.float32
    MW, RW = MOBA_WIDTH, RET_WIDTH
    w = w_in[0].astype(bf)
    wqT = w[:, 0:MW].T
    wk = _pad_heads(w[:, MW:2 * MW], QK_PAD)
    wvT = w[:, 2 * MW:3 * MW].T
    o = 3 * MW
    wrq = w[:, o:o + RW]
    wrkT = w[:, o + RW:o + 2 * RW].T
    wrv = w[:, o + 2 * RW:o + 3 * RW]
    wrg = w[:, o + 3 * RW:o + 4 * RW]
    row = lambda a: a.reshape(1, -1).astype(f32)

    slopes_l2 = jnp.exp2(-8.0 * jnp.arange(1, MOBA_HEADS + 1, dtype=f32) / MOBA_HEADS) * LOG2E
    kterm = slopes_l2[None, :] * jnp.arange(BLK, dtype=f32)[:, None]
    kterm_hi = lax.bitcast_convert_type(
        lax.bitcast_convert_type(kterm, jnp.uint32) & jnp.uint32(0xFFFF0000), f32)
    kterm_lo = kterm - kterm_hi
    kpos = jnp.zeros((BLK, MOBA_HEADS, QK_PAD), f32)
    kpos = kpos.at[:, :, ALIBI_ROW].set(kterm_hi).at[:, :, ALIBI_ROW + 1].set(kterm_lo)
    kpos = kpos.reshape(BLK, MOBA_HEADS * QK_PAD)

    qT, k, vT, sel, rq, rkT, rv, rg = _in_proj(
        x, row(attn_norm[0]), wqT, wk, kpos, wvT, wrq, wrkT, wrv, wrg)

    moba_out = _moba(slopes_l2, qT, k, vT, sel)
    ret_out = _retention(rq, rkT, rv, rg, ret_norm[0])

    wo = w_out[0].astype(bf)
    out = _tail(
        x.reshape(B * S, D), moba_out.reshape(B * S, MW), ret_out.reshape(B * S, RW),
        p[0].reshape(B * S, PLE_DIM),
        wo, row(ffn_norm[0]),
        w_up[0].astype(bf), w_gate[0].astype(bf), conv_w[0].astype(f32), row(conv_b[0]),
        w_down[0].astype(bf),
        row(ple_norm[0]), w_ple[0].astype(bf), w_ple_gate[0].astype(bf), row(final_norm),
        seq_len=S)
    return out.reshape(B, S, D)
```

```python
import functools
import math

import jax
import jax.numpy as jnp
from jax import lax
from jax.experimental import pallas as pl
from jax.experimental.pallas import tpu as pltpu

D_MODEL = 1024
PLE_DIM = 256
MOBA_HEADS = 8
MOBA_HEAD_DIM = 64
MOBA_WIDTH = MOBA_HEADS * MOBA_HEAD_DIM
MOBA_BLOCK = 256
MOBA_TOPK = 3
RET_HEADS = 4
RET_HEAD_DIM = 128
RET_WIDTH = RET_HEADS * RET_HEAD_DIM
RET_CHUNK = 256
D_FF = 2816
CONV_WIDTH = 3
RMS_EPS = 1e-6
GN_EPS = 1e-5

BLK = 256
LANES = 128
SUBLANES = 8
FF_CHUNK = 256
N_FF_CHUNKS = D_FF // FF_CHUNK
ROW_TILE = 512
VMEM_LIMIT = 56 * 1024 * 1024
NEG = -1e30
LOG2E = math.log2(math.e)

QK_PAD = LANES
ALIBI_ROW = MOBA_HEAD_DIM
V_PAD = 96
ONES_ROW = MOBA_HEAD_DIM
GROUP = 16
GROUP_OWN = 8

_NT = (((1,), (1,)), ((), ()))


def _const_spec(shape):
    nd = len(shape)
    return pl.BlockSpec(shape, lambda *_: (0,) * nd, pipeline_mode=pl.Buffered(1))


def _rms(x, g):
    ms = jnp.mean(x * x, axis=-1, keepdims=True)
    return x * lax.rsqrt(ms + RMS_EPS) * g


def _in_proj_kernel(x_ref, g_ref, wqT_ref, wk_ref, kpos_ref, wvT_ref,
                    wrq_ref, wrkT_ref, wrv_ref, wrg_ref,
                    qT_ref, k_ref, vT_ref, sel_ref, rq_ref, rkT_ref, rv_ref, rg_ref, kmean_ref):
    t = pl.program_id(1)
    h = _rms(x_ref[...], g_ref[...]).astype(jnp.bfloat16)
    nblk = ROW_TILE // BLK
    nb = kmean_ref.shape[0]

    @pl.when(t == 0)
    def _():
        kmean_ref[...] = jnp.zeros_like(kmean_ref)

    def nt(w_ref):
        return lax.dot_general(w_ref[...], h, _NT, preferred_element_type=jnp.float32)

    def nn(w_ref):
        return jnp.dot(h, w_ref[...], preferred_element_type=jnp.float32)

    Dh = MOBA_HEAD_DIM
    qT = (nt(wqT_ref) * (Dh ** -0.5 * LOG2E)).astype(jnp.bfloat16)
    vT = nt(wvT_ref).astype(jnp.bfloat16)
    rkT = (nt(wrkT_ref) * (RET_HEAD_DIM ** -0.5)).astype(jnp.bfloat16)
    q_pad = (lax.broadcasted_iota(jnp.int32, (QK_PAD - Dh, BLK), 0) < 2).astype(jnp.bfloat16)
    v_pad = (lax.broadcasted_iota(jnp.int32, (V_PAD - Dh, BLK), 0) < 1).astype(jnp.bfloat16)
    for b in range(nblk):
        sl = slice(b * BLK, (b + 1) * BLK)
        for a in range(MOBA_HEADS):
            qT_ref[b, a * QK_PAD:a * QK_PAD + Dh, :] = qT[a * Dh:(a + 1) * Dh, sl]
            qT_ref[b, a * QK_PAD + Dh:(a + 1) * QK_PAD, :] = q_pad
            vT_ref[b, a * V_PAD:a * V_PAD + Dh, :] = vT[a * Dh:(a + 1) * Dh, sl]
            vT_ref[b, a * V_PAD + Dh:(a + 1) * V_PAD, :] = v_pad
        rkT_ref[b] = rkT[:, sl]
    k = nn(wk_ref)
    for b in range(nblk):
        kb = k[b * BLK:(b + 1) * BLK]
        kmean_ref[pl.ds(t * nblk + b, 1), :] = jnp.mean(kb, axis=0, keepdims=True)
        k_ref[b * BLK:(b + 1) * BLK, :] = (kb + kpos_ref[...]).astype(jnp.bfloat16)
    rq_ref[...] = nn(wrq_ref).astype(jnp.bfloat16)
    rv_ref[...] = nn(wrv_ref).astype(jnp.bfloat16)
    rg_ref[...] = nn(wrg_ref).astype(jnp.bfloat16)

    km = kmean_ref[...]
    km_hi = km.astype(jnp.bfloat16)
    km_lo = (km - km_hi.astype(jnp.float32)).astype(jnp.bfloat16)
    blk_id = lax.broadcasted_iota(jnp.int32, (nb, BLK), 0)
    none = jnp.full((SUBLANES - MOBA_TOPK, BLK), nb, jnp.int32)
    for b in range(nblk):
        i = t * nblk + b
        for a in range(MOBA_HEADS):
            dims = slice(a * QK_PAD, a * QK_PAD + Dh)
            qTh = qT[a * Dh:(a + 1) * Dh, b * BLK:(b + 1) * BLK]
            gate = (jnp.dot(km_hi[:, dims], qTh, preferred_element_type=jnp.float32)
                    + jnp.dot(km_lo[:, dims], qTh, preferred_element_type=jnp.float32))
            gate = jnp.where(blk_id < i, gate, -jnp.inf)
            picks = []
            for _ in range(MOBA_TOPK):
                top = jnp.max(gate, axis=0, keepdims=True)
                is_top = (gate == top) & (top > -jnp.inf)
                first = jnp.min(jnp.where(is_top, blk_id, nb), axis=0, keepdims=True)
                picks.append(first)
                gate = jnp.where(blk_id == first, -jnp.inf, gate)
            sel_ref[a, b] = jnp.concatenate(picks + [none], axis=0)


def _in_proj(x, g, wqT, wk, kpos, wvT, wrq, wrkT, wrv, wrg):
    B, S, D = x.shape
    nb = S // BLK
    nblk = ROW_TILE // BLK
    grid = (B, S // ROW_TILE)
    bf = jnp.bfloat16
    QW, VW = MOBA_HEADS * QK_PAD, MOBA_HEADS * V_PAD
    nat = lambda w: pl.BlockSpec((None, ROW_TILE, w), lambda b, t: (b, t, 0))
    tr = lambda w: pl.BlockSpec((None, nblk, w, BLK), lambda b, t: (b, t, 0, 0))
    out_shape = (
        jax.ShapeDtypeStruct((B, nb, QW, BLK), bf),
        jax.ShapeDtypeStruct((B, S, QW), bf),
        jax.ShapeDtypeStruct((B, nb, VW, BLK), bf),
        jax.ShapeDtypeStruct((B, MOBA_HEADS, nb, SUBLANES, BLK), jnp.int32),
        jax.ShapeDtypeStruct((B, S, RET_WIDTH), bf),
        jax.ShapeDtypeStruct((B, nb, RET_WIDTH, BLK), bf),
        jax.ShapeDtypeStruct((B, S, RET_WIDTH), bf),
        jax.ShapeDtypeStruct((B, S, RET_WIDTH), bf),
    )
    out_specs = (
        tr(QW), nat(QW), tr(VW),
        pl.BlockSpec((None, MOBA_HEADS, nblk, SUBLANES, BLK), lambda b, t: (b, 0, t, 0, 0)),
        nat(RET_WIDTH), tr(RET_WIDTH), nat(RET_WIDTH), nat(RET_WIDTH),
    )
    consts = (g, wqT, wk, kpos, wvT, wrq, wrkT, wrv, wrg)
    in_specs = [pl.BlockSpec((None, ROW_TILE, D), lambda b, t: (b, t, 0))] + [
        _const_spec(a.shape) for a in consts]
    return pl.pallas_call(
        _in_proj_kernel, grid=grid, in_specs=in_specs, out_specs=out_specs, out_shape=out_shape,
        scratch_shapes=[pltpu.VMEM((nb, QW), jnp.float32)],
        compiler_params=pltpu.CompilerParams(
            dimension_semantics=("arbitrary", "arbitrary"), vmem_limit_bytes=VMEM_LIMIT),
        name="in_proj",
    )(x, *consts)


def _moba_kernel(slopes_ref, ti_ref, tj_ref, qT_ref, k_ref, vT_ref, sel_ref, o_ref,
                 m_ref, acc_ref, s_ref, smax_ref, alpha_ref, p_ref,
                 s_own_ref, smax_own_ref, alpha_own_ref, p_own_ref, *, n_past_groups):
    h = pl.program_id(1)
    nb = k_ref.shape[0]
    slope = slopes_ref[h]
    kpos = lax.broadcasted_iota(jnp.int32, (BLK, BLK), 0)
    qpos = lax.broadcasted_iota(jnp.int32, (BLK, BLK), 1)
    qterm = -slope * lax.broadcasted_iota(jnp.int32, (1, BLK), 1).astype(jnp.float32)

    m_ref[...] = jnp.full(m_ref.shape, -3e38, jnp.float32)
    acc_ref[...] = jnp.zeros(acc_ref.shape, jnp.float32)

    def own_tile(g, e):
        t = g * GROUP_OWN + e
        return t, t

    def past_tile(g, e):
        t = g * GROUP + e
        return ti_ref[t], tj_ref[t]

    def run_stream(n_groups, tile_of, own, bufs):
        s_ref, smax_ref, alpha_ref, p_ref = bufs
        GROUP = s_ref.shape[1]
        def scores(par, e, i_t, j_t):
            s = jnp.dot(k_ref[j_t], qT_ref[i_t], preferred_element_type=jnp.float32)
            if own:
                s = jnp.where(kpos <= qpos, s, NEG)
            s_ref[par, e] = s
            smax_ref[par, e:e + 1, :] = jnp.max(s, axis=0, keepdims=True)

        def values(par, e, i_t, j_t):
            pv = jnp.dot(vT_ref[j_t], p_ref[par, e * BLK:(e + 1) * BLK, :],
                         preferred_element_type=jnp.float32)
            acc_ref[i_t] = alpha_ref[par, e:e + 1, :] * acc_ref[i_t] + pv

        def softmax(par, e, i_t, j_t):
            if own:
                c = qterm
            else:
                sel = sel_ref[i_t]
                hit = (sel[0:1] == j_t) | (sel[1:2] == j_t) | (sel[2:3] == j_t)
                blk_dist = ((i_t - j_t) * MOBA_BLOCK).astype(jnp.float32)
                c = jnp.where(hit, -slope * blk_dist, NEG) + qterm
            m_old = m_ref[pl.ds(i_t, 1), :]
            m_new = jnp.maximum(m_old, smax_ref[par, e:e + 1, :] + c)
            alpha_ref[par, e:e + 1, :] = jnp.exp2(m_old - m_new)
            m_ref[pl.ds(i_t, 1), :] = m_new
            p_ref[par, e * BLK:(e + 1) * BLK, :] = (
                jnp.exp2(s_ref[par, e] - (m_new - c)).astype(jnp.bfloat16))

        def trip(g, par):
            g_next = jnp.minimum(g + 1, n_groups - 1)
            g_prev = jnp.maximum(g - 1, 0)
            for e in range(-2, GROUP):
                if e + 2 < GROUP:
                    scores(1 - par, e + 2, *tile_of(g_next, e + 2))
                if e >= 0:
                    values(1 - par, e, *tile_of(g_prev, e))
                    softmax(par, e, *tile_of(g, e))

        alpha_ref[1] = jnp.ones(alpha_ref.shape[1:], jnp.float32)
        p_ref[1] = jnp.zeros(p_ref.shape[1:], jnp.bfloat16)
        for e in range(GROUP):
            scores(0, e, *tile_of(0, e))

        def body(w, carry):
            trip(2 * w, 0)
            trip(2 * w + 1, 1)
            return carry

        assert n_groups % 2 == 0
        lax.fori_loop(0, n_groups // 2, body, 0)
        for e in range(GROUP):
            values((n_groups - 1) % 2, e, *tile_of(n_groups - 1, e))

    run_stream(nb // GROUP_OWN, own_tile, True, (s_own_ref, smax_own_ref, alpha_own_ref, p_own_ref))
    run_stream(n_past_groups, past_tile, False, (s_ref, smax_ref, alpha_ref, p_ref))

    def write_out(half):
        unroll = 4
        def some(u, carry):
            for d in range(unroll):
                i = u * unroll + d
                acc = acc_ref[i]
                oT = acc[0:MOBA_HEAD_DIM] / acc[ONES_ROW:ONES_ROW + 1]
                pad = jnp.zeros_like(oT)
                o2 = jnp.concatenate([pad, oT] if half else [oT, pad], axis=0).T
                rows = pl.ds(pl.multiple_of(i * BLK, BLK), BLK)
                lanes = slice(half * MOBA_HEAD_DIM, (half + 1) * MOBA_HEAD_DIM)
                o_ref[rows, lanes] = o2[:, lanes].astype(o_ref.dtype)
            return carry
        lax.fori_loop(0, nb // unroll, some, 0)

    lax.cond(h % 2 == 0, lambda: write_out(0), lambda: write_out(1))


def _moba(slopes_l2, qT, k, vT, sel):
    B, nb, QW, _ = qT.shape
    S = nb * BLK
    assert nb % GROUP == 0
    k4 = k.reshape(B, nb, BLK, QW)
    pairs = [(i, j) for j in range(nb - 1) for i in range(j + 1, nb)]
    pairs += [(0, 0)] * (-len(pairs) % GROUP)
    ti = jnp.asarray([p[0] for p in pairs], jnp.int32)
    tj = jnp.asarray([p[1] for p in pairs], jnp.int32)
    grid_spec = pltpu.PrefetchScalarGridSpec(
        num_scalar_prefetch=3,
        grid=(B, MOBA_HEADS),
        in_specs=[
            pl.BlockSpec((None, nb, QK_PAD, BLK), lambda b, h, *_: (b, 0, h, 0)),
            pl.BlockSpec((None, nb, BLK, QK_PAD), lambda b, h, *_: (b, 0, 0, h)),
            pl.BlockSpec((None, nb, V_PAD, BLK), lambda b, h, *_: (b, 0, h, 0)),
            pl.BlockSpec((None, None, nb, SUBLANES, BLK), lambda b, h, *_: (b, h, 0, 0, 0)),
        ],
        out_specs=pl.BlockSpec((None, S, 2 * MOBA_HEAD_DIM), lambda b, h, *_: (b, 0, h // 2)),
        scratch_shapes=[
            pltpu.VMEM((nb, BLK), jnp.float32),
            pltpu.VMEM((nb, V_PAD, BLK), jnp.float32),
            pltpu.VMEM((2, GROUP, BLK, BLK), jnp.float32),
            pltpu.VMEM((2, GROUP, BLK), jnp.float32),
            pltpu.VMEM((2, GROUP, BLK), jnp.float32),
            pltpu.VMEM((2, GROUP * BLK, BLK), jnp.bfloat16),
            pltpu.VMEM((2, GROUP_OWN, BLK, BLK), jnp.float32),
            pltpu.VMEM((2, GROUP_OWN, BLK), jnp.float32),
            pltpu.VMEM((2, GROUP_OWN, BLK), jnp.float32),
            pltpu.VMEM((2, GROUP_OWN * BLK, BLK), jnp.bfloat16),
        ],
    )
    return pl.pallas_call(
        functools.partial(_moba_kernel, n_past_groups=len(pairs) // GROUP),
        grid_spec=grid_spec,
        out_shape=jax.ShapeDtypeStruct((B, S, MOBA_WIDTH), jnp.bfloat16),
        compiler_params=pltpu.CompilerParams(
            dimension_semantics=("arbitrary", "arbitrary"), vmem_limit_bytes=VMEM_LIMIT),
        name="moba",
    )(slopes_l2, ti, tj, qT, k4, vT, sel)


def _ret_kernel(cd_ref, q_ref, kT_ref, v_ref, g_ref, dintra_ref, qdec_ref, kdec_ref, gn_ref,
                o_ref, state_ref):
    c = pl.program_id(0)

    @pl.when(c == 0)
    def _():
        state_ref[...] = jnp.zeros_like(state_ref)

    for b in range(q_ref.shape[0]):
        for h in range(RET_HEADS):
            cols = slice(h * RET_HEAD_DIM, (h + 1) * RET_HEAD_DIM)
            q = q_ref[b, :, cols]
            kT = kT_ref[b, cols, :]
            v = v_ref[b, :, cols]
            state = state_ref[b, h]
            s = jnp.dot(q, kT, preferred_element_type=jnp.float32) * dintra_ref[h]
            inner = jnp.dot(s.astype(jnp.bfloat16), v, preferred_element_type=jnp.float32)
            qd = (q.astype(jnp.float32) * qdec_ref[h]).astype(jnp.bfloat16)
            cross = jnp.dot(qd, state.astype(jnp.bfloat16), preferred_element_type=jnp.float32)
            kd = (kT.astype(jnp.float32) * kdec_ref[h]).astype(jnp.bfloat16)
            state_ref[b, h] = cd_ref[h] * state + jnp.dot(kd, v, preferred_element_type=jnp.float32)
            o = inner + cross
            mu = jnp.mean(o, axis=-1, keepdims=True)
            d = o - mu
            var = jnp.mean(d * d, axis=-1, keepdims=True)
            gate = g_ref[b, :, cols].astype(jnp.float32)
            y = d * lax.rsqrt(var + GN_EPS) * gn_ref[:, cols] * (gate * jax.nn.sigmoid(gate))
            o_ref[b, :, cols] = y.astype(o_ref.dtype)


def _retention(rq, rkT, rv, rg, ret_norm):
    B, S, W = rq.shape
    nc = S // BLK
    H = RET_HEADS
    C = RET_CHUNK
    f32 = jnp.float32
    gamma = 1.0 - jnp.exp2(-5.0 - jnp.arange(H, dtype=f32))
    log_g = jnp.log(gamma)
    idx = jnp.arange(C, dtype=f32)
    diff = idx[:, None] - idx[None, :]
    dintra = jnp.where(diff >= 0, jnp.exp(log_g[:, None, None] * jnp.maximum(diff, 0.0)), 0.0)
    qdec = jnp.broadcast_to(jnp.exp(log_g[:, None] * (idx + 1.0))[..., None], (H, C, LANES))
    kdec = jnp.exp(log_g[:, None] * (C - 1.0 - idx))[:, None, :]
    cdec = jnp.exp(log_g * C)
    gn = ret_norm.reshape(1, W).astype(f32)

    nat = pl.BlockSpec((B, C, W), lambda c, s: (0, c, 0))
    const = lambda a: pl.BlockSpec(a.shape, lambda c, s: (0,) * a.ndim,
                                   pipeline_mode=pl.Buffered(1))
    grid_spec = pltpu.PrefetchScalarGridSpec(
        num_scalar_prefetch=1,
        grid=(nc,),
        in_specs=[
            nat,
            pl.BlockSpec((B, None, W, C), lambda c, s: (0, c, 0, 0)),
            nat, nat,
            const(dintra), const(qdec), const(kdec), const(gn),
        ],
        out_specs=nat,
        scratch_shapes=[pltpu.VMEM((B, H, RET_HEAD_DIM, RET_HEAD_DIM), f32)],
    )
    return pl.pallas_call(
        _ret_kernel, grid_spec=grid_spec,
        out_shape=jax.ShapeDtypeStruct((B, S, W), jnp.bfloat16),
        compiler_params=pltpu.CompilerParams(
            dimension_semantics=("arbitrary",),
            vmem_limit_bytes=VMEM_LIMIT),
        name="retention",
    )(cdec, rq, rkT, rv, rg, dintra, qdec, kdec, gn)


def _gelu_tanh(u):
    k = -2.0 * math.sqrt(2.0 / math.pi) * LOG2E
    return u / (1.0 + jnp.exp2(u * (k + (k * 0.044715) * (u * u))))


def _shift_rows(cur, prev_tail, n):
    rolled = pltpu.roll(cur, n, axis=0)
    prev_rolled = pltpu.roll(prev_tail, n, axis=0)
    row = lax.broadcasted_iota(jnp.int32, prev_tail.shape, 0)
    head = jnp.where(row < n, prev_rolled, rolled[0:SUBLANES])
    return jnp.concatenate([head, rolled[SUBLANES:]], axis=0)


def _tail_kernel(x_ref, moba_ref, ret_ref, p_ref, wo_ref, ffn_g_ref, wup_ref, wgate_ref,
                 convw_ref, convb_ref, wdown_ref, ple_g_ref, wple_ref, wpg_ref, fin_g_ref,
                 o_ref, y_ref, carry_ref, *, tiles_per_seq):
    t = pl.program_id(0)
    R = ROW_TILE

    @pl.when(t % tiles_per_seq == 0)
    def _():
        carry_ref[...] = jnp.zeros_like(carry_ref)

    mix = jnp.concatenate([moba_ref[...], ret_ref[...]], axis=1)
    x1 = x_ref[...] + jnp.dot(mix, wo_ref[...], preferred_element_type=jnp.float32)
    h = _rms(x1, ffn_g_ref[...]).astype(jnp.bfloat16)

    for c in range(N_FF_CHUNKS):
        cols = slice(c * FF_CHUNK, (c + 1) * FF_CHUNK)
        up = jnp.dot(h, wup_ref[:, cols], preferred_element_type=jnp.float32)
        gt = jnp.dot(h, wgate_ref[:, cols], preferred_element_type=jnp.float32)
        prev_tail = carry_ref[c]
        carry_ref[c] = up[R - SUBLANES:R, :]
        w = convw_ref[:, cols]
        u = (convb_ref[:, cols]
             + w[0:1, :] * _shift_rows(up, prev_tail, 2)
             + w[1:2, :] * _shift_rows(up, prev_tail, 1)
             + w[2:3, :] * up)
        y_ref[:, cols] = (_gelu_tanh(u) * gt).astype(jnp.bfloat16)
    x2 = x1 + jnp.dot(y_ref[...], wdown_ref[...], preferred_element_type=jnp.float32)

    hn = _rms(x2, ple_g_ref[...]).astype(jnp.bfloat16)
    g = jax.nn.sigmoid(jnp.dot(hn, wpg_ref[...], preferred_element_type=jnp.float32))
    pe = jnp.dot(p_ref[...].astype(jnp.bfloat16), wple_ref[...], preferred_element_type=jnp.float32)
    x3 = x2 + pe * g
    o_ref[...] = _rms(x3, fin_g_ref[...])


def _tail(x2d, moba2d, ret2d, p2d, wo, ffn_g, wup, wgate, convw, convb, wdown, ple_g, wple,
          wpg, fin_g, seq_len):
    T, D = x2d.shape
    R = ROW_TILE
    row = lambda w: pl.BlockSpec((R, w), lambda t: (t, 0))
    consts = (wo, ffn_g, wup, wgate, convw, convb, wdown, ple_g, wple, wpg, fin_g)
    in_specs = [row(D), row(MOBA_WIDTH), row(RET_WIDTH), row(PLE_DIM)] + [
        _const_spec(a.shape) for a in consts]
    return pl.pallas_call(
        functools.partial(_tail_kernel, tiles_per_seq=seq_len // R),
        grid=(T // R,), in_specs=in_specs, out_specs=row(D),
        out_shape=jax.ShapeDtypeStruct((T, D), jnp.float32),
        scratch_shapes=[
            pltpu.VMEM((R, D_FF), jnp.bfloat16),
            pltpu.VMEM((N_FF_CHUNKS, SUBLANES, FF_CHUNK), jnp.float32),
        ],
        compiler_params=pltpu.CompilerParams(
            dimension_semantics=("arbitrary",), vmem_limit_bytes=VMEM_LIMIT),
        name="tail",
    )(x2d, moba2d, ret2d, p2d, *consts)


def _pad_heads(w, pad):
    d = w.shape[0]
    w3 = w.reshape(d, MOBA_HEADS, MOBA_HEAD_DIM)
    return jnp.pad(w3, ((0, 0), (0, 0), (0, pad - MOBA_HEAD_DIM))).reshape(d, MOBA_HEADS * pad)


def kernel(x, p, attn_norm, w_in, ret_norm, w_out, ffn_norm, w_up, w_gate, conv_w, conv_b, w_down,
           ple_norm, w_ple, w_ple_gate, final_norm):
    B, S, D = x.shape
    assert D == D_MODEL and S % ROW_TILE == 0 and w_in.shape[0] == 1
    bf, f32 = jnp.bfloat16, jnp.float32
    MW, RW = MOBA_WIDTH, RET_WIDTH
    w = w_in[0].astype(bf)
    wqT = w[:, 0:MW].T
    wk = _pad_heads(w[:, MW:2 * MW], QK_PAD)
    wvT = w[:, 2 * MW:3 * MW].T
    o = 3 * MW
    wrq = w[:, o:o + RW]
    wrkT = w[:, o + RW:o + 2 * RW].T
    wrv = w[:, o + 2 * RW:o + 3 * RW]
    wrg = w[:, o + 3 * RW:o + 4 * RW]
    row = lambda a: a.reshape(1, -1).astype(f32)

    slopes_l2 = jnp.exp2(-8.0 * jnp.arange(1, MOBA_HEADS + 1, dtype=f32) / MOBA_HEADS) * LOG2E
    kterm = slopes_l2[None, :] * jnp.arange(BLK, dtype=f32)[:, None]
    kterm_hi = lax.bitcast_convert_type(
        lax.bitcast_convert_type(kterm, jnp.uint32) & jnp.uint32(0xFFFF0000), f32)
    kterm_lo = kterm - kterm_hi
    kpos = jnp.zeros((BLK, MOBA_HEADS, QK_PAD), f32)
    kpos = kpos.at[:, :, ALIBI_ROW].set(kterm_hi).at[:, :, ALIBI_ROW + 1].set(kterm_lo)
    kpos = kpos.reshape(BLK, MOBA_HEADS * QK_PAD)

    qT, k, vT, sel, rq, rkT, rv, rg = _in_proj(
        x, row(attn_norm[0]), wqT, wk, kpos, wvT, wrq, wrkT, wrv, wrg)

    moba_out = _moba(slopes_l2, qT, k, vT, sel)
    ret_out = _retention(rq, rkT, rv, rg, ret_norm[0])

    wo = w_out[0].astype(bf)
    out = _tail(
        x.reshape(B * S, D), moba_out.reshape(B * S, MW), ret_out.reshape(B * S, RW),
        p[0].reshape(B * S, PLE_DIM),
        wo, row(ffn_norm[0]),
        w_up[0].astype(bf), w_gate[0].astype(bf), conv_w[0].astype(f32), row(conv_b[0]),
        w_down[0].astype(bf),
        row(ple_norm[0]), w_ple[0].astype(bf), w_ple_gate[0].astype(bf), row(final_norm),
        seq_len=S)
    return out.reshape(B, S, D)
```

```python
import functools
import math

import jax
import jax.numpy as jnp
from jax import lax
from jax.experimental import pallas as pl
from jax.experimental.pallas import tpu as pltpu

D_MODEL = 1024
PLE_DIM = 256
MOBA_HEADS = 8
MOBA_HEAD_DIM = 64
MOBA_WIDTH = MOBA_HEADS * MOBA_HEAD_DIM
MOBA_BLOCK = 256
MOBA_TOPK = 3
RET_HEADS = 4
RET_HEAD_DIM = 128
RET_WIDTH = RET_HEADS * RET_HEAD_DIM
RET_CHUNK = 256
D_FF = 2816
CONV_WIDTH = 3
RMS_EPS = 1e-6
GN_EPS = 1e-5

BLK = 256
LANES = 128
SUBLANES = 8
FF_CHUNK = 256
N_FF_CHUNKS = D_FF // FF_CHUNK
ROW_TILE = 512
VMEM_LIMIT = 56 * 1024 * 1024
NEG = -1e30
LOG2E = math.log2(math.e)

QK_PAD = LANES
ALIBI_ROW = MOBA_HEAD_DIM
V_PAD = 80
ONES_ROW = MOBA_HEAD_DIM
GROUP = 16
GROUP_OWN = 8

_NT = (((1,), (1,)), ((), ()))


def _const_spec(shape):
    nd = len(shape)
    return pl.BlockSpec(shape, lambda *_: (0,) * nd, pipeline_mode=pl.Buffered(1))


def _rms(x, g):
    ms = jnp.mean(x * x, axis=-1, keepdims=True)
    return x * lax.rsqrt(ms + RMS_EPS) * g


def _in_proj_kernel(x_ref, g_ref, wqT_ref, wk_ref, kpos_ref, wvT_ref,
                    wrq_ref, wrkT_ref, wrv_ref, wrg_ref,
                    qT_ref, k_ref, vT_ref, sel_ref, rq_ref, rkT_ref, rv_ref, rg_ref, kmean_ref):
    t = pl.program_id(1)
    h = _rms(x_ref[...], g_ref[...]).astype(jnp.bfloat16)
    nblk = ROW_TILE // BLK
    nb = kmean_ref.shape[0]

    @pl.when(t == 0)
    def _():
        kmean_ref[...] = jnp.zeros_like(kmean_ref)

    def nt(w_ref):
        return lax.dot_general(w_ref[...], h, _NT, preferred_element_type=jnp.float32)

    def nn(w_ref):
        return jnp.dot(h, w_ref[...], preferred_element_type=jnp.float32)

    Dh = MOBA_HEAD_DIM
    qT = (nt(wqT_ref) * (Dh ** -0.5 * LOG2E)).astype(jnp.bfloat16)
    vT = nt(wvT_ref).astype(jnp.bfloat16)
    rkT = (nt(wrkT_ref) * (RET_HEAD_DIM ** -0.5)).astype(jnp.bfloat16)
    q_pad = (lax.broadcasted_iota(jnp.int32, (QK_PAD - Dh, BLK), 0) < 2).astype(jnp.bfloat16)
    v_pad = (lax.broadcasted_iota(jnp.int32, (V_PAD - Dh, BLK), 0) < 1).astype(jnp.bfloat16)
    for b in range(nblk):
        sl = slice(b * BLK, (b + 1) * BLK)
        for a in range(MOBA_HEADS):
            qT_ref[b, a * QK_PAD:a * QK_PAD + Dh, :] = qT[a * Dh:(a + 1) * Dh, sl]
            qT_ref[b, a * QK_PAD + Dh:(a + 1) * QK_PAD, :] = q_pad
            vT_ref[b, a * V_PAD:a * V_PAD + Dh, :] = vT[a * Dh:(a + 1) * Dh, sl]
            vT_ref[b, a * V_PAD + Dh:(a + 1) * V_PAD, :] = v_pad
        rkT_ref[b] = rkT[:, sl]
    k = nn(wk_ref)
    lane = lax.broadcasted_iota(jnp.int32, (BLK, LANES), 1)
    for b in range(nblk):
        kb = k[b * BLK:(b + 1) * BLK]
        kmean_ref[pl.ds(t * nblk + b, 1), :] = jnp.mean(kb, axis=0, keepdims=True)
        for a in range(MOBA_HEADS):
            pair = kb[:, (a // 2) * LANES:(a // 2 + 1) * LANES]
            if a % 2:
                pair = pltpu.roll(pair, Dh, axis=1)
            cols = slice(a * QK_PAD, (a + 1) * QK_PAD)
            k_ref[b * BLK:(b + 1) * BLK, cols] = (
                jnp.where(lane < Dh, pair, 0.0) + kpos_ref[:, cols]).astype(jnp.bfloat16)
    rq_ref[...] = nn(wrq_ref).astype(jnp.bfloat16)
    rv_ref[...] = nn(wrv_ref).astype(jnp.bfloat16)
    rg_ref[...] = nn(wrg_ref).astype(jnp.bfloat16)

    km = kmean_ref[...]
    km_hi = km.astype(jnp.bfloat16)
    km_lo = (km - km_hi.astype(jnp.float32)).astype(jnp.bfloat16)
    blk_id = lax.broadcasted_iota(jnp.int32, (nb, BLK), 0)
    none = jnp.full((SUBLANES - MOBA_TOPK, BLK), nb, jnp.int32)
    for b in range(nblk):
        i = t * nblk + b
        for a in range(MOBA_HEADS):
            dims = slice(a * Dh, (a + 1) * Dh)
            qTh = qT[a * Dh:(a + 1) * Dh, b * BLK:(b + 1) * BLK]
            gate = (jnp.dot(km_hi[:, dims], qTh, preferred_element_type=jnp.float32)
                    + jnp.dot(km_lo[:, dims], qTh, preferred_element_type=jnp.float32))
            gate = jnp.where(blk_id < i, gate, -jnp.inf)
            picks = []
            for _ in range(MOBA_TOPK):
                top = jnp.max(gate, axis=0, keepdims=True)
                is_top = (gate == top) & (top > -jnp.inf)
                first = jnp.min(jnp.where(is_top, blk_id, nb), axis=0, keepdims=True)
                picks.append(first)
                gate = jnp.where(blk_id == first, -jnp.inf, gate)
            sel_ref[a, b] = jnp.concatenate(picks + [none], axis=0)


def _in_proj(x, g, wqT, wk, kpos, wvT, wrq, wrkT, wrv, wrg):
    B, S, D = x.shape
    nb = S // BLK
    nblk = ROW_TILE // BLK
    grid = (B, S // ROW_TILE)
    bf = jnp.bfloat16
    QW, VW = MOBA_HEADS * QK_PAD, MOBA_HEADS * V_PAD
    nat = lambda w: pl.BlockSpec((None, ROW_TILE, w), lambda b, t: (b, t, 0))
    tr = lambda w: pl.BlockSpec((None, nblk, w, BLK), lambda b, t: (b, t, 0, 0))
    out_shape = (
        jax.ShapeDtypeStruct((B, nb, QW, BLK), bf),
        jax.ShapeDtypeStruct((B, S, QW), bf),
        jax.ShapeDtypeStruct((B, nb, VW, BLK), bf),
        jax.ShapeDtypeStruct((B, MOBA_HEADS, nb, SUBLANES, BLK), jnp.int32),
        jax.ShapeDtypeStruct((B, S, RET_WIDTH), bf),
        jax.ShapeDtypeStruct((B, nb, RET_WIDTH, BLK), bf),
        jax.ShapeDtypeStruct((B, S, RET_WIDTH), bf),
        jax.ShapeDtypeStruct((B, S, RET_WIDTH), bf),
    )
    out_specs = (
        tr(QW), nat(QW), tr(VW),
        pl.BlockSpec((None, MOBA_HEADS, nblk, SUBLANES, BLK), lambda b, t: (b, 0, t, 0, 0)),
        nat(RET_WIDTH), tr(RET_WIDTH), nat(RET_WIDTH), nat(RET_WIDTH),
    )
    consts = (g, wqT, wk, kpos, wvT, wrq, wrkT, wrv, wrg)
    in_specs = [pl.BlockSpec((None, ROW_TILE, D), lambda b, t: (b, t, 0))] + [
        _const_spec(a.shape) for a in consts]
    return pl.pallas_call(
        _in_proj_kernel, grid=grid, in_specs=in_specs, out_specs=out_specs, out_shape=out_shape,
        scratch_shapes=[pltpu.VMEM((nb, MOBA_WIDTH), jnp.float32)],
        compiler_params=pltpu.CompilerParams(
            dimension_semantics=("arbitrary", "arbitrary"), vmem_limit_bytes=VMEM_LIMIT),
        name="in_proj",
    )(x, *consts)


def _moba_kernel(slopes_ref, ti_ref, tj_ref, qT_ref, k_ref, vT_ref, sel_ref, o_ref,
                 m_ref, acc_ref, s_ref, smax_ref, alpha_ref, p_ref,
                 s_own_ref, smax_own_ref, alpha_own_ref, p_own_ref, *, n_past_groups):
    h = pl.program_id(1)
    nb = k_ref.shape[0]
    slope = slopes_ref[h]
    kpos = lax.broadcasted_iota(jnp.int32, (BLK, BLK), 0)
    qpos = lax.broadcasted_iota(jnp.int32, (BLK, BLK), 1)
    qterm = -slope * lax.broadcasted_iota(jnp.int32, (1, BLK), 1).astype(jnp.float32)

    m_ref[...] = jnp.full(m_ref.shape, -3e38, jnp.float32)
    acc_ref[...] = jnp.zeros(acc_ref.shape, jnp.float32)

    def own_tile(g, e):
        t = g * GROUP_OWN + e
        return t, t

    def past_tile(g, e):
        t = g * GROUP + e
        return ti_ref[t], tj_ref[t]

    def run_stream(n_groups, tile_of, own, bufs):
        s_ref, smax_ref, alpha_ref, p_ref = bufs
        GROUP = s_ref.shape[1]
        def scores(par, e, i_t, j_t):
            s = jnp.dot(k_ref[j_t], qT_ref[i_t], preferred_element_type=jnp.float32)
            if own:
                s = jnp.where(kpos <= qpos, s, NEG)
            s_ref[par, e] = s
            smax_ref[par, e:e + 1, :] = jnp.max(s, axis=0, keepdims=True)

        def values(par, e, i_t, j_t):
            pv = jnp.dot(vT_ref[j_t], p_ref[par, e * BLK:(e + 1) * BLK, :],
                         preferred_element_type=jnp.float32)
            acc_ref[i_t] = alpha_ref[par, e:e + 1, :] * acc_ref[i_t] + pv

        def softmax(par, e, i_t, j_t):
            if own:
                c = qterm
            else:
                sel = sel_ref[i_t]
                hit = (sel[0:1] == j_t) | (sel[1:2] == j_t) | (sel[2:3] == j_t)
                blk_dist = ((i_t - j_t) * MOBA_BLOCK).astype(jnp.float32)
                c = jnp.where(hit, -slope * blk_dist, NEG) + qterm
            m_old = m_ref[pl.ds(i_t, 1), :]
            m_new = jnp.maximum(m_old, smax_ref[par, e:e + 1, :] + c)
            alpha_ref[par, e:e + 1, :] = jnp.exp2(m_old - m_new)
            m_ref[pl.ds(i_t, 1), :] = m_new
            p_ref[par, e * BLK:(e + 1) * BLK, :] = (
                jnp.exp2(s_ref[par, e] - (m_new - c)).astype(jnp.bfloat16))

        def trip(g, par):
            g_next = jnp.minimum(g + 1, n_groups - 1)
            g_prev = jnp.maximum(g - 1, 0)
            for e in range(-2, GROUP):
                if e + 2 < GROUP:
                    scores(1 - par, e + 2, *tile_of(g_next, e + 2))
                if e >= 0:
                    values(1 - par, e, *tile_of(g_prev, e))
                    softmax(par, e, *tile_of(g, e))

        alpha_ref[1] = jnp.ones(alpha_ref.shape[1:], jnp.float32)
        p_ref[1] = jnp.zeros(p_ref.shape[1:], jnp.bfloat16)
        for e in range(GROUP):
            scores(0, e, *tile_of(0, e))

        def body(w, carry):
            trip(2 * w, 0)
            trip(2 * w + 1, 1)
            return carry

        assert n_groups % 2 == 0
        lax.fori_loop(0, n_groups // 2, body, 0)
        for e in range(GROUP):
            values((n_groups - 1) % 2, e, *tile_of(n_groups - 1, e))

    run_stream(nb // GROUP_OWN, own_tile, True, (s_own_ref, smax_own_ref, alpha_own_ref, p_own_ref))
    run_stream(n_past_groups, past_tile, False, (s_ref, smax_ref, alpha_ref, p_ref))

    def write_out(half):
        unroll = 4
        def some(u, carry):
            for d in range(unroll):
                i = u * unroll + d
                acc = acc_ref[i]
                oT = acc[0:MOBA_HEAD_DIM] / acc[ONES_ROW:ONES_ROW + 1]
                pad = jnp.zeros_like(oT)
                o2 = jnp.concatenate([pad, oT] if half else [oT, pad], axis=0).T
                rows = pl.ds(pl.multiple_of(i * BLK, BLK), BLK)
                lanes = slice(half * MOBA_HEAD_DIM, (half + 1) * MOBA_HEAD_DIM)
                o_ref[rows, lanes] = o2[:, lanes].astype(o_ref.dtype)
            return carry
        lax.fori_loop(0, nb // unroll, some, 0)

    lax.cond(h % 2 == 0, lambda: write_out(0), lambda: write_out(1))


def _moba(slopes_l2, qT, k, vT, sel):
    B, nb, QW, _ = qT.shape
    S = nb * BLK
    assert nb % GROUP == 0
    k4 = k.reshape(B, nb, BLK, QW)
    pairs = [(i, j) for j in range(nb - 1) for i in range(j + 1, nb)]
    pairs += [(0, 0)] * (-len(pairs) % GROUP)
    ti = jnp.asarray([p[0] for p in pairs], jnp.int32)
    tj = jnp.asarray([p[1] for p in pairs], jnp.int32)
    grid_spec = pltpu.PrefetchScalarGridSpec(
        num_scalar_prefetch=3,
        grid=(B, MOBA_HEADS),
        in_specs=[
            pl.BlockSpec((None, nb, QK_PAD, BLK), lambda b, h, *_: (b, 0, h, 0)),
            pl.BlockSpec((None, nb, BLK, QK_PAD), lambda b, h, *_: (b, 0, 0, h)),
            pl.BlockSpec((None, nb, V_PAD, BLK), lambda b, h, *_: (b, 0, h, 0)),
            pl.BlockSpec((None, None, nb, SUBLANES, BLK), lambda b, h, *_: (b, h, 0, 0, 0)),
        ],
        out_specs=pl.BlockSpec((None, S, 2 * MOBA_HEAD_DIM), lambda b, h, *_: (b, 0, h // 2)),
        scratch_shapes=[
            pltpu.VMEM((nb, BLK), jnp.float32),
            pltpu.VMEM((nb, V_PAD, BLK), jnp.float32),
            pltpu.VMEM((2, GROUP, BLK, BLK), jnp.float32),
            pltpu.VMEM((2, GROUP, BLK), jnp.float32),
            pltpu.VMEM((2, GROUP, BLK), jnp.float32),
            pltpu.VMEM((2, GROUP * BLK, BLK), jnp.bfloat16),
            pltpu.VMEM((2, GROUP_OWN, BLK, BLK), jnp.float32),
            pltpu.VMEM((2, GROUP_OWN, BLK), jnp.float32),
            pltpu.VMEM((2, GROUP_OWN, BLK), jnp.float32),
            pltpu.VMEM((2, GROUP_OWN * BLK, BLK), jnp.bfloat16),
        ],
    )
    return pl.pallas_call(
        functools.partial(_moba_kernel, n_past_groups=len(pairs) // GROUP),
        grid_spec=grid_spec,
        out_shape=jax.ShapeDtypeStruct((B, S, MOBA_WIDTH), jnp.bfloat16),
        compiler_params=pltpu.CompilerParams(
            dimension_semantics=("arbitrary", "arbitrary"), vmem_limit_bytes=VMEM_LIMIT),
        name="moba",
    )(slopes_l2, ti, tj, qT, k4, vT, sel)


def _ret_kernel(cd_ref, q_ref, kT_ref, v_ref, g_ref, dintra_ref, qdec_ref, kdec_ref, gn_ref,
                o_ref, state_ref):
    c = pl.program_id(0)

    @pl.when(c == 0)
    def _():
        state_ref[...] = jnp.zeros_like(state_ref)

    for b in range(q_ref.shape[0]):
        for h in range(RET_HEADS):
            cols = slice(h * RET_HEAD_DIM, (h + 1) * RET_HEAD_DIM)
            q = q_ref[b, :, cols]
            kT = kT_ref[b, cols, :]
            v = v_ref[b, :, cols]
            state = state_ref[b, h]
            s = jnp.dot(q, kT, preferred_element_type=jnp.float32) * dintra_ref[h]
            inner = jnp.dot(s.astype(jnp.bfloat16), v, preferred_element_type=jnp.float32)
            qd = (q.astype(jnp.float32) * qdec_ref[h]).astype(jnp.bfloat16)
            cross = jnp.dot(qd, state.astype(jnp.bfloat16), preferred_element_type=jnp.float32)
            kd = (kT.astype(jnp.float32) * kdec_ref[h]).astype(jnp.bfloat16)
            state_ref[b, h] = cd_ref[h] * state + jnp.dot(kd, v, preferred_element_type=jnp.float32)
            o = inner + cross
            mu = jnp.mean(o, axis=-1, keepdims=True)
            d = o - mu
            var = jnp.mean(d * d, axis=-1, keepdims=True)
            gate = g_ref[b, :, cols].astype(jnp.float32)
            y = d * lax.rsqrt(var + GN_EPS) * gn_ref[:, cols] * (gate * jax.nn.sigmoid(gate))
            o_ref[b, :, cols] = y.astype(o_ref.dtype)


def _retention(rq, rkT, rv, rg, ret_norm):
    B, S, W = rq.shape
    nc = S // BLK
    H = RET_HEADS
    C = RET_CHUNK
    f32 = jnp.float32
    gamma = 1.0 - jnp.exp2(-5.0 - jnp.arange(H, dtype=f32))
    log_g = jnp.log(gamma)
    idx = jnp.arange(C, dtype=f32)
    diff = idx[:, None] - idx[None, :]
    dintra = jnp.where(diff >= 0, jnp.exp(log_g[:, None, None] * jnp.maximum(diff, 0.0)), 0.0)
    qdec = jnp.broadcast_to(jnp.exp(log_g[:, None] * (idx + 1.0))[..., None], (H, C, LANES))
    kdec = jnp.exp(log_g[:, None] * (C - 1.0 - idx))[:, None, :]
    cdec = jnp.exp(log_g * C)
    gn = ret_norm.reshape(1, W).astype(f32)

    nat = pl.BlockSpec((B, C, W), lambda c, s: (0, c, 0))
    const = lambda a: pl.BlockSpec(a.shape, lambda c, s: (0,) * a.ndim,
                                   pipeline_mode=pl.Buffered(1))
    grid_spec = pltpu.PrefetchScalarGridSpec(
        num_scalar_prefetch=1,
        grid=(nc,),
        in_specs=[
            nat,
            pl.BlockSpec((B, None, W, C), lambda c, s: (0, c, 0, 0)),
            nat, nat,
            const(dintra), const(qdec), const(kdec), const(gn),
        ],
        out_specs=nat,
        scratch_shapes=[pltpu.VMEM((B, H, RET_HEAD_DIM, RET_HEAD_DIM), f32)],
    )
    return pl.pallas_call(
        _ret_kernel, grid_spec=grid_spec,
        out_shape=jax.ShapeDtypeStruct((B, S, W), jnp.bfloat16),
        compiler_params=pltpu.CompilerParams(
            dimension_semantics=("arbitrary",),
            vmem_limit_bytes=VMEM_LIMIT),
        name="retention",
    )(cdec, rq, rkT, rv, rg, dintra, qdec, kdec, gn)


def _gelu_tanh(u):
    k = -2.0 * math.sqrt(2.0 / math.pi) * LOG2E
    return u / (1.0 + jnp.exp2(u * (k + (k * 0.044715) * (u * u))))


def _shift_rows(cur, prev_tail, n):
    rolled = pltpu.roll(cur, n, axis=0)
    prev_rolled = pltpu.roll(prev_tail, n, axis=0)
    row = lax.broadcasted_iota(jnp.int32, prev_tail.shape, 0)
    head = jnp.where(row < n, prev_rolled, rolled[0:SUBLANES])
    return jnp.concatenate([head, rolled[SUBLANES:]], axis=0)


def _tail_kernel(x_ref, moba_ref, ret_ref, p_ref, wo_ref, ffn_g_ref, wup_ref, wgate_ref,
                 convw_ref, convb_ref, wdown_ref, ple_g_ref, wple_ref, wpg_ref, fin_g_ref,
                 o_ref, y_ref, carry_ref, *, tiles_per_seq):
    t = pl.program_id(0)
    R = ROW_TILE

    @pl.when(t % tiles_per_seq == 0)
    def _():
        carry_ref[...] = jnp.zeros_like(carry_ref)

    mix = jnp.concatenate([moba_ref[...], ret_ref[...]], axis=1)
    x1 = x_ref[...] + jnp.dot(mix, wo_ref[...], preferred_element_type=jnp.float32)
    h = _rms(x1, ffn_g_ref[...]).astype(jnp.bfloat16)

    for c in range(N_FF_CHUNKS):
        cols = slice(c * FF_CHUNK, (c + 1) * FF_CHUNK)
        up = jnp.dot(h, wup_ref[:, cols], preferred_element_type=jnp.float32)
        gt = jnp.dot(h, wgate_ref[:, cols], preferred_element_type=jnp.float32)
        prev_tail = carry_ref[c]
        carry_ref[c] = up[R - SUBLANES:R, :]
        w = convw_ref[:, cols]
        u = (convb_ref[:, cols]
             + w[0:1, :] * _shift_rows(up, prev_tail, 2)
             + w[1:2, :] * _shift_rows(up, prev_tail, 1)
             + w[2:3, :] * up)
        y_ref[:, cols] = (_gelu_tanh(u) * gt).astype(jnp.bfloat16)
    x2 = x1 + jnp.dot(y_ref[...], wdown_ref[...], preferred_element_type=jnp.float32)

    hn = _rms(x2, ple_g_ref[...]).astype(jnp.bfloat16)
    g = jax.nn.sigmoid(jnp.dot(hn, wpg_ref[...], preferred_element_type=jnp.float32))
    pe = jnp.dot(p_ref[...].astype(jnp.bfloat16), wple_ref[...], preferred_element_type=jnp.float32)
    x3 = x2 + pe * g
    o_ref[...] = _rms(x3, fin_g_ref[...])


def _tail(x2d, moba2d, ret2d, p2d, wo, ffn_g, wup, wgate, convw, convb, wdown, ple_g, wple,
          wpg, fin_g, seq_len):
    T, D = x2d.shape
    R = ROW_TILE
    row = lambda w: pl.BlockSpec((R, w), lambda t: (t, 0))
    consts = (wo, ffn_g, wup, wgate, convw, convb, wdown, ple_g, wple, wpg, fin_g)
    in_specs = [row(D), row(MOBA_WIDTH), row(RET_WIDTH), row(PLE_DIM)] + [
        _const_spec(a.shape) for a in consts]
    return pl.pallas_call(
        functools.partial(_tail_kernel, tiles_per_seq=seq_len // R),
        grid=(T // R,), in_specs=in_specs, out_specs=row(D),
        out_shape=jax.ShapeDtypeStruct((T, D), jnp.float32),
        scratch_shapes=[
            pltpu.VMEM((R, D_FF), jnp.bfloat16),
            pltpu.VMEM((N_FF_CHUNKS, SUBLANES, FF_CHUNK), jnp.float32),
        ],
        compiler_params=pltpu.CompilerParams(
            dimension_semantics=("arbitrary",), vmem_limit_bytes=VMEM_LIMIT),
        name="tail",
    )(x2d, moba2d, ret2d, p2d, *consts)


def kernel(x, p, attn_norm, w_in, ret_norm, w_out, ffn_norm, w_up, w_gate, conv_w, conv_b, w_down,
           ple_norm, w_ple, w_ple_gate, final_norm):
    B, S, D = x.shape
    assert D == D_MODEL and S % ROW_TILE == 0 and w_in.shape[0] == 1
    bf, f32 = jnp.bfloat16, jnp.float32
    MW, RW = MOBA_WIDTH, RET_WIDTH
    w = w_in[0].astype(bf)
    wqT = w[:, 0:MW].T
    wk = w[:, MW:2 * MW]
    wvT = w[:, 2 * MW:3 * MW].T
    o = 3 * MW
    wrq = w[:, o:o + RW]
    wrkT = w[:, o + RW:o + 2 * RW].T
    wrv = w[:, o + 2 * RW:o + 3 * RW]
    wrg = w[:, o + 3 * RW:o + 4 * RW]
    row = lambda a: a.reshape(1, -1).astype(f32)

    slopes_l2 = jnp.exp2(-8.0 * jnp.arange(1, MOBA_HEADS + 1, dtype=f32) / MOBA_HEADS) * LOG2E
    kterm = slopes_l2[None, :] * jnp.arange(BLK, dtype=f32)[:, None]
    kterm_hi = lax.bitcast_convert_type(
        lax.bitcast_convert_type(kterm, jnp.uint32) & jnp.uint32(0xFFFF0000), f32)
    kterm_lo = kterm - kterm_hi
    kpos = jnp.zeros((BLK, MOBA_HEADS, QK_PAD), f32)
    kpos = kpos.at[:, :, ALIBI_ROW].set(kterm_hi).at[:, :, ALIBI_ROW + 1].set(kterm_lo)
    kpos = kpos.reshape(BLK, MOBA_HEADS * QK_PAD)

    qT, k, vT, sel, rq, rkT, rv, rg = _in_proj(
        x, row(attn_norm[0]), wqT, wk, kpos, wvT, wrq, wrkT, wrv, wrg)

    moba_out = _moba(slopes_l2, qT, k, vT, sel)
    ret_out = _retention(rq, rkT, rv, rg, ret_norm[0])

    wo = w_out[0].astype(bf)
    out = _tail(
        x.reshape(B * S, D), moba_out.reshape(B * S, MW), ret_out.reshape(B * S, RW),
        p[0].reshape(B * S, PLE_DIM),
        wo, row(ffn_norm[0]),
        w_up[0].astype(bf), w_gate[0].astype(bf), conv_w[0].astype(f32), row(conv_b[0]),
        w_down[0].astype(bf),
        row(ple_norm[0]), w_ple[0].astype(bf), w_ple_gate[0].astype(bf), row(final_norm),
        seq_len=S)
    return out.reshape(B, S, D)
```

```python
import functools
import math

import jax
import jax.numpy as jnp
from jax import lax
from jax.experimental import pallas as pl
from jax.experimental.pallas import tpu as pltpu

D_MODEL = 1024
PLE_DIM = 256
MOBA_HEADS = 8
MOBA_HEAD_DIM = 64
MOBA_WIDTH = MOBA_HEADS * MOBA_HEAD_DIM
MOBA_BLOCK = 256
MOBA_TOPK = 3
RET_HEADS = 4
RET_HEAD_DIM = 128
RET_WIDTH = RET_HEADS * RET_HEAD_DIM
RET_CHUNK = 256
D_FF = 2816
CONV_WIDTH = 3
RMS_EPS = 1e-6
GN_EPS = 1e-5

BLK = 256
LANES = 128
SUBLANES = 8
FF_CHUNK = 256
N_FF_CHUNKS = D_FF // FF_CHUNK
ROW_TILE = 512
VMEM_LIMIT = 56 * 1024 * 1024
NEG = -1e30
LOG2E = math.log2(math.e)

QK_PAD = LANES
ALIBI_ROW = MOBA_HEAD_DIM
V_PAD = 80
ONES_ROW = MOBA_HEAD_DIM
GROUP = 16
GROUP_OWN = 8

_NT = (((1,), (1,)), ((), ()))


def _const_spec(shape):
    nd = len(shape)
    return pl.BlockSpec(shape, lambda *_: (0,) * nd, pipeline_mode=pl.Buffered(1))


def _rms(x, g):
    ms = jnp.mean(x * x, axis=-1, keepdims=True)
    return x * lax.rsqrt(ms + RMS_EPS) * g


def _in_proj_kernel(x_ref, g_ref, wqT_ref, wk_ref, kpos_ref, wvT_ref,
                    wrq_ref, wrkT_ref, wrv_ref, wrg_ref,
                    qT_ref, k_ref, vT_ref, sel_ref, rq_ref, rkT_ref, rv_ref, rg_ref, kmean_ref):
    t = pl.program_id(1)
    h = _rms(x_ref[...], g_ref[...]).astype(jnp.bfloat16)
    nblk = ROW_TILE // BLK
    nb = kmean_ref.shape[0]

    @pl.when(t == 0)
    def _():
        kmean_ref[...] = jnp.zeros_like(kmean_ref)

    def nt(w_ref):
        return lax.dot_general(w_ref[...], h, _NT, preferred_element_type=jnp.float32)

    def nn(w_ref):
        return jnp.dot(h, w_ref[...], preferred_element_type=jnp.float32)

    Dh = MOBA_HEAD_DIM
    qT = (nt(wqT_ref) * (Dh ** -0.5 * LOG2E)).astype(jnp.bfloat16)
    vT = nt(wvT_ref).astype(jnp.bfloat16)
    rkT = (nt(wrkT_ref) * (RET_HEAD_DIM ** -0.5)).astype(jnp.bfloat16)
    q_pad = (lax.broadcasted_iota(jnp.int32, (QK_PAD - Dh, BLK), 0) < 2).astype(jnp.bfloat16)
    v_pad = (lax.broadcasted_iota(jnp.int32, (V_PAD - Dh, BLK), 0) < 1).astype(jnp.bfloat16)
    for b in range(nblk):
        sl = slice(b * BLK, (b + 1) * BLK)
        for a in range(MOBA_HEADS):
            qT_ref[b, a * QK_PAD:a * QK_PAD + Dh, :] = qT[a * Dh:(a + 1) * Dh, sl]
            qT_ref[b, a * QK_PAD + Dh:(a + 1) * QK_PAD, :] = q_pad
            vT_ref[b, a * V_PAD:a * V_PAD + Dh, :] = vT[a * Dh:(a + 1) * Dh, sl]
            vT_ref[b, a * V_PAD + Dh:(a + 1) * V_PAD, :] = v_pad
        rkT_ref[b] = rkT[:, sl]
    k = nn(wk_ref)
    lane = lax.broadcasted_iota(jnp.int32, (BLK, LANES), 1)
    for b in range(nblk):
        kb = k[b * BLK:(b + 1) * BLK]
        kmean_ref[pl.ds(t * nblk + b, 1), :] = jnp.mean(kb, axis=0, keepdims=True)
        for a in range(MOBA_HEADS):
            pair = kb[:, (a // 2) * LANES:(a // 2 + 1) * LANES]
            if a % 2:
                pair = pltpu.roll(pair, Dh, axis=1)
            cols = slice(a * QK_PAD, (a + 1) * QK_PAD)
            k_ref[b * BLK:(b + 1) * BLK, cols] = (
                jnp.where(lane < Dh, pair, 0.0) + kpos_ref[:, cols]).astype(jnp.bfloat16)
    rq_ref[...] = nn(wrq_ref).astype(jnp.bfloat16)
    rv_ref[...] = nn(wrv_ref).astype(jnp.bfloat16)
    rg_ref[...] = nn(wrg_ref).astype(jnp.bfloat16)

    km = kmean_ref[...]
    km_hi = km.astype(jnp.bfloat16)
    km_lo = (km - km_hi.astype(jnp.float32)).astype(jnp.bfloat16)
    blk_id = lax.broadcasted_iota(jnp.int32, (nb, BLK), 0)
    none = jnp.full((SUBLANES - MOBA_TOPK, BLK), nb, jnp.int32)
    for b in range(nblk):
        i = t * nblk + b
        for a in range(MOBA_HEADS):
            dims = slice(a * Dh, (a + 1) * Dh)
            qTh = qT[a * Dh:(a + 1) * Dh, b * BLK:(b + 1) * BLK]
            gate = (jnp.dot(km_hi[:, dims], qTh, preferred_element_type=jnp.float32)
                    + jnp.dot(km_lo[:, dims], qTh, preferred_element_type=jnp.float32))
            gate = jnp.where(blk_id < i, gate, -jnp.inf)
            picks = []
            for _ in range(MOBA_TOPK):
                top = jnp.max(gate, axis=0, keepdims=True)
                first = jnp.min(jnp.where(gate == top, blk_id, nb), axis=0, keepdims=True)
                first = jnp.where(top > -jnp.inf, first, nb)
                picks.append(first)
                gate = jnp.where(blk_id == first, -jnp.inf, gate)
            sel_ref[a, b] = jnp.concatenate(picks + [none], axis=0)


def _in_proj(x, g, wqT, wk, kpos, wvT, wrq, wrkT, wrv, wrg):
    B, S, D = x.shape
    nb = S // BLK
    nblk = ROW_TILE // BLK
    grid = (B, S // ROW_TILE)
    bf = jnp.bfloat16
    QW, VW = MOBA_HEADS * QK_PAD, MOBA_HEADS * V_PAD
    nat = lambda w: pl.BlockSpec((None, ROW_TILE, w), lambda b, t: (b, t, 0))
    tr = lambda w: pl.BlockSpec((None, nblk, w, BLK), lambda b, t: (b, t, 0, 0))
    out_shape = (
        jax.ShapeDtypeStruct((B, nb, QW, BLK), bf),
        jax.ShapeDtypeStruct((B, S, QW), bf),
        jax.ShapeDtypeStruct((B, nb, VW, BLK), bf),
        jax.ShapeDtypeStruct((B, MOBA_HEADS, nb, SUBLANES, BLK), jnp.int32),
        jax.ShapeDtypeStruct((B, S, RET_WIDTH), bf),
        jax.ShapeDtypeStruct((B, nb, RET_WIDTH, BLK), bf),
        jax.ShapeDtypeStruct((B, S, RET_WIDTH), bf),
        jax.ShapeDtypeStruct((B, S, RET_WIDTH), bf),
    )
    out_specs = (
        tr(QW), nat(QW), tr(VW),
        pl.BlockSpec((None, MOBA_HEADS, nblk, SUBLANES, BLK), lambda b, t: (b, 0, t, 0, 0)),
        nat(RET_WIDTH), tr(RET_WIDTH), nat(RET_WIDTH), nat(RET_WIDTH),
    )
    consts = (g, wqT, wk, kpos, wvT, wrq, wrkT, wrv, wrg)
    in_specs = [pl.BlockSpec((None, ROW_TILE, D), lambda b, t: (b, t, 0))] + [
        _const_spec(a.shape) for a in consts]
    return pl.pallas_call(
        _in_proj_kernel, grid=grid, in_specs=in_specs, out_specs=out_specs, out_shape=out_shape,
        scratch_shapes=[pltpu.VMEM((nb, MOBA_WIDTH), jnp.float32)],
        compiler_params=pltpu.CompilerParams(
            dimension_semantics=("arbitrary", "arbitrary"), vmem_limit_bytes=VMEM_LIMIT),
        name="in_proj",
    )(x, *consts)


def _moba_kernel(slopes_ref, ti_ref, tj_ref, qT_ref, k_ref, vT_ref, sel_ref, o_ref,
                 m_ref, acc_ref, s_ref, smax_ref, alpha_ref, p_ref,
                 s_own_ref, smax_own_ref, alpha_own_ref, p_own_ref, *, n_past_groups):
    h = pl.program_id(1)
    nb = k_ref.shape[0]
    slope = slopes_ref[h]
    kpos = lax.broadcasted_iota(jnp.int32, (BLK, BLK), 0)
    qpos = lax.broadcasted_iota(jnp.int32, (BLK, BLK), 1)
    qterm = -slope * lax.broadcasted_iota(jnp.int32, (1, BLK), 1).astype(jnp.float32)

    m_ref[...] = jnp.full(m_ref.shape, -3e38, jnp.float32)
    acc_ref[...] = jnp.zeros(acc_ref.shape, jnp.float32)

    def own_tile(g, e):
        t = g * GROUP_OWN + e
        return t, t

    def past_tile(g, e):
        t = g * GROUP + e
        return ti_ref[t], tj_ref[t]

    def run_stream(n_groups, tile_of, own, bufs):
        s_ref, smax_ref, alpha_ref, p_ref = bufs
        GROUP = s_ref.shape[1]
        def scores(par, e, i_t, j_t):
            s = jnp.dot(k_ref[j_t], qT_ref[i_t], preferred_element_type=jnp.float32)
            if own:
                s = jnp.where(kpos <= qpos, s, NEG)
            s_ref[par, e] = s
            smax_ref[par, e:e + 1, :] = jnp.max(s, axis=0, keepdims=True)

        def values(par, e, i_t, j_t):
            pv = jnp.dot(vT_ref[j_t], p_ref[par, e * BLK:(e + 1) * BLK, :],
                         preferred_element_type=jnp.float32)
            acc_ref[i_t] = alpha_ref[par, e:e + 1, :] * acc_ref[i_t] + pv

        def softmax(par, e, i_t, j_t):
            if own:
                c = qterm
            else:
                sel = sel_ref[i_t]
                hit = (sel[0:1] == j_t) | (sel[1:2] == j_t) | (sel[2:3] == j_t)
                blk_dist = ((i_t - j_t) * MOBA_BLOCK).astype(jnp.float32)
                c = jnp.where(hit, -slope * blk_dist, NEG) + qterm
            m_old = m_ref[pl.ds(i_t, 1), :]
            m_new = jnp.maximum(m_old, smax_ref[par, e:e + 1, :] + c)
            alpha_ref[par, e:e + 1, :] = jnp.exp2(m_old - m_new)
            m_ref[pl.ds(i_t, 1), :] = m_new
            p_ref[par, e * BLK:(e + 1) * BLK, :] = (
                jnp.exp2(s_ref[par, e] - (m_new - c)).astype(jnp.bfloat16))

        def trip(g, par):
            g_next = jnp.minimum(g + 1, n_groups - 1)
            g_prev = jnp.maximum(g - 1, 0)
            for e in range(-2, GROUP):
                if e + 2 < GROUP:
                    scores(1 - par, e + 2, *tile_of(g_next, e + 2))
                if e >= 0:
                    values(1 - par, e, *tile_of(g_prev, e))
                    softmax(par, e, *tile_of(g, e))

        alpha_ref[1] = jnp.ones(alpha_ref.shape[1:], jnp.float32)
        p_ref[1] = jnp.zeros(p_ref.shape[1:], jnp.bfloat16)
        for e in range(GROUP):
            scores(0, e, *tile_of(0, e))

        def body(w, carry):
            trip(2 * w, 0)
            trip(2 * w + 1, 1)
            return carry

        assert n_groups % 2 == 0
        lax.fori_loop(0, n_groups // 2, body, 0)
        for e in range(GROUP):
            values((n_groups - 1) % 2, e, *tile_of(n_groups - 1, e))

    run_stream(nb // GROUP_OWN, own_tile, True, (s_own_ref, smax_own_ref, alpha_own_ref, p_own_ref))
    run_stream(n_past_groups, past_tile, False, (s_ref, smax_ref, alpha_ref, p_ref))

    def write_out(half):
        unroll = 4
        def some(u, carry):
            for d in range(unroll):
                i = u * unroll + d
                acc = acc_ref[i]
                oT = acc[0:MOBA_HEAD_DIM] / acc[ONES_ROW:ONES_ROW + 1]
                pad = jnp.zeros_like(oT)
                o2 = jnp.concatenate([pad, oT] if half else [oT, pad], axis=0).T
                rows = pl.ds(pl.multiple_of(i * BLK, BLK), BLK)
                lanes = slice(half * MOBA_HEAD_DIM, (half + 1) * MOBA_HEAD_DIM)
                o_ref[rows, lanes] = o2[:, lanes].astype(o_ref.dtype)
            return carry
        lax.fori_loop(0, nb // unroll, some, 0)

    lax.cond(h % 2 == 0, lambda: write_out(0), lambda: write_out(1))


def _moba(slopes_l2, qT, k, vT, sel):
    B, nb, QW, _ = qT.shape
    S = nb * BLK
    assert nb % GROUP == 0
    k4 = k.reshape(B, nb, BLK, QW)
    pairs = [(i, j) for j in range(nb - 1) for i in range(j + 1, nb)]
    pairs += [(0, 0)] * (-len(pairs) % GROUP)
    ti = jnp.asarray([p[0] for p in pairs], jnp.int32)
    tj = jnp.asarray([p[1] for p in pairs], jnp.int32)
    grid_spec = pltpu.PrefetchScalarGridSpec(
        num_scalar_prefetch=3,
        grid=(B, MOBA_HEADS),
        in_specs=[
            pl.BlockSpec((None, nb, QK_PAD, BLK), lambda b, h, *_: (b, 0, h, 0)),
            pl.BlockSpec((None, nb, BLK, QK_PAD), lambda b, h, *_: (b, 0, 0, h)),
            pl.BlockSpec((None, nb, V_PAD, BLK), lambda b, h, *_: (b, 0, h, 0)),
            pl.BlockSpec((None, None, nb, SUBLANES, BLK), lambda b, h, *_: (b, h, 0, 0, 0)),
        ],
        out_specs=pl.BlockSpec((None, S, 2 * MOBA_HEAD_DIM), lambda b, h, *_: (b, 0, h // 2)),
        scratch_shapes=[
            pltpu.VMEM((nb, BLK), jnp.float32),
            pltpu.VMEM((nb, V_PAD, BLK), jnp.float32),
            pltpu.VMEM((2, GROUP, BLK, BLK), jnp.float32),
            pltpu.VMEM((2, GROUP, BLK), jnp.float32),
            pltpu.VMEM((2, GROUP, BLK), jnp.float32),
            pltpu.VMEM((2, GROUP * BLK, BLK), jnp.bfloat16),
            pltpu.VMEM((2, GROUP_OWN, BLK, BLK), jnp.float32),
            pltpu.VMEM((2, GROUP_OWN, BLK), jnp.float32),
            pltpu.VMEM((2, GROUP_OWN, BLK), jnp.float32),
            pltpu.VMEM((2, GROUP_OWN * BLK, BLK), jnp.bfloat16),
        ],
    )
    return pl.pallas_call(
        functools.partial(_moba_kernel, n_past_groups=len(pairs) // GROUP),
        grid_spec=grid_spec,
        out_shape=jax.ShapeDtypeStruct((B, S, MOBA_WIDTH), jnp.bfloat16),
        compiler_params=pltpu.CompilerParams(
            dimension_semantics=("arbitrary", "arbitrary"), vmem_limit_bytes=VMEM_LIMIT),
        name="moba",
    )(slopes_l2, ti, tj, qT, k4, vT, sel)


def _ret_kernel(cd_ref, q_ref, kT_ref, v_ref, g_ref, dintra_ref, qdec_ref, kdec_ref, gn_ref,
                o_ref, state_ref):
    c = pl.program_id(0)

    @pl.when(c == 0)
    def _():
        state_ref[...] = jnp.zeros_like(state_ref)

    for b in range(q_ref.shape[0]):
        for h in range(RET_HEADS):
            cols = slice(h * RET_HEAD_DIM, (h + 1) * RET_HEAD_DIM)
            q = q_ref[b, :, cols]
            kT = kT_ref[b, cols, :]
            v = v_ref[b, :, cols]
            state = state_ref[b, h]
            s = jnp.dot(q, kT, preferred_element_type=jnp.float32) * dintra_ref[h]
            inner = jnp.dot(s.astype(jnp.bfloat16), v, preferred_element_type=jnp.float32)
            qd = (q.astype(jnp.float32) * qdec_ref[h]).astype(jnp.bfloat16)
            cross = jnp.dot(qd, state.astype(jnp.bfloat16), preferred_element_type=jnp.float32)
            kd = (kT.astype(jnp.float32) * kdec_ref[h]).astype(jnp.bfloat16)
            state_ref[b, h] = cd_ref[h] * state + jnp.dot(kd, v, preferred_element_type=jnp.float32)
            o = inner + cross
            mu = jnp.mean(o, axis=-1, keepdims=True)
            d = o - mu
            var = jnp.mean(d * d, axis=-1, keepdims=True)
            gate = g_ref[b, :, cols].astype(jnp.float32)
            y = d * lax.rsqrt(var + GN_EPS) * gn_ref[:, cols] * (gate * jax.nn.sigmoid(gate))
            o_ref[b, :, cols] = y.astype(o_ref.dtype)


def _retention(rq, rkT, rv, rg, ret_norm):
    B, S, W = rq.shape
    nc = S // BLK
    H = RET_HEADS
    C = RET_CHUNK
    f32 = jnp.float32
    gamma = 1.0 - jnp.exp2(-5.0 - jnp.arange(H, dtype=f32))
    log_g = jnp.log(gamma)
    idx = jnp.arange(C, dtype=f32)
    diff = idx[:, None] - idx[None, :]
    dintra = jnp.where(diff >= 0, jnp.exp(log_g[:, None, None] * jnp.maximum(diff, 0.0)), 0.0)
    qdec = jnp.broadcast_to(jnp.exp(log_g[:, None] * (idx + 1.0))[..., None], (H, C, LANES))
    kdec = jnp.exp(log_g[:, None] * (C - 1.0 - idx))[:, None, :]
    cdec = jnp.exp(log_g * C)
    gn = ret_norm.reshape(1, W).astype(f32)

    nat = pl.BlockSpec((B, C, W), lambda c, s: (0, c, 0))
    const = lambda a: pl.BlockSpec(a.shape, lambda c, s: (0,) * a.ndim,
                                   pipeline_mode=pl.Buffered(1))
    grid_spec = pltpu.PrefetchScalarGridSpec(
        num_scalar_prefetch=1,
        grid=(nc,),
        in_specs=[
            nat,
            pl.BlockSpec((B, None, W, C), lambda c, s: (0, c, 0, 0)),
            nat, nat,
            const(dintra), const(qdec), const(kdec), const(gn),
        ],
        out_specs=nat,
        scratch_shapes=[pltpu.VMEM((B, H, RET_HEAD_DIM, RET_HEAD_DIM), f32)],
    )
    return pl.pallas_call(
        _ret_kernel, grid_spec=grid_spec,
        out_shape=jax.ShapeDtypeStruct((B, S, W), jnp.bfloat16),
        compiler_params=pltpu.CompilerParams(
            dimension_semantics=("arbitrary",),
            vmem_limit_bytes=VMEM_LIMIT),
        name="retention",
    )(cdec, rq, rkT, rv, rg, dintra, qdec, kdec, gn)


def _gelu_tanh(u):
    k = -2.0 * math.sqrt(2.0 / math.pi) * LOG2E
    return u / (1.0 + jnp.exp2(u * (k + (k * 0.044715) * (u * u))))


def _shift_rows(cur, prev_tail, n):
    rolled = pltpu.roll(cur, n, axis=0)
    prev_rolled = pltpu.roll(prev_tail, n, axis=0)
    row = lax.broadcasted_iota(jnp.int32, prev_tail.shape, 0)
    head = jnp.where(row < n, prev_rolled, rolled[0:SUBLANES])
    return jnp.concatenate([head, rolled[SUBLANES:]], axis=0)


def _tail_kernel(x_ref, moba_ref, ret_ref, p_ref, wo_ref, ffn_g_ref, wup_ref, wgate_ref,
                 convw_ref, convb_ref, wdown_ref, ple_g_ref, wple_ref, wpg_ref, fin_g_ref,
                 o_ref, y_ref, carry_ref, *, tiles_per_seq):
    t = pl.program_id(0)
    R = ROW_TILE

    @pl.when(t % tiles_per_seq == 0)
    def _():
        carry_ref[...] = jnp.zeros_like(carry_ref)

    mix = jnp.concatenate([moba_ref[...], ret_ref[...]], axis=1)
    x1 = x_ref[...] + jnp.dot(mix, wo_ref[...], preferred_element_type=jnp.float32)
    h = _rms(x1, ffn_g_ref[...]).astype(jnp.bfloat16)

    for c in range(N_FF_CHUNKS):
        cols = slice(c * FF_CHUNK, (c + 1) * FF_CHUNK)
        up = jnp.dot(h, wup_ref[:, cols], preferred_element_type=jnp.float32)
        gt = jnp.dot(h, wgate_ref[:, cols], preferred_element_type=jnp.float32)
        prev_tail = carry_ref[c]
        carry_ref[c] = up[R - SUBLANES:R, :]
        w = convw_ref[:, cols]
        u = (convb_ref[:, cols]
             + w[0:1, :] * _shift_rows(up, prev_tail, 2)
             + w[1:2, :] * _shift_rows(up, prev_tail, 1)
             + w[2:3, :] * up)
        y_ref[:, cols] = (_gelu_tanh(u) * gt).astype(jnp.bfloat16)
    x2 = x1 + jnp.dot(y_ref[...], wdown_ref[...], preferred_element_type=jnp.float32)

    hn = _rms(x2, ple_g_ref[...]).astype(jnp.bfloat16)
    g = jax.nn.sigmoid(jnp.dot(hn, wpg_ref[...], preferred_element_type=jnp.float32))
    pe = jnp.dot(p_ref[...].astype(jnp.bfloat16), wple_ref[...], preferred_element_type=jnp.float32)
    x3 = x2 + pe * g
    o_ref[...] = _rms(x3, fin_g_ref[...])


def _tail(x2d, moba2d, ret2d, p2d, wo, ffn_g, wup, wgate, convw, convb, wdown, ple_g, wple,
          wpg, fin_g, seq_len):
    T, D = x2d.shape
    R = ROW_TILE
    row = lambda w: pl.BlockSpec((R, w), lambda t: (t, 0))
    consts = (wo, ffn_g, wup, wgate, convw, convb, wdown, ple_g, wple, wpg, fin_g)
    in_specs = [row(D), row(MOBA_WIDTH), row(RET_WIDTH), row(PLE_DIM)] + [
        _const_spec(a.shape) for a in consts]
    return pl.pallas_call(
        functools.partial(_tail_kernel, tiles_per_seq=seq_len // R),
        grid=(T // R,), in_specs=in_specs, out_specs=row(D),
        out_shape=jax.ShapeDtypeStruct((T, D), jnp.float32),
        scratch_shapes=[
            pltpu.VMEM((R, D_FF), jnp.bfloat16),
            pltpu.VMEM((N_FF_CHUNKS, SUBLANES, FF_CHUNK), jnp.float32),
        ],
        compiler_params=pltpu.CompilerParams(
            dimension_semantics=("arbitrary",), vmem_limit_bytes=VMEM_LIMIT),
        name="tail",
    )(x2d, moba2d, ret2d, p2d, *consts)


def kernel(x, p, attn_norm, w_in, ret_norm, w_out, ffn_norm, w_up, w_gate, conv_w, conv_b, w_down,
           ple_norm, w_ple, w_ple_gate, final_norm):
    B, S, D = x.shape
    assert D == D_MODEL and S % ROW_TILE == 0 and w_in.shape[0] == 1
    bf, f32 = jnp.bfloat16, jnp.float32
    MW, RW = MOBA_WIDTH, RET_WIDTH
    w = w_in[0].astype(bf)
    wqT = w[:, 0:MW].T
    wk = w[:, MW:2 * MW]
    wvT = w[:, 2 * MW:3 * MW].T
    o = 3 * MW
    wrq = w[:, o:o + RW]
    wrkT = w[:, o + RW:o + 2 * RW].T
    wrv = w[:, o + 2 * RW:o + 3 * RW]
    wrg = w[:, o + 3 * RW:o + 4 * RW]
    row = lambda a: a.reshape(1, -1).astype(f32)

    slopes_l2 = jnp.exp2(-8.0 * jnp.arange(1, MOBA_HEADS + 1, dtype=f32) / MOBA_HEADS) * LOG2E
    kterm = slopes_l2[None, :] * jnp.arange(BLK, dtype=f32)[:, None]
    kterm_hi = lax.bitcast_convert_type(
        lax.bitcast_convert_type(kterm, jnp.uint32) & jnp.uint32(0xFFFF0000), f32)
    kterm_lo = kterm - kterm_hi
    kpos = jnp.zeros((BLK, MOBA_HEADS, QK_PAD), f32)
    kpos = kpos.at[:, :, ALIBI_ROW].set(kterm_hi).at[:, :, ALIBI_ROW + 1].set(kterm_lo)
    kpos = kpos.reshape(BLK, MOBA_HEADS * QK_PAD)

    qT, k, vT, sel, rq, rkT, rv, rg = _in_proj(
        x, row(attn_norm[0]), wqT, wk, kpos, wvT, wrq, wrkT, wrv, wrg)

    moba_out = _moba(slopes_l2, qT, k, vT, sel)
    ret_out = _retention(rq, rkT, rv, rg, ret_norm[0])

    wo = w_out[0].astype(bf)
    out = _tail(
        x.reshape(B * S, D), moba_out.reshape(B * S, MW), ret_out.reshape(B * S, RW),
        p[0].reshape(B * S, PLE_DIM),
        wo, row(ffn_norm[0]),
        w_up[0].astype(bf), w_gate[0].astype(bf), conv_w[0].astype(f32), row(conv_b[0]),
        w_down[0].astype(bf),
        row(ple_norm[0]), w_ple[0].astype(bf), w_ple_gate[0].astype(bf), row(final_norm),
        seq_len=S)
    return out.reshape(B, S, D)
```

```python
import functools
import math

import jax
import jax.numpy as jnp
from jax import lax
from jax.experimental import pallas as pl
from jax.experimental.pallas import tpu as pltpu

D_MODEL = 1024
PLE_DIM = 256
MOBA_HEADS = 8
MOBA_HEAD_DIM = 64
MOBA_WIDTH = MOBA_HEADS * MOBA_HEAD_DIM
MOBA_BLOCK = 256
MOBA_TOPK = 3
RET_HEADS = 4
RET_HEAD_DIM = 128
RET_WIDTH = RET_HEADS * RET_HEAD_DIM
RET_CHUNK = 256
D_FF = 2816
CONV_WIDTH = 3
RMS_EPS = 1e-6
GN_EPS = 1e-5

BLK = 256
LANES = 128
SUBLANES = 8
FF_CHUNK = 256
N_FF_CHUNKS = D_FF // FF_CHUNK
ROW_TILE = 512
VMEM_LIMIT = 56 * 1024 * 1024
NEG = -1e30
LOG2E = math.log2(math.e)

QK_PAD = LANES
ALIBI_ROW = MOBA_HEAD_DIM
V_PAD = 80
ONES_ROW = MOBA_HEAD_DIM
GROUP = 16
GROUP_OWN = 8

_NT = (((1,), (1,)), ((), ()))


def _const_spec(shape):
    nd = len(shape)
    return pl.BlockSpec(shape, lambda *_: (0,) * nd, pipeline_mode=pl.Buffered(1))


def _rms(x, g):
    ms = jnp.mean(x * x, axis=-1, keepdims=True)
    return x * lax.rsqrt(ms + RMS_EPS) * g


def _in_proj_kernel(x_ref, g_ref, wqT_ref, wk_ref, kpos_ref, wvT_ref,
                    wrq_ref, wrkT_ref, wrv_ref, wrg_ref,
                    qT_ref, k_ref, vT_ref, sel_ref, rq_ref, rkT_ref, rv_ref, rg_ref, kmean_ref):
    t = pl.program_id(1)
    h = _rms(x_ref[...], g_ref[...]).astype(jnp.bfloat16)
    nblk = ROW_TILE // BLK
    nb = kmean_ref.shape[0]

    @pl.when(t == 0)
    def _():
        kmean_ref[...] = jnp.zeros_like(kmean_ref)

    def nt(w_ref):
        return lax.dot_general(w_ref[...], h, _NT, preferred_element_type=jnp.float32)

    def nn(w_ref):
        return jnp.dot(h, w_ref[...], preferred_element_type=jnp.float32)

    Dh = MOBA_HEAD_DIM
    qT = (nt(wqT_ref) * (Dh ** -0.5 * LOG2E)).astype(jnp.bfloat16)
    vT = nt(wvT_ref).astype(jnp.bfloat16)
    rkT = (nt(wrkT_ref) * (RET_HEAD_DIM ** -0.5)).astype(jnp.bfloat16)
    q_pad = (lax.broadcasted_iota(jnp.int32, (QK_PAD - Dh, BLK), 0) < 2).astype(jnp.bfloat16)
    v_pad = (lax.broadcasted_iota(jnp.int32, (V_PAD - Dh, BLK), 0) < 1).astype(jnp.bfloat16)
    for b in range(nblk):
        sl = slice(b * BLK, (b + 1) * BLK)
        for a in range(MOBA_HEADS):
            qT_ref[b, a * QK_PAD:a * QK_PAD + Dh, :] = qT[a * Dh:(a + 1) * Dh, sl]
            qT_ref[b, a * QK_PAD + Dh:(a + 1) * QK_PAD, :] = q_pad
            vT_ref[b, a * V_PAD:a * V_PAD + Dh, :] = vT[a * Dh:(a + 1) * Dh, sl]
            vT_ref[b, a * V_PAD + Dh:(a + 1) * V_PAD, :] = v_pad
        rkT_ref[b] = rkT[:, sl]
    k = nn(wk_ref)
    lane = lax.broadcasted_iota(jnp.int32, (BLK, LANES), 1)
    for b in range(nblk):
        kb = k[b * BLK:(b + 1) * BLK]
        kmean_ref[pl.ds(t * nblk + b, 1), :] = jnp.mean(kb, axis=0, keepdims=True)
        for a in range(MOBA_HEADS):
            pair = kb[:, (a // 2) * LANES:(a // 2 + 1) * LANES]
            if a % 2:
                pair = pltpu.roll(pair, Dh, axis=1)
            cols = slice(a * QK_PAD, (a + 1) * QK_PAD)
            k_ref[b * BLK:(b + 1) * BLK, cols] = (
                jnp.where(lane < Dh, pair, 0.0) + kpos_ref[:, cols]).astype(jnp.bfloat16)
    rq_ref[...] = nn(wrq_ref).astype(jnp.bfloat16)
    rv_ref[...] = nn(wrv_ref).astype(jnp.bfloat16)
    rg_ref[...] = nn(wrg_ref).astype(jnp.bfloat16)

    none = jnp.full((SUBLANES - MOBA_TOPK, BLK), nb, jnp.int32)

    def route(rows):
        km = kmean_ref[0:rows, :]
        km_hi = km.astype(jnp.bfloat16)
        km_lo = (km - km_hi.astype(jnp.float32)).astype(jnp.bfloat16)
        blk_id = lax.broadcasted_iota(jnp.int32, (rows, BLK), 0)
        for b in range(nblk):
            i = t * nblk + b
            for a in range(MOBA_HEADS):
                dims = slice(a * Dh, (a + 1) * Dh)
                qTh = qT[a * Dh:(a + 1) * Dh, b * BLK:(b + 1) * BLK]
                gate = (jnp.dot(km_hi[:, dims], qTh, preferred_element_type=jnp.float32)
                        + jnp.dot(km_lo[:, dims], qTh, preferred_element_type=jnp.float32))
                gate = jnp.where(blk_id < i, gate, -jnp.inf)
                picks = []
                for _ in range(MOBA_TOPK):
                    top = jnp.max(gate, axis=0, keepdims=True)
                    first = jnp.min(jnp.where(gate == top, blk_id, nb), axis=0, keepdims=True)
                    first = jnp.where(top > -jnp.inf, first, nb)
                    picks.append(first)
                    gate = jnp.where(blk_id == first, -jnp.inf, gate)
                sel_ref[a, b] = jnp.concatenate(picks + [none], axis=0)

    quarter = nb // 4
    last_block = t * nblk + nblk - 1
    lax.cond(last_block < 2 * quarter,
             lambda: lax.cond(last_block < quarter, lambda: route(quarter), lambda: route(2 * quarter)),
             lambda: lax.cond(last_block < 3 * quarter, lambda: route(3 * quarter), lambda: route(nb)))


def _in_proj(x, g, wqT, wk, kpos, wvT, wrq, wrkT, wrv, wrg):
    B, S, D = x.shape
    nb = S // BLK
    nblk = ROW_TILE // BLK
    grid = (B, S // ROW_TILE)
    bf = jnp.bfloat16
    QW, VW = MOBA_HEADS * QK_PAD, MOBA_HEADS * V_PAD
    nat = lambda w: pl.BlockSpec((None, ROW_TILE, w), lambda b, t: (b, t, 0))
    tr = lambda w: pl.BlockSpec((None, nblk, w, BLK), lambda b, t: (b, t, 0, 0))
    out_shape = (
        jax.ShapeDtypeStruct((B, nb, QW, BLK), bf),
        jax.ShapeDtypeStruct((B, S, QW), bf),
        jax.ShapeDtypeStruct((B, nb, VW, BLK), bf),
        jax.ShapeDtypeStruct((B, MOBA_HEADS, nb, SUBLANES, BLK), jnp.int32),
        jax.ShapeDtypeStruct((B, S, RET_WIDTH), bf),
        jax.ShapeDtypeStruct((B, nb, RET_WIDTH, BLK), bf),
        jax.ShapeDtypeStruct((B, S, RET_WIDTH), bf),
        jax.ShapeDtypeStruct((B, S, RET_WIDTH), bf),
    )
    out_specs = (
        tr(QW), nat(QW), tr(VW),
        pl.BlockSpec((None, MOBA_HEADS, nblk, SUBLANES, BLK), lambda b, t: (b, 0, t, 0, 0)),
        nat(RET_WIDTH), tr(RET_WIDTH), nat(RET_WIDTH), nat(RET_WIDTH),
    )
    consts = (g, wqT, wk, kpos, wvT, wrq, wrkT, wrv, wrg)
    in_specs = [pl.BlockSpec((None, ROW_TILE, D), lambda b, t: (b, t, 0))] + [
        _const_spec(a.shape) for a in consts]
    return pl.pallas_call(
        _in_proj_kernel, grid=grid, in_specs=in_specs, out_specs=out_specs, out_shape=out_shape,
        scratch_shapes=[pltpu.VMEM((nb, MOBA_WIDTH), jnp.float32)],
        compiler_params=pltpu.CompilerParams(
            dimension_semantics=("arbitrary", "arbitrary"), vmem_limit_bytes=VMEM_LIMIT),
        name="in_proj",
    )(x, *consts)


def _moba_kernel(slopes_ref, ti_ref, tj_ref, qT_ref, k_ref, vT_ref, sel_ref, o_ref,
                 m_ref, acc_ref, s_ref, smax_ref, alpha_ref, p_ref,
                 s_own_ref, smax_own_ref, alpha_own_ref, p_own_ref, *, n_past_groups):
    h = pl.program_id(1)
    nb = k_ref.shape[0]
    slope = slopes_ref[h]
    kpos = lax.broadcasted_iota(jnp.int32, (BLK, BLK), 0)
    qpos = lax.broadcasted_iota(jnp.int32, (BLK, BLK), 1)
    qterm = -slope * lax.broadcasted_iota(jnp.int32, (1, BLK), 1).astype(jnp.float32)

    m_ref[...] = jnp.full(m_ref.shape, -3e38, jnp.float32)
    acc_ref[...] = jnp.zeros(acc_ref.shape, jnp.float32)

    def own_tile(g, e):
        t = g * GROUP_OWN + e
        return t, t

    def past_tile(g, e):
        t = g * GROUP + e
        return ti_ref[t], tj_ref[t]

    def run_stream(n_groups, tile_of, own, bufs):
        s_ref, smax_ref, alpha_ref, p_ref = bufs
        GROUP = s_ref.shape[1]
        def scores(par, e, i_t, j_t):
            s = jnp.dot(k_ref[j_t], qT_ref[i_t], preferred_element_type=jnp.float32)
            if own:
                s = jnp.where(kpos <= qpos, s, NEG)
            s_ref[par, e] = s
            smax_ref[par, e:e + 1, :] = jnp.max(s, axis=0, keepdims=True)

        def values(par, e, i_t, j_t):
            pv = jnp.dot(vT_ref[j_t], p_ref[par, e * BLK:(e + 1) * BLK, :],
                         preferred_element_type=jnp.float32)
            acc_ref[i_t] = alpha_ref[par, e:e + 1, :] * acc_ref[i_t] + pv

        def softmax(par, e, i_t, j_t):
            if own:
                c = qterm
            else:
                sel = sel_ref[i_t]
                hit = (sel[0:1] == j_t) | (sel[1:2] == j_t) | (sel[2:3] == j_t)
                blk_dist = ((i_t - j_t) * MOBA_BLOCK).astype(jnp.float32)
                c = jnp.where(hit, -slope * blk_dist, NEG) + qterm
            m_old = m_ref[pl.ds(i_t, 1), :]
            m_new = jnp.maximum(m_old, smax_ref[par, e:e + 1, :] + c)
            alpha_ref[par, e:e + 1, :] = jnp.exp2(m_old - m_new)
            m_ref[pl.ds(i_t, 1), :] = m_new
            p_ref[par, e * BLK:(e + 1) * BLK, :] = (
                jnp.exp2(s_ref[par, e] - (m_new - c)).astype(jnp.bfloat16))

        def trip(g, par):
            g_next = jnp.minimum(g + 1, n_groups - 1)
            g_prev = jnp.maximum(g - 1, 0)
            for e in range(-2, GROUP):
                if e + 2 < GROUP:
                    scores(1 - par, e + 2, *tile_of(g_next, e + 2))
                if e >= 0:
                    values(1 - par, e, *tile_of(g_prev, e))
                    softmax(par, e, *tile_of(g, e))

        alpha_ref[1] = jnp.ones(alpha_ref.shape[1:], jnp.float32)
        p_ref[1] = jnp.zeros(p_ref.shape[1:], jnp.bfloat16)
        for e in range(GROUP):
            scores(0, e, *tile_of(0, e))

        def body(w, carry):
            trip(2 * w, 0)
            trip(2 * w + 1, 1)
            return carry

        assert n_groups % 2 == 0
        lax.fori_loop(0, n_groups // 2, body, 0)
        for e in range(GROUP):
            values((n_groups - 1) % 2, e, *tile_of(n_groups - 1, e))

    run_stream(nb // GROUP_OWN, own_tile, True, (s_own_ref, smax_own_ref, alpha_own_ref, p_own_ref))
    run_stream(n_past_groups, past_tile, False, (s_ref, smax_ref, alpha_ref, p_ref))

    def write_out(half):
        unroll = 4
        def some(u, carry):
            for d in range(unroll):
                i = u * unroll + d
                acc = acc_ref[i]
                oT = acc[0:MOBA_HEAD_DIM] / acc[ONES_ROW:ONES_ROW + 1]
                pad = jnp.zeros_like(oT)
                o2 = jnp.concatenate([pad, oT] if half else [oT, pad], axis=0).T
                rows = pl.ds(pl.multiple_of(i * BLK, BLK), BLK)
                lanes = slice(half * MOBA_HEAD_DIM, (half + 1) * MOBA_HEAD_DIM)
                o_ref[rows, lanes] = o2[:, lanes].astype(o_ref.dtype)
            return carry
        lax.fori_loop(0, nb // unroll, some, 0)

    lax.cond(h % 2 == 0, lambda: write_out(0), lambda: write_out(1))


def _moba(slopes_l2, qT, k, vT, sel):
    B, nb, QW, _ = qT.shape
    S = nb * BLK
    assert nb % GROUP == 0
    k4 = k.reshape(B, nb, BLK, QW)
    pairs = [(i, j) for j in range(nb - 1) for i in range(j + 1, nb)]
    pairs += [(0, 0)] * (-len(pairs) % GROUP)
    ti = jnp.asarray([p[0] for p in pairs], jnp.int32)
    tj = jnp.asarray([p[1] for p in pairs], jnp.int32)
    grid_spec = pltpu.PrefetchScalarGridSpec(
        num_scalar_prefetch=3,
        grid=(B, MOBA_HEADS),
        in_specs=[
            pl.BlockSpec((None, nb, QK_PAD, BLK), lambda b, h, *_: (b, 0, h, 0)),
            pl.BlockSpec((None, nb, BLK, QK_PAD), lambda b, h, *_: (b, 0, 0, h)),
            pl.BlockSpec((None, nb, V_PAD, BLK), lambda b, h, *_: (b, 0, h, 0)),
            pl.BlockSpec((None, None, nb, SUBLANES, BLK), lambda b, h, *_: (b, h, 0, 0, 0)),
        ],
        out_specs=pl.BlockSpec((None, S, 2 * MOBA_HEAD_DIM), lambda b, h, *_: (b, 0, h // 2)),
        scratch_shapes=[
            pltpu.VMEM((nb, BLK), jnp.float32),
            pltpu.VMEM((nb, V_PAD, BLK), jnp.float32),
            pltpu.VMEM((2, GROUP, BLK, BLK), jnp.float32),
            pltpu.VMEM((2, GROUP, BLK), jnp.float32),
            pltpu.VMEM((2, GROUP, BLK), jnp.float32),
            pltpu.VMEM((2, GROUP * BLK, BLK), jnp.bfloat16),
            pltpu.VMEM((2, GROUP_OWN, BLK, BLK), jnp.float32),
            pltpu.VMEM((2, GROUP_OWN, BLK), jnp.float32),
            pltpu.VMEM((2, GROUP_OWN, BLK), jnp.float32),
            pltpu.VMEM((2, GROUP_OWN * BLK, BLK), jnp.bfloat16),
        ],
    )
    return pl.pallas_call(
        functools.partial(_moba_kernel, n_past_groups=len(pairs) // GROUP),
        grid_spec=grid_spec,
        out_shape=jax.ShapeDtypeStruct((B, S, MOBA_WIDTH), jnp.bfloat16),
        compiler_params=pltpu.CompilerParams(
            dimension_semantics=("arbitrary", "arbitrary"), vmem_limit_bytes=VMEM_LIMIT),
        name="moba",
    )(slopes_l2, ti, tj, qT, k4, vT, sel)


def _ret_kernel(cd_ref, q_ref, kT_ref, v_ref, g_ref, dintra_ref, qdec_ref, kdec_ref, gn_ref,
                o_ref, state_ref):
    c = pl.program_id(0)

    @pl.when(c == 0)
    def _():
        state_ref[...] = jnp.zeros_like(state_ref)

    for b in range(q_ref.shape[0]):
        for h in range(RET_HEADS):
            cols = slice(h * RET_HEAD_DIM, (h + 1) * RET_HEAD_DIM)
            q = q_ref[b, :, cols]
            kT = kT_ref[b, cols, :]
            v = v_ref[b, :, cols]
            state = state_ref[b, h]
            s = jnp.dot(q, kT, preferred_element_type=jnp.float32) * dintra_ref[h]
            inner = jnp.dot(s.astype(jnp.bfloat16), v, preferred_element_type=jnp.float32)
            qd = (q.astype(jnp.float32) * qdec_ref[h]).astype(jnp.bfloat16)
            cross = jnp.dot(qd, state.astype(jnp.bfloat16), preferred_element_type=jnp.float32)
            kd = (kT.astype(jnp.float32) * kdec_ref[h]).astype(jnp.bfloat16)
            state_ref[b, h] = cd_ref[h] * state + jnp.dot(kd, v, preferred_element_type=jnp.float32)
            o = inner + cross
            mu = jnp.mean(o, axis=-1, keepdims=True)
            d = o - mu
            var = jnp.mean(d * d, axis=-1, keepdims=True)
            gate = g_ref[b, :, cols].astype(jnp.float32)
            y = d * lax.rsqrt(var + GN_EPS) * gn_ref[:, cols] * (gate * jax.nn.sigmoid(gate))
            o_ref[b, :, cols] = y.astype(o_ref.dtype)


def _retention(rq, rkT, rv, rg, ret_norm):
    B, S, W = rq.shape
    nc = S // BLK
    H = RET_HEADS
    C = RET_CHUNK
    f32 = jnp.float32
    gamma = 1.0 - jnp.exp2(-5.0 - jnp.arange(H, dtype=f32))
    log_g = jnp.log(gamma)
    idx = jnp.arange(C, dtype=f32)
    diff = idx[:, None] - idx[None, :]
    dintra = jnp.where(diff >= 0, jnp.exp(log_g[:, None, None] * jnp.maximum(diff, 0.0)), 0.0)
    qdec = jnp.broadcast_to(jnp.exp(log_g[:, None] * (idx + 1.0))[..., None], (H, C, LANES))
    kdec = jnp.exp(log_g[:, None] * (C - 1.0 - idx))[:, None, :]
    cdec = jnp.exp(log_g * C)
    gn = ret_norm.reshape(1, W).astype(f32)

    nat = pl.BlockSpec((B, C, W), lambda c, s: (0, c, 0))
    const = lambda a: pl.BlockSpec(a.shape, lambda c, s: (0,) * a.ndim,
                                   pipeline_mode=pl.Buffered(1))
    grid_spec = pltpu.PrefetchScalarGridSpec(
        num_scalar_prefetch=1,
        grid=(nc,),
        in_specs=[
            nat,
            pl.BlockSpec((B, None, W, C), lambda c, s: (0, c, 0, 0)),
            nat, nat,
            const(dintra), const(qdec), const(kdec), const(gn),
        ],
        out_specs=nat,
        scratch_shapes=[pltpu.VMEM((B, H, RET_HEAD_DIM, RET_HEAD_DIM), f32)],
    )
    return pl.pallas_call(
        _ret_kernel, grid_spec=grid_spec,
        out_shape=jax.ShapeDtypeStruct((B, S, W), jnp.bfloat16),
        compiler_params=pltpu.CompilerParams(
            dimension_semantics=("arbitrary",),
            vmem_limit_bytes=VMEM_LIMIT),
        name="retention",
    )(cdec, rq, rkT, rv, rg, dintra, qdec, kdec, gn)


def _gelu_tanh(u):
    k = -2.0 * math.sqrt(2.0 / math.pi) * LOG2E
    return u / (1.0 + jnp.exp2(u * (k + (k * 0.044715) * (u * u))))


def _shift_rows(cur, prev_tail, n):
    rolled = pltpu.roll(cur, n, axis=0)
    prev_rolled = pltpu.roll(prev_tail, n, axis=0)
    row = lax.broadcasted_iota(jnp.int32, prev_tail.shape, 0)
    head = jnp.where(row < n, prev_rolled, rolled[0:SUBLANES])
    return jnp.concatenate([head, rolled[SUBLANES:]], axis=0)


def _tail_kernel(x_ref, moba_ref, ret_ref, p_ref, wo_ref, ffn_g_ref, wup_ref, wgate_ref,
                 convw_ref, convb_ref, wdown_ref, ple_g_ref, wple_ref, wpg_ref, fin_g_ref,
                 o_ref, y_ref, carry_ref, *, tiles_per_seq):
    t = pl.program_id(0)
    R = ROW_TILE

    @pl.when(t % tiles_per_seq == 0)
    def _():
        carry_ref[...] = jnp.zeros_like(carry_ref)

    mix = jnp.concatenate([moba_ref[...], ret_ref[...]], axis=1)
    x1 = x_ref[...] + jnp.dot(mix, wo_ref[...], preferred_element_type=jnp.float32)
    h = _rms(x1, ffn_g_ref[...]).astype(jnp.bfloat16)

    for c in range(N_FF_CHUNKS):
        cols = slice(c * FF_CHUNK, (c + 1) * FF_CHUNK)
        up = jnp.dot(h, wup_ref[:, cols], preferred_element_type=jnp.float32)
        gt = jnp.dot(h, wgate_ref[:, cols], preferred_element_type=jnp.float32)
        prev_tail = carry_ref[c]
        carry_ref[c] = up[R - SUBLANES:R, :]
        w = convw_ref[:, cols]
        u = (convb_ref[:, cols]
             + w[0:1, :] * _shift_rows(up, prev_tail, 2)
             + w[1:2, :] * _shift_rows(up, prev_tail, 1)
             + w[2:3, :] * up)
        y_ref[:, cols] = (_gelu_tanh(u) * gt).astype(jnp.bfloat16)
    x2 = x1 + jnp.dot(y_ref[...], wdown_ref[...], preferred_element_type=jnp.float32)

    hn = _rms(x2, ple_g_ref[...]).astype(jnp.bfloat16)
    g = jax.nn.sigmoid(jnp.dot(hn, wpg_ref[...], preferred_element_type=jnp.float32))
    pe = jnp.dot(p_ref[...].astype(jnp.bfloat16), wple_ref[...], preferred_element_type=jnp.float32)
    x3 = x2 + pe * g
    o_ref[...] = _rms(x3, fin_g_ref[...])


def _tail(x2d, moba2d, ret2d, p2d, wo, ffn_g, wup, wgate, convw, convb, wdown, ple_g, wple,
          wpg, fin_g, seq_len):
    T, D = x2d.shape
    R = ROW_TILE
    row = lambda w: pl.BlockSpec((R, w), lambda t: (t, 0))
    consts = (wo, ffn_g, wup, wgate, convw, convb, wdown, ple_g, wple, wpg, fin_g)
    in_specs = [row(D), row(MOBA_WIDTH), row(RET_WIDTH), row(PLE_DIM)] + [
        _const_spec(a.shape) for a in consts]
    return pl.pallas_call(
        functools.partial(_tail_kernel, tiles_per_seq=seq_len // R),
        grid=(T // R,), in_specs=in_specs, out_specs=row(D),
        out_shape=jax.ShapeDtypeStruct((T, D), jnp.float32),
        scratch_shapes=[
            pltpu.VMEM((R, D_FF), jnp.bfloat16),
            pltpu.VMEM((N_FF_CHUNKS, SUBLANES, FF_CHUNK), jnp.float32),
        ],
        compiler_params=pltpu.CompilerParams(
            dimension_semantics=("arbitrary",), vmem_limit_bytes=VMEM_LIMIT),
        name="tail",
    )(x2d, moba2d, ret2d, p2d, *consts)


def kernel(x, p, attn_norm, w_in, ret_norm, w_out, ffn_norm, w_up, w_gate, conv_w, conv_b, w_down,
           ple_norm, w_ple, w_ple_gate, final_norm):
    B, S, D = x.shape
    assert D == D_MODEL and S % ROW_TILE == 0 and w_in.shape[0] == 1
    bf, f32 = jnp.bfloat16, jnp.float32
    MW, RW = MOBA_WIDTH, RET_WIDTH
    w = w_in[0].astype(bf)
    wqT = w[:, 0:MW].T
    wk = w[:, MW:2 * MW]
    wvT = w[:, 2 * MW:3 * MW].T
    o = 3 * MW
    wrq = w[:, o:o + RW]
    wrkT = w[:, o + RW:o + 2 * RW].T
    wrv = w[:, o + 2 * RW:o + 3 * RW]
    wrg = w[:, o + 3 * RW:o + 4 * RW]
    row = lambda a: a.reshape(1, -1).astype(f32)

    slopes_l2 = jnp.exp2(-8.0 * jnp.arange(1, MOBA_HEADS + 1, dtype=f32) / MOBA_HEADS) * LOG2E
    kterm = slopes_l2[None, :] * jnp.arange(BLK, dtype=f32)[:, None]
    kterm_hi = lax.bitcast_convert_type(
        lax.bitcast_convert_type(kterm, jnp.uint32) & jnp.uint32(0xFFFF0000), f32)
    kterm_lo = kterm - kterm_hi
    kpos = jnp.zeros((BLK, MOBA_HEADS, QK_PAD), f32)
    kpos = kpos.at[:, :, ALIBI_ROW].set(kterm_hi).at[:, :, ALIBI_ROW + 1].set(kterm_lo)
    kpos = kpos.reshape(BLK, MOBA_HEADS * QK_PAD)

    qT, k, vT, sel, rq, rkT, rv, rg = _in_proj(
        x, row(attn_norm[0]), wqT, wk, kpos, wvT, wrq, wrkT, wrv, wrg)

    moba_out = _moba(slopes_l2, qT, k, vT, sel)
    ret_out = _retention(rq, rkT, rv, rg, ret_norm[0])

    wo = w_out[0].astype(bf)
    out = _tail(
        x.reshape(B * S, D), moba_out.reshape(B * S, MW), ret_out.reshape(B * S, RW),
        p[0].reshape(B * S, PLE_DIM),
        wo, row(ffn_norm[0]),
        w_up[0].astype(bf), w_gate[0].astype(bf), conv_w[0].astype(f32), row(conv_b[0]),
        w_down[0].astype(bf),
        row(ple_norm[0]), w_ple[0].astype(bf), w_ple_gate[0].astype(bf), row(final_norm),
        seq_len=S)
    return out.reshape(B, S, D)
```

```python
import functools
import math

import jax
import jax.numpy as jnp
from jax import lax
from jax.experimental import pallas as pl
from jax.experimental.pallas import tpu as pltpu

D_MODEL = 1024
PLE_DIM = 256
MOBA_HEADS = 8
MOBA_HEAD_DIM = 64
MOBA_WIDTH = MOBA_HEADS * MOBA_HEAD_DIM
MOBA_BLOCK = 256
MOBA_TOPK = 3
RET_HEADS = 4
RET_HEAD_DIM = 128
RET_WIDTH = RET_HEADS * RET_HEAD_DIM
RET_CHUNK = 256
D_FF = 2816
CONV_WIDTH = 3
RMS_EPS = 1e-6
GN_EPS = 1e-5

BLK = 256
LANES = 128
SUBLANES = 8
FF_CHUNK = 256
N_FF_CHUNKS = D_FF // FF_CHUNK
ROW_TILE = 512
VMEM_LIMIT = 56 * 1024 * 1024
NEG = -1e30
LOG2E = math.log2(math.e)

QK_PAD = LANES
ALIBI_ROW = MOBA_HEAD_DIM
V_PAD = 80
ONES_ROW = MOBA_HEAD_DIM
GROUP = 16
GROUP_OWN = 8

_NT = (((1,), (1,)), ((), ()))


def _const_spec(shape):
    nd = len(shape)
    return pl.BlockSpec(shape, lambda *_: (0,) * nd, pipeline_mode=pl.Buffered(1))


def _rms(x, g):
    ms = jnp.mean(x * x, axis=-1, keepdims=True)
    return x * lax.rsqrt(ms + RMS_EPS) * g


def _in_proj_kernel(x_ref, g_ref, wqT_ref, wk_ref, kpos_ref, wvT_ref,
                    wrq_ref, wrkT_ref, wrv_ref, wrg_ref, dintra_ref, qdec_ref, kdec_ref, cdec_ref, gn_ref,
                    qT_ref, k_ref, vT_ref, sel_ref, ret_ref, kmean_ref, state_ref):
    t = pl.program_id(1)
    h = _rms(x_ref[...], g_ref[...]).astype(jnp.bfloat16)
    nblk = ROW_TILE // BLK
    nb = kmean_ref.shape[0]

    @pl.when(t == 0)
    def _():
        kmean_ref[...] = jnp.zeros_like(kmean_ref)
        state_ref[...] = jnp.zeros_like(state_ref)

    def nt(w_ref):
        return lax.dot_general(w_ref[...], h, _NT, preferred_element_type=jnp.float32)

    def nn(w_ref):
        return jnp.dot(h, w_ref[...], preferred_element_type=jnp.float32)

    Dh = MOBA_HEAD_DIM
    qT = (nt(wqT_ref) * (Dh ** -0.5 * LOG2E)).astype(jnp.bfloat16)
    vT = nt(wvT_ref).astype(jnp.bfloat16)
    rkT = (nt(wrkT_ref) * (RET_HEAD_DIM ** -0.5)).astype(jnp.bfloat16)
    q_pad = (lax.broadcasted_iota(jnp.int32, (QK_PAD - Dh, BLK), 0) < 2).astype(jnp.bfloat16)
    v_pad = (lax.broadcasted_iota(jnp.int32, (V_PAD - Dh, BLK), 0) < 1).astype(jnp.bfloat16)
    for b in range(nblk):
        sl = slice(b * BLK, (b + 1) * BLK)
        for a in range(MOBA_HEADS):
            qT_ref[b, a * QK_PAD:a * QK_PAD + Dh, :] = qT[a * Dh:(a + 1) * Dh, sl]
            qT_ref[b, a * QK_PAD + Dh:(a + 1) * QK_PAD, :] = q_pad
            vT_ref[b, a * V_PAD:a * V_PAD + Dh, :] = vT[a * Dh:(a + 1) * Dh, sl]
            vT_ref[b, a * V_PAD + Dh:(a + 1) * V_PAD, :] = v_pad
    k = nn(wk_ref)
    lane = lax.broadcasted_iota(jnp.int32, (BLK, LANES), 1)
    for b in range(nblk):
        kb = k[b * BLK:(b + 1) * BLK]
        kmean_ref[pl.ds(t * nblk + b, 1), :] = jnp.mean(kb, axis=0, keepdims=True)
        for a in range(MOBA_HEADS):
            pair = kb[:, (a // 2) * LANES:(a // 2 + 1) * LANES]
            if a % 2:
                pair = pltpu.roll(pair, Dh, axis=1)
            cols = slice(a * QK_PAD, (a + 1) * QK_PAD)
            k_ref[b * BLK:(b + 1) * BLK, cols] = (
                jnp.where(lane < Dh, pair, 0.0) + kpos_ref[:, cols]).astype(jnp.bfloat16)
    rq = nn(wrq_ref).astype(jnp.bfloat16)
    rv = nn(wrv_ref).astype(jnp.bfloat16)
    rg = nn(wrg_ref).astype(jnp.bfloat16)

    for b in range(nblk):
        rows = slice(b * BLK, (b + 1) * BLK)
        for hd in range(RET_HEADS):
            cols = slice(hd * RET_HEAD_DIM, (hd + 1) * RET_HEAD_DIM)
            q = rq[rows, cols]
            kT = rkT[cols, rows]
            v = rv[rows, cols]
            state = state_ref[hd]
            s = jnp.dot(q, kT, preferred_element_type=jnp.float32) * dintra_ref[hd]
            inner = jnp.dot(s.astype(jnp.bfloat16), v, preferred_element_type=jnp.float32)
            qd = (q.astype(jnp.float32) * qdec_ref[hd]).astype(jnp.bfloat16)
            cross = jnp.dot(qd, state.astype(jnp.bfloat16), preferred_element_type=jnp.float32)
            kd = (kT.astype(jnp.float32) * kdec_ref[hd]).astype(jnp.bfloat16)
            state_ref[hd] = cdec_ref[hd] * state + jnp.dot(kd, v, preferred_element_type=jnp.float32)
            o = inner + cross
            mu = jnp.mean(o, axis=-1, keepdims=True)
            d = o - mu
            var = jnp.mean(d * d, axis=-1, keepdims=True)
            gate = rg[rows, cols].astype(jnp.float32)
            y = d * lax.rsqrt(var + GN_EPS) * gn_ref[:, cols] * (gate * jax.nn.sigmoid(gate))
            ret_ref[rows, cols] = y.astype(ret_ref.dtype)

    km = kmean_ref[...]
    km_hi = km.astype(jnp.bfloat16)
    km_lo = (km - km_hi.astype(jnp.float32)).astype(jnp.bfloat16)
    blk_id = lax.broadcasted_iota(jnp.int32, (nb, BLK), 0)
    none = jnp.full((SUBLANES - MOBA_TOPK, BLK), nb, jnp.int32)
    for b in range(nblk):
        i = t * nblk + b
        for a in range(MOBA_HEADS):
            dims = slice(a * Dh, (a + 1) * Dh)
            qTh = qT[a * Dh:(a + 1) * Dh, b * BLK:(b + 1) * BLK]
            gate = (jnp.dot(km_hi[:, dims], qTh, preferred_element_type=jnp.float32)
                    + jnp.dot(km_lo[:, dims], qTh, preferred_element_type=jnp.float32))
            gate = jnp.where(blk_id < i, gate, -jnp.inf)
            picks = []
            for _ in range(MOBA_TOPK):
                top = jnp.max(gate, axis=0, keepdims=True)
                first = jnp.min(jnp.where(gate == top, blk_id, nb), axis=0, keepdims=True)
                first = jnp.where(top > -jnp.inf, first, nb)
                picks.append(first)
                gate = jnp.where(blk_id == first, -jnp.inf, gate)
            sel_ref[a, b] = jnp.concatenate(picks + [none], axis=0)


def _retention_tables(ret_norm):
    H, C, f32 = RET_HEADS, RET_CHUNK, jnp.float32
    gamma = 1.0 - jnp.exp2(-5.0 - jnp.arange(H, dtype=f32))
    log_g = jnp.log(gamma)
    idx = jnp.arange(C, dtype=f32)
    diff = idx[:, None] - idx[None, :]
    dintra = jnp.where(diff >= 0, jnp.exp(log_g[:, None, None] * jnp.maximum(diff, 0.0)), 0.0)
    qdec = jnp.broadcast_to(jnp.exp(log_g[:, None] * (idx + 1.0))[..., None], (H, C, LANES))
    kdec = jnp.exp(log_g[:, None] * (C - 1.0 - idx))[:, None, :]
    cdec = jnp.broadcast_to(jnp.exp(log_g * C)[:, None, None], (H, 1, LANES))
    return dintra, qdec, kdec, cdec, ret_norm.reshape(1, RET_WIDTH).astype(f32)


def _in_proj(x, g, wqT, wk, kpos, wvT, wrq, wrkT, wrv, wrg, ret_tables):
    B, S, D = x.shape
    nb = S // BLK
    nblk = ROW_TILE // BLK
    grid = (B, S // ROW_TILE)
    bf = jnp.bfloat16
    QW, VW = MOBA_HEADS * QK_PAD, MOBA_HEADS * V_PAD
    nat = lambda w: pl.BlockSpec((None, ROW_TILE, w), lambda b, t: (b, t, 0))
    tr = lambda w: pl.BlockSpec((None, nblk, w, BLK), lambda b, t: (b, t, 0, 0))
    out_shape = (
        jax.ShapeDtypeStruct((B, nb, QW, BLK), bf),
        jax.ShapeDtypeStruct((B, S, QW), bf),
        jax.ShapeDtypeStruct((B, nb, VW, BLK), bf),
        jax.ShapeDtypeStruct((B, MOBA_HEADS, nb, SUBLANES, BLK), jnp.int32),
        jax.ShapeDtypeStruct((B, S, RET_WIDTH), bf),
    )
    out_specs = (
        tr(QW), nat(QW), tr(VW),
        pl.BlockSpec((None, MOBA_HEADS, nblk, SUBLANES, BLK), lambda b, t: (b, 0, t, 0, 0)),
        nat(RET_WIDTH),
    )
    consts = (g, wqT, wk, kpos, wvT, wrq, wrkT, wrv, wrg) + tuple(ret_tables)
    in_specs = [pl.BlockSpec((None, ROW_TILE, D), lambda b, t: (b, t, 0))] + [
        _const_spec(a.shape) for a in consts]
    return pl.pallas_call(
        _in_proj_kernel, grid=grid, in_specs=in_specs, out_specs=out_specs, out_shape=out_shape,
        scratch_shapes=[
            pltpu.VMEM((nb, MOBA_WIDTH), jnp.float32),
            pltpu.VMEM((RET_HEADS, RET_HEAD_DIM, RET_HEAD_DIM), jnp.float32),
        ],
        compiler_params=pltpu.CompilerParams(
            dimension_semantics=("arbitrary", "arbitrary"), vmem_limit_bytes=VMEM_LIMIT),
        name="in_proj",
    )(x, *consts)


def _moba_kernel(slopes_ref, ti_ref, tj_ref, qT_ref, k_ref, vT_ref, sel_ref, o_ref,
                 m_ref, acc_ref, s_ref, smax_ref, alpha_ref, p_ref,
                 s_own_ref, smax_own_ref, alpha_own_ref, p_own_ref, *, n_past_groups):
    h = pl.program_id(1)
    nb = k_ref.shape[0]
    slope = slopes_ref[h]
    kpos = lax.broadcasted_iota(jnp.int32, (BLK, BLK), 0)
    qpos = lax.broadcasted_iota(jnp.int32, (BLK, BLK), 1)
    qterm = -slope * lax.broadcasted_iota(jnp.int32, (1, BLK), 1).astype(jnp.float32)

    m_ref[...] = jnp.full(m_ref.shape, -3e38, jnp.float32)
    acc_ref[...] = jnp.zeros(acc_ref.shape, jnp.float32)

    def own_tile(g, e):
        t = g * GROUP_OWN + e
        return t, t

    def past_tile(g, e):
        t = g * GROUP + e
        return ti_ref[t], tj_ref[t]

    def run_stream(n_groups, tile_of, own, bufs):
        s_ref, smax_ref, alpha_ref, p_ref = bufs
        GROUP = s_ref.shape[1]
        def scores(par, e, i_t, j_t):
            s = jnp.dot(k_ref[j_t], qT_ref[i_t], preferred_element_type=jnp.float32)
            if own:
                s = jnp.where(kpos <= qpos, s, NEG)
            s_ref[par, e] = s
            smax_ref[par, e:e + 1, :] = jnp.max(s, axis=0, keepdims=True)

        def values(par, e, i_t, j_t):
            pv = jnp.dot(vT_ref[j_t], p_ref[par, e * BLK:(e + 1) * BLK, :],
                         preferred_element_type=jnp.float32)
            acc_ref[i_t] = alpha_ref[par, e:e + 1, :] * acc_ref[i_t] + pv

        def softmax(par, e, i_t, j_t):
            if own:
                c = qterm
            else:
                sel = sel_ref[i_t]
                hit = (sel[0:1] == j_t) | (sel[1:2] == j_t) | (sel[2:3] == j_t)
                blk_dist = ((i_t - j_t) * MOBA_BLOCK).astype(jnp.float32)
                c = jnp.where(hit, -slope * blk_dist, NEG) + qterm
            m_old = m_ref[pl.ds(i_t, 1), :]
            m_new = jnp.maximum(m_old, smax_ref[par, e:e + 1, :] + c)
            alpha_ref[par, e:e + 1, :] = jnp.exp2(m_old - m_new)
            m_ref[pl.ds(i_t, 1), :] = m_new
            p_ref[par, e * BLK:(e + 1) * BLK, :] = (
                jnp.exp2(s_ref[par, e] - (m_new - c)).astype(jnp.bfloat16))

        def trip(g, par):
            g_next = jnp.minimum(g + 1, n_groups - 1)
            g_prev = jnp.maximum(g - 1, 0)
            for e in range(-2, GROUP):
                if e + 2 < GROUP:
                    scores(1 - par, e + 2, *tile_of(g_next, e + 2))
                if e >= 0:
                    values(1 - par, e, *tile_of(g_prev, e))
                    softmax(par, e, *tile_of(g, e))

        alpha_ref[1] = jnp.ones(alpha_ref.shape[1:], jnp.float32)
        p_ref[1] = jnp.zeros(p_ref.shape[1:], jnp.bfloat16)
        for e in range(GROUP):
            scores(0, e, *tile_of(0, e))

        def body(w, carry):
            trip(2 * w, 0)
            trip(2 * w + 1, 1)
            return carry

        assert n_groups % 2 == 0
        lax.fori_loop(0, n_groups // 2, body, 0)
        for e in range(GROUP):
            values((n_groups - 1) % 2, e, *tile_of(n_groups - 1, e))

    run_stream(nb // GROUP_OWN, own_tile, True, (s_own_ref, smax_own_ref, alpha_own_ref, p_own_ref))
    run_stream(n_past_groups, past_tile, False, (s_ref, smax_ref, alpha_ref, p_ref))

    def write_out(half):
        unroll = 4
        def some(u, carry):
            for d in range(unroll):
                i = u * unroll + d
                acc = acc_ref[i]
                oT = acc[0:MOBA_HEAD_DIM] / acc[ONES_ROW:ONES_ROW + 1]
                pad = jnp.zeros_like(oT)
                o2 = jnp.concatenate([pad, oT] if half else [oT, pad], axis=0).T
                rows = pl.ds(pl.multiple_of(i * BLK, BLK), BLK)
                lanes = slice(half * MOBA_HEAD_DIM, (half + 1) * MOBA_HEAD_DIM)
                o_ref[rows, lanes] = o2[:, lanes].astype(o_ref.dtype)
            return carry
        lax.fori_loop(0, nb // unroll, some, 0)

    lax.cond(h % 2 == 0, lambda: write_out(0), lambda: write_out(1))


def _moba(slopes_l2, qT, k, vT, sel):
    B, nb, QW, _ = qT.shape
    S = nb * BLK
    assert nb % GROUP == 0
    k4 = k.reshape(B, nb, BLK, QW)
    pairs = [(i, j) for j in range(nb - 1) for i in range(j + 1, nb)]
    pairs += [(0, 0)] * (-len(pairs) % GROUP)
    ti = jnp.asarray([p[0] for p in pairs], jnp.int32)
    tj = jnp.asarray([p[1] for p in pairs], jnp.int32)
    grid_spec = pltpu.PrefetchScalarGridSpec(
        num_scalar_prefetch=3,
        grid=(B, MOBA_HEADS),
        in_specs=[
            pl.BlockSpec((None, nb, QK_PAD, BLK), lambda b, h, *_: (b, 0, h, 0)),
            pl.BlockSpec((None, nb, BLK, QK_PAD), lambda b, h, *_: (b, 0, 0, h)),
            pl.BlockSpec((None, nb, V_PAD, BLK), lambda b, h, *_: (b, 0, h, 0)),
            pl.BlockSpec((None, None, nb, SUBLANES, BLK), lambda b, h, *_: (b, h, 0, 0, 0)),
        ],
        out_specs=pl.BlockSpec((None, S, 2 * MOBA_HEAD_DIM), lambda b, h, *_: (b, 0, h // 2)),
        scratch_shapes=[
            pltpu.VMEM((nb, BLK), jnp.float32),
            pltpu.VMEM((nb, V_PAD, BLK), jnp.float32),
            pltpu.VMEM((2, GROUP, BLK, BLK), jnp.float32),
            pltpu.VMEM((2, GROUP, BLK), jnp.float32),
            pltpu.VMEM((2, GROUP, BLK), jnp.float32),
            pltpu.VMEM((2, GROUP * BLK, BLK), jnp.bfloat16),
            pltpu.VMEM((2, GROUP_OWN, BLK, BLK), jnp.float32),
            pltpu.VMEM((2, GROUP_OWN, BLK), jnp.float32),
            pltpu.VMEM((2, GROUP_OWN, BLK), jnp.float32),
            pltpu.VMEM((2, GROUP_OWN * BLK, BLK), jnp.bfloat16),
        ],
    )
    return pl.pallas_call(
        functools.partial(_moba_kernel, n_past_groups=len(pairs) // GROUP),
        grid_spec=grid_spec,
        out_shape=jax.ShapeDtypeStruct((B, S, MOBA_WIDTH), jnp.bfloat16),
        compiler_params=pltpu.CompilerParams(
            dimension_semantics=("arbitrary", "arbitrary"), vmem_limit_bytes=VMEM_LIMIT),
        name="moba",
    )(slopes_l2, ti, tj, qT, k4, vT, sel)


def _gelu_tanh(u):
    k = -2.0 * math.sqrt(2.0 / math.pi) * LOG2E
    return u / (1.0 + jnp.exp2(u * (k + (k * 0.044715) * (u * u))))


def _shift_rows(cur, prev_tail, n):
    rolled = pltpu.roll(cur, n, axis=0)
    prev_rolled = pltpu.roll(prev_tail, n, axis=0)
    row = lax.broadcasted_iota(jnp.int32, prev_tail.shape, 0)
    head = jnp.where(row < n, prev_rolled, rolled[0:SUBLANES])
    return jnp.concatenate([head, rolled[SUBLANES:]], axis=0)


def _tail_kernel(x_ref, moba_ref, ret_ref, p_ref, wo_ref, ffn_g_ref, wup_ref, wgate_ref,
                 convw_ref, convb_ref, wdown_ref, ple_g_ref, wple_ref, wpg_ref, fin_g_ref,
                 o_ref, y_ref, carry_ref, *, tiles_per_seq):
    t = pl.program_id(0)
    R = ROW_TILE

    @pl.when(t % tiles_per_seq == 0)
    def _():
        carry_ref[...] = jnp.zeros_like(carry_ref)

    mix = jnp.concatenate([moba_ref[...], ret_ref[...]], axis=1)
    x1 = x_ref[...] + jnp.dot(mix, wo_ref[...], preferred_element_type=jnp.float32)
    h = _rms(x1, ffn_g_ref[...]).astype(jnp.bfloat16)

    for c in range(N_FF_CHUNKS):
        cols = slice(c * FF_CHUNK, (c + 1) * FF_CHUNK)
        up = jnp.dot(h, wup_ref[:, cols], preferred_element_type=jnp.float32)
        gt = jnp.dot(h, wgate_ref[:, cols], preferred_element_type=jnp.float32)
        prev_tail = carry_ref[c]
        carry_ref[c] = up[R - SUBLANES:R, :]
        w = convw_ref[:, cols]
        u = (convb_ref[:, cols]
             + w[0:1, :] * _shift_rows(up, prev_tail, 2)
             + w[1:2, :] * _shift_rows(up, prev_tail, 1)
             + w[2:3, :] * up)
        y_ref[:, cols] = (_gelu_tanh(u) * gt).astype(jnp.bfloat16)
    x2 = x1 + jnp.dot(y_ref[...], wdown_ref[...], preferred_element_type=jnp.float32)

    hn = _rms(x2, ple_g_ref[...]).astype(jnp.bfloat16)
    g = jax.nn.sigmoid(jnp.dot(hn, wpg_ref[...], preferred_element_type=jnp.float32))
    pe = jnp.dot(p_ref[...].astype(jnp.bfloat16), wple_ref[...], preferred_element_type=jnp.float32)
    x3 = x2 + pe * g
    o_ref[...] = _rms(x3, fin_g_ref[...])


def _tail(x2d, moba2d, ret2d, p2d, wo, ffn_g, wup, wgate, convw, convb, wdown, ple_g, wple,
          wpg, fin_g, seq_len):
    T, D = x2d.shape
    R = ROW_TILE
    row = lambda w: pl.BlockSpec((R, w), lambda t: (t, 0))
    consts = (wo, ffn_g, wup, wgate, convw, convb, wdown, ple_g, wple, wpg, fin_g)
    in_specs = [row(D), row(MOBA_WIDTH), row(RET_WIDTH), row(PLE_DIM)] + [
        _const_spec(a.shape) for a in consts]
    return pl.pallas_call(
        functools.partial(_tail_kernel, tiles_per_seq=seq_len // R),
        grid=(T // R,), in_specs=in_specs, out_specs=row(D),
        out_shape=jax.ShapeDtypeStruct((T, D), jnp.float32),
        scratch_shapes=[
            pltpu.VMEM((R, D_FF), jnp.bfloat16),
            pltpu.VMEM((N_FF_CHUNKS, SUBLANES, FF_CHUNK), jnp.float32),
        ],
        compiler_params=pltpu.CompilerParams(
            dimension_semantics=("arbitrary",), vmem_limit_bytes=VMEM_LIMIT),
        name="tail",
    )(x2d, moba2d, ret2d, p2d, *consts)


def kernel(x, p, attn_norm, w_in, ret_norm, w_out, ffn_norm, w_up, w_gate, conv_w, conv_b, w_down,
           ple_norm, w_ple, w_ple_gate, final_norm):
    B, S, D = x.shape
    assert D == D_MODEL and S % ROW_TILE == 0 and w_in.shape[0] == 1
    bf, f32 = jnp.bfloat16, jnp.float32
    MW, RW = MOBA_WIDTH, RET_WIDTH
    w = w_in[0].astype(bf)
    wqT = w[:, 0:MW].T
    wk = w[:, MW:2 * MW]
    wvT = w[:, 2 * MW:3 * MW].T
    o = 3 * MW
    wrq = w[:, o:o + RW]
    wrkT = w[:, o + RW:o + 2 * RW].T
    wrv = w[:, o + 2 * RW:o + 3 * RW]
    wrg = w[:, o + 3 * RW:o + 4 * RW]
    row = lambda a: a.reshape(1, -1).astype(f32)

    slopes_l2 = jnp.exp2(-8.0 * jnp.arange(1, MOBA_HEADS + 1, dtype=f32) / MOBA_HEADS) * LOG2E
    kterm = slopes_l2[None, :] * jnp.arange(BLK, dtype=f32)[:, None]
    kterm_hi = lax.bitcast_convert_type(
        lax.bitcast_convert_type(kterm, jnp.uint32) & jnp.uint32(0xFFFF0000), f32)
    kterm_lo = kterm - kterm_hi
    kpos = jnp.zeros((BLK, MOBA_HEADS, QK_PAD), f32)
    kpos = kpos.at[:, :, ALIBI_ROW].set(kterm_hi).at[:, :, ALIBI_ROW + 1].set(kterm_lo)
    kpos = kpos.reshape(BLK, MOBA_HEADS * QK_PAD)

    qT, k, vT, sel, ret_out = _in_proj(
        x, row(attn_norm[0]), wqT, wk, kpos, wvT, wrq, wrkT, wrv, wrg, _retention_tables(ret_norm[0]))

    moba_out = _moba(slopes_l2, qT, k, vT, sel)

    wo = w_out[0].astype(bf)
    out = _tail(
        x.reshape(B * S, D), moba_out.reshape(B * S, MW), ret_out.reshape(B * S, RW),
        p[0].reshape(B * S, PLE_DIM),
        wo, row(ffn_norm[0]),
        w_up[0].astype(bf), w_gate[0].astype(bf), conv_w[0].astype(f32), row(conv_b[0]),
        w_down[0].astype(bf),
        row(ple_norm[0]), w_ple[0].astype(bf), w_ple_gate[0].astype(bf), row(final_norm),
        seq_len=S)
    return out.reshape(B, S, D)
```

```python
import functools
import math

import jax
import jax.numpy as jnp
from jax import lax
from jax.experimental import pallas as pl
from jax.experimental.pallas import tpu as pltpu

D_MODEL = 1024
PLE_DIM = 256
MOBA_HEADS = 8
MOBA_HEAD_DIM = 64
MOBA_WIDTH = MOBA_HEADS * MOBA_HEAD_DIM
MOBA_BLOCK = 256
MOBA_TOPK = 3
RET_HEADS = 4
RET_HEAD_DIM = 128
RET_WIDTH = RET_HEADS * RET_HEAD_DIM
RET_CHUNK = 256
D_FF = 2816
CONV_WIDTH = 3
RMS_EPS = 1e-6
GN_EPS = 1e-5

BLK = 256
LANES = 128
SUBLANES = 8
FF_CHUNK = 256
N_FF_CHUNKS = D_FF // FF_CHUNK
ROW_TILE = 512
VMEM_LIMIT = 56 * 1024 * 1024
NEG = -1e30
LOG2E = math.log2(math.e)

QK_PAD = LANES
ALIBI_ROW = MOBA_HEAD_DIM
V_PAD = 80
ONES_ROW = MOBA_HEAD_DIM
GROUP = 16
GROUP_OWN = 8

_NT = (((1,), (1,)), ((), ()))


def _const_spec(shape):
    nd = len(shape)
    return pl.BlockSpec(shape, lambda *_: (0,) * nd, pipeline_mode=pl.Buffered(1))


def _rms(x, g):
    ms = jnp.mean(x * x, axis=-1, keepdims=True)
    return x * lax.rsqrt(ms + RMS_EPS) * g


def _in_proj_kernel(x_ref, g_ref, wqT_ref, wk_ref, kpos_ref, wvT_ref,
                    wrq_ref, wrkT_ref, wrv_ref, wrg_ref, dintra_ref, qdec_ref, kdec_ref, cdec_ref, gn_ref,
                    qT_ref, k_ref, vT_ref, sel_ref, ret_ref, kmean_ref, state_ref):
    t = pl.program_id(1)
    h = _rms(x_ref[...], g_ref[...]).astype(jnp.bfloat16)
    nblk = ROW_TILE // BLK
    nb = kmean_ref.shape[0]

    @pl.when(t == 0)
    def _():
        kmean_ref[...] = jnp.zeros_like(kmean_ref)
        state_ref[...] = jnp.zeros_like(state_ref)

    def nt(w_ref):
        return lax.dot_general(w_ref[...], h, _NT, preferred_element_type=jnp.float32)

    def nn(w_ref):
        return jnp.dot(h, w_ref[...], preferred_element_type=jnp.float32)

    Dh = MOBA_HEAD_DIM
    qT = (nt(wqT_ref) * (Dh ** -0.5 * LOG2E)).astype(jnp.bfloat16)
    vT = nt(wvT_ref).astype(jnp.bfloat16)
    rkT = nt(wrkT_ref) * (RET_HEAD_DIM ** -0.5)
    q_pad = (lax.broadcasted_iota(jnp.int32, (QK_PAD - Dh, BLK), 0) < 2).astype(jnp.bfloat16)
    v_pad = (lax.broadcasted_iota(jnp.int32, (V_PAD - Dh, BLK), 0) < 1).astype(jnp.bfloat16)
    for b in range(nblk):
        sl = slice(b * BLK, (b + 1) * BLK)
        for a in range(MOBA_HEADS):
            qT_ref[b, a * QK_PAD:a * QK_PAD + Dh, :] = qT[a * Dh:(a + 1) * Dh, sl]
            qT_ref[b, a * QK_PAD + Dh:(a + 1) * QK_PAD, :] = q_pad
            vT_ref[b, a * V_PAD:a * V_PAD + Dh, :] = vT[a * Dh:(a + 1) * Dh, sl]
            vT_ref[b, a * V_PAD + Dh:(a + 1) * V_PAD, :] = v_pad
    k = nn(wk_ref)
    lane = lax.broadcasted_iota(jnp.int32, (BLK, LANES), 1)
    for b in range(nblk):
        kb = k[b * BLK:(b + 1) * BLK]
        kmean_ref[pl.ds(t * nblk + b, 1), :] = jnp.mean(kb, axis=0, keepdims=True)
        for a in range(MOBA_HEADS):
            pair = kb[:, (a // 2) * LANES:(a // 2 + 1) * LANES]
            if a % 2:
                pair = pltpu.roll(pair, Dh, axis=1)
            cols = slice(a * QK_PAD, (a + 1) * QK_PAD)
            k_ref[b * BLK:(b + 1) * BLK, cols] = (
                jnp.where(lane < Dh, pair, 0.0) + kpos_ref[:, cols]).astype(jnp.bfloat16)
    rq = nn(wrq_ref)
    rv = nn(wrv_ref).astype(jnp.bfloat16)
    rg = nn(wrg_ref)

    for b in range(nblk):
        rows = slice(b * BLK, (b + 1) * BLK)
        for hd in range(RET_HEADS):
            cols = slice(hd * RET_HEAD_DIM, (hd + 1) * RET_HEAD_DIM)
            qf = rq[rows, cols]
            kTf = rkT[cols, rows]
            q, kT = qf.astype(jnp.bfloat16), kTf.astype(jnp.bfloat16)
            v = rv[rows, cols]
            state = state_ref[hd]
            s = jnp.dot(q, kT, preferred_element_type=jnp.float32) * dintra_ref[hd]
            inner = jnp.dot(s.astype(jnp.bfloat16), v, preferred_element_type=jnp.float32)
            qd = (qf * qdec_ref[hd]).astype(jnp.bfloat16)
            cross = jnp.dot(qd, state.astype(jnp.bfloat16), preferred_element_type=jnp.float32)
            kd = (kTf * kdec_ref[hd]).astype(jnp.bfloat16)
            state_ref[hd] = cdec_ref[hd] * state + jnp.dot(kd, v, preferred_element_type=jnp.float32)
            o = inner + cross
            mu = jnp.mean(o, axis=-1, keepdims=True)
            d = o - mu
            var = jnp.mean(d * d, axis=-1, keepdims=True)
            gate = rg[rows, cols]
            y = d * lax.rsqrt(var + GN_EPS) * gn_ref[:, cols] * (gate * jax.nn.sigmoid(gate))
            ret_ref[rows, cols] = y.astype(ret_ref.dtype)

    km = kmean_ref[...]
    km_hi = km.astype(jnp.bfloat16)
    km_lo = (km - km_hi.astype(jnp.float32)).astype(jnp.bfloat16)
    blk_id = lax.broadcasted_iota(jnp.int32, (nb, BLK), 0)
    none = jnp.full((SUBLANES - MOBA_TOPK, BLK), nb, jnp.int32)
    for b in range(nblk):
        i = t * nblk + b
        for a in range(MOBA_HEADS):
            dims = slice(a * Dh, (a + 1) * Dh)
            qTh = qT[a * Dh:(a + 1) * Dh, b * BLK:(b + 1) * BLK]
            gate = (jnp.dot(km_hi[:, dims], qTh, preferred_element_type=jnp.float32)
                    + jnp.dot(km_lo[:, dims], qTh, preferred_element_type=jnp.float32))
            gate = jnp.where(blk_id < i, gate, -jnp.inf)
            picks = []
            for _ in range(MOBA_TOPK):
                top = jnp.max(gate, axis=0, keepdims=True)
                first = jnp.min(jnp.where(gate == top, blk_id, nb), axis=0, keepdims=True)
                first = jnp.where(top > -jnp.inf, first, nb)
                picks.append(first)
                gate = jnp.where(blk_id == first, -jnp.inf, gate)
            sel_ref[a, b] = jnp.concatenate(picks + [none], axis=0)


def _retention_tables(ret_norm):
    H, C, f32 = RET_HEADS, RET_CHUNK, jnp.float32
    gamma = 1.0 - jnp.exp2(-5.0 - jnp.arange(H, dtype=f32))
    log_g = jnp.log(gamma)
    idx = jnp.arange(C, dtype=f32)
    diff = idx[:, None] - idx[None, :]
    dintra = jnp.where(diff >= 0, jnp.exp(log_g[:, None, None] * jnp.maximum(diff, 0.0)), 0.0)
    qdec = jnp.broadcast_to(jnp.exp(log_g[:, None] * (idx + 1.0))[..., None], (H, C, LANES))
    kdec = jnp.exp(log_g[:, None] * (C - 1.0 - idx))[:, None, :]
    cdec = jnp.broadcast_to(jnp.exp(log_g * C)[:, None, None], (H, 1, LANES))
    return dintra, qdec, kdec, cdec, ret_norm.reshape(1, RET_WIDTH).astype(f32)


def _in_proj(x, g, wqT, wk, kpos, wvT, wrq, wrkT, wrv, wrg, ret_tables):
    B, S, D = x.shape
    nb = S // BLK
    nblk = ROW_TILE // BLK
    grid = (B, S // ROW_TILE)
    bf = jnp.bfloat16
    QW, VW = MOBA_HEADS * QK_PAD, MOBA_HEADS * V_PAD
    nat = lambda w: pl.BlockSpec((None, ROW_TILE, w), lambda b, t: (b, t, 0))
    tr = lambda w: pl.BlockSpec((None, nblk, w, BLK), lambda b, t: (b, t, 0, 0))
    out_shape = (
        jax.ShapeDtypeStruct((B, nb, QW, BLK), bf),
        jax.ShapeDtypeStruct((B, S, QW), bf),
        jax.ShapeDtypeStruct((B, nb, VW, BLK), bf),
        jax.ShapeDtypeStruct((B, MOBA_HEADS, nb, SUBLANES, BLK), jnp.int32),
        jax.ShapeDtypeStruct((B, S, RET_WIDTH), bf),
    )
    out_specs = (
        tr(QW), nat(QW), tr(VW),
        pl.BlockSpec((None, MOBA_HEADS, nblk, SUBLANES, BLK), lambda b, t: (b, 0, t, 0, 0)),
        nat(RET_WIDTH),
    )
    consts = (g, wqT, wk, kpos, wvT, wrq, wrkT, wrv, wrg) + tuple(ret_tables)
    in_specs = [pl.BlockSpec((None, ROW_TILE, D), lambda b, t: (b, t, 0))] + [
        _const_spec(a.shape) for a in consts]
    return pl.pallas_call(
        _in_proj_kernel, grid=grid, in_specs=in_specs, out_specs=out_specs, out_shape=out_shape,
        scratch_shapes=[
            pltpu.VMEM((nb, MOBA_WIDTH), jnp.float32),
            pltpu.VMEM((RET_HEADS, RET_HEAD_DIM, RET_HEAD_DIM), jnp.float32),
        ],
        compiler_params=pltpu.CompilerParams(
            dimension_semantics=("arbitrary", "arbitrary"), vmem_limit_bytes=VMEM_LIMIT),
        name="in_proj",
    )(x, *consts)


def _moba_kernel(slopes_ref, ti_ref, tj_ref, qT_ref, k_ref, vT_ref, sel_ref, o_ref,
                 m_ref, acc_ref, s_ref, smax_ref, alpha_ref, p_ref,
                 s_own_ref, smax_own_ref, alpha_own_ref, p_own_ref, *, n_past_groups):
    h = pl.program_id(1)
    nb = k_ref.shape[0]
    slope = slopes_ref[h]
    kpos = lax.broadcasted_iota(jnp.int32, (BLK, BLK), 0)
    qpos = lax.broadcasted_iota(jnp.int32, (BLK, BLK), 1)
    qterm = -slope * lax.broadcasted_iota(jnp.int32, (1, BLK), 1).astype(jnp.float32)

    m_ref[...] = jnp.full(m_ref.shape, -3e38, jnp.float32)
    acc_ref[...] = jnp.zeros(acc_ref.shape, jnp.float32)

    def own_tile(g, e):
        t = g * GROUP_OWN + e
        return t, t

    def past_tile(g, e):
        t = g * GROUP + e
        return ti_ref[t], tj_ref[t]

    def run_stream(n_groups, tile_of, own, bufs):
        s_ref, smax_ref, alpha_ref, p_ref = bufs
        GROUP = s_ref.shape[1]
        def scores(par, e, i_t, j_t):
            s = jnp.dot(k_ref[j_t], qT_ref[i_t], preferred_element_type=jnp.float32)
            if own:
                s = jnp.where(kpos <= qpos, s, NEG)
            s_ref[par, e] = s
            smax_ref[par, e:e + 1, :] = jnp.max(s, axis=0, keepdims=True)

        def values(par, e, i_t, j_t):
            pv = jnp.dot(vT_ref[j_t], p_ref[par, e * BLK:(e + 1) * BLK, :],
                         preferred_element_type=jnp.float32)
            acc_ref[i_t] = alpha_ref[par, e:e + 1, :] * acc_ref[i_t] + pv

        def softmax(par, e, i_t, j_t):
            if own:
                c = qterm
            else:
                sel = sel_ref[i_t]
                hit = (sel[0:1] == j_t) | (sel[1:2] == j_t) | (sel[2:3] == j_t)
                blk_dist = ((i_t - j_t) * MOBA_BLOCK).astype(jnp.float32)
                c = jnp.where(hit, -slope * blk_dist, NEG) + qterm
            m_old = m_ref[pl.ds(i_t, 1), :]
            m_new = jnp.maximum(m_old, smax_ref[par, e:e + 1, :] + c)
            alpha_ref[par, e:e + 1, :] = jnp.exp2(m_old - m_new)
            m_ref[pl.ds(i_t, 1), :] = m_new
            p_ref[par, e * BLK:(e + 1) * BLK, :] = (
                jnp.exp2(s_ref[par, e] - (m_new - c)).astype(jnp.bfloat16))

        def trip(g, par):
            g_next = jnp.minimum(g + 1, n_groups - 1)
            g_prev = jnp.maximum(g - 1, 0)
            for e in range(-2, GROUP):
                if e + 2 < GROUP:
                    scores(1 - par, e + 2, *tile_of(g_next, e + 2))
                if e >= 0:
                    values(1 - par, e, *tile_of(g_prev, e))
                    softmax(par, e, *tile_of(g, e))

        alpha_ref[1] = jnp.ones(alpha_ref.shape[1:], jnp.float32)
        p_ref[1] = jnp.zeros(p_ref.shape[1:], jnp.bfloat16)
        for e in range(GROUP):
            scores(0, e, *tile_of(0, e))

        def body(w, carry):
            trip(2 * w, 0)
            trip(2 * w + 1, 1)
            return carry

        assert n_groups % 2 == 0
        lax.fori_loop(0, n_groups // 2, body, 0)
        for e in range(GROUP):
            values((n_groups - 1) % 2, e, *tile_of(n_groups - 1, e))

    run_stream(nb // GROUP_OWN, own_tile, True, (s_own_ref, smax_own_ref, alpha_own_ref, p_own_ref))
    run_stream(n_past_groups, past_tile, False, (s_ref, smax_ref, alpha_ref, p_ref))

    def write_out(half):
        unroll = 4
        def some(u, carry):
            for d in range(unroll):
                i = u * unroll + d
                acc = acc_ref[i]
                oT = acc[0:MOBA_HEAD_DIM] / acc[ONES_ROW:ONES_ROW + 1]
                pad = jnp.zeros_like(oT)
                o2 = jnp.concatenate([pad, oT] if half else [oT, pad], axis=0).T
                rows = pl.ds(pl.multiple_of(i * BLK, BLK), BLK)
                lanes = slice(half * MOBA_HEAD_DIM, (half + 1) * MOBA_HEAD_DIM)
                o_ref[rows, lanes] = o2[:, lanes].astype(o_ref.dtype)
            return carry
        lax.fori_loop(0, nb // unroll, some, 0)

    lax.cond(h % 2 == 0, lambda: write_out(0), lambda: write_out(1))


def _moba(slopes_l2, qT, k, vT, sel):
    B, nb, QW, _ = qT.shape
    S = nb * BLK
    assert nb % GROUP == 0
    k4 = k.reshape(B, nb, BLK, QW)
    pairs = [(i, j) for j in range(nb - 1) for i in range(j + 1, nb)]
    pairs += [(0, 0)] * (-len(pairs) % GROUP)
    ti = jnp.asarray([p[0] for p in pairs], jnp.int32)
    tj = jnp.asarray([p[1] for p in pairs], jnp.int32)
    grid_spec = pltpu.PrefetchScalarGridSpec(
        num_scalar_prefetch=3,
        grid=(B, MOBA_HEADS),
        in_specs=[
            pl.BlockSpec((None, nb, QK_PAD, BLK), lambda b, h, *_: (b, 0, h, 0)),
            pl.BlockSpec((None, nb, BLK, QK_PAD), lambda b, h, *_: (b, 0, 0, h)),
            pl.BlockSpec((None, nb, V_PAD, BLK), lambda b, h, *_: (b, 0, h, 0)),
            pl.BlockSpec((None, None, nb, SUBLANES, BLK), lambda b, h, *_: (b, h, 0, 0, 0)),
        ],
        out_specs=pl.BlockSpec((None, S, 2 * MOBA_HEAD_DIM), lambda b, h, *_: (b, 0, h // 2)),
        scratch_shapes=[
            pltpu.VMEM((nb, BLK), jnp.float32),
            pltpu.VMEM((nb, V_PAD, BLK), jnp.float32),
            pltpu.VMEM((2, GROUP, BLK, BLK), jnp.float32),
            pltpu.VMEM((2, GROUP, BLK), jnp.float32),
            pltpu.VMEM((2, GROUP, BLK), jnp.float32),
            pltpu.VMEM((2, GROUP * BLK, BLK), jnp.bfloat16),
            pltpu.VMEM((2, GROUP_OWN, BLK, BLK), jnp.float32),
            pltpu.VMEM((2, GROUP_OWN, BLK), jnp.float32),
            pltpu.VMEM((2, GROUP_OWN, BLK), jnp.float32),
            pltpu.VMEM((2, GROUP_OWN * BLK, BLK), jnp.bfloat16),
        ],
    )
    return pl.pallas_call(
        functools.partial(_moba_kernel, n_past_groups=len(pairs) // GROUP),
        grid_spec=grid_spec,
        out_shape=jax.ShapeDtypeStruct((B, S, MOBA_WIDTH), jnp.bfloat16),
        compiler_params=pltpu.CompilerParams(
            dimension_semantics=("arbitrary", "arbitrary"), vmem_limit_bytes=VMEM_LIMIT),
        name="moba",
    )(slopes_l2, ti, tj, qT, k4, vT, sel)


def _gelu_tanh(u):
    k = -2.0 * math.sqrt(2.0 / math.pi) * LOG2E
    return u / (1.0 + jnp.exp2(u * (k + (k * 0.044715) * (u * u))))


def _shift_rows(cur, prev_tail, n):
    rolled = pltpu.roll(cur, n, axis=0)
    prev_rolled = pltpu.roll(prev_tail, n, axis=0)
    row = lax.broadcasted_iota(jnp.int32, prev_tail.shape, 0)
    head = jnp.where(row < n, prev_rolled, rolled[0:SUBLANES])
    return jnp.concatenate([head, rolled[SUBLANES:]], axis=0)


def _tail_kernel(x_ref, moba_ref, ret_ref, p_ref, wo_ref, ffn_g_ref, wup_ref, wgate_ref,
                 convw_ref, convb_ref, wdown_ref, ple_g_ref, wple_ref, wpg_ref, fin_g_ref,
                 o_ref, y_ref, carry_ref, *, tiles_per_seq):
    t = pl.program_id(0)
    R = ROW_TILE

    @pl.when(t % tiles_per_seq == 0)
    def _():
        carry_ref[...] = jnp.zeros_like(carry_ref)

    mix = jnp.concatenate([moba_ref[...], ret_ref[...]], axis=1)
    x1 = x_ref[...] + jnp.dot(mix, wo_ref[...], preferred_element_type=jnp.float32)
    h = _rms(x1, ffn_g_ref[...]).astype(jnp.bfloat16)

    for c in range(N_FF_CHUNKS):
        cols = slice(c * FF_CHUNK, (c + 1) * FF_CHUNK)
        up = jnp.dot(h, wup_ref[:, cols], preferred_element_type=jnp.float32)
        gt = jnp.dot(h, wgate_ref[:, cols], preferred_element_type=jnp.float32)
        prev_tail = carry_ref[c]
        carry_ref[c] = up[R - SUBLANES:R, :]
        w = convw_ref[:, cols]
        u = (convb_ref[:, cols]
             + w[0:1, :] * _shift_rows(up, prev_tail, 2)
             + w[1:2, :] * _shift_rows(up, prev_tail, 1)
             + w[2:3, :] * up)
        y_ref[:, cols] = (_gelu_tanh(u) * gt).astype(jnp.bfloat16)
    x2 = x1 + jnp.dot(y_ref[...], wdown_ref[...], preferred_element_type=jnp.float32)

    hn = _rms(x2, ple_g_ref[...]).astype(jnp.bfloat16)
    g = jax.nn.sigmoid(jnp.dot(hn, wpg_ref[...], preferred_element_type=jnp.float32))
    pe = jnp.dot(p_ref[...].astype(jnp.bfloat16), wple_ref[...], preferred_element_type=jnp.float32)
    x3 = x2 + pe * g
    o_ref[...] = _rms(x3, fin_g_ref[...])


def _tail(x2d, moba2d, ret2d, p2d, wo, ffn_g, wup, wgate, convw, convb, wdown, ple_g, wple,
          wpg, fin_g, seq_len):
    T, D = x2d.shape
    R = ROW_TILE
    row = lambda w: pl.BlockSpec((R, w), lambda t: (t, 0))
    consts = (wo, ffn_g, wup, wgate, convw, convb, wdown, ple_g, wple, wpg, fin_g)
    in_specs = [row(D), row(MOBA_WIDTH), row(RET_WIDTH), row(PLE_DIM)] + [
        _const_spec(a.shape) for a in consts]
    return pl.pallas_call(
        functools.partial(_tail_kernel, tiles_per_seq=seq_len // R),
        grid=(T // R,), in_specs=in_specs, out_specs=row(D),
        out_shape=jax.ShapeDtypeStruct((T, D), jnp.float32),
        scratch_shapes=[
            pltpu.VMEM((R, D_FF), jnp.bfloat16),
            pltpu.VMEM((N_FF_CHUNKS, SUBLANES, FF_CHUNK), jnp.float32),
        ],
        compiler_params=pltpu.CompilerParams(
            dimension_semantics=("arbitrary",), vmem_limit_bytes=VMEM_LIMIT),
        name="tail",
    )(x2d, moba2d, ret2d, p2d, *consts)


def kernel(x, p, attn_norm, w_in, ret_norm, w_out, ffn_norm, w_up, w_gate, conv_w, conv_b, w_down,
           ple_norm, w_ple, w_ple_gate, final_norm):
    B, S, D = x.shape
    assert D == D_MODEL and S % ROW_TILE == 0 and w_in.shape[0] == 1
    bf, f32 = jnp.bfloat16, jnp.float32
    MW, RW = MOBA_WIDTH, RET_WIDTH
    w = w_in[0].astype(bf)
    wqT = w[:, 0:MW].T
    wk = w[:, MW:2 * MW]
    wvT = w[:, 2 * MW:3 * MW].T
    o = 3 * MW
    wrq = w[:, o:o + RW]
    wrkT = w[:, o + RW:o + 2 * RW].T
    wrv = w[:, o + 2 * RW:o + 3 * RW]
    wrg = w[:, o + 3 * RW:o + 4 * RW]
    row = lambda a: a.reshape(1, -1).astype(f32)

    slopes_l2 = jnp.exp2(-8.0 * jnp.arange(1, MOBA_HEADS + 1, dtype=f32) / MOBA_HEADS) * LOG2E
    kterm = slopes_l2[None, :] * jnp.arange(BLK, dtype=f32)[:, None]
    kterm_hi = lax.bitcast_convert_type(
        lax.bitcast_convert_type(kterm, jnp.uint32) & jnp.uint32(0xFFFF0000), f32)
    kterm_lo = kterm - kterm_hi
    kpos = jnp.zeros((BLK, MOBA_HEADS, QK_PAD), f32)
    kpos = kpos.at[:, :, ALIBI_ROW].set(kterm_hi).at[:, :, ALIBI_ROW + 1].set(kterm_lo)
    kpos = kpos.reshape(BLK, MOBA_HEADS * QK_PAD)

    qT, k, vT, sel, ret_out = _in_proj(
        x, row(attn_norm[0]), wqT, wk, kpos, wvT, wrq, wrkT, wrv, wrg, _retention_tables(ret_norm[0]))

    moba_out = _moba(slopes_l2, qT, k, vT, sel)

    wo = w_out[0].astype(bf)
    out = _tail(
        x.reshape(B * S, D), moba_out.reshape(B * S, MW), ret_out.reshape(B * S, RW),
        p[0].reshape(B * S, PLE_DIM),
        wo, row(ffn_norm[0]),
        w_up[0].astype(bf), w_gate[0].astype(bf), conv_w[0].astype(f32), row(conv_b[0]),
        w_down[0].astype(bf),
        row(ple_norm[0]), w_ple[0].astype(bf), w_ple_gate[0].astype(bf), row(final_norm),
        seq_len=S)
    return out.reshape(B, S, D)
```

```python
import functools
import math

import jax
import jax.numpy as jnp
from jax import lax
from jax.experimental import pallas as pl
from jax.experimental.pallas import tpu as pltpu

D_MODEL = 1024
PLE_DIM = 256
MOBA_HEADS = 8
MOBA_HEAD_DIM = 64
MOBA_WIDTH = MOBA_HEADS * MOBA_HEAD_DIM
MOBA_BLOCK = 256
MOBA_TOPK = 3
RET_HEADS = 4
RET_HEAD_DIM = 128
RET_WIDTH = RET_HEADS * RET_HEAD_DIM
RET_CHUNK = 256
D_FF = 2816
CONV_WIDTH = 3
RMS_EPS = 1e-6
GN_EPS = 1e-5

BLK = 256
LANES = 128
SUBLANES = 8
FF_CHUNK = 256
N_FF_CHUNKS = D_FF // FF_CHUNK
ROW_TILE = 512
VMEM_LIMIT = 56 * 1024 * 1024
NEG = -1e30
LOG2E = math.log2(math.e)

QK_PAD = LANES
ALIBI_ROW = MOBA_HEAD_DIM
V_PAD = 80
ONES_ROW = MOBA_HEAD_DIM
GROUP = 16
GROUP_OWN = 8

_NT = (((1,), (1,)), ((), ()))


def _const_spec(shape):
    nd = len(shape)
    return pl.BlockSpec(shape, lambda *_: (0,) * nd, pipeline_mode=pl.Buffered(1))


def _rms(x, g):
    ms = jnp.mean(x * x, axis=-1, keepdims=True)
    return x * lax.rsqrt(ms + RMS_EPS) * g


def _in_proj_kernel(x_ref, g_ref, wqT_ref, wk_ref, kpos_ref, wvT_ref,
                    wrq_ref, wrkT_ref, wrv_ref, wrg_ref, dintra_ref, qdec_ref, kdec_ref, cdec_ref, gn_ref,
                    qT_ref, k_ref, vT_ref, sel_ref, ret_ref, kmean_ref, state_ref):
    t = pl.program_id(1)
    h = _rms(x_ref[...], g_ref[...]).astype(jnp.bfloat16)
    nblk = ROW_TILE // BLK
    nb = kmean_ref.shape[0]

    @pl.when(t == 0)
    def _():
        kmean_ref[...] = jnp.zeros_like(kmean_ref)
        state_ref[...] = jnp.zeros_like(state_ref)

    def nt(w_ref):
        return lax.dot_general(w_ref[...], h, _NT, preferred_element_type=jnp.float32)

    def nn(w_ref):
        return jnp.dot(h, w_ref[...], preferred_element_type=jnp.float32)

    Dh = MOBA_HEAD_DIM
    qT = (nt(wqT_ref) * (Dh ** -0.5 * LOG2E)).astype(jnp.bfloat16)
    vT = nt(wvT_ref).astype(jnp.bfloat16)
    rkT = nt(wrkT_ref) * (RET_HEAD_DIM ** -0.5)
    q_pad = (lax.broadcasted_iota(jnp.int32, (QK_PAD - Dh, BLK), 0) < 2).astype(jnp.bfloat16)
    v_pad = (lax.broadcasted_iota(jnp.int32, (V_PAD - Dh, BLK), 0) < 1).astype(jnp.bfloat16)
    for b in range(nblk):
        sl = slice(b * BLK, (b + 1) * BLK)
        for a in range(MOBA_HEADS):
            qT_ref[b, a * QK_PAD:a * QK_PAD + Dh, :] = qT[a * Dh:(a + 1) * Dh, sl]
            qT_ref[b, a * QK_PAD + Dh:(a + 1) * QK_PAD, :] = q_pad
            vT_ref[b, a * V_PAD:a * V_PAD + Dh, :] = vT[a * Dh:(a + 1) * Dh, sl]
            vT_ref[b, a * V_PAD + Dh:(a + 1) * V_PAD, :] = v_pad
    k = nn(wk_ref)
    lane = lax.broadcasted_iota(jnp.int32, (BLK, LANES), 1)
    for b in range(nblk):
        kb = k[b * BLK:(b + 1) * BLK]
        kmean_ref[pl.ds(t * nblk + b, 1), :] = jnp.mean(kb, axis=0, keepdims=True)
        for a in range(MOBA_HEADS):
            pair = kb[:, (a // 2) * LANES:(a // 2 + 1) * LANES]
            if a % 2:
                pair = pltpu.roll(pair, Dh, axis=1)
            cols = slice(a * QK_PAD, (a + 1) * QK_PAD)
            k_ref[b * BLK:(b + 1) * BLK, cols] = (
                jnp.where(lane < Dh, pair, 0.0) + kpos_ref[:, cols]).astype(jnp.bfloat16)
    rq = nn(wrq_ref)
    rv = nn(wrv_ref).astype(jnp.bfloat16)
    rg = nn(wrg_ref)

    for b in range(nblk):
        rows = slice(b * BLK, (b + 1) * BLK)
        for hd in range(RET_HEADS):
            cols = slice(hd * RET_HEAD_DIM, (hd + 1) * RET_HEAD_DIM)
            qf = rq[rows, cols]
            kTf = rkT[cols, rows]
            q, kT = qf.astype(jnp.bfloat16), kTf.astype(jnp.bfloat16)
            v = rv[rows, cols]
            state = state_ref[hd]
            s = jnp.dot(q, kT, preferred_element_type=jnp.float32) * dintra_ref[hd]
            inner = jnp.dot(s.astype(jnp.bfloat16), v, preferred_element_type=jnp.float32)
            qd = (qf * qdec_ref[hd]).astype(jnp.bfloat16)
            cross = jnp.dot(qd, state.astype(jnp.bfloat16), preferred_element_type=jnp.float32)
            kd = (kTf * kdec_ref[hd]).astype(jnp.bfloat16)
            state_ref[hd] = cdec_ref[hd] * state + jnp.dot(kd, v, preferred_element_type=jnp.float32)
            o = inner + cross
            mu = jnp.mean(o, axis=-1, keepdims=True)
            d = o - mu
            var = jnp.mean(d * d, axis=-1, keepdims=True)
            gate = rg[rows, cols]
            y = d * lax.rsqrt(var + GN_EPS) * gn_ref[:, cols] * (gate * jax.nn.sigmoid(gate))
            ret_ref[rows, cols] = y.astype(ret_ref.dtype)

    km = kmean_ref[...]
    km_hi = km.astype(jnp.bfloat16)
    km_lo = (km - km_hi.astype(jnp.float32)).astype(jnp.bfloat16)
    blk_id = lax.broadcasted_iota(jnp.int32, (nb, BLK), 0)
    none = jnp.full((SUBLANES - MOBA_TOPK, BLK), nb, jnp.int32)
    for b in range(nblk):
        i = t * nblk + b
        for a in range(MOBA_HEADS):
            dims = slice(a * Dh, (a + 1) * Dh)
            qTh = qT[a * Dh:(a + 1) * Dh, b * BLK:(b + 1) * BLK]
            gate = (jnp.dot(km_hi[:, dims], qTh, preferred_element_type=jnp.float32)
                    + jnp.dot(km_lo[:, dims], qTh, preferred_element_type=jnp.float32))
            gate = jnp.where(blk_id < i, gate, -jnp.inf)
            picks = []
            for _ in range(MOBA_TOPK):
                top = jnp.max(gate, axis=0, keepdims=True)
                first = jnp.min(jnp.where(gate == top, blk_id, nb), axis=0, keepdims=True)
                first = jnp.where(top > -jnp.inf, first, nb)
                picks.append(first)
                gate = jnp.where(blk_id == first, -jnp.inf, gate)
            sel_ref[a, b] = jnp.concatenate(picks + [none], axis=0)


def _retention_tables(ret_norm):
    H, C, f32 = RET_HEADS, RET_CHUNK, jnp.float32
    gamma = 1.0 - jnp.exp2(-5.0 - jnp.arange(H, dtype=f32))
    log_g = jnp.log(gamma)
    idx = jnp.arange(C, dtype=f32)
    diff = idx[:, None] - idx[None, :]
    dintra = jnp.where(diff >= 0, jnp.exp(log_g[:, None, None] * jnp.maximum(diff, 0.0)), 0.0)
    qdec = jnp.broadcast_to(jnp.exp(log_g[:, None] * (idx + 1.0))[..., None], (H, C, LANES))
    kdec = jnp.exp(log_g[:, None] * (C - 1.0 - idx))[:, None, :]
    cdec = jnp.broadcast_to(jnp.exp(log_g * C)[:, None, None], (H, 1, LANES))
    return dintra, qdec, kdec, cdec, ret_norm.reshape(1, RET_WIDTH).astype(f32)


def _in_proj(x, g, wqT, wk, kpos, wvT, wrq, wrkT, wrv, wrg, ret_tables):
    B, S, D = x.shape
    nb = S // BLK
    nblk = ROW_TILE // BLK
    grid = (B, S // ROW_TILE)
    bf = jnp.bfloat16
    QW, VW = MOBA_HEADS * QK_PAD, MOBA_HEADS * V_PAD
    nat = lambda w: pl.BlockSpec((None, ROW_TILE, w), lambda b, t: (b, t, 0))
    tr = lambda w: pl.BlockSpec((None, nblk, w, BLK), lambda b, t: (b, t, 0, 0))
    out_shape = (
        jax.ShapeDtypeStruct((B, nb, QW, BLK), bf),
        jax.ShapeDtypeStruct((B, S, QW), bf),
        jax.ShapeDtypeStruct((B, nb, VW, BLK), bf),
        jax.ShapeDtypeStruct((B, MOBA_HEADS, nb, SUBLANES, BLK), jnp.int32),
        jax.ShapeDtypeStruct((B, S, RET_WIDTH), bf),
    )
    out_specs = (
        tr(QW), nat(QW), tr(VW),
        pl.BlockSpec((None, MOBA_HEADS, nblk, SUBLANES, BLK), lambda b, t: (b, 0, t, 0, 0)),
        nat(RET_WIDTH),
    )
    consts = (g, wqT, wk, kpos, wvT, wrq, wrkT, wrv, wrg) + tuple(ret_tables)
    in_specs = [pl.BlockSpec((None, ROW_TILE, D), lambda b, t: (b, t, 0))] + [
        _const_spec(a.shape) for a in consts]
    return pl.pallas_call(
        _in_proj_kernel, grid=grid, in_specs=in_specs, out_specs=out_specs, out_shape=out_shape,
        scratch_shapes=[
            pltpu.VMEM((nb, MOBA_WIDTH), jnp.float32),
            pltpu.VMEM((RET_HEADS, RET_HEAD_DIM, RET_HEAD_DIM), jnp.float32),
        ],
        compiler_params=pltpu.CompilerParams(
            dimension_semantics=("arbitrary", "arbitrary"), vmem_limit_bytes=VMEM_LIMIT),
        name="in_proj",
    )(x, *consts)


def _moba_kernel(slopes_ref, ti_ref, tj_ref, qT_ref, k_ref, vT_ref, sel_ref, o_ref,
                 m_ref, acc_ref, s_ref, smax_ref, alpha_ref, p_ref,
                 s_own_ref, smax_own_ref, alpha_own_ref, p_own_ref, *, n_past_groups):
    h = pl.program_id(1)
    nb = k_ref.shape[0]
    slope = slopes_ref[h]
    kpos = lax.broadcasted_iota(jnp.int32, (BLK, BLK), 0)
    qpos = lax.broadcasted_iota(jnp.int32, (BLK, BLK), 1)
    qterm = -slope * lax.broadcasted_iota(jnp.int32, (1, BLK), 1).astype(jnp.float32)

    m_ref[...] = jnp.full(m_ref.shape, -3e38, jnp.float32)
    acc_ref[...] = jnp.zeros(acc_ref.shape, jnp.float32)

    def own_tile(g, e):
        t = g * GROUP_OWN + e
        return t, t

    def past_tile(g, e):
        t = g * GROUP + e
        return ti_ref[t], tj_ref[t]

    def run_stream(n_groups, tile_of, own, bufs):
        s_ref, smax_ref, alpha_ref, p_ref = bufs
        GROUP = s_ref.shape[1]
        def scores(par, e, i_t, j_t):
            s = jnp.dot(k_ref[j_t], qT_ref[i_t], preferred_element_type=jnp.float32)
            if own:
                s = jnp.where(kpos <= qpos, s, NEG)
            s_ref[par, e] = s
            smax_ref[par, e:e + 1, :] = jnp.max(s, axis=0, keepdims=True)

        def values(par, e, i_t, j_t):
            pv = jnp.dot(vT_ref[j_t], p_ref[par, e * BLK:(e + 1) * BLK, :],
                         preferred_element_type=jnp.float32)
            acc_ref[i_t] = alpha_ref[par, e:e + 1, :] * acc_ref[i_t] + pv

        def softmax(par, e, i_t, j_t):
            if own:
                c = qterm
            else:
                sel = sel_ref[i_t]
                hit = (sel[0:1] == j_t) | (sel[1:2] == j_t) | (sel[2:3] == j_t)
                blk_dist = ((i_t - j_t) * MOBA_BLOCK).astype(jnp.float32)
                c = jnp.where(hit, -slope * blk_dist, NEG) + qterm
            m_old = m_ref[pl.ds(i_t, 1), :]
            m_new = jnp.maximum(m_old, smax_ref[par, e:e + 1, :] + c)
            alpha_ref[par, e:e + 1, :] = jnp.exp2(m_old - m_new)
            m_ref[pl.ds(i_t, 1), :] = m_new
            p_ref[par, e * BLK:(e + 1) * BLK, :] = (
                jnp.exp2(s_ref[par, e] - (m_new - c)).astype(jnp.bfloat16))

        def trip(g, par):
            g_next = jnp.minimum(g + 1, n_groups - 1)
            g_prev = jnp.maximum(g - 1, 0)
            for e in range(-2, GROUP):
                if e + 2 < GROUP:
                    scores(1 - par, e + 2, *tile_of(g_next, e + 2))
                if e >= 0:
                    values(1 - par, e, *tile_of(g_prev, e))
                    softmax(par, e, *tile_of(g, e))

        alpha_ref[1] = jnp.ones(alpha_ref.shape[1:], jnp.float32)
        p_ref[1] = jnp.zeros(p_ref.shape[1:], jnp.bfloat16)
        for e in range(GROUP):
            scores(0, e, *tile_of(0, e))

        def body(w, carry):
            trip(2 * w, 0)
            trip(2 * w + 1, 1)
            return carry

        assert n_groups % 2 == 0
        lax.fori_loop(0, n_groups // 2, body, 0)
        for e in range(GROUP):
            values((n_groups - 1) % 2, e, *tile_of(n_groups - 1, e))

    run_stream(nb // GROUP_OWN, own_tile, True, (s_own_ref, smax_own_ref, alpha_own_ref, p_own_ref))
    run_stream(n_past_groups, past_tile, False, (s_ref, smax_ref, alpha_ref, p_ref))

    def write_out(half):
        unroll = 4
        def some(u, carry):
            for d in range(unroll):
                i = u * unroll + d
                acc = acc_ref[i]
                oT = acc[0:MOBA_HEAD_DIM] / acc[ONES_ROW:ONES_ROW + 1]
                pad = jnp.zeros_like(oT)
                o2 = jnp.concatenate([pad, oT] if half else [oT, pad], axis=0).T
                rows = pl.ds(pl.multiple_of(i * BLK, BLK), BLK)
                lanes = slice(half * MOBA_HEAD_DIM, (half + 1) * MOBA_HEAD_DIM)
                o_ref[rows, lanes] = o2[:, lanes].astype(o_ref.dtype)
            return carry
        lax.fori_loop(0, nb // unroll, some, 0)

    lax.cond(h % 2 == 0, lambda: write_out(0), lambda: write_out(1))


def _moba(slopes_l2, qT, k, vT, sel):
    B, nb, QW, _ = qT.shape
    S = nb * BLK
    assert nb % GROUP == 0
    k4 = k.reshape(B, nb, BLK, QW)
    pairs = [(i, j) for j in range(nb - 1) for i in range(j + 1, nb)]
    pairs += [(0, 0)] * (-len(pairs) % GROUP)
    ti = jnp.asarray([p[0] for p in pairs], jnp.int32)
    tj = jnp.asarray([p[1] for p in pairs], jnp.int32)
    grid_spec = pltpu.PrefetchScalarGridSpec(
        num_scalar_prefetch=3,
        grid=(B, MOBA_HEADS),
        in_specs=[
            pl.BlockSpec((None, nb, QK_PAD, BLK), lambda b, h, *_: (b, 0, h, 0)),
            pl.BlockSpec((None, nb, BLK, QK_PAD), lambda b, h, *_: (b, 0, 0, h)),
            pl.BlockSpec((None, nb, V_PAD, BLK), lambda b, h, *_: (b, 0, h, 0)),
            pl.BlockSpec((None, None, nb, SUBLANES, BLK), lambda b, h, *_: (b, h, 0, 0, 0)),
        ],
        out_specs=pl.BlockSpec((None, S, 2 * MOBA_HEAD_DIM), lambda b, h, *_: (b, 0, h // 2)),
        scratch_shapes=[
            pltpu.VMEM((nb, BLK), jnp.float32),
            pltpu.VMEM((nb, V_PAD, BLK), jnp.float32),
            pltpu.VMEM((2, GROUP, BLK, BLK), jnp.float32),
            pltpu.VMEM((2, GROUP, BLK), jnp.float32),
            pltpu.VMEM((2, GROUP, BLK), jnp.float32),
            pltpu.VMEM((2, GROUP * BLK, BLK), jnp.bfloat16),
            pltpu.VMEM((2, GROUP_OWN, BLK, BLK), jnp.float32),
            pltpu.VMEM((2, GROUP_OWN, BLK), jnp.float32),
            pltpu.VMEM((2, GROUP_OWN, BLK), jnp.float32),
            pltpu.VMEM((2, GROUP_OWN * BLK, BLK), jnp.bfloat16),
        ],
    )
    return pl.pallas_call(
        functools.partial(_moba_kernel, n_past_groups=len(pairs) // GROUP),
        grid_spec=grid_spec,
        out_shape=jax.ShapeDtypeStruct((B, S, MOBA_WIDTH), jnp.bfloat16),
        compiler_params=pltpu.CompilerParams(
            dimension_semantics=("arbitrary", "arbitrary"), vmem_limit_bytes=VMEM_LIMIT),
        name="moba",
    )(slopes_l2, ti, tj, qT, k4, vT, sel)


def _gelu_tanh(u):
    k = -2.0 * math.sqrt(2.0 / math.pi) * LOG2E
    return u / (1.0 + jnp.exp2(u * (k + (k * 0.044715) * (u * u))))


def _shift_rows(cur, prev_tail, n):
    rolled = pltpu.roll(cur, n, axis=0)
    prev_rolled = pltpu.roll(prev_tail, n, axis=0)
    row = lax.broadcasted_iota(jnp.int32, prev_tail.shape, 0)
    head = jnp.where(row < n, prev_rolled, rolled[0:SUBLANES])
    return jnp.concatenate([head, rolled[SUBLANES:]], axis=0)


def _tail_kernel(x_ref, moba_ref, ret_ref, p_ref, wo_ref, ffn_g_ref, wup_ref, wgate_ref,
                 convw_ref, convb_ref, wdown_ref, ple_g_ref, wple_ref, wpg_ref, fin_g_ref,
                 o_ref, y_ref, carry_ref, *, tiles_per_seq):
    t = pl.program_id(0)
    R = ROW_TILE

    @pl.when(t % tiles_per_seq == 0)
    def _():
        carry_ref[...] = jnp.zeros_like(carry_ref)

    mix = jnp.concatenate([moba_ref[...], ret_ref[...]], axis=1)
    x1 = x_ref[...] + jnp.dot(mix, wo_ref[...], preferred_element_type=jnp.float32)
    h = _rms(x1, ffn_g_ref[...]).astype(jnp.bfloat16)

    for c in range(N_FF_CHUNKS):
        cols = slice(c * FF_CHUNK, (c + 1) * FF_CHUNK)
        up = jnp.dot(h, wup_ref[:, cols], preferred_element_type=jnp.float32)
        gt = jnp.dot(h, wgate_ref[:, cols], preferred_element_type=jnp.float32)
        prev_tail = carry_ref[c]
        carry_ref[c] = up[R - SUBLANES:R, :]
        w = convw_ref[:, cols]
        u = (convb_ref[:, cols]
             + w[0:1, :] * _shift_rows(up, prev_tail, 2)
             + w[1:2, :] * _shift_rows(up, prev_tail, 1)
             + w[2:3, :] * up)
        y_ref[:, cols] = (_gelu_tanh(u) * gt).astype(jnp.bfloat16)
    x2 = x1 + jnp.dot(y_ref[...], wdown_ref[...], preferred_element_type=jnp.float32)

    hn = _rms(x2, ple_g_ref[...]).astype(jnp.bfloat16)
    g = jax.nn.sigmoid(jnp.dot(hn, wpg_ref[...], preferred_element_type=jnp.float32))
    pe = jnp.dot(p_ref[...].astype(jnp.bfloat16), wple_ref[...], preferred_element_type=jnp.float32)
    x3 = x2 + pe * g
    o_ref[...] = _rms(x3, fin_g_ref[...])


def _tail(x2d, moba2d, ret2d, p2d, wo, ffn_g, wup, wgate, convw, convb, wdown, ple_g, wple,
          wpg, fin_g, seq_len):
    T, D = x2d.shape
    R = ROW_TILE
    row = lambda w: pl.BlockSpec((R, w), lambda t: (t, 0))
    consts = (wo, ffn_g, wup, wgate, convw, convb, wdown, ple_g, wple, wpg, fin_g)
    in_specs = [row(D), row(MOBA_WIDTH), row(RET_WIDTH), row(PLE_DIM)] + [
        _const_spec(a.shape) for a in consts]
    return pl.pallas_call(
        functools.partial(_tail_kernel, tiles_per_seq=seq_len // R),
        grid=(T // R,), in_specs=in_specs, out_specs=row(D),
        out_shape=jax.ShapeDtypeStruct((T, D), jnp.float32),
        scratch_shapes=[
            pltpu.VMEM((R, D_FF), jnp.bfloat16),
            pltpu.VMEM((N_FF_CHUNKS, SUBLANES, FF_CHUNK), jnp.float32),
        ],
        compiler_params=pltpu.CompilerParams(
            dimension_semantics=("arbitrary",), vmem_limit_bytes=VMEM_LIMIT),
        name="tail",
    )(x2d, moba2d, ret2d, p2d, *consts)


CAST_STEPS = 8


def _cast_kernel(*refs):
    n = len(refs) // 2
    for src, dst in zip(refs[:n], refs[n:]):
        dst[...] = src[...].astype(dst.dtype)


def _cast_bf16(*ws):
    specs = [pl.BlockSpec((w.shape[0] // CAST_STEPS, w.shape[1]), lambda s: (s, 0)) for w in ws]
    assert all(w.shape[0] % (CAST_STEPS * 16) == 0 for w in ws)
    return pl.pallas_call(
        _cast_kernel,
        grid=(CAST_STEPS,),
        in_specs=specs,
        out_specs=specs,
        out_shape=[jax.ShapeDtypeStruct(w.shape, jnp.bfloat16) for w in ws],
        compiler_params=pltpu.CompilerParams(
            dimension_semantics=("arbitrary",), vmem_limit_bytes=VMEM_LIMIT),
        name="cast_weights",
    )(*ws)


def kernel(x, p, attn_norm, w_in, ret_norm, w_out, ffn_norm, w_up, w_gate, conv_w, conv_b, w_down,
           ple_norm, w_ple, w_ple_gate, final_norm):
    B, S, D = x.shape
    assert D == D_MODEL and S % ROW_TILE == 0 and w_in.shape[0] == 1
    bf, f32 = jnp.bfloat16, jnp.float32
    MW, RW = MOBA_WIDTH, RET_WIDTH
    w = w_in[0].astype(bf)
    wqT = w[:, 0:MW].T
    wk = w[:, MW:2 * MW]
    wvT = w[:, 2 * MW:3 * MW].T
    o = 3 * MW
    wrq = w[:, o:o + RW]
    wrkT = w[:, o + RW:o + 2 * RW].T
    wrv = w[:, o + 2 * RW:o + 3 * RW]
    wrg = w[:, o + 3 * RW:o + 4 * RW]
    row = lambda a: a.reshape(1, -1).astype(f32)

    slopes_l2 = jnp.exp2(-8.0 * jnp.arange(1, MOBA_HEADS + 1, dtype=f32) / MOBA_HEADS) * LOG2E
    kterm = slopes_l2[None, :] * jnp.arange(BLK, dtype=f32)[:, None]
    kterm_hi = lax.bitcast_convert_type(
        lax.bitcast_convert_type(kterm, jnp.uint32) & jnp.uint32(0xFFFF0000), f32)
    kterm_lo = kterm - kterm_hi
    kpos = jnp.zeros((BLK, MOBA_HEADS, QK_PAD), f32)
    kpos = kpos.at[:, :, ALIBI_ROW].set(kterm_hi).at[:, :, ALIBI_ROW + 1].set(kterm_lo)
    kpos = kpos.reshape(BLK, MOBA_HEADS * QK_PAD)

    qT, k, vT, sel, ret_out = _in_proj(
        x, row(attn_norm[0]), wqT, wk, kpos, wvT, wrq, wrkT, wrv, wrg, _retention_tables(ret_norm[0]))

    moba_out = _moba(slopes_l2, qT, k, vT, sel)

    wo, wu, wg, wd, wp, wpg = _cast_bf16(
        w_out[0], w_up[0], w_gate[0], w_down[0], w_ple[0], w_ple_gate[0])
    out = _tail(
        x.reshape(B * S, D), moba_out.reshape(B * S, MW), ret_out.reshape(B * S, RW),
        p[0].reshape(B * S, PLE_DIM),
        wo, row(ffn_norm[0]),
        wu, wg, conv_w[0].astype(f32), row(conv_b[0]),
        wd,
        row(ple_norm[0]), wp, wpg, row(final_norm),
        seq_len=S)
    return out.reshape(B, S, D)
```

```python
import functools
import math

import jax
import jax.numpy as jnp
import numpy as np
from jax import lax
from jax.experimental import pallas as pl
from jax.experimental.pallas import tpu as pltpu

D_MODEL = 1024
PLE_DIM = 256
MOBA_HEADS = 8
MOBA_HEAD_DIM = 64
MOBA_WIDTH = MOBA_HEADS * MOBA_HEAD_DIM
MOBA_BLOCK = 256
MOBA_TOPK = 3
RET_HEADS = 4
RET_HEAD_DIM = 128
RET_WIDTH = RET_HEADS * RET_HEAD_DIM
RET_CHUNK = 256
D_FF = 2816
CONV_WIDTH = 3
RMS_EPS = 1e-6
GN_EPS = 1e-5

BLK = 256
LANES = 128
SUBLANES = 8
FF_CHUNK = 256
N_FF_CHUNKS = D_FF // FF_CHUNK
ROW_TILE = 512
VMEM_LIMIT = 56 * 1024 * 1024
NEG = -1e30
LOG2E = math.log2(math.e)

QK_PAD = LANES
ALIBI_ROW = MOBA_HEAD_DIM
V_PAD = 80
ONES_ROW = MOBA_HEAD_DIM
GROUP = 16
GROUP_OWN = 8

_NT = (((1,), (1,)), ((), ()))


def _const_spec(shape):
    nd = len(shape)
    return pl.BlockSpec(shape, lambda *_: (0,) * nd, pipeline_mode=pl.Buffered(1))


def _rms(x, g):
    ms = jnp.mean(x * x, axis=-1, keepdims=True)
    return x * lax.rsqrt(ms + RMS_EPS) * g


def _in_proj_kernel(x_ref, g_ref, wqT_ref, wk_ref, kpos_ref, wvT_ref,
                    wrq_ref, wrkT_ref, wrv_ref, wrg_ref, dintra_ref, qdec_ref, kdec_ref, cdec_ref, gn_ref,
                    qT_ref, k_ref, vT_ref, sel_ref, ret_ref, kmean_ref, state_ref):
    t = pl.program_id(1)
    h = _rms(x_ref[...], g_ref[...]).astype(jnp.bfloat16)
    nblk = ROW_TILE // BLK
    nb = kmean_ref.shape[0]

    @pl.when(t == 0)
    def _():
        kmean_ref[...] = jnp.zeros_like(kmean_ref)
        state_ref[...] = jnp.zeros_like(state_ref)

    def nt(w_ref):
        return lax.dot_general(w_ref[...], h, _NT, preferred_element_type=jnp.float32)

    def nn(w_ref):
        return jnp.dot(h, w_ref[...], preferred_element_type=jnp.float32)

    Dh = MOBA_HEAD_DIM
    qT = (nt(wqT_ref) * (Dh ** -0.5 * LOG2E)).astype(jnp.bfloat16)
    vT = nt(wvT_ref).astype(jnp.bfloat16)
    rkT = nt(wrkT_ref) * (RET_HEAD_DIM ** -0.5)
    q_pad = (lax.broadcasted_iota(jnp.int32, (QK_PAD - Dh, BLK), 0) < 2).astype(jnp.bfloat16)
    v_pad = (lax.broadcasted_iota(jnp.int32, (V_PAD - Dh, BLK), 0) < 1).astype(jnp.bfloat16)
    for b in range(nblk):
        sl = slice(b * BLK, (b + 1) * BLK)
        for a in range(MOBA_HEADS):
            qT_ref[b, a * QK_PAD:a * QK_PAD + Dh, :] = qT[a * Dh:(a + 1) * Dh, sl]
            qT_ref[b, a * QK_PAD + Dh:(a + 1) * QK_PAD, :] = q_pad
            vT_ref[b, a * V_PAD:a * V_PAD + Dh, :] = vT[a * Dh:(a + 1) * Dh, sl]
            vT_ref[b, a * V_PAD + Dh:(a + 1) * V_PAD, :] = v_pad
    k = nn(wk_ref)
    lane = lax.broadcasted_iota(jnp.int32, (BLK, LANES), 1)
    for b in range(nblk):
        kb = k[b * BLK:(b + 1) * BLK]
        kmean_ref[pl.ds(t * nblk + b, 1), :] = jnp.mean(kb, axis=0, keepdims=True)
        for a in range(MOBA_HEADS):
            pair = kb[:, (a // 2) * LANES:(a // 2 + 1) * LANES]
            if a % 2:
                pair = pltpu.roll(pair, Dh, axis=1)
            cols = slice(a * QK_PAD, (a + 1) * QK_PAD)
            k_ref[b * BLK:(b + 1) * BLK, cols] = (
                jnp.where(lane < Dh, pair, 0.0) + kpos_ref[:, cols]).astype(jnp.bfloat16)
    rq = nn(wrq_ref)
    rv = nn(wrv_ref).astype(jnp.bfloat16)
    rg = nn(wrg_ref)

    for b in range(nblk):
        rows = slice(b * BLK, (b + 1) * BLK)
        for hd in range(RET_HEADS):
            cols = slice(hd * RET_HEAD_DIM, (hd + 1) * RET_HEAD_DIM)
            qf = rq[rows, cols]
            kTf = rkT[cols, rows]
            q, kT = qf.astype(jnp.bfloat16), kTf.astype(jnp.bfloat16)
            v = rv[rows, cols]
            state = state_ref[hd]
            s = jnp.dot(q, kT, preferred_element_type=jnp.float32) * dintra_ref[hd]
            inner = jnp.dot(s.astype(jnp.bfloat16), v, preferred_element_type=jnp.float32)
            qd = (qf * qdec_ref[hd]).astype(jnp.bfloat16)
            cross = jnp.dot(qd, state.astype(jnp.bfloat16), preferred_element_type=jnp.float32)
            kd = (kTf * kdec_ref[hd]).astype(jnp.bfloat16)
            state_ref[hd] = cdec_ref[hd] * state + jnp.dot(kd, v, preferred_element_type=jnp.float32)
            o = inner + cross
            mu = jnp.mean(o, axis=-1, keepdims=True)
            d = o - mu
            var = jnp.mean(d * d, axis=-1, keepdims=True)
            gate = rg[rows, cols]
            y = d * lax.rsqrt(var + GN_EPS) * gn_ref[:, cols] * (gate * jax.nn.sigmoid(gate))
            ret_ref[rows, cols] = y.astype(ret_ref.dtype)

    km = kmean_ref[...]
    km_hi = km.astype(jnp.bfloat16)
    km_lo = (km - km_hi.astype(jnp.float32)).astype(jnp.bfloat16)
    blk_id = lax.broadcasted_iota(jnp.int32, (nb, BLK), 0)
    none = jnp.full((SUBLANES - MOBA_TOPK, BLK), nb, jnp.int32)
    for b in range(nblk):
        i = t * nblk + b
        for a in range(MOBA_HEADS):
            dims = slice(a * Dh, (a + 1) * Dh)
            qTh = qT[a * Dh:(a + 1) * Dh, b * BLK:(b + 1) * BLK]
            gate = (jnp.dot(km_hi[:, dims], qTh, preferred_element_type=jnp.float32)
                    + jnp.dot(km_lo[:, dims], qTh, preferred_element_type=jnp.float32))
            gate = jnp.where(blk_id < i, gate, -jnp.inf)
            picks = []
            for _ in range(MOBA_TOPK):
                top = jnp.max(gate, axis=0, keepdims=True)
                first = jnp.min(jnp.where(gate == top, blk_id, nb), axis=0, keepdims=True)
                first = jnp.where(top > -jnp.inf, first, nb)
                picks.append(first)
                gate = jnp.where(blk_id == first, -jnp.inf, gate)
            sel_ref[a, b] = jnp.concatenate(picks + [none], axis=0)


def _retention_tables(ret_norm):
    H, C, f32 = RET_HEADS, RET_CHUNK, np.float32
    gamma = (f32(1.0) - np.exp2(f32(-5.0) - np.arange(H, dtype=f32))).astype(f32)
    log_g = np.log(gamma).astype(f32)
    idx = np.arange(C, dtype=f32)
    diff = idx[:, None] - idx[None, :]
    dintra = np.where(diff >= 0, np.exp(log_g[:, None, None] * np.maximum(diff, f32(0.0))), f32(0.0))
    dintra = dintra.astype(f32)
    qdec = np.broadcast_to(np.exp(log_g[:, None] * (idx + f32(1.0)))[..., None], (H, C, LANES)).astype(f32)
    kdec = np.exp(log_g[:, None] * (f32(C - 1.0) - idx))[:, None, :].astype(f32)
    cdec = np.broadcast_to(np.exp(log_g * f32(C))[:, None, None], (H, 1, LANES)).astype(f32)
    return dintra, qdec, kdec, cdec, ret_norm.reshape(1, RET_WIDTH).astype(jnp.float32)


def _in_proj(x, g, wqT, wk, kpos, wvT, wrq, wrkT, wrv, wrg, ret_tables):
    B, S, D = x.shape
    nb = S // BLK
    nblk = ROW_TILE // BLK
    grid = (B, S // ROW_TILE)
    bf = jnp.bfloat16
    QW, VW = MOBA_HEADS * QK_PAD, MOBA_HEADS * V_PAD
    nat = lambda w: pl.BlockSpec((None, ROW_TILE, w), lambda b, t: (b, t, 0))
    tr = lambda w: pl.BlockSpec((None, nblk, w, BLK), lambda b, t: (b, t, 0, 0))
    out_shape = (
        jax.ShapeDtypeStruct((B, nb, QW, BLK), bf),
        jax.ShapeDtypeStruct((B, S, QW), bf),
        jax.ShapeDtypeStruct((B, nb, VW, BLK), bf),
        jax.ShapeDtypeStruct((B, MOBA_HEADS, nb, SUBLANES, BLK), jnp.int32),
        jax.ShapeDtypeStruct((B, S, RET_WIDTH), bf),
    )
    out_specs = (
        tr(QW), nat(QW), tr(VW),
        pl.BlockSpec((None, MOBA_HEADS, nblk, SUBLANES, BLK), lambda b, t: (b, 0, t, 0, 0)),
        nat(RET_WIDTH),
    )
    consts = (g, wqT, wk, kpos, wvT, wrq, wrkT, wrv, wrg) + tuple(ret_tables)
    in_specs = [pl.BlockSpec((None, ROW_TILE, D), lambda b, t: (b, t, 0))] + [
        _const_spec(a.shape) for a in consts]
    return pl.pallas_call(
        _in_proj_kernel, grid=grid, in_specs=in_specs, out_specs=out_specs, out_shape=out_shape,
        scratch_shapes=[
            pltpu.VMEM((nb, MOBA_WIDTH), jnp.float32),
            pltpu.VMEM((RET_HEADS, RET_HEAD_DIM, RET_HEAD_DIM), jnp.float32),
        ],
        compiler_params=pltpu.CompilerParams(
            dimension_semantics=("arbitrary", "arbitrary"), vmem_limit_bytes=VMEM_LIMIT),
        name="in_proj",
    )(x, *consts)


def _moba_kernel(slopes_ref, ti_ref, tj_ref, qT_ref, k_ref, vT_ref, sel_ref, o_ref,
                 m_ref, acc_ref, s_ref, smax_ref, alpha_ref, p_ref,
                 s_own_ref, smax_own_ref, alpha_own_ref, p_own_ref, *, n_past_groups):
    h = pl.program_id(1)
    nb = k_ref.shape[0]
    slope = slopes_ref[h]
    kpos = lax.broadcasted_iota(jnp.int32, (BLK, BLK), 0)
    qpos = lax.broadcasted_iota(jnp.int32, (BLK, BLK), 1)
    qterm = -slope * lax.broadcasted_iota(jnp.int32, (1, BLK), 1).astype(jnp.float32)

    m_ref[...] = jnp.full(m_ref.shape, -3e38, jnp.float32)
    acc_ref[...] = jnp.zeros(acc_ref.shape, jnp.float32)

    def own_tile(g, e):
        t = g * GROUP_OWN + e
        return t, t

    def past_tile(g, e):
        t = g * GROUP + e
        return ti_ref[t], tj_ref[t]

    def run_stream(n_groups, tile_of, own, bufs):
        s_ref, smax_ref, alpha_ref, p_ref = bufs
        GROUP = s_ref.shape[1]
        def scores(par, e, i_t, j_t):
            s = jnp.dot(k_ref[j_t], qT_ref[i_t], preferred_element_type=jnp.float32)
            if own:
                s = jnp.where(kpos <= qpos, s, NEG)
            s_ref[par, e] = s
            smax_ref[par, e:e + 1, :] = jnp.max(s, axis=0, keepdims=True)

        def values(par, e, i_t, j_t):
            pv = jnp.dot(vT_ref[j_t], p_ref[par, e * BLK:(e + 1) * BLK, :],
                         preferred_element_type=jnp.float32)
            acc_ref[i_t] = alpha_ref[par, e:e + 1, :] * acc_ref[i_t] + pv

        def softmax(par, e, i_t, j_t):
            if own:
                c = qterm
            else:
                sel = sel_ref[i_t]
                hit = (sel[0:1] == j_t) | (sel[1:2] == j_t) | (sel[2:3] == j_t)
                blk_dist = ((i_t - j_t) * MOBA_BLOCK).astype(jnp.float32)
                c = jnp.where(hit, -slope * blk_dist, NEG) + qterm
            m_old = m_ref[pl.ds(i_t, 1), :]
            m_new = jnp.maximum(m_old, smax_ref[par, e:e + 1, :] + c)
            alpha_ref[par, e:e + 1, :] = jnp.exp2(m_old - m_new)
            m_ref[pl.ds(i_t, 1), :] = m_new
            p_ref[par, e * BLK:(e + 1) * BLK, :] = (
                jnp.exp2(s_ref[par, e] - (m_new - c)).astype(jnp.bfloat16))

        def trip(g, par):
            g_next = jnp.minimum(g + 1, n_groups - 1)
            g_prev = jnp.maximum(g - 1, 0)
            for e in range(-2, GROUP):
                if e + 2 < GROUP:
                    scores(1 - par, e + 2, *tile_of(g_next, e + 2))
                if e >= 0:
                    values(1 - par, e, *tile_of(g_prev, e))
                    softmax(par, e, *tile_of(g, e))

        alpha_ref[1] = jnp.ones(alpha_ref.shape[1:], jnp.float32)
        p_ref[1] = jnp.zeros(p_ref.shape[1:], jnp.bfloat16)
        for e in range(GROUP):
            scores(0, e, *tile_of(0, e))

        def body(w, carry):
            trip(2 * w, 0)
            trip(2 * w + 1, 1)
            return carry

        assert n_groups % 2 == 0
        lax.fori_loop(0, n_groups // 2, body, 0)
        for e in range(GROUP):
            values((n_groups - 1) % 2, e, *tile_of(n_groups - 1, e))

    run_stream(nb // GROUP_OWN, own_tile, True, (s_own_ref, smax_own_ref, alpha_own_ref, p_own_ref))
    run_stream(n_past_groups, past_tile, False, (s_ref, smax_ref, alpha_ref, p_ref))

    def write_out(half):
        unroll = 4
        def some(u, carry):
            for d in range(unroll):
                i = u * unroll + d
                acc = acc_ref[i]
                oT = acc[0:MOBA_HEAD_DIM] / acc[ONES_ROW:ONES_ROW + 1]
                pad = jnp.zeros_like(oT)
                o2 = jnp.concatenate([pad, oT] if half else [oT, pad], axis=0).T
                rows = pl.ds(pl.multiple_of(i * BLK, BLK), BLK)
                lanes = slice(half * MOBA_HEAD_DIM, (half + 1) * MOBA_HEAD_DIM)
                o_ref[rows, lanes] = o2[:, lanes].astype(o_ref.dtype)
            return carry
        lax.fori_loop(0, nb // unroll, some, 0)

    lax.cond(h % 2 == 0, lambda: write_out(0), lambda: write_out(1))


def _moba(slopes_l2, qT, k, vT, sel):
    B, nb, QW, _ = qT.shape
    S = nb * BLK
    assert nb % GROUP == 0
    k4 = k.reshape(B, nb, BLK, QW)
    pairs = [(i, j) for j in range(nb - 1) for i in range(j + 1, nb)]
    pairs += [(0, 0)] * (-len(pairs) % GROUP)
    ti = jnp.asarray([p[0] for p in pairs], jnp.int32)
    tj = jnp.asarray([p[1] for p in pairs], jnp.int32)
    grid_spec = pltpu.PrefetchScalarGridSpec(
        num_scalar_prefetch=3,
        grid=(B, MOBA_HEADS),
        in_specs=[
            pl.BlockSpec((None, nb, QK_PAD, BLK), lambda b, h, *_: (b, 0, h, 0)),
            pl.BlockSpec((None, nb, BLK, QK_PAD), lambda b, h, *_: (b, 0, 0, h)),
            pl.BlockSpec((None, nb, V_PAD, BLK), lambda b, h, *_: (b, 0, h, 0)),
            pl.BlockSpec((None, None, nb, SUBLANES, BLK), lambda b, h, *_: (b, h, 0, 0, 0)),
        ],
        out_specs=pl.BlockSpec((None, S, 2 * MOBA_HEAD_DIM), lambda b, h, *_: (b, 0, h // 2)),
        scratch_shapes=[
            pltpu.VMEM((nb, BLK), jnp.float32),
            pltpu.VMEM((nb, V_PAD, BLK), jnp.float32),
            pltpu.VMEM((2, GROUP, BLK, BLK), jnp.float32),
            pltpu.VMEM((2, GROUP, BLK), jnp.float32),
            pltpu.VMEM((2, GROUP, BLK), jnp.float32),
            pltpu.VMEM((2, GROUP * BLK, BLK), jnp.bfloat16),
            pltpu.VMEM((2, GROUP_OWN, BLK, BLK), jnp.float32),
            pltpu.VMEM((2, GROUP_OWN, BLK), jnp.float32),
            pltpu.VMEM((2, GROUP_OWN, BLK), jnp.float32),
            pltpu.VMEM((2, GROUP_OWN * BLK, BLK), jnp.bfloat16),
        ],
    )
    return pl.pallas_call(
        functools.partial(_moba_kernel, n_past_groups=len(pairs) // GROUP),
        grid_spec=grid_spec,
        out_shape=jax.ShapeDtypeStruct((B, S, MOBA_WIDTH), jnp.bfloat16),
        compiler_params=pltpu.CompilerParams(
            dimension_semantics=("arbitrary", "arbitrary"), vmem_limit_bytes=VMEM_LIMIT),
        name="moba",
    )(slopes_l2, ti, tj, qT, k4, vT, sel)


def _gelu_tanh(u):
    k = -2.0 * math.sqrt(2.0 / math.pi) * LOG2E
    return u / (1.0 + jnp.exp2(u * (k + (k * 0.044715) * (u * u))))


def _shift_rows(cur, prev_tail, n):
    rolled = pltpu.roll(cur, n, axis=0)
    prev_rolled = pltpu.roll(prev_tail, n, axis=0)
    row = lax.broadcasted_iota(jnp.int32, prev_tail.shape, 0)
    head = jnp.where(row < n, prev_rolled, rolled[0:SUBLANES])
    return jnp.concatenate([head, rolled[SUBLANES:]], axis=0)


def _tail_kernel(x_ref, moba_ref, ret_ref, p_ref, wo_ref, ffn_g_ref, wup_ref, wgate_ref,
                 convw_ref, convb_ref, wdown_ref, ple_g_ref, wple_ref, wpg_ref, fin_g_ref,
                 o_ref, y_ref, carry_ref, *, tiles_per_seq):
    t = pl.program_id(0)
    R = ROW_TILE

    @pl.when(t % tiles_per_seq == 0)
    def _():
        carry_ref[...] = jnp.zeros_like(carry_ref)

    mix = jnp.concatenate([moba_ref[...], ret_ref[...]], axis=1)
    x1 = x_ref[...] + jnp.dot(mix, wo_ref[...], preferred_element_type=jnp.float32)
    h = _rms(x1, ffn_g_ref[...]).astype(jnp.bfloat16)

    for c in range(N_FF_CHUNKS):
        cols = slice(c * FF_CHUNK, (c + 1) * FF_CHUNK)
        up = jnp.dot(h, wup_ref[:, cols], preferred_element_type=jnp.float32)
        gt = jnp.dot(h, wgate_ref[:, cols], preferred_element_type=jnp.float32)
        prev_tail = carry_ref[c]
        carry_ref[c] = up[R - SUBLANES:R, :]
        w = convw_ref[:, cols]
        u = (convb_ref[:, cols]
             + w[0:1, :] * _shift_rows(up, prev_tail, 2)
             + w[1:2, :] * _shift_rows(up, prev_tail, 1)
             + w[2:3, :] * up)
        y_ref[:, cols] = (_gelu_tanh(u) * gt).astype(jnp.bfloat16)
    x2 = x1 + jnp.dot(y_ref[...], wdown_ref[...], preferred_element_type=jnp.float32)

    hn = _rms(x2, ple_g_ref[...]).astype(jnp.bfloat16)
    g = jax.nn.sigmoid(jnp.dot(hn, wpg_ref[...], preferred_element_type=jnp.float32))
    pe = jnp.dot(p_ref[...].astype(jnp.bfloat16), wple_ref[...], preferred_element_type=jnp.float32)
    x3 = x2 + pe * g
    o_ref[...] = _rms(x3, fin_g_ref[...])


def _tail(x2d, moba2d, ret2d, p2d, wo, ffn_g, wup, wgate, convw, convb, wdown, ple_g, wple,
          wpg, fin_g, seq_len):
    T, D = x2d.shape
    R = ROW_TILE
    row = lambda w: pl.BlockSpec((R, w), lambda t: (t, 0))
    consts = (wo, ffn_g, wup, wgate, convw, convb, wdown, ple_g, wple, wpg, fin_g)
    in_specs = [row(D), row(MOBA_WIDTH), row(RET_WIDTH), row(PLE_DIM)] + [
        _const_spec(a.shape) for a in consts]
    return pl.pallas_call(
        functools.partial(_tail_kernel, tiles_per_seq=seq_len // R),
        grid=(T // R,), in_specs=in_specs, out_specs=row(D),
        out_shape=jax.ShapeDtypeStruct((T, D), jnp.float32),
        scratch_shapes=[
            pltpu.VMEM((R, D_FF), jnp.bfloat16),
            pltpu.VMEM((N_FF_CHUNKS, SUBLANES, FF_CHUNK), jnp.float32),
        ],
        compiler_params=pltpu.CompilerParams(
            dimension_semantics=("arbitrary",), vmem_limit_bytes=VMEM_LIMIT),
        name="tail",
    )(x2d, moba2d, ret2d, p2d, *consts)


CAST_STEPS = 8


def _cast_kernel(*refs):
    n = len(refs) // 2
    for src, dst in zip(refs[:n], refs[n:]):
        dst[...] = src[...].astype(dst.dtype)


def _cast_bf16(*ws):
    specs = [pl.BlockSpec((w.shape[0] // CAST_STEPS, w.shape[1]), lambda s: (s, 0)) for w in ws]
    assert all(w.shape[0] % (CAST_STEPS * 16) == 0 for w in ws)
    return pl.pallas_call(
        _cast_kernel,
        grid=(CAST_STEPS,),
        in_specs=specs,
        out_specs=specs,
        out_shape=[jax.ShapeDtypeStruct(w.shape, jnp.bfloat16) for w in ws],
        compiler_params=pltpu.CompilerParams(
            dimension_semantics=("arbitrary",), vmem_limit_bytes=VMEM_LIMIT),
        name="cast_weights",
    )(*ws)


def kernel(x, p, attn_norm, w_in, ret_norm, w_out, ffn_norm, w_up, w_gate, conv_w, conv_b, w_down,
           ple_norm, w_ple, w_ple_gate, final_norm):
    B, S, D = x.shape
    assert D == D_MODEL and S % ROW_TILE == 0 and w_in.shape[0] == 1
    bf, f32 = jnp.bfloat16, jnp.float32
    MW, RW = MOBA_WIDTH, RET_WIDTH
    w, wo, wu, wg, wd, wp, wpg = _cast_bf16(
        w_in[0], w_out[0], w_up[0], w_gate[0], w_down[0], w_ple[0], w_ple_gate[0])
    wqT = w[:, 0:MW].T
    wk = w[:, MW:2 * MW]
    wvT = w[:, 2 * MW:3 * MW].T
    o = 3 * MW
    wrq = w[:, o:o + RW]
    wrkT = w[:, o + RW:o + 2 * RW].T
    wrv = w[:, o + 2 * RW:o + 3 * RW]
    wrg = w[:, o + 3 * RW:o + 4 * RW]
    row = lambda a: a.reshape(1, -1).astype(f32)

    nf32 = np.float32
    slopes_l2 = (np.exp2(-8.0 * np.arange(1, MOBA_HEADS + 1, dtype=nf32) / MOBA_HEADS).astype(nf32)
                 * nf32(LOG2E))
    kterm = slopes_l2[None, :] * np.arange(BLK, dtype=nf32)[:, None]
    kterm_hi = (kterm.view(np.uint32) & np.uint32(0xFFFF0000)).view(nf32)
    kterm_lo = kterm - kterm_hi
    kpos = np.zeros((BLK, MOBA_HEADS, QK_PAD), nf32)
    kpos[:, :, ALIBI_ROW] = kterm_hi
    kpos[:, :, ALIBI_ROW + 1] = kterm_lo
    kpos = kpos.reshape(BLK, MOBA_HEADS * QK_PAD)

    qT, k, vT, sel, ret_out = _in_proj(
        x, row(attn_norm[0]), wqT, wk, kpos, wvT, wrq, wrkT, wrv, wrg, _retention_tables(ret_norm[0]))

    moba_out = _moba(slopes_l2, qT, k, vT, sel)

    out = _tail(
        x.reshape(B * S, D), moba_out.reshape(B * S, MW), ret_out.reshape(B * S, RW),
        p[0].reshape(B * S, PLE_DIM),
        wo, row(ffn_norm[0]),
        wu, wg, conv_w[0].astype(f32), row(conv_b[0]),
        wd,
        row(ple_norm[0]), wp, wpg, row(final_norm),
        seq_len=S)
    return out.reshape(B, S, D)
```

```python
import functools
import math

import jax
import jax.numpy as jnp
import numpy as np
from jax import lax
from jax.experimental import pallas as pl
from jax.experimental.pallas import tpu as pltpu

D_MODEL = 1024
PLE_DIM = 256
MOBA_HEADS = 8
MOBA_HEAD_DIM = 64
MOBA_WIDTH = MOBA_HEADS * MOBA_HEAD_DIM
MOBA_BLOCK = 256
MOBA_TOPK = 3
RET_HEADS = 4
RET_HEAD_DIM = 128
RET_WIDTH = RET_HEADS * RET_HEAD_DIM
RET_CHUNK = 256
D_FF = 2816
CONV_WIDTH = 3
RMS_EPS = 1e-6
GN_EPS = 1e-5

BLK = 256
LANES = 128
SUBLANES = 8
FF_CHUNK = 256
N_FF_CHUNKS = D_FF // FF_CHUNK
ROW_TILE = 512
VMEM_LIMIT = 56 * 1024 * 1024
NEG = -1e30
LOG2E = math.log2(math.e)

QK_PAD = LANES
ALIBI_ROW = MOBA_HEAD_DIM
V_PAD = 80
ONES_ROW = MOBA_HEAD_DIM
GROUP = 16
GROUP_OWN = 8

_NT = (((1,), (1,)), ((), ()))


def _const_spec(shape):
    nd = len(shape)
    return pl.BlockSpec(shape, lambda *_: (0,) * nd, pipeline_mode=pl.Buffered(1))


def _rms(x, g):
    ms = jnp.mean(x * x, axis=-1, keepdims=True)
    return x * lax.rsqrt(ms + RMS_EPS) * g


def _in_proj_kernel(x_ref, g_ref, wqT_ref, wk_ref, kpos_ref, wvT_ref,
                    wrq_ref, wrkT_ref, wrv_ref, wrg_ref, dintra_ref, qdec_ref, kdec_ref, cdec_ref, gn_ref,
                    qT_ref, k_ref, vT_ref, sel_ref, ret_ref, kmean_ref, state_ref):
    t = pl.program_id(1)
    h = _rms(x_ref[...], g_ref[...]).astype(jnp.bfloat16)
    nblk = ROW_TILE // BLK
    nb = kmean_ref.shape[0]

    @pl.when(t == 0)
    def _():
        kmean_ref[...] = jnp.zeros_like(kmean_ref)
        state_ref[...] = jnp.zeros_like(state_ref)

    def nt(w_ref):
        return lax.dot_general(w_ref[...], h, _NT, preferred_element_type=jnp.float32)

    def nn(w_ref):
        return jnp.dot(h, w_ref[...], preferred_element_type=jnp.float32)

    Dh = MOBA_HEAD_DIM
    qT = (nt(wqT_ref) * (Dh ** -0.5 * LOG2E)).astype(jnp.bfloat16)
    vT = nt(wvT_ref).astype(jnp.bfloat16)
    rkT = nt(wrkT_ref) * (RET_HEAD_DIM ** -0.5)
    q_pad = (lax.broadcasted_iota(jnp.int32, (QK_PAD - Dh, BLK), 0) < 2).astype(jnp.bfloat16)
    v_pad = (lax.broadcasted_iota(jnp.int32, (V_PAD - Dh, BLK), 0) < 1).astype(jnp.bfloat16)
    for b in range(nblk):
        sl = slice(b * BLK, (b + 1) * BLK)
        for a in range(MOBA_HEADS):
            qT_ref[b, a * QK_PAD:a * QK_PAD + Dh, :] = qT[a * Dh:(a + 1) * Dh, sl]
            qT_ref[b, a * QK_PAD + Dh:(a + 1) * QK_PAD, :] = q_pad
            vT_ref[b, a * V_PAD:a * V_PAD + Dh, :] = vT[a * Dh:(a + 1) * Dh, sl]
            vT_ref[b, a * V_PAD + Dh:(a + 1) * V_PAD, :] = v_pad
    k = nn(wk_ref)
    lane = lax.broadcasted_iota(jnp.int32, (BLK, LANES), 1)
    for b in range(nblk):
        kb = k[b * BLK:(b + 1) * BLK]
        kmean_ref[pl.ds(t * nblk + b, 1), :] = jnp.mean(kb, axis=0, keepdims=True)
        for a in range(MOBA_HEADS):
            pair = kb[:, (a // 2) * LANES:(a // 2 + 1) * LANES]
            if a % 2:
                pair = pltpu.roll(pair, Dh, axis=1)
            cols = slice(a * QK_PAD, (a + 1) * QK_PAD)
            k_ref[b * BLK:(b + 1) * BLK, cols] = (
                jnp.where(lane < Dh, pair, 0.0) + kpos_ref[:, cols]).astype(jnp.bfloat16)
    rq = nn(wrq_ref)
    rv = nn(wrv_ref).astype(jnp.bfloat16)
    rg = nn(wrg_ref)

    for b in range(nblk):
        rows = slice(b * BLK, (b + 1) * BLK)
        for hd in range(RET_HEADS):
            cols = slice(hd * RET_HEAD_DIM, (hd + 1) * RET_HEAD_DIM)
            qf = rq[rows, cols]
            kTf = rkT[cols, rows]
            q, kT = qf.astype(jnp.bfloat16), kTf.astype(jnp.bfloat16)
            v = rv[rows, cols]
            state = state_ref[hd]
            s = jnp.dot(q, kT, preferred_element_type=jnp.float32) * dintra_ref[hd]
            inner = jnp.dot(s.astype(jnp.bfloat16), v, preferred_element_type=jnp.float32)
            qd = (qf * qdec_ref[hd]).astype(jnp.bfloat16)
            cross = jnp.dot(qd, state.astype(jnp.bfloat16), preferred_element_type=jnp.float32)
            kd = (kTf * kdec_ref[hd]).astype(jnp.bfloat16)
            state_ref[hd] = cdec_ref[hd] * state + jnp.dot(kd, v, preferred_element_type=jnp.float32)
            o = inner + cross
            mu = jnp.mean(o, axis=-1, keepdims=True)
            d = o - mu
            var = jnp.mean(d * d, axis=-1, keepdims=True)
            gate = rg[rows, cols]
            y = d * lax.rsqrt(var + GN_EPS) * gn_ref[:, cols] * (gate * jax.nn.sigmoid(gate))
            ret_ref[rows, cols] = y.astype(ret_ref.dtype)

    km = kmean_ref[...]
    km_hi = km.astype(jnp.bfloat16)
    km_lo = (km - km_hi.astype(jnp.float32)).astype(jnp.bfloat16)
    blk_id = lax.broadcasted_iota(jnp.int32, (nb, BLK), 0)
    none = jnp.full((SUBLANES - MOBA_TOPK, BLK), nb, jnp.int32)
    for b in range(nblk):
        i = t * nblk + b
        for a in range(MOBA_HEADS):
            dims = slice(a * Dh, (a + 1) * Dh)
            qTh = qT[a * Dh:(a + 1) * Dh, b * BLK:(b + 1) * BLK]
            gate = (jnp.dot(km_hi[:, dims], qTh, preferred_element_type=jnp.float32)
                    + jnp.dot(km_lo[:, dims], qTh, preferred_element_type=jnp.float32))
            gate = jnp.where(blk_id < i, gate, -jnp.inf)
            picks = []
            for _ in range(MOBA_TOPK):
                top = jnp.max(gate, axis=0, keepdims=True)
                first = jnp.min(jnp.where(gate == top, blk_id, nb), axis=0, keepdims=True)
                first = jnp.where(top > -jnp.inf, first, nb)
                picks.append(first)
                gate = jnp.where(blk_id == first, -jnp.inf, gate)
            sel_ref[a, b] = jnp.concatenate(picks + [none], axis=0)


def _retention_tables(ret_norm):
    H, C, f32 = RET_HEADS, RET_CHUNK, np.float32
    gamma = (f32(1.0) - np.exp2(f32(-5.0) - np.arange(H, dtype=f32))).astype(f32)
    log_g = np.log(gamma).astype(f32)
    idx = np.arange(C, dtype=f32)
    diff = idx[:, None] - idx[None, :]
    dintra = np.where(diff >= 0, np.exp(log_g[:, None, None] * np.maximum(diff, f32(0.0))), f32(0.0))
    dintra = dintra.astype(f32)
    qdec = np.broadcast_to(np.exp(log_g[:, None] * (idx + f32(1.0)))[..., None], (H, C, LANES)).astype(f32)
    kdec = np.exp(log_g[:, None] * (f32(C - 1.0) - idx))[:, None, :].astype(f32)
    cdec = np.broadcast_to(np.exp(log_g * f32(C))[:, None, None], (H, 1, LANES)).astype(f32)
    return dintra, qdec, kdec, cdec, ret_norm.reshape(1, RET_WIDTH).astype(jnp.float32)


def _in_proj(x, g, wqT, wk, kpos, wvT, wrq, wrkT, wrv, wrg, ret_tables):
    B, S, D = x.shape
    nb = S // BLK
    nblk = ROW_TILE // BLK
    grid = (B, S // ROW_TILE)
    bf = jnp.bfloat16
    QW, VW = MOBA_HEADS * QK_PAD, MOBA_HEADS * V_PAD
    nat = lambda w: pl.BlockSpec((None, ROW_TILE, w), lambda b, t: (b, t, 0))
    tr = lambda w: pl.BlockSpec((None, nblk, w, BLK), lambda b, t: (b, t, 0, 0))
    out_shape = (
        jax.ShapeDtypeStruct((B, nb, QW, BLK), bf),
        jax.ShapeDtypeStruct((B, S, QW), bf),
        jax.ShapeDtypeStruct((B, nb, VW, BLK), bf),
        jax.ShapeDtypeStruct((B, MOBA_HEADS, nb, SUBLANES, BLK), jnp.int32),
        jax.ShapeDtypeStruct((B, S, RET_WIDTH), bf),
    )
    out_specs = (
        tr(QW), nat(QW), tr(VW),
        pl.BlockSpec((None, MOBA_HEADS, nblk, SUBLANES, BLK), lambda b, t: (b, 0, t, 0, 0)),
        nat(RET_WIDTH),
    )
    consts = (g, wqT, wk, kpos, wvT, wrq, wrkT, wrv, wrg) + tuple(ret_tables)
    in_specs = [pl.BlockSpec((None, ROW_TILE, D), lambda b, t: (b, t, 0))] + [
        _const_spec(a.shape) for a in consts]
    return pl.pallas_call(
        _in_proj_kernel, grid=grid, in_specs=in_specs, out_specs=out_specs, out_shape=out_shape,
        scratch_shapes=[
            pltpu.VMEM((nb, MOBA_WIDTH), jnp.float32),
            pltpu.VMEM((RET_HEADS, RET_HEAD_DIM, RET_HEAD_DIM), jnp.float32),
        ],
        compiler_params=pltpu.CompilerParams(
            dimension_semantics=("arbitrary", "arbitrary"), vmem_limit_bytes=VMEM_LIMIT),
        name="in_proj",
    )(x, *consts)


def _moba_kernel(slopes_ref, ti_ref, tj_ref, qT_ref, k_ref, vT_ref, sel_ref, o_ref,
                 m_ref, acc_ref, s_ref, smax_ref, alpha_ref, p_ref,
                 s_own_ref, smax_own_ref, alpha_own_ref, p_own_ref, *, n_past_groups):
    h = pl.program_id(1)
    nb = k_ref.shape[0]
    slope = slopes_ref[h]
    kpos = lax.broadcasted_iota(jnp.int32, (BLK, BLK), 0)
    qpos = lax.broadcasted_iota(jnp.int32, (BLK, BLK), 1)
    qterm = -slope * lax.broadcasted_iota(jnp.int32, (1, BLK), 1).astype(jnp.float32)

    m_ref[...] = jnp.full(m_ref.shape, -3e38, jnp.float32)
    acc_ref[...] = jnp.zeros(acc_ref.shape, jnp.float32)

    def own_tile(g, e):
        t = g * GROUP_OWN + e
        return t, t

    def past_tile(g, e):
        t = g * GROUP + e
        return ti_ref[t], tj_ref[t]

    def run_stream(n_groups, tile_of, own, bufs):
        s_ref, smax_ref, alpha_ref, p_ref = bufs
        GROUP = s_ref.shape[1]
        def scores(par, e, i_t, j_t):
            s = jnp.dot(k_ref[j_t], qT_ref[i_t], preferred_element_type=jnp.float32)
            if own:
                s = jnp.where(kpos <= qpos, s, NEG)
            s_ref[par, e] = s
            smax_ref[par, e:e + 1, :] = jnp.max(s, axis=0, keepdims=True)

        def values(par, e, i_t, j_t):
            pv = jnp.dot(vT_ref[j_t], p_ref[par, e * BLK:(e + 1) * BLK, :],
                         preferred_element_type=jnp.float32)
            acc_ref[i_t] = alpha_ref[par, e:e + 1, :] * acc_ref[i_t] + pv

        def softmax(par, e, i_t, j_t):
            if own:
                c = qterm
            else:
                sel = sel_ref[i_t]
                hit = (sel[0:1] == j_t) | (sel[1:2] == j_t) | (sel[2:3] == j_t)
                blk_dist = ((i_t - j_t) * MOBA_BLOCK).astype(jnp.float32)
                c = jnp.where(hit, -slope * blk_dist, NEG) + qterm
            m_old = m_ref[pl.ds(i_t, 1), :]
            m_new = jnp.maximum(m_old, smax_ref[par, e:e + 1, :] + c)
            alpha_ref[par, e:e + 1, :] = jnp.exp2(m_old - m_new)
            m_ref[pl.ds(i_t, 1), :] = m_new
            p_ref[par, e * BLK:(e + 1) * BLK, :] = (
                jnp.exp2(s_ref[par, e] - (m_new - c)).astype(jnp.bfloat16))

        def trip(g, par):
            g_next = jnp.minimum(g + 1, n_groups - 1)
            g_prev = jnp.maximum(g - 1, 0)
            for e in range(-2, GROUP):
                if e + 2 < GROUP:
                    scores(1 - par, e + 2, *tile_of(g_next, e + 2))
                if e >= 0:
                    values(1 - par, e, *tile_of(g_prev, e))
                    softmax(par, e, *tile_of(g, e))

        alpha_ref[1] = jnp.ones(alpha_ref.shape[1:], jnp.float32)
        p_ref[1] = jnp.zeros(p_ref.shape[1:], jnp.bfloat16)
        for e in range(GROUP):
            scores(0, e, *tile_of(0, e))

        def body(w, carry):
            trip(2 * w, 0)
            trip(2 * w + 1, 1)
            return carry

        assert n_groups % 2 == 0
        lax.fori_loop(0, n_groups // 2, body, 0)
        for e in range(GROUP):
            values((n_groups - 1) % 2, e, *tile_of(n_groups - 1, e))

    run_stream(nb // GROUP_OWN, own_tile, True, (s_own_ref, smax_own_ref, alpha_own_ref, p_own_ref))
    run_stream(n_past_groups, past_tile, False, (s_ref, smax_ref, alpha_ref, p_ref))

    def write_out(half):
        unroll = 4
        def some(u, carry):
            for d in range(unroll):
                i = u * unroll + d
                acc = acc_ref[i]
                oT = acc[0:MOBA_HEAD_DIM] / acc[ONES_ROW:ONES_ROW + 1]
                pad = jnp.zeros_like(oT)
                o2 = jnp.concatenate([pad, oT] if half else [oT, pad], axis=0).T
                rows = pl.ds(pl.multiple_of(i * BLK, BLK), BLK)
                lanes = slice(half * MOBA_HEAD_DIM, (half + 1) * MOBA_HEAD_DIM)
                o_ref[rows, lanes] = o2[:, lanes].astype(o_ref.dtype)
            return carry
        lax.fori_loop(0, nb // unroll, some, 0)

    lax.cond(h % 2 == 0, lambda: write_out(0), lambda: write_out(1))


def _moba(slopes_l2, qT, k, vT, sel):
    B, nb, QW, _ = qT.shape
    S = nb * BLK
    assert nb % GROUP == 0
    k4 = k.reshape(B, nb, BLK, QW)
    pairs = [(i, j) for j in range(nb - 1) for i in range(j + 1, nb)]
    pairs += [(0, 0)] * (-len(pairs) % GROUP)
    ti = jnp.asarray([p[0] for p in pairs], jnp.int32)
    tj = jnp.asarray([p[1] for p in pairs], jnp.int32)
    grid_spec = pltpu.PrefetchScalarGridSpec(
        num_scalar_prefetch=3,
        grid=(B, MOBA_HEADS),
        in_specs=[
            pl.BlockSpec((None, nb, QK_PAD, BLK), lambda b, h, *_: (b, 0, h, 0)),
            pl.BlockSpec((None, nb, BLK, QK_PAD), lambda b, h, *_: (b, 0, 0, h)),
            pl.BlockSpec((None, nb, V_PAD, BLK), lambda b, h, *_: (b, 0, h, 0)),
            pl.BlockSpec((None, None, nb, SUBLANES, BLK), lambda b, h, *_: (b, h, 0, 0, 0)),
        ],
        out_specs=pl.BlockSpec((None, S, 2 * MOBA_HEAD_DIM), lambda b, h, *_: (b, 0, h // 2)),
        scratch_shapes=[
            pltpu.VMEM((nb, BLK), jnp.float32),
            pltpu.VMEM((nb, V_PAD, BLK), jnp.float32),
            pltpu.VMEM((2, GROUP, BLK, BLK), jnp.float32),
            pltpu.VMEM((2, GROUP, BLK), jnp.float32),
            pltpu.VMEM((2, GROUP, BLK), jnp.float32),
            pltpu.VMEM((2, GROUP * BLK, BLK), jnp.bfloat16),
            pltpu.VMEM((2, GROUP_OWN, BLK, BLK), jnp.float32),
            pltpu.VMEM((2, GROUP_OWN, BLK), jnp.float32),
            pltpu.VMEM((2, GROUP_OWN, BLK), jnp.float32),
            pltpu.VMEM((2, GROUP_OWN * BLK, BLK), jnp.bfloat16),
        ],
    )
    return pl.pallas_call(
        functools.partial(_moba_kernel, n_past_groups=len(pairs) // GROUP),
        grid_spec=grid_spec,
        out_shape=jax.ShapeDtypeStruct((B, S, MOBA_WIDTH), jnp.bfloat16),
        compiler_params=pltpu.CompilerParams(
            dimension_semantics=("arbitrary", "arbitrary"), vmem_limit_bytes=VMEM_LIMIT),
        name="moba",
    )(slopes_l2, ti, tj, qT, k4, vT, sel)


def _gelu_tanh(u):
    k = -2.0 * math.sqrt(2.0 / math.pi) * LOG2E
    return u / (1.0 + jnp.exp2(u * (k + (k * 0.044715) * (u * u))))


def _shift_rows(cur, prev_tail, n):
    rolled = pltpu.roll(cur, n, axis=0)
    prev_rolled = pltpu.roll(prev_tail, n, axis=0)
    row = lax.broadcasted_iota(jnp.int32, prev_tail.shape, 0)
    head = jnp.where(row < n, prev_rolled, rolled[0:SUBLANES])
    return jnp.concatenate([head, rolled[SUBLANES:]], axis=0)


def _tail_kernel(x_ref, moba_ref, ret_ref, p_ref, wo_ref, ffn_g_ref, wup_ref, wgate_ref,
                 convw_ref, convb_ref, wdown_ref, ple_g_ref, wple_ref, wpg_ref, fin_g_ref,
                 o_ref, y_ref, carry_ref, *, tiles_per_seq):
    t = pl.program_id(0)
    R = ROW_TILE

    @pl.when(t % tiles_per_seq == 0)
    def _():
        carry_ref[...] = jnp.zeros_like(carry_ref)

    mix = jnp.concatenate([moba_ref[...], ret_ref[...]], axis=1)
    x1 = x_ref[...] + jnp.dot(mix, wo_ref[...], preferred_element_type=jnp.float32)
    h = _rms(x1, ffn_g_ref[...]).astype(jnp.bfloat16)

    for c in range(N_FF_CHUNKS):
        cols = slice(c * FF_CHUNK, (c + 1) * FF_CHUNK)
        up = jnp.dot(h, wup_ref[:, cols], preferred_element_type=jnp.float32)
        gt = jnp.dot(h, wgate_ref[:, cols], preferred_element_type=jnp.float32)
        prev_tail = carry_ref[c]
        carry_ref[c] = up[R - SUBLANES:R, :]
        w = convw_ref[:, cols]
        u = (convb_ref[:, cols]
             + w[0:1, :] * _shift_rows(up, prev_tail, 2)
             + w[1:2, :] * _shift_rows(up, prev_tail, 1)
             + w[2:3, :] * up)
        y_ref[:, cols] = (_gelu_tanh(u) * gt).astype(jnp.bfloat16)
    x2 = x1 + jnp.dot(y_ref[...], wdown_ref[...], preferred_element_type=jnp.float32)

    hn = _rms(x2, ple_g_ref[...]).astype(jnp.bfloat16)
    g = jax.nn.sigmoid(jnp.dot(hn, wpg_ref[...], preferred_element_type=jnp.float32))
    pe = jnp.dot(p_ref[...].astype(jnp.bfloat16), wple_ref[...], preferred_element_type=jnp.float32)
    x3 = x2 + pe * g
    o_ref[...] = _rms(x3, fin_g_ref[...])


def _tail(x2d, moba2d, ret2d, p2d, wo, ffn_g, wup, wgate, convw, convb, wdown, ple_g, wple,
          wpg, fin_g, seq_len):
    T, D = x2d.shape
    R = ROW_TILE
    row = lambda w: pl.BlockSpec((R, w), lambda t: (t, 0))
    consts = (wo, ffn_g, wup, wgate, convw, convb, wdown, ple_g, wple, wpg, fin_g)
    in_specs = [row(D), row(MOBA_WIDTH), row(RET_WIDTH), row(PLE_DIM)] + [
        _const_spec(a.shape) for a in consts]
    return pl.pallas_call(
        functools.partial(_tail_kernel, tiles_per_seq=seq_len // R),
        grid=(T // R,), in_specs=in_specs, out_specs=row(D),
        out_shape=jax.ShapeDtypeStruct((T, D), jnp.float32),
        scratch_shapes=[
            pltpu.VMEM((R, D_FF), jnp.bfloat16),
            pltpu.VMEM((N_FF_CHUNKS, SUBLANES, FF_CHUNK), jnp.float32),
        ],
        compiler_params=pltpu.CompilerParams(
            dimension_semantics=("arbitrary",), vmem_limit_bytes=VMEM_LIMIT),
        name="tail",
    )(x2d, moba2d, ret2d, p2d, *consts)


CAST_STEPS = 8


IN_PARTS = 7
IN_TRANSPOSED = (0, 2, 4)


def _cast_kernel(*refs):
    n = (len(refs) - IN_PARTS) // 2
    w_in_ref, srcs = refs[0], refs[1:1 + n]
    parts, dsts = refs[1 + n:1 + n + IN_PARTS], refs[1 + n + IN_PARTS:]
    for c, dst in enumerate(parts):
        blk = w_in_ref[:, c * MOBA_WIDTH:(c + 1) * MOBA_WIDTH]
        dst[...] = (blk.T if c in IN_TRANSPOSED else blk).astype(dst.dtype)
    for src, dst in zip(srcs, dsts):
        dst[...] = src[...].astype(dst.dtype)


def _cast_bf16(w_in, *ws):
    assert MOBA_WIDTH == RET_WIDTH and w_in.shape[1] == IN_PARTS * MOBA_WIDTH
    D, W = w_in.shape[0], MOBA_WIDTH
    rows = D // CAST_STEPS
    slab = lambda w: pl.BlockSpec((w.shape[0] // CAST_STEPS, w.shape[1]), lambda s: (s, 0))
    assert all(w.shape[0] % (CAST_STEPS * 16) == 0 for w in ws) and rows % LANES == 0
    part_specs = [pl.BlockSpec((W, rows), lambda s: (0, s)) if c in IN_TRANSPOSED
                  else pl.BlockSpec((rows, W), lambda s: (s, 0)) for c in range(IN_PARTS)]
    part_shapes = [jax.ShapeDtypeStruct((W, D) if c in IN_TRANSPOSED else (D, W), jnp.bfloat16)
                   for c in range(IN_PARTS)]
    out = pl.pallas_call(
        _cast_kernel,
        grid=(CAST_STEPS,),
        in_specs=[slab(w_in)] + [slab(w) for w in ws],
        out_specs=part_specs + [slab(w) for w in ws],
        out_shape=part_shapes + [jax.ShapeDtypeStruct(w.shape, jnp.bfloat16) for w in ws],
        compiler_params=pltpu.CompilerParams(
            dimension_semantics=("arbitrary",), vmem_limit_bytes=VMEM_LIMIT),
        name="cast_weights",
    )(w_in, *ws)
    return out[:IN_PARTS], out[IN_PARTS:]


def kernel(x, p, attn_norm, w_in, ret_norm, w_out, ffn_norm, w_up, w_gate, conv_w, conv_b, w_down,
           ple_norm, w_ple, w_ple_gate, final_norm):
    B, S, D = x.shape
    assert D == D_MODEL and S % ROW_TILE == 0 and w_in.shape[0] == 1
    bf, f32 = jnp.bfloat16, jnp.float32
    MW, RW = MOBA_WIDTH, RET_WIDTH
    (wqT, wk, wvT, wrq, wrkT, wrv, wrg), (wo, wu, wg, wd, wp, wpg) = _cast_bf16(
        w_in[0], w_out[0], w_up[0], w_gate[0], w_down[0], w_ple[0], w_ple_gate[0])
    row = lambda a: a.reshape(1, -1).astype(f32)

    nf32 = np.float32
    slopes_l2 = (np.exp2(-8.0 * np.arange(1, MOBA_HEADS + 1, dtype=nf32) / MOBA_HEADS).astype(nf32)
                 * nf32(LOG2E))
    kterm = slopes_l2[None, :] * np.arange(BLK, dtype=nf32)[:, None]
    kterm_hi = (kterm.view(np.uint32) & np.uint32(0xFFFF0000)).view(nf32)
    kterm_lo = kterm - kterm_hi
    kpos = np.zeros((BLK, MOBA_HEADS, QK_PAD), nf32)
    kpos[:, :, ALIBI_ROW] = kterm_hi
    kpos[:, :, ALIBI_ROW + 1] = kterm_lo
    kpos = kpos.reshape(BLK, MOBA_HEADS * QK_PAD)

    qT, k, vT, sel, ret_out = _in_proj(
        x, row(attn_norm[0]), wqT, wk, kpos, wvT, wrq, wrkT, wrv, wrg, _retention_tables(ret_norm[0]))

    moba_out = _moba(slopes_l2, qT, k, vT, sel)

    out = _tail(
        x.reshape(B * S, D), moba_out.reshape(B * S, MW), ret_out.reshape(B * S, RW),
        p[0].reshape(B * S, PLE_DIM),
        wo, row(ffn_norm[0]),
        wu, wg, conv_w[0].astype(f32), row(conv_b[0]),
        wd,
        row(ple_norm[0]), wp, wpg, row(final_norm),
        seq_len=S)
    return out.reshape(B, S, D)
```
